```python
import math
import jax, jax.numpy as jnp
from jax import lax
import numpy as np

D_MODEL = 1024
BATCH = 8
SEQ = 4096
DEPTH = 1
DEC_BATCH = 32
DEC_SEQ = 1
PAST_LEN = 16384
PAGE_SIZE = 128

DILATED_PATTERNS = ((128, 1), (512, 4), (2048, 16))
N_GROUPS = 3
HEADS_PER_GROUP = 4
HEAD_DIM = 64
ATTN_WIDTH = N_GROUPS * HEADS_PER_GROUP * HEAD_DIM
ATTN_OUT_WIDTH = HEADS_PER_GROUP * HEAD_DIM
ATTN_BLOCK = 128
ATTN_SCALE = HEAD_DIM ** -0.5
ROPE_DIM = HEAD_DIM // 4
ROPE_THETA = 500000.0
RET_HEADS = 4
RET_QK_DIM = 128
RET_V_DIM = 256
RET_QK_WIDTH = RET_HEADS * RET_QK_DIM
RET_V_WIDTH = RET_HEADS * RET_V_DIM
RET_CHUNK = 128
RET_ROPE_THETA = 10000.0
IN_WIDTH = 3 * ATTN_WIDTH + 2 * RET_QK_WIDTH + 2 * RET_V_WIDTH + 2 * D_MODEL
D_FF = ((-(-8 * D_MODEL // 3)) + 255) // 256 * 256
DEEPNORM_ALPHA = (2 * DEPTH) ** 0.25
DEEPNORM_BETA = (8 * DEPTH) ** -0.25
LN_EPS = 1e-5
GN_EPS = 1e-6

kernel_name = 'hybrid_dilated_retention_decoder_step'


def _in_splits():
    widths = (ATTN_WIDTH,) * 3 + (RET_QK_WIDTH,) * 2 + (RET_V_WIDTH, RET_V_WIDTH, D_MODEL, D_MODEL)
    return [int(v) for v in np.cumsum(widths)[:-1]]


def _layer_norm(x, g, b):
    xf = x.astype(jnp.float32)
    mu = jnp.mean(xf, -1, keepdims=True)
    var = jnp.mean(jnp.square(xf - mu), -1, keepdims=True)
    y = (xf - mu) * lax.rsqrt(var + LN_EPS) * g.astype(jnp.float32) + b.astype(jnp.float32)
    return y.astype(x.dtype)


def _group_norm(y):
    mu = jnp.mean(y, -1, keepdims=True)
    var = jnp.mean(jnp.square(y - mu), -1, keepdims=True)
    return (y - mu) * lax.rsqrt(var + GN_EPS)


def _rope(x, pos, rot_dim, theta):
    half = rot_dim // 2
    inv = jnp.exp(-math.log(theta) * jnp.arange(half, dtype=jnp.float32) * (2.0 / rot_dim))
    ang = pos.astype(jnp.float32)[:, None] * inv[None, :]
    cos = jnp.cos(ang)[:, None, :]
    sin = jnp.sin(ang)[:, None, :]
    xf = x.astype(jnp.float32)
    x1 = xf[..., :half]
    x2 = xf[..., half:rot_dim]
    out = jnp.concatenate([x1 * cos - x2 * sin, x1 * sin + x2 * cos, xf[..., rot_dim:]], axis=-1)
    return out.astype(x.dtype)


def _attend(s, mask, v, eq):
    s = jnp.where(mask, s.astype(jnp.float32), -jnp.inf)
    m = jnp.max(s, -1, keepdims=True)
    p = jnp.exp(s - m)
    l = jnp.sum(p, -1, keepdims=True)
    o = jnp.einsum(eq, (p / l).astype(v.dtype), v)
    return o, (m + jnp.log(l))[..., 0]


def _dilated_group_prompt(q, k, v, window, dil):
    B, S, H, dh = q.shape
    n = window // dil
    L = S // dil
    bq = min(ATTN_BLOCK, L)
    nblk = -(-L // bq)
    Lp = nblk * bq

    def sub(t, front):
        t = t.reshape(B, L, dil, H, dh).transpose(0, 2, 1, 3, 4)
        return jnp.pad(t, ((0, 0), (0, 0), (front, Lp - L), (0, 0), (0, 0)))

    qs = sub(q, 0).reshape(B, dil, nblk, bq, H, dh)
    win = (jnp.arange(nblk) * bq)[:, None] + jnp.arange(n + bq)[None, :]
    kb = sub(k, n)[:, :, win]
    vb = sub(v, n)[:, :, win]
    s = jnp.einsum('brnqhd,brnkhd->brnhqk', qs, kb) * ATTN_SCALE
    i = jnp.arange(bq)[:, None]
    j = jnp.arange(n + bq)[None, :]
    dist = i + n - j
    key_pos = (jnp.arange(nblk) * bq)[:, None, None] + j[None] - n
    mask = (dist >= 0) & (dist <= n) & (key_pos >= 0)
    o, lse = _attend(s, mask[None, None, :, None], vb, 'brnhqk,brnkhd->brnqhd')
    o = o.reshape(B, dil, Lp, H, dh)[:, :, :L].transpose(0, 2, 1, 3, 4).reshape(B, S, H, dh)
    lse = lse.transpose(0, 1, 2, 4, 3).reshape(B, dil, Lp, H)[:, :, :L].transpose(0, 2, 1, 3).reshape(B, S, H)
    return o, lse


def _dilated_group_cached(q, k, v, k_buf, v_buf, window, dil):
    T = q.shape[1]
    Lg = k_buf.shape[1]
    n = window // dil
    kc = jnp.concatenate([k_buf, k], axis=1)
    vc = jnp.concatenate([v_buf, v], axis=1)
    idx = Lg + jnp.arange(T)[:, None] - dil * jnp.arange(n + 1)[None, :]
    valid = idx >= 0
    idx = jnp.maximum(idx, 0)
    kg = kc[:, idx]
    vg = vc[:, idx]
    s = jnp.einsum('bthd,btkhd->bthk', q, kg) * ATTN_SCALE
    o, lse = _attend(s, valid[None, :, None, :], vg, 'bthk,btkhd->bthd')
    return o, lse, kc[:, -Lg:], vc[:, -Lg:]


def _dilated_mixture(q, k, v, bufs):
    outs, lses, new_bufs = [], [], []
    for g, (window, dil) in enumerate(DILATED_PATTERNS):
        qg, kg, vg = q[:, :, g], k[:, :, g], v[:, :, g]
        if bufs is None:
            o, lse = _dilated_group_prompt(qg, kg, vg, window, dil)
            keep = min(window, qg.shape[1])
            new_bufs += [kg[:, -keep:], vg[:, -keep:]]
        else:
            o, lse, kn, vn = _dilated_group_cached(qg, kg, vg, bufs[2 * g], bufs[2 * g + 1], window, dil)
            new_bufs += [kn, vn]
        outs.append(o.astype(jnp.float32))
        lses.append(lse)
    w = jax.nn.softmax(jnp.stack(lses), axis=0)
    o = jnp.einsum('gbsh,gbshd->bshd', w, jnp.stack(outs)).astype(q.dtype)
    return o, new_bufs


def _retention(q, k, v, state0):
    B, S, H, dk = q.shape
    dv = v.shape[-1]
    C = math.gcd(S, RET_CHUNK)
    nc = S // C
    lg = jnp.log(1.0 - 2.0 ** (-5.0 - jnp.arange(H, dtype=jnp.float32)))
    idx = jnp.arange(C, dtype=jnp.float32)
    diff = idx[:, None] - idx[None, :]
    decay_in = jnp.where(diff >= 0, jnp.exp(lg[:, None, None] * jnp.maximum(diff, 0.0)), 0.0)
    decay_q = jnp.exp(lg[None, :] * (idx[:, None] + 1.0))
    decay_k = jnp.exp(lg[None, :] * (C - 1.0 - idx[:, None]))
    decay_c = jnp.exp(lg * C)

    def chunks(t):
        return jnp.moveaxis(t.astype(jnp.float32).reshape(B, nc, C, H, t.shape[-1]), 1, 0)

    def step(R, qkv):
        qc, kc, vc = qkv
        s = jnp.einsum('bihd,bjhd->bhij', qc, kc) * decay_in
        o = (jnp.einsum('bhij,bjhe->bihe', s, vc)
             + jnp.einsum('bihd,bhde->bihe', qc, R) * decay_q[None, :, :, None])
        R = R * decay_c[None, :, None, None] + jnp.einsum('bjhd,bjhe->bhde', kc * decay_k[None, :, :, None], vc)
        return R, o

    R, o = lax.scan(step, state0.astype(jnp.float32), (chunks(q), chunks(k), chunks(v)))
    return jnp.moveaxis(o, 0, 1).reshape(B, S, H, dv), R


def _layer(x, c, pos, att_bufs, ret_state, w_ada, b_ada, w_in, w_att_out, w_ret_out, w_o,
           ln1_g, ln1_b, w_ffn_in, w_ffn_out, ln2_g, ln2_b):
    B, S, _ = x.shape
    mod = (jax.nn.silu(c) @ w_ada + b_ada)[:, None, :]
    shift1, scale1, gate1, shift2, scale2, gate2 = jnp.split(mod, 6, axis=-1)
    h = x * (1 + scale1) + shift1
    qa, ka, va, rq, rk, rv, rg, ga, gb = jnp.split(h @ w_in, _in_splits(), axis=-1)
    n_att = N_GROUPS * HEADS_PER_GROUP
    qa = _rope(qa.reshape(B, S, n_att, HEAD_DIM), pos, ROPE_DIM, ROPE_THETA).reshape(B, S, N_GROUPS, HEADS_PER_GROUP, HEAD_DIM)
    ka = _rope(ka.reshape(B, S, n_att, HEAD_DIM), pos, ROPE_DIM, ROPE_THETA).reshape(B, S, N_GROUPS, HEADS_PER_GROUP, HEAD_DIM)
    va = va.reshape(B, S, N_GROUPS, HEADS_PER_GROUP, HEAD_DIM)
    o_att, att_state = _dilated_mixture(qa, ka, va, att_bufs)
    y_att = o_att.reshape(B, S, ATTN_OUT_WIDTH) @ w_att_out
    rq = _rope(rq.reshape(B, S, RET_HEADS, RET_QK_DIM), pos, RET_QK_DIM, RET_ROPE_THETA)
    rk = _rope(rk.reshape(B, S, RET_HEADS, RET_QK_DIM), pos, RET_QK_DIM, RET_ROPE_THETA) * (RET_QK_DIM ** -0.5)
    rv = rv.reshape(B, S, RET_HEADS, RET_V_DIM)
    o_ret, ret_state = _retention(rq, rk, rv, ret_state)
    o_ret = _group_norm(o_ret).astype(x.dtype).reshape(B, S, RET_V_WIDTH)
    y_ret = (jax.nn.silu(rg) * o_ret) @ w_ret_out
    mixed = (jax.nn.sigmoid(ga) * y_att + jax.nn.sigmoid(gb) * y_ret) @ w_o
    x = _layer_norm(DEEPNORM_ALPHA * x + gate1 * mixed, ln1_g, ln1_b)
    h2 = x * (1 + scale2) + shift2
    f_gate, f_up = jnp.split(h2 @ w_ffn_in, 2, axis=-1)
    ff = (jax.nn.silu(f_gate) * f_up) @ w_ffn_out
    x = _layer_norm(DEEPNORM_ALPHA * x + gate2 * ff, ln2_g, ln2_b)
    return x, att_state, ret_state


def setup_inputs(seed: int = 0) -> dict:
    key = jax.random.key(seed)
    ks = iter(jax.random.split(key, 40))
    D = D_MODEL
    fan = D ** -0.5
    beta = DEEPNORM_BETA

    def nrm(shape, scale=1.0):
        return jax.random.normal(next(ks), shape, jnp.float32) * scale

    inputs = {}
    inputs['x_prompt'] = nrm((BATCH, SEQ, D))
    inputs['x_sample'] = nrm((DEC_BATCH, DEC_SEQ, D))
    for window, _ in DILATED_PATTERNS:
        L = min(window, PAST_LEN)
        inputs['cache_k_w%d' % window] = nrm((DEPTH, DEC_BATCH, L, HEADS_PER_GROUP, HEAD_DIM))
        inputs['cache_v_w%d' % window] = nrm((DEPTH, DEC_BATCH, L, HEADS_PER_GROUP, HEAD_DIM))
    inputs['state_retention'] = nrm((DEPTH, DEC_BATCH, RET_HEADS, RET_QK_DIM, RET_V_DIM), 0.1)
    inputs['c_prompt'] = nrm((BATCH, D))
    inputs['c_sample'] = nrm((DEC_BATCH, D))
    inputs['w_ada'] = nrm((DEPTH, D, 6 * D), 0.5 * fan)
    inputs['b_ada'] = nrm((DEPTH, 6 * D), 0.02)
    inputs['w_in'] = jnp.concatenate([
        nrm((DEPTH, D, ATTN_WIDTH), fan),
        nrm((DEPTH, D, ATTN_WIDTH), fan),
        nrm((DEPTH, D, ATTN_WIDTH), beta * fan),
        nrm((DEPTH, D, RET_QK_WIDTH), fan),
        nrm((DEPTH, D, RET_QK_WIDTH), fan),
        nrm((DEPTH, D, RET_V_WIDTH), beta * fan),
        nrm((DEPTH, D, RET_V_WIDTH), fan),
        nrm((DEPTH, D, 2 * D), fan),
    ], axis=-1)
    inputs['w_att_out'] = nrm((DEPTH, ATTN_OUT_WIDTH, D), beta * ATTN_OUT_WIDTH ** -0.5)
    inputs['w_ret_out'] = nrm((DEPTH, RET_V_WIDTH, D), beta * RET_V_WIDTH ** -0.5)
    inputs['w_o'] = nrm((DEPTH, D, D), beta * fan)
    inputs['ln1_g'] = 1.0 + nrm((DEPTH, D), 0.02)
    inputs['ln1_b'] = nrm((DEPTH, D), 0.02)
    inputs['w_ffn_in'] = nrm((DEPTH, D, 2 * D_FF), beta * fan)
    inputs['w_ffn_out'] = nrm((DEPTH, D_FF, D), beta * D_FF ** -0.5)
    inputs['ln2_g'] = 1.0 + nrm((DEPTH, D), 0.02)
    inputs['ln2_b'] = nrm((DEPTH, D), 0.02)
    return inputs


def reference(x_prompt, x_sample, cache_k_w128, cache_v_w128, cache_k_w512, cache_v_w512,
              cache_k_w2048, cache_v_w2048, state_retention, c_prompt, c_sample,
              w_ada, b_ada, w_in, w_att_out, w_ret_out, w_o, ln1_g, ln1_b,
              w_ffn_in, w_ffn_out, ln2_g, ln2_b):
    pos_p = jnp.arange(x_prompt.shape[1], dtype=jnp.int32)
    pos_s = PAST_LEN + jnp.arange(x_sample.shape[1], dtype=jnp.int32)
    sample_caches = (cache_k_w128, cache_v_w128, cache_k_w512, cache_v_w512, cache_k_w2048, cache_v_w2048)
    x_p, x_s = x_prompt, x_sample
    p_new = [[] for _ in range(7)]
    s_new = [[] for _ in range(7)]
    for layer in range(DEPTH):
        lw = (w_ada[layer], b_ada[layer], w_in[layer], w_att_out[layer], w_ret_out[layer], w_o[layer],
              ln1_g[layer], ln1_b[layer], w_ffn_in[layer], w_ffn_out[layer], ln2_g[layer], ln2_b[layer])
        zero_state = jnp.zeros((x_p.shape[0], RET_HEADS, RET_QK_DIM, RET_V_DIM), jnp.float32)
        x_p, att_p, ret_p = _layer(x_p, c_prompt, pos_p, None, zero_state, *lw)
        x_s, att_s, ret_s = _layer(x_s, c_sample, pos_s, [cc[layer] for cc in sample_caches],
                                   state_retention[layer], *lw)
        for i, t in enumerate(att_p + [ret_p]):
            p_new[i].append(t)
        for i, t in enumerate(att_s + [ret_s]):
            s_new[i].append(t)
    p_new = [jnp.stack(t) for t in p_new]
    s_new = [jnp.stack(t) for t in s_new]
    return (x_p, x_s,
            p_new[0], p_new[1], p_new[2], p_new[3], p_new[4], p_new[5], p_new[6],
            s_new[0], s_new[1], s_new[2], s_new[3], s_new[4], s_new[5], s_new[6])
```

```python
import functools
import math

import jax
import jax.numpy as jnp
from jax import lax
from jax.experimental import pallas as pl
from jax.experimental.pallas import tpu as pltpu

F32 = jnp.float32
BF16 = jnp.bfloat16

D_MODEL = 1024
PAST_LEN = 16384
DILATED_PATTERNS = ((128, 1), (512, 4), (2048, 16))
N_GROUPS = 3
HEADS_PER_GROUP = 4
HEAD_DIM = 64
GROUP_WIDTH = HEADS_PER_GROUP * HEAD_DIM
ATTN_WIDTH = N_GROUPS * GROUP_WIDTH
ATTN_BAND = 128
ATTN_SCALE = HEAD_DIM ** -0.5
ROPE_DIM = HEAD_DIM // 4
ROPE_THETA = 500000.0
RET_HEADS = 4
RET_QK_DIM = 128
RET_V_DIM = 256
RET_QK_WIDTH = RET_HEADS * RET_QK_DIM
RET_V_WIDTH = RET_HEADS * RET_V_DIM
RET_CHUNK = 128
RET_ROPE_THETA = 10000.0
IN_WIDTH = 3 * ATTN_WIDTH + 2 * RET_QK_WIDTH + 2 * RET_V_WIDTH + 2 * D_MODEL
D_FF = 2816
DEEPNORM_ALPHA = 2.0 ** 0.25
LN_EPS = 1e-5
GN_EPS = 1e-6

LANES = 128
SEG = 256
N_SEG = IN_WIDTH // SEG
OFF_RQ, OFF_RK, OFF_RV, OFF_RG, OFF_GA, OFF_GB = 0, 512, 1024, 2048, 3072, 4096
OFF_QA, OFF_KA, OFF_VA = 5120, 5888, 6656
W_IN_ROTATE = 3 * ATTN_WIDTH
NEG_BIG = -1e30
VMEM_LIMIT = 56 * 1024 * 1024


def _cparams(sem):
    return pltpu.CompilerParams(dimension_semantics=sem, vmem_limit_bytes=VMEM_LIMIT)


def _resident(shape):
    nd = len(shape)
    return pl.BlockSpec(shape, lambda *_: (0,) * nd, pipeline_mode=pl.Buffered(1))


def _sigmoid(x):
    return 1.0 / (1.0 + jnp.exp(-x))


def _ada_kernel(c_ref, w_ref, b_ref, o_ref):
    c = c_ref[...]
    a = (c * _sigmoid(c)).astype(BF16)
    o_ref[...] = jnp.dot(a, w_ref[...].astype(BF16), preferred_element_type=F32) + b_ref[...]


def _modulation(c_all, w_ada, b_ada):
    rows = c_all.shape[0]
    tn = 1024
    return pl.pallas_call(
        _ada_kernel,
        grid=(6 * D_MODEL // tn,),
        in_specs=[pl.BlockSpec((rows, D_MODEL), lambda j: (0, 0)),
                  pl.BlockSpec((D_MODEL, tn), lambda j: (0, j)),
                  pl.BlockSpec((1, tn), lambda j: (0, j))],
        out_specs=pl.BlockSpec((rows, tn), lambda j: (0, j)),
        out_shape=jax.ShapeDtypeStruct((rows, 6 * D_MODEL), F32),
        compiler_params=_cparams(("arbitrary",)),
        name="modulation",
    )(c_all, w_ada, b_ada.reshape(1, -1))


def _rope_tables(pos):
    posf = pos.astype(F32)[:, None]
    lane = jnp.arange(LANES)
    half = ROPE_DIM // 2
    inv = jnp.exp(-math.log(ROPE_THETA) * jnp.arange(half, dtype=F32) * (2.0 / ROPE_DIM))
    ang = posf * inv[None, :]
    cos, sin = jnp.cos(ang), jnp.sin(ang)
    in_head = lane % HEAD_DIM
    f = in_head % half
    att_c = jnp.where(in_head < ROPE_DIM, cos[:, f], 1.0)
    att_lo = jnp.where(in_head < half, -sin[:, f], 0.0)
    att_hi = jnp.where((in_head >= half) & (in_head < ROPE_DIM), sin[:, f], 0.0)
    rhalf = RET_QK_DIM // 2
    rinv = jnp.exp(-math.log(RET_ROPE_THETA) * jnp.arange(rhalf, dtype=F32) * (2.0 / RET_QK_DIM))
    rang = posf * rinv[None, :]
    rcos, rsin = jnp.cos(rang), jnp.sin(rang)
    ret_c = jnp.concatenate([rcos, rcos], axis=-1)
    ret_s = jnp.concatenate([-rsin, rsin], axis=-1)
    return att_c, att_lo, att_hi, ret_c, ret_s


def _inproj_kernel(x_ref, shift_ref, scale_ref, w_ref, ac_ref, alo_ref, ahi_ref, rc_ref, rs_ref,
                   proj_ref, k0_ref, v0_ref, k1_ref, v1_ref, k2_ref, v2_ref, *, tm, win_blocks, win_first):
    i = pl.program_id(1)
    h = (x_ref[0] * (1.0 + scale_ref[0]) + shift_ref[0]).astype(BF16)
    ac, alo, ahi = ac_ref[...], alo_ref[...], ahi_ref[...]
    rc, rs = rc_ref[...], rs_ref[...]
    k_wins = (k0_ref, k1_ref, k2_ref)
    v_wins = (v0_ref, v1_ref, v2_ref)

    def att_rope(a):
        parts = []
        for u in range(SEG // LANES):
            xs = a[:, u * LANES:(u + 1) * LANES]
            parts.append(xs * ac + pltpu.roll(xs, LANES - ROPE_DIM // 2, 1) * alo
                         + pltpu.roll(xs, ROPE_DIM // 2, 1) * ahi)
        return jnp.concatenate(parts, axis=1)

    def ret_rope(a):
        parts = []
        for u in range(SEG // LANES):
            xs = a[:, u * LANES:(u + 1) * LANES]
            parts.append(xs * rc + pltpu.roll(xs, RET_QK_DIM // 2, 1) * rs)
        return jnp.concatenate(parts, axis=1)

    def write_window(ref, g, a):
        wb = win_blocks[g]

        @pl.when(i >= win_first[g])
        def _():
            ref[0] = a[tm - wb:, :]

    for seg in range(N_SEG):
        c0 = seg * SEG
        a = jnp.dot(h, w_ref[:, c0:c0 + SEG], preferred_element_type=F32)
        if c0 < OFF_RK:
            a = ret_rope(a)
        elif c0 < OFF_RV:
            a = ret_rope(a) * (RET_QK_DIM ** -0.5)
        elif c0 < OFF_RG:
            pass
        elif c0 < OFF_GA:
            a = a * _sigmoid(a)
        elif c0 < OFF_QA:
            a = _sigmoid(a)
        elif c0 < OFF_KA:
            a = att_rope(a) * ATTN_SCALE
        elif c0 < OFF_VA:
            a = att_rope(a)
            write_window(k_wins[(c0 - OFF_KA) // SEG], (c0 - OFF_KA) // SEG, a)
        else:
            write_window(v_wins[(c0 - OFF_VA) // SEG], (c0 - OFF_VA) // SEG, a)
        proj_ref[0, :, c0:c0 + SEG] = a.astype(proj_ref.dtype)


def _in_projection(x, mod, w_in, tables, *, tm, win_rows, proj_dtype):
    G, R, _ = x.shape
    n_tiles = R // tm
    mod_rows = mod.shape[1]
    mod_blk = tm if mod_rows == R else 1
    tab_rows = tables[0].shape[0]
    tab_blk = tm if tab_rows == R else 1
    win_blocks = tuple(min(w, tm) for w in win_rows)
    win_first = tuple(n_tiles - w // wb for w, wb in zip(win_rows, win_blocks))

    def mod_spec(col):
        if mod_blk == 1:
            return pl.BlockSpec((1, 1, D_MODEL), lambda b, i: (b, 0, col))
        return pl.BlockSpec((1, tm, D_MODEL), lambda b, i: (b, i, col))

    if tab_blk == 1:
        tab_spec = pl.BlockSpec((1, LANES), lambda b, i: (0, 0))
    else:
        tab_spec = pl.BlockSpec((tm, LANES), lambda b, i: (i, 0))

    def win_spec(g):
        first = win_first[g]
        return pl.BlockSpec((1, win_blocks[g], GROUP_WIDTH), lambda b, i: (b, jnp.maximum(i - first, 0), 0))

    win_shapes, win_specs = [], []
    for g in range(N_GROUPS):
        for _ in range(2):
            win_shapes.append(jax.ShapeDtypeStruct((G, win_rows[g], GROUP_WIDTH), F32))
            win_specs.append(win_spec(g))

    kern = functools.partial(_inproj_kernel, tm=tm, win_blocks=win_blocks, win_first=win_first)
    return pl.pallas_call(
        kern,
        grid=(G, n_tiles),
        in_specs=[pl.BlockSpec((1, tm, D_MODEL), lambda b, i: (b, i, 0)),
                  mod_spec(0), mod_spec(1),
                  _resident((D_MODEL, IN_WIDTH))] + [tab_spec] * 5,
        out_specs=[pl.BlockSpec((1, tm, IN_WIDTH), lambda b, i: (b, i, 0))] + win_specs,
        out_shape=[jax.ShapeDtypeStruct((G, R, IN_WIDTH), proj_dtype)] + win_shapes,
        compiler_params=_cparams(("arbitrary", "arbitrary")),
        name="in_projection",
    )(x, mod, mod, w_in, *tables)


def _attn_kernel(q_ref, k_ref, v_ref, kp_ref, vp_ref, o_ref, lse_ref, *, tl):
    t = pl.program_id(2)
    row = lax.broadcasted_iota(jnp.int32, (ATTN_BAND, ATTN_BAND), 0)
    col = lax.broadcasted_iota(jnp.int32, (ATTN_BAND, ATTN_BAND), 1)
    lane = lax.broadcasted_iota(jnp.int32, (1, LANES), 1)
    cur_ok = col <= row

    def block(q, k_prev, v_prev, k_cur, v_cur, prev_shift):
        prev_ok = col >= row + prev_shift
        mask = jnp.concatenate([prev_ok, cur_ok], axis=1)
        o_parts, l_parts = [], []
        for p in range(GROUP_WIDTH // LANES):
            sl = slice(p * LANES, (p + 1) * LANES)
            kk = jnp.concatenate([k_prev[:, sl], k_cur[:, sl]], axis=0)
            vv = jnp.concatenate([v_prev[:, sl], v_cur[:, sl]], axis=0)
            qp = q[:, sl]
            o_pair = jnp.zeros((ATTN_BAND, LANES), F32)
            l_pair = jnp.zeros((ATTN_BAND, LANES), F32)
            for hh in range(LANES // HEAD_DIM):
                head = (lane >= hh * HEAD_DIM) & (lane < (hh + 1) * HEAD_DIM)
                qm = jnp.where(head, qp, jnp.zeros_like(qp))
                s = lax.dot_general(qm, kk, (((1,), (1,)), ((), ())), preferred_element_type=F32)
                s = jnp.where(mask, s, NEG_BIG)
                m = jnp.max(s, axis=1, keepdims=True)
                e = jnp.exp(s - m)
                l = jnp.sum(e, axis=1, keepdims=True)
                pn = (e / l).astype(BF16)
                oh = jnp.dot(pn, vv, preferred_element_type=F32)
                o_pair = jnp.where(head, oh, o_pair)
                l_pair = jnp.where(head, m + jnp.log(l), l_pair)
            o_parts.append(o_pair)
            l_parts.append(l_pair)
        return jnp.concatenate(o_parts, axis=1), jnp.concatenate(l_parts, axis=1)

    def emit(j0, o, lse):
        o_ref[0, pl.ds(j0, ATTN_BAND), :] = o.astype(o_ref.dtype)
        lse_ref[0, pl.ds(j0, ATTN_BAND), :] = lse

    first = jnp.where(t > 0, 0, ATTN_BAND)
    o, lse = block(q_ref[0, 0:ATTN_BAND, :], kp_ref[0], vp_ref[0],
                   k_ref[0, 0:ATTN_BAND, :], v_ref[0, 0:ATTN_BAND, :], first)
    emit(0, o, lse)

    def body(j, carry):
        j0 = pl.multiple_of(j * ATTN_BAND, ATTN_BAND)
        jp = pl.multiple_of((j - 1) * ATTN_BAND, ATTN_BAND)
        o, lse = block(q_ref[0, pl.ds(j0, ATTN_BAND), :],
                       k_ref[0, pl.ds(jp, ATTN_BAND), :], v_ref[0, pl.ds(jp, ATTN_BAND), :],
                       k_ref[0, pl.ds(j0, ATTN_BAND), :], v_ref[0, pl.ds(j0, ATTN_BAND), :], 0)
        emit(j0, o, lse)
        return carry

    if tl > ATTN_BAND:
        lax.fori_loop(1, tl // ATTN_BAND, body, 0)


def _prompt_attention(proj, g, *, tl):
    B, S, _ = proj.shape
    dil = DILATED_PATTERNS[g][1]
    L = S // dil
    view = proj.reshape(B, L, dil * IN_WIDTH)
    nseg = IN_WIDTH // SEG
    qc, kc, vc = OFF_QA // SEG + g, OFF_KA // SEG + g, OFF_VA // SEG + g
    sub = tl // ATTN_BAND

    def cur(c):
        return pl.BlockSpec((1, tl, SEG), lambda b, r, t: (b, t, r * nseg + c))

    def prev(c):
        return pl.BlockSpec((1, ATTN_BAND, SEG), lambda b, r, t: (b, jnp.maximum(t * sub - 1, 0), r * nseg + c))

    out_spec = pl.BlockSpec((1, tl, SEG), lambda b, r, t: (b, t, r))
    o, lse = pl.pallas_call(
        functools.partial(_attn_kernel, tl=tl),
        grid=(B, dil, L // tl),
        in_specs=[cur(qc), cur(kc), cur(vc), prev(kc), prev(vc)],
        out_specs=[out_spec, out_spec],
        out_shape=[jax.ShapeDtypeStruct((B, L, dil * SEG), BF16),
                   jax.ShapeDtypeStruct((B, L, dil * SEG), F32)],
        compiler_params=_cparams(("arbitrary", "arbitrary", "arbitrary")),
        name="prompt_attention_g%d" % g,
    )(view, view, view, view, view)
    return o.reshape(B, S, SEG), lse.reshape(B, S, SEG)


def _retention_tables():
    C = RET_CHUNK
    lg = jnp.log(1.0 - 2.0 ** (-5.0 - jnp.arange(RET_HEADS, dtype=F32)))
    idx = jnp.arange(C, dtype=F32)
    diff = idx[:, None] - idx[None, :]
    decay_in = jnp.where(diff >= 0, jnp.exp(lg[:, None, None] * jnp.maximum(diff, 0.0)), 0.0)
    decay_q = jnp.exp(lg[:, None] * (idx[None, :] + 1.0))[:, :, None]
    decay_k = jnp.exp(lg[:, None] * (C - 1.0 - idx[None, :]))[:, :, None]
    return decay_in, decay_q, decay_k


def _ret_gamma(h):
    return 1.0 - 2.0 ** (-5.0 - h)


def _retention_kernel(qk_ref, v_ref, g_ref, din_ref, dq_ref, dk_ref, o_ref, state_ref, r_scr, *, tr):
    t = pl.program_id(1)

    @pl.when(t == 0)
    def _():
        r_scr[...] = jnp.zeros_like(r_scr)

    for c in range(tr // RET_CHUNK):
        rows = slice(c * RET_CHUNK, (c + 1) * RET_CHUNK)
        for h in range(RET_HEADS):
            q = qk_ref[0, rows, OFF_RQ + h * RET_QK_DIM:OFF_RQ + (h + 1) * RET_QK_DIM]
            k = qk_ref[0, rows, OFF_RK + h * RET_QK_DIM:OFF_RK + (h + 1) * RET_QK_DIM]
            v = v_ref[0, rows, h * RET_V_DIM:(h + 1) * RET_V_DIM]
            r_old = r_scr[h]
            s = lax.dot_general(q, k, (((1,), (1,)), ((), ())), preferred_element_type=F32) * din_ref[h]
            o = (jnp.dot(s.astype(BF16), v, preferred_element_type=F32)
                 + jnp.dot(q, r_old.astype(BF16), preferred_element_type=F32) * dq_ref[h])
            kd = (k.astype(F32) * dk_ref[h]).astype(BF16)
            upd = lax.dot_general(kd, v, (((0,), (0,)), ((), ())), preferred_element_type=F32)
            r_scr[h] = r_old * (_ret_gamma(h) ** RET_CHUNK) + upd
            mu = jnp.mean(o, axis=1, keepdims=True)
            d = o - mu
            var = jnp.mean(d * d, axis=1, keepdims=True)
            gn = d * lax.rsqrt(var + GN_EPS)
            gate = g_ref[0, rows, h * RET_V_DIM:(h + 1) * RET_V_DIM].astype(F32)
            o_ref[0, rows, h * RET_V_DIM:(h + 1) * RET_V_DIM] = (gate * gn).astype(o_ref.dtype)

    @pl.when(t == pl.num_programs(1) - 1)
    def _():
        state_ref[0] = r_scr[...]


def _prompt_retention(proj, *, tr):
    B, S, _ = proj.shape
    tabs = _retention_tables()
    blk = RET_V_WIDTH

    def colblk(c):
        return pl.BlockSpec((1, tr, blk), lambda b, t: (b, t, c))

    return pl.pallas_call(
        functools.partial(_retention_kernel, tr=tr),
        grid=(B, S // tr),
        in_specs=[colblk(0), colblk(OFF_RV // blk), colblk(OFF_RG // blk),
                  pl.BlockSpec(tabs[0].shape, lambda b, t: (0, 0, 0)),
                  pl.BlockSpec(tabs[1].shape, lambda b, t: (0, 0, 0)),
                  pl.BlockSpec(tabs[2].shape, lambda b, t: (0, 0, 0))],
        out_specs=[pl.BlockSpec((1, tr, RET_V_WIDTH), lambda b, t: (b, t, 0)),
                   pl.BlockSpec((1, RET_HEADS, RET_QK_DIM, RET_V_DIM), lambda b, t: (b, 0, 0, 0))],
        out_shape=[jax.ShapeDtypeStruct((B, S, RET_V_WIDTH), BF16),
                   jax.ShapeDtypeStruct((B, RET_HEADS, RET_QK_DIM, RET_V_DIM), F32)],
        scratch_shapes=[pltpu.VMEM((RET_HEADS, RET_QK_DIM, RET_V_DIM), F32)],
        compiler_params=_cparams(("arbitrary", "arbitrary")),
        name="prompt_retention",
    )(proj, proj, proj, *tabs)


def _layer_norm(u, g, b):
    mu = jnp.mean(u, axis=1, keepdims=True)
    d = u - mu
    var = jnp.mean(d * d, axis=1, keepdims=True)
    return d * lax.rsqrt(var + LN_EPS) * g + b


def _merge_ffn_kernel(x_ref, ga_ref, gb_ref, ret_ref, o0_ref, o1_ref, o2_ref, l0_ref, l1_ref, l2_ref,
                      gate1_ref, shift2_ref, scale2_ref, gate2_ref,
                      watt_ref, wret_ref, wo_ref, wfi_ref, wfo_ref,
                      ln1g_ref, ln1b_ref, ln2g_ref, ln2b_ref, y_ref, act_scr):
    l0, l1, l2 = l0_ref[0], l1_ref[0], l2_ref[0]
    m = jnp.maximum(jnp.maximum(l0, l1), l2)
    e0, e1, e2 = jnp.exp(l0 - m), jnp.exp(l1 - m), jnp.exp(l2 - m)
    att = (e0 * o0_ref[0].astype(F32) + e1 * o1_ref[0].astype(F32) + e2 * o2_ref[0].astype(F32)) / (e0 + e1 + e2)
    y_att = jnp.dot(att.astype(BF16), watt_ref[...], preferred_element_type=F32)
    y_ret = jnp.dot(ret_ref[0].astype(BF16), wret_ref[...], preferred_element_type=F32)
    mix = (ga_ref[0].astype(F32) * y_att + gb_ref[0].astype(F32) * y_ret).astype(BF16)
    mixed = jnp.dot(mix, wo_ref[...], preferred_element_type=F32)
    x1 = _layer_norm(DEEPNORM_ALPHA * x_ref[0] + gate1_ref[0] * mixed, ln1g_ref[...], ln1b_ref[...])
    h2 = (x1 * (1.0 + scale2_ref[0]) + shift2_ref[0]).astype(BF16)
    for c0 in range(0, D_FF, SEG):
        fg = jnp.dot(h2, wfi_ref[:, c0:c0 + SEG], preferred_element_type=F32)
        fu = jnp.dot(h2, wfi_ref[:, D_FF + c0:D_FF + c0 + SEG], preferred_element_type=F32)
        act_scr[:, c0:c0 + SEG] = (fg * _sigmoid(fg) * fu).astype(BF16)
    ff = jnp.dot(act_scr[...], wfo_ref[...], preferred_element_type=F32)
    y_ref[0] = _layer_norm(DEEPNORM_ALPHA * x1 + gate2_ref[0] * ff, ln2g_ref[...], ln2b_ref[...])


def _merge_ffn(x, proj, ret, o_groups, lse_groups, mod, weights, ln, *, tm):
    G, R, _ = x.shape
    mod_rows = mod.shape[1]

    def mod_spec(col):
        if mod_rows == 1:
            return pl.BlockSpec((1, 1, D_MODEL), lambda b, i: (b, 0, col))
        return pl.BlockSpec((1, tm, D_MODEL), lambda b, i: (b, i, col))

    def rows(width, col=0):
        return pl.BlockSpec((1, tm, width), lambda b, i: (b, i, col))

    vec = pl.BlockSpec((1, D_MODEL), lambda b, i: (0, 0))
    return pl.pallas_call(
        _merge_ffn_kernel,
        grid=(G, R // tm),
        in_specs=[rows(D_MODEL), rows(D_MODEL, OFF_GA // D_MODEL), rows(D_MODEL, OFF_GB // D_MODEL),
                  rows(RET_V_WIDTH)] + [rows(GROUP_WIDTH)] * 6
                 + [mod_spec(2), mod_spec(3), mod_spec(4), mod_spec(5)]
                 + [_resident(w.shape) for w in weights] + [vec] * 4,
        out_specs=rows(D_MODEL),
        out_shape=jax.ShapeDtypeStruct((G, R, D_MODEL), F32),
        scratch_shapes=[pltpu.VMEM((tm, D_FF), BF16)],
        compiler_params=_cparams(("arbitrary", "arbitrary")),
        name="merge_ffn",
    )(x, proj, proj, ret, *o_groups, *lse_groups, mod, mod, mod, mod, *weights, *ln)


def _sample_mixer_kernel(proj_ref, ck0_ref, cv0_ref, ck1_ref, cv1_ref, ck2_ref, cv2_ref, state_ref,
                         o_ref, lse_ref, ret_ref, nstate_ref):
    lane_g = lax.broadcasted_iota(jnp.int32, (1, GROUP_WIDTH), 1)

    def per_head(a):
        out = jnp.zeros_like(a)
        for hh in range(HEADS_PER_GROUP):
            head = (lane_g >= hh * HEAD_DIM) & (lane_g < (hh + 1) * HEAD_DIM)
            sh = jnp.sum(jnp.where(head, a, 0.0), axis=1, keepdims=True)
            out = jnp.where(head, sh, out)
        return out

    caches = ((ck0_ref, cv0_ref), (ck1_ref, cv1_ref), (ck2_ref, cv2_ref))
    for g in range(N_GROUPS):
        q = proj_ref[0, :, OFF_QA + g * SEG:OFF_QA + (g + 1) * SEG].astype(F32)
        kn = proj_ref[0, :, OFF_KA + g * SEG:OFF_KA + (g + 1) * SEG].astype(F32)
        vn = proj_ref[0, :, OFF_VA + g * SEG:OFF_VA + (g + 1) * SEG].astype(F32)
        kc = caches[g][0][0]
        vc = caches[g][1][0]
        s = per_head(kc * q)
        sn = per_head(kn * q)
        m = jnp.maximum(jnp.max(s, axis=0, keepdims=True), sn)
        e = jnp.exp(s - m)
        en = jnp.exp(sn - m)
        l = jnp.sum(e, axis=0, keepdims=True) + en
        o = (jnp.sum(e * vc, axis=0, keepdims=True) + en * vn) / l
        o_ref[0, :, g * SEG:(g + 1) * SEG] = o
        lse_ref[0, :, g * SEG:(g + 1) * SEG] = m + jnp.log(l)

    for h in range(RET_HEADS):
        gamma = _ret_gamma(h)
        q = proj_ref[0, :, OFF_RQ + h * RET_QK_DIM:OFF_RQ + (h + 1) * RET_QK_DIM].astype(F32)
        k = proj_ref[0, :, OFF_RK + h * RET_QK_DIM:OFF_RK + (h + 1) * RET_QK_DIM].astype(F32)
        v = proj_ref[0, :, OFF_RV + h * RET_V_DIM:OFF_RV + (h + 1) * RET_V_DIM].astype(F32)
        gate = proj_ref[0, :, OFF_RG + h * RET_V_DIM:OFF_RG + (h + 1) * RET_V_DIM].astype(F32)
        r_old = state_ref[0, h]
        q_col = jnp.broadcast_to(q, (RET_QK_DIM, RET_QK_DIM)).T
        k_col = jnp.broadcast_to(k, (RET_QK_DIM, RET_QK_DIM)).T
        q_col = jnp.concatenate([q_col, q_col], axis=1)
        k_col = jnp.concatenate([k_col, k_col], axis=1)
        qk = jnp.sum(q * k, axis=1, keepdims=True)
        o = qk * v + jnp.sum(q_col * r_old, axis=0, keepdims=True) * gamma
        nstate_ref[0, h] = r_old * gamma + k_col * v
        mu = jnp.mean(o, axis=1, keepdims=True)
        d = o - mu
        var = jnp.mean(d * d, axis=1, keepdims=True)
        ret_ref[0, :, h * RET_V_DIM:(h + 1) * RET_V_DIM] = gate * (d * lax.rsqrt(var + GN_EPS))


def _sample_mixers(proj, caches, state):
    Bs = proj.shape[0]
    cache_specs, cache_views = [], []
    for g, (window, dil) in enumerate(DILATED_PATTERNS):
        for a in caches[2 * g:2 * g + 2]:
            assert a.shape[1] == window
            cache_views.append(a.reshape(Bs, window // dil, dil * GROUP_WIDTH))
            cache_specs.append(pl.BlockSpec((1, ATTN_BAND, GROUP_WIDTH), lambda b: (b, 0, 0)))
    row = lambda w: pl.BlockSpec((1, 1, w), lambda b: (b, 0, 0))
    st = pl.BlockSpec((1, RET_HEADS, RET_QK_DIM, RET_V_DIM), lambda b: (b, 0, 0, 0))
    return pl.pallas_call(
        _sample_mixer_kernel,
        grid=(Bs,),
        in_specs=[row(IN_WIDTH)] + cache_specs + [st],
        out_specs=[row(ATTN_WIDTH), row(ATTN_WIDTH), row(RET_V_WIDTH), st],
        out_shape=[jax.ShapeDtypeStruct((Bs, 1, ATTN_WIDTH), F32),
                   jax.ShapeDtypeStruct((Bs, 1, ATTN_WIDTH), F32),
                   jax.ShapeDtypeStruct((Bs, 1, RET_V_WIDTH), F32),
                   jax.ShapeDtypeStruct(state.shape, F32)],
        compiler_params=_cparams(("arbitrary",)),
        name="sample_mixers",
    )(proj, *cache_views, state)


def _roll_kernel(*refs):
    n = len(refs) // 3
    for c_ref, new_ref, o_ref in zip(refs[:n], refs[n:2 * n], refs[2 * n:]):
        w = c_ref.shape[1]
        o_ref[0, pl.ds(0, w - 1), :] = c_ref[0, pl.ds(1, w - 1), :]
        o_ref[0, pl.ds(w - 1, 1), :] = new_ref[0]


def _roll_caches(caches, new_rows):
    Bs = caches[0].shape[0]
    cspecs = [pl.BlockSpec((1, a.shape[1], GROUP_WIDTH), lambda b: (b, 0, 0)) for a in caches]
    nspecs = [pl.BlockSpec((1, 1, GROUP_WIDTH), lambda b: (b, 0, 0)) for _ in new_rows]
    return pl.pallas_call(
        _roll_kernel,
        grid=(Bs,),
        in_specs=cspecs + nspecs,
        out_specs=cspecs,
        out_shape=[jax.ShapeDtypeStruct(a.shape, a.dtype) for a in caches],
        compiler_params=_cparams(("arbitrary",)),
        name="cache_roll",
    )(*caches, *new_rows)


def kernel(x_prompt, x_sample, cache_k_w128, cache_v_w128, cache_k_w512, cache_v_w512, cache_k_w2048,
           cache_v_w2048, state_retention, c_prompt, c_sample, w_ada, b_ada, w_in, w_att_out, w_ret_out,
           w_o, ln1_g, ln1_b, w_ffn_in, w_ffn_out, ln2_g, ln2_b):
    assert w_in.shape[0] == 1 and x_sample.shape[1] == 1
    B, S, D = x_prompt.shape
    Bs = x_sample.shape[0]

    w_in_b = jnp.concatenate([w_in[0][:, W_IN_ROTATE:], w_in[0][:, :W_IN_ROTATE]], axis=1).astype(BF16)
    weights = tuple(w[0].astype(BF16) for w in (w_att_out, w_ret_out, w_o, w_ffn_in, w_ffn_out))
    ln = (ln1_g, ln1_b, ln2_g, ln2_b)

    mod = _modulation(jnp.concatenate([c_prompt, c_sample], axis=0), w_ada[0], b_ada[0])
    mod_p = mod[:B].reshape(B, 1, 6 * D)
    mod_s = mod[B:].reshape(1, Bs, 6 * D)

    tabs_p = _rope_tables(jnp.arange(S, dtype=jnp.int32))
    win_rows = tuple(min(w, S) for w, _ in DILATED_PATTERNS)
    proj_p, *wins_p = _in_projection(x_prompt, mod_p, w_in_b, tabs_p, tm=256, win_rows=win_rows, proj_dtype=BF16)
    o_p, lse_p = [], []
    for g, (_, dil) in enumerate(DILATED_PATTERNS):
        o, lse = _prompt_attention(proj_p, g, tl=min(S // dil, 1024))
        o_p.append(o)
        lse_p.append(lse)
    ret_p, state_p = _prompt_retention(proj_p, tr=512)
    y_p = _merge_ffn(x_prompt, proj_p, ret_p, o_p, lse_p, mod_p, weights, ln, tm=256)

    xs = x_sample.reshape(1, Bs, D)
    tabs_s = _rope_tables(PAST_LEN + jnp.arange(1, dtype=jnp.int32))
    proj_s, *wins_s = _in_projection(xs, mod_s, w_in_b, tabs_s, tm=Bs, win_rows=(Bs,) * N_GROUPS, proj_dtype=F32)
    caches = [c[0].reshape(Bs, c.shape[2], GROUP_WIDTH)
              for c in (cache_k_w128, cache_v_w128, cache_k_w512, cache_v_w512, cache_k_w2048, cache_v_w2048)]
    o_s, lse_s, ret_s, state_s = _sample_mixers(proj_s.reshape(Bs, 1, IN_WIDTH), caches, state_retention[0])
    o_s = o_s.reshape(1, Bs, ATTN_WIDTH)
    lse_s = lse_s.reshape(1, Bs, ATTN_WIDTH)
    o_sg = [o_s[:, :, g * SEG:(g + 1) * SEG] for g in range(N_GROUPS)]
    lse_sg = [lse_s[:, :, g * SEG:(g + 1) * SEG] for g in range(N_GROUPS)]
    y_s = _merge_ffn(xs, proj_s, ret_s.reshape(1, Bs, RET_V_WIDTH), o_sg, lse_sg, mod_s, weights, ln, tm=Bs)
    rolled = _roll_caches(caches, [w.reshape(Bs, 1, GROUP_WIDTH) for w in wins_s])

    def heads(a):
        return a.reshape(1, a.shape[0], a.shape[1], HEADS_PER_GROUP, HEAD_DIM)

    return (y_p, y_s.reshape(Bs, 1, D),
            *[heads(w) for w in wins_p], state_p[None],
            *[heads(r) for r in rolled], state_s[None])
```

```python
import functools
import math

import jax
import jax.numpy as jnp
from jax import lax
from jax.experimental import pallas as pl
from jax.experimental.pallas import tpu as pltpu

F32 = jnp.float32
BF16 = jnp.bfloat16

D_MODEL = 1024
PAST_LEN = 16384
DILATED_PATTERNS = ((128, 1), (512, 4), (2048, 16))
N_GROUPS = 3
HEADS_PER_GROUP = 4
HEAD_DIM = 64
GROUP_WIDTH = HEADS_PER_GROUP * HEAD_DIM
ATTN_WIDTH = N_GROUPS * GROUP_WIDTH
ATTN_BAND = 128
ATTN_SCALE = HEAD_DIM ** -0.5
ROPE_DIM = HEAD_DIM // 4
ROPE_THETA = 500000.0
RET_HEADS = 4
RET_QK_DIM = 128
RET_V_DIM = 256
RET_QK_WIDTH = RET_HEADS * RET_QK_DIM
RET_V_WIDTH = RET_HEADS * RET_V_DIM
RET_CHUNK = 128
RET_ROPE_THETA = 10000.0
IN_WIDTH = 3 * ATTN_WIDTH + 2 * RET_QK_WIDTH + 2 * RET_V_WIDTH + 2 * D_MODEL
D_FF = 2816
DEEPNORM_ALPHA = 2.0 ** 0.25
LN_EPS = 1e-5
GN_EPS = 1e-6

LANES = 128
SEG = 256
N_SEG = IN_WIDTH // SEG
OFF_RQ, OFF_RK, OFF_RV, OFF_RG, OFF_GA, OFF_GB = 0, 512, 1024, 2048, 3072, 4096
OFF_QA, OFF_KA, OFF_VA = 5120, 5888, 6656
DENSE_WIDTH = OFF_QA
W_IN_ROTATE = 3 * ATTN_WIDTH
NEG_BIG = -1e30
VMEM_LIMIT = 56 * 1024 * 1024


def _cparams(sem):
    return pltpu.CompilerParams(dimension_semantics=sem, vmem_limit_bytes=VMEM_LIMIT)


def _resident(shape):
    nd = len(shape)
    return pl.BlockSpec(shape, lambda *_: (0,) * nd, pipeline_mode=pl.Buffered(1))


def _sigmoid(x):
    return 1.0 / (1.0 + jnp.exp(-x))


def _ret_gamma(h):
    return 1.0 - 2.0 ** (-5.0 - h)


def _ada_kernel(c_ref, w_ref, b_ref, o_ref):
    c = c_ref[...]
    a = (c * _sigmoid(c)).astype(BF16)
    o_ref[...] = jnp.dot(a, w_ref[...].astype(BF16), preferred_element_type=F32) + b_ref[...]


def _modulation(c_all, w_ada, b_ada):
    rows = c_all.shape[0]
    tn = 1024
    return pl.pallas_call(
        _ada_kernel,
        grid=(6 * D_MODEL // tn,),
        in_specs=[pl.BlockSpec((rows, D_MODEL), lambda j: (0, 0)),
                  pl.BlockSpec((D_MODEL, tn), lambda j: (0, j)),
                  pl.BlockSpec((1, tn), lambda j: (0, j))],
        out_specs=pl.BlockSpec((rows, tn), lambda j: (0, j)),
        out_shape=jax.ShapeDtypeStruct((rows, 6 * D_MODEL), F32),
        compiler_params=_cparams(("arbitrary",)),
        name="modulation",
    )(c_all, w_ada, b_ada.reshape(1, -1))


def _rope_tables(pos):
    posf = pos.astype(F32)[:, None]
    lane = jnp.arange(LANES)
    half = ROPE_DIM // 2
    inv = jnp.exp(-math.log(ROPE_THETA) * jnp.arange(half, dtype=F32) * (2.0 / ROPE_DIM))
    ang = posf * inv[None, :]
    cos, sin = jnp.cos(ang), jnp.sin(ang)
    in_head = lane % HEAD_DIM
    f = in_head % half
    att_c = jnp.where(in_head < ROPE_DIM, cos[:, f], 1.0)
    att_lo = jnp.where(in_head < half, -sin[:, f], 0.0)
    att_hi = jnp.where((in_head >= half) & (in_head < ROPE_DIM), sin[:, f], 0.0)
    rhalf = RET_QK_DIM // 2
    rinv = jnp.exp(-math.log(RET_ROPE_THETA) * jnp.arange(rhalf, dtype=F32) * (2.0 / RET_QK_DIM))
    rang = posf * rinv[None, :]
    rcos, rsin = jnp.cos(rang), jnp.sin(rang)
    ret_c = jnp.concatenate([rcos, rcos], axis=-1)
    ret_s = jnp.concatenate([-rsin, rsin], axis=-1)
    return att_c, att_lo, att_hi, ret_c, ret_s


def _inproj_kernel(x_ref, shift_ref, scale_ref, w_ref, ac_ref, alo_ref, ahi_ref, rc_ref, rs_ref, *refs,
                   tm, dils, win_blocks, win_first):
    dense_ref = refs[0]
    qkv_refs = refs[1:1 + N_GROUPS]
    win_refs = refs[1 + N_GROUPS:-1]
    stage_ref = refs[-1]
    i = pl.program_id(1)
    h = (x_ref[0] * (1.0 + scale_ref[0]) + shift_ref[0]).astype(BF16)
    ac, alo, ahi = ac_ref[...], alo_ref[...], ahi_ref[...]
    rc, rs = rc_ref[...], rs_ref[...]

    def att_rope(a):
        parts = []
        for u in range(SEG // LANES):
            xs = a[:, u * LANES:(u + 1) * LANES]
            parts.append(xs * ac + pltpu.roll(xs, LANES - ROPE_DIM // 2, 1) * alo
                         + pltpu.roll(xs, ROPE_DIM // 2, 1) * ahi)
        return jnp.concatenate(parts, axis=1)

    def ret_rope(a):
        parts = []
        for u in range(SEG // LANES):
            xs = a[:, u * LANES:(u + 1) * LANES]
            parts.append(xs * rc + pltpu.roll(xs, RET_QK_DIM // 2, 1) * rs)
        return jnp.concatenate(parts, axis=1)

    def emit_group(g, c, a):
        ref, d = qkv_refs[g], dils[g]
        cols = slice(c * SEG, (c + 1) * SEG)
        if d == 1:
            ref[0, 0, :, cols] = a.astype(ref.dtype)
        else:
            for u in range(SEG // LANES):
                stage_ref[u] = a[:, u * LANES:(u + 1) * LANES]
            for r in range(d):
                for u in range(SEG // LANES):
                    ref[0, r, :, c * SEG + u * LANES:c * SEG + (u + 1) * LANES] = (
                        stage_ref[u, pl.ds(r, tm // d, stride=d), :].astype(ref.dtype))

    def emit_window(g, c, a):
        if not win_refs:
            return
        wb = win_blocks[g]

        @pl.when(i >= win_first[g])
        def _():
            win_refs[2 * g + c][0] = a[tm - wb:, :].T

    for seg in range(N_SEG):
        c0 = seg * SEG
        a = jnp.dot(h, w_ref[:, c0:c0 + SEG], preferred_element_type=F32)
        if c0 >= OFF_QA:
            g = (c0 - OFF_QA) // SEG % N_GROUPS
            if c0 < OFF_KA:
                emit_group(g, 0, att_rope(a) * ATTN_SCALE)
            elif c0 < OFF_VA:
                a = att_rope(a)
                emit_group(g, 1, a)
                emit_window(g, 0, a)
            else:
                emit_group(g, 2, a)
                emit_window(g, 1, a)
            continue
        if c0 < OFF_RK:
            a = ret_rope(a)
        elif c0 < OFF_RV:
            a = ret_rope(a) * (RET_QK_DIM ** -0.5)
        elif c0 < OFF_RG:
            pass
        elif c0 < OFF_GA:
            a = a * _sigmoid(a)
        else:
            a = _sigmoid(a)
        dense_ref[0, :, c0:c0 + SEG] = a.astype(dense_ref.dtype)


def _in_projection(x, mod, w_in, tables, *, tm, dils, win_rows, out_dtype):
    G, R, _ = x.shape
    n_tiles = R // tm
    mod_blk = tm if mod.shape[1] == R else 1
    tab_blk = tm if tables[0].shape[0] == R else 1

    def mod_spec(col):
        if mod_blk == 1:
            return pl.BlockSpec((1, 1, D_MODEL), lambda b, i: (b, 0, col))
        return pl.BlockSpec((1, tm, D_MODEL), lambda b, i: (b, i, col))

    if tab_blk == 1:
        tab_spec = pl.BlockSpec((1, LANES), lambda b, i: (0, 0))
    else:
        tab_spec = pl.BlockSpec((tm, LANES), lambda b, i: (i, 0))

    out_shapes = [jax.ShapeDtypeStruct((G, R, DENSE_WIDTH), out_dtype)]
    out_specs = [pl.BlockSpec((1, tm, DENSE_WIDTH), lambda b, i: (b, i, 0))]
    for d in dils:
        out_shapes.append(jax.ShapeDtypeStruct((G, d, R // d, ATTN_WIDTH), out_dtype))
        out_specs.append(pl.BlockSpec((1, d, tm // d, ATTN_WIDTH), lambda b, i: (b, 0, i, 0)))
    win_blocks = win_first = ()
    if win_rows:
        win_blocks = tuple(min(w, tm) for w in win_rows)
        win_first = tuple(n_tiles - w // wb for w, wb in zip(win_rows, win_blocks))
        for g in range(N_GROUPS):
            first = win_first[g]
            for _ in range(2):
                out_shapes.append(jax.ShapeDtypeStruct((G, GROUP_WIDTH, win_rows[g]), F32))
                out_specs.append(pl.BlockSpec((1, GROUP_WIDTH, win_blocks[g]),
                                              lambda b, i, first=first: (b, 0, jnp.maximum(i - first, 0))))

    kern = functools.partial(_inproj_kernel, tm=tm, dils=dils, win_blocks=win_blocks, win_first=win_first)
    return pl.pallas_call(
        kern,
        grid=(G, n_tiles),
        in_specs=[pl.BlockSpec((1, tm, D_MODEL), lambda b, i: (b, i, 0)),
                  mod_spec(0), mod_spec(1),
                  _resident((D_MODEL, IN_WIDTH))] + [tab_spec] * 5,
        out_specs=out_specs,
        out_shape=out_shapes,
        scratch_shapes=[pltpu.VMEM((SEG // LANES, tm, LANES), F32)],
        compiler_params=_cparams(("arbitrary", "arbitrary")),
        name="in_projection",
    )(x, mod, mod, w_in, *tables)


def _attn_kernel(q_ref, k_ref, v_ref, kp_ref, vp_ref, o_ref, lse_ref, *, tl):
    t = pl.program_id(2)
    row = lax.broadcasted_iota(jnp.int32, (ATTN_BAND, ATTN_BAND), 0)
    col = lax.broadcasted_iota(jnp.int32, (ATTN_BAND, ATTN_BAND), 1)
    lane = lax.broadcasted_iota(jnp.int32, (1, LANES), 1)
    cur_ok = col <= row

    def block(q, k_prev, v_prev, k_cur, v_cur, prev_shift):
        prev_ok = col >= row + prev_shift
        mask = jnp.concatenate([prev_ok, cur_ok], axis=1)
        o_parts, l_parts = [], []
        for p in range(GROUP_WIDTH // LANES):
            sl = slice(p * LANES, (p + 1) * LANES)
            kk = jnp.concatenate([k_prev[:, sl], k_cur[:, sl]], axis=0)
            vv = jnp.concatenate([v_prev[:, sl], v_cur[:, sl]], axis=0)
            qp = q[:, sl]
            o_pair = jnp.zeros((ATTN_BAND, LANES), F32)
            l_pair = jnp.zeros((ATTN_BAND, LANES), F32)
            for hh in range(LANES // HEAD_DIM):
                head = (lane >= hh * HEAD_DIM) & (lane < (hh + 1) * HEAD_DIM)
                qm = jnp.where(head, qp, jnp.zeros_like(qp))
                s = lax.dot_general(qm, kk, (((1,), (1,)), ((), ())), preferred_element_type=F32)
                s = jnp.where(mask, s, NEG_BIG)
                m = jnp.max(s, axis=1, keepdims=True)
                e = jnp.exp(s - m)
                l = jnp.sum(e, axis=1, keepdims=True)
                pn = (e / l).astype(BF16)
                oh = jnp.dot(pn, vv, preferred_element_type=F32)
                o_pair = jnp.where(head, oh, o_pair)
                l_pair = jnp.where(head, m + jnp.log(l), l_pair)
            o_parts.append(o_pair)
            l_parts.append(l_pair)
        return jnp.concatenate(o_parts, axis=1), jnp.concatenate(l_parts, axis=1)

    def emit(j0, o, lse):
        o_ref[0, 0, pl.ds(j0, ATTN_BAND), :] = o.astype(o_ref.dtype)
        lse_ref[0, 0, pl.ds(j0, ATTN_BAND), :] = lse

    first = jnp.where(t > 0, 0, ATTN_BAND)
    o, lse = block(q_ref[0, 0, 0:ATTN_BAND, :], kp_ref[0, 0], vp_ref[0, 0],
                   k_ref[0, 0, 0:ATTN_BAND, :], v_ref[0, 0, 0:ATTN_BAND, :], first)
    emit(0, o, lse)

    def body(j, carry):
        j0 = pl.multiple_of(j * ATTN_BAND, ATTN_BAND)
        jp = pl.multiple_of((j - 1) * ATTN_BAND, ATTN_BAND)
        o, lse = block(q_ref[0, 0, pl.ds(j0, ATTN_BAND), :],
                       k_ref[0, 0, pl.ds(jp, ATTN_BAND), :], v_ref[0, 0, pl.ds(jp, ATTN_BAND), :],
                       k_ref[0, 0, pl.ds(j0, ATTN_BAND), :], v_ref[0, 0, pl.ds(j0, ATTN_BAND), :], 0)
        emit(j0, o, lse)
        return carry

    if tl > ATTN_BAND:
        lax.fori_loop(1, tl // ATTN_BAND, body, 0)


def _prompt_attention(qkv, g, *, tl):
    B, dil, L, _ = qkv.shape
    sub = tl // ATTN_BAND

    def cur(c):
        return pl.BlockSpec((1, 1, tl, SEG), lambda b, r, t: (b, r, t, c))

    def prev(c):
        return pl.BlockSpec((1, 1, ATTN_BAND, SEG), lambda b, r, t: (b, r, jnp.maximum(t * sub - 1, 0), c))

    out_spec = pl.BlockSpec((1, 1, tl, SEG), lambda b, r, t: (b, r, t, 0))
    return pl.pallas_call(
        functools.partial(_attn_kernel, tl=tl),
        grid=(B, dil, L // tl),
        in_specs=[cur(0), cur(1), cur(2), prev(1), prev(2)],
        out_specs=[out_spec, out_spec],
        out_shape=[jax.ShapeDtypeStruct((B, dil, L, SEG), BF16),
                   jax.ShapeDtypeStruct((B, dil, L, SEG), F32)],
        compiler_params=_cparams(("arbitrary", "arbitrary", "arbitrary")),
        name="prompt_attention_g%d" % g,
    )(qkv, qkv, qkv, qkv, qkv)


def _retention_tables():
    C = RET_CHUNK
    lg = jnp.log(1.0 - 2.0 ** (-5.0 - jnp.arange(RET_HEADS, dtype=F32)))
    idx = jnp.arange(C, dtype=F32)
    diff = idx[:, None] - idx[None, :]
    decay_in = jnp.where(diff >= 0, jnp.exp(lg[:, None, None] * jnp.maximum(diff, 0.0)), 0.0)
    decay_q = jnp.exp(lg[:, None] * (idx[None, :] + 1.0))[:, :, None]
    decay_k = jnp.exp(lg[:, None] * (C - 1.0 - idx[None, :]))[:, :, None]
    return decay_in, decay_q, decay_k


def _retention_kernel(qk_ref, v_ref, g_ref, din_ref, dq_ref, dk_ref, o_ref, state_ref, r_scr, *, tr):
    t = pl.program_id(1)

    @pl.when(t == 0)
    def _():
        r_scr[...] = jnp.zeros_like(r_scr)

    for c in range(tr // RET_CHUNK):
        rows = slice(c * RET_CHUNK, (c + 1) * RET_CHUNK)
        for h in range(RET_HEADS):
            q = qk_ref[0, rows, OFF_RQ + h * RET_QK_DIM:OFF_RQ + (h + 1) * RET_QK_DIM]
            k = qk_ref[0, rows, OFF_RK + h * RET_QK_DIM:OFF_RK + (h + 1) * RET_QK_DIM]
            v = v_ref[0, rows, h * RET_V_DIM:(h + 1) * RET_V_DIM]
            r_old = r_scr[h]
            s = lax.dot_general(q, k, (((1,), (1,)), ((), ())), preferred_element_type=F32) * din_ref[h]
            o = (jnp.dot(s.astype(BF16), v, preferred_element_type=F32)
                 + jnp.dot(q, r_old.astype(BF16), preferred_element_type=F32) * dq_ref[h])
            kd = (k.astype(F32) * dk_ref[h]).astype(BF16)
            upd = lax.dot_general(kd, v, (((0,), (0,)), ((), ())), preferred_element_type=F32)
            r_scr[h] = r_old * (_ret_gamma(h) ** RET_CHUNK) + upd
            mu = jnp.mean(o, axis=1, keepdims=True)
            d = o - mu
            var = jnp.mean(d * d, axis=1, keepdims=True)
            gn = d * lax.rsqrt(var + GN_EPS)
            gate = g_ref[0, rows, h * RET_V_DIM:(h + 1) * RET_V_DIM].astype(F32)
            o_ref[0, rows, h * RET_V_DIM:(h + 1) * RET_V_DIM] = (gate * gn).astype(o_ref.dtype)

    @pl.when(t == pl.num_programs(1) - 1)
    def _():
        state_ref[0] = r_scr[...]


def _prompt_retention(dense, *, tr):
    B, S, _ = dense.shape
    tabs = _retention_tables()
    blk = RET_V_WIDTH

    def colblk(c):
        return pl.BlockSpec((1, tr, blk), lambda b, t: (b, t, c))

    return pl.pallas_call(
        functools.partial(_retention_kernel, tr=tr),
        grid=(B, S // tr),
        in_specs=[colblk(0), colblk(OFF_RV // blk), colblk(OFF_RG // blk)]
                 + [pl.BlockSpec(tab.shape, lambda b, t: (0, 0, 0)) for tab in tabs],
        out_specs=[pl.BlockSpec((1, tr, RET_V_WIDTH), lambda b, t: (b, t, 0)),
                   pl.BlockSpec((1, RET_HEADS, RET_QK_DIM, RET_V_DIM), lambda b, t: (b, 0, 0, 0))],
        out_shape=[jax.ShapeDtypeStruct((B, S, RET_V_WIDTH), BF16),
                   jax.ShapeDtypeStruct((B, RET_HEADS, RET_QK_DIM, RET_V_DIM), F32)],
        scratch_shapes=[pltpu.VMEM((RET_HEADS, RET_QK_DIM, RET_V_DIM), F32)],
        compiler_params=_cparams(("arbitrary", "arbitrary")),
        name="prompt_retention",
    )(dense, dense, dense, *tabs)


def _layer_norm(u, g, b):
    mu = jnp.mean(u, axis=1, keepdims=True)
    d = u - mu
    var = jnp.mean(d * d, axis=1, keepdims=True)
    return d * lax.rsqrt(var + LN_EPS) * g + b


def _merge_ffn_kernel(x_ref, ga_ref, gb_ref, ret_ref, o0_ref, o1_ref, o2_ref, l0_ref, l1_ref, l2_ref,
                      gate1_ref, shift2_ref, scale2_ref, gate2_ref,
                      watt_ref, wret_ref, wo_ref, wfi_ref, wfo_ref,
                      ln1g_ref, ln1b_ref, ln2g_ref, ln2b_ref, y_ref, act_scr, *stage_refs, tm, dils):
    def natural(ref, g, stage_ref):
        d = dils[g]
        if d == 1:
            return ref[0, 0].astype(F32)
        for r in range(d):
            for u in range(GROUP_WIDTH // LANES):
                stage_ref[u, pl.ds(r, tm // d, stride=d), :] = ref[0, r, :, u * LANES:(u + 1) * LANES].astype(F32)
        return jnp.concatenate([stage_ref[u] for u in range(GROUP_WIDTH // LANES)], axis=1)

    lses = [natural(ref, g, stage_refs[g]) for g, ref in enumerate((l0_ref, l1_ref, l2_ref))]
    outs = [natural(ref, g, stage_refs[N_GROUPS + g]) for g, ref in enumerate((o0_ref, o1_ref, o2_ref))]
    m = jnp.maximum(jnp.maximum(lses[0], lses[1]), lses[2])
    es = [jnp.exp(l - m) for l in lses]
    att = (es[0] * outs[0] + es[1] * outs[1] + es[2] * outs[2]) / (es[0] + es[1] + es[2])
    y_att = jnp.dot(att.astype(BF16), watt_ref[...], preferred_element_type=F32)
    y_ret = jnp.dot(ret_ref[0].astype(BF16), wret_ref[...], preferred_element_type=F32)
    mix = (ga_ref[0].astype(F32) * y_att + gb_ref[0].astype(F32) * y_ret).astype(BF16)
    mixed = jnp.dot(mix, wo_ref[...], preferred_element_type=F32)
    x1 = _layer_norm(DEEPNORM_ALPHA * x_ref[0] + gate1_ref[0] * mixed, ln1g_ref[...], ln1b_ref[...])
    h2 = (x1 * (1.0 + scale2_ref[0]) + shift2_ref[0]).astype(BF16)
    for c0 in range(0, D_FF, SEG):
        fg = jnp.dot(h2, wfi_ref[:, c0:c0 + SEG], preferred_element_type=F32)
        fu = jnp.dot(h2, wfi_ref[:, D_FF + c0:D_FF + c0 + SEG], preferred_element_type=F32)
        act_scr[:, c0:c0 + SEG] = (fg * _sigmoid(fg) * fu).astype(BF16)
    ff = jnp.dot(act_scr[...], wfo_ref[...], preferred_element_type=F32)
    y_ref[0] = _layer_norm(DEEPNORM_ALPHA * x1 + gate2_ref[0] * ff, ln2g_ref[...], ln2b_ref[...])


def _merge_ffn(x, dense, ret, o_groups, lse_groups, mod, weights, ln, *, tm, dils):
    G, R, _ = x.shape
    mod_rows = mod.shape[1]

    def mod_spec(col):
        if mod_rows == 1:
            return pl.BlockSpec((1, 1, D_MODEL), lambda b, i: (b, 0, col))
        return pl.BlockSpec((1, tm, D_MODEL), lambda b, i: (b, i, col))

    def rows(width, col=0):
        return pl.BlockSpec((1, tm, width), lambda b, i: (b, i, col))

    group_specs = [pl.BlockSpec((1, d, tm // d, GROUP_WIDTH), lambda b, i: (b, 0, i, 0)) for d in dils]
    vec = pl.BlockSpec((1, D_MODEL), lambda b, i: (0, 0))
    return pl.pallas_call(
        functools.partial(_merge_ffn_kernel, tm=tm, dils=dils),
        grid=(G, R // tm),
        in_specs=[rows(D_MODEL), rows(D_MODEL, OFF_GA // D_MODEL), rows(D_MODEL, OFF_GB // D_MODEL),
                  rows(RET_V_WIDTH)] + group_specs + group_specs
                 + [mod_spec(2), mod_spec(3), mod_spec(4), mod_spec(5)]
                 + [_resident(w.shape) for w in weights] + [vec] * 4,
        out_specs=rows(D_MODEL),
        out_shape=jax.ShapeDtypeStruct((G, R, D_MODEL), F32),
        scratch_shapes=[pltpu.VMEM((tm, D_FF), BF16)] + [pltpu.VMEM((GROUP_WIDTH // LANES, tm, LANES), F32)] * (2 * N_GROUPS),
        compiler_params=_cparams(("arbitrary", "arbitrary")),
        name="merge_ffn",
    )(x, dense, dense, ret, *o_groups, *lse_groups, mod, mod, mod, mod, *weights, *ln)


def _column(row):
    return jnp.broadcast_to(row, (LANES, LANES)).T


def _sample_mixer_kernel(dense_ref, qkv0_ref, qkv1_ref, qkv2_ref, ck0_ref, cv0_ref, ck1_ref, cv1_ref,
                         ck2_ref, cv2_ref, state_ref, o_ref, lse_ref, ret_ref, nstate_ref,
                         nk0_ref, nv0_ref, nk1_ref, nv1_ref, nk2_ref, nv2_ref):
    pad = 16
    row_h = lax.broadcasted_iota(jnp.int32, (pad, GROUP_WIDTH), 0)
    lane_h = lax.broadcasted_iota(jnp.int32, (pad, GROUP_WIDTH), 1) // HEAD_DIM
    own = row_h == lane_h
    lane = lax.broadcasted_iota(jnp.int32, (GROUP_WIDTH, LANES), 1)

    def rolled(c_ref, n_ref, new_row):
        w = c_ref.shape[2]
        new_col = jnp.concatenate([_column(new_row[:, u * LANES:(u + 1) * LANES])
                                   for u in range(GROUP_WIDTH // LANES)], axis=0)
        nblk = w // LANES
        cur = pltpu.roll(c_ref[0, :, 0:LANES], LANES - 1, 1)
        for j in range(nblk):
            if j + 1 < nblk:
                nxt = pltpu.roll(c_ref[0, :, (j + 1) * LANES:(j + 2) * LANES], LANES - 1, 1)
            else:
                nxt = new_col
            n_ref[0, :, j * LANES:(j + 1) * LANES] = jnp.where(lane < LANES - 1, cur, nxt)
            cur = nxt

    groups = ((qkv0_ref, ck0_ref, cv0_ref, nk0_ref, nv0_ref),
              (qkv1_ref, ck1_ref, cv1_ref, nk1_ref, nv1_ref),
              (qkv2_ref, ck2_ref, cv2_ref, nk2_ref, nv2_ref))
    for g, (qkv_ref, ck_ref, cv_ref, nk_ref, nv_ref) in enumerate(groups):
        dil = DILATED_PATTERNS[g][1]
        w = ck_ref.shape[2]
        q = qkv_ref[0, :, 0:SEG]
        kn = qkv_ref[0, :, SEG:2 * SEG]
        vn = qkv_ref[0, :, 2 * SEG:3 * SEG]
        qsel = jnp.where(own, jnp.broadcast_to(q, (pad, GROUP_WIDTH)), 0.0)
        s = jnp.dot(qsel.astype(BF16), ck_ref[0].astype(BF16), preferred_element_type=F32)
        pos = lax.broadcasted_iota(jnp.int32, (pad, w), 1)
        s = jnp.where((pos & (dil - 1)) == 0, s, NEG_BIG)
        sn = jnp.sum(qsel * kn, axis=1, keepdims=True)
        m = jnp.maximum(jnp.max(s, axis=1, keepdims=True), sn)
        e = jnp.exp(s - m)
        en = jnp.exp(sn - m)
        l = jnp.sum(e, axis=1, keepdims=True) + en
        o_all = lax.dot_general((e / l).astype(BF16), cv_ref[0].astype(BF16), (((1,), (1,)), ((), ())),
                                preferred_element_type=F32) + (en / l) * vn
        o_ref[0, :, g * SEG:(g + 1) * SEG] = jnp.sum(jnp.where(own, o_all, 0.0), axis=0, keepdims=True)
        lse_ref[0, :, g * SEG:(g + 1) * SEG] = jnp.sum(jnp.where(own, m + jnp.log(l), 0.0), axis=0, keepdims=True)
        rolled(ck_ref, nk_ref, kn)
        rolled(cv_ref, nv_ref, vn)

    for h in range(RET_HEADS):
        gamma = _ret_gamma(h)
        q = dense_ref[0, :, OFF_RQ + h * RET_QK_DIM:OFF_RQ + (h + 1) * RET_QK_DIM]
        k = dense_ref[0, :, OFF_RK + h * RET_QK_DIM:OFF_RK + (h + 1) * RET_QK_DIM]
        v = dense_ref[0, :, OFF_RV + h * RET_V_DIM:OFF_RV + (h + 1) * RET_V_DIM]
        gate = dense_ref[0, :, OFF_RG + h * RET_V_DIM:OFF_RG + (h + 1) * RET_V_DIM]
        r_old = state_ref[0, h]
        q_col = jnp.concatenate([_column(q)] * (RET_V_DIM // LANES), axis=1)
        k_col = jnp.concatenate([_column(k)] * (RET_V_DIM // LANES), axis=1)
        qk = jnp.sum(q * k, axis=1, keepdims=True)
        o = qk * v + jnp.sum(q_col * r_old, axis=0, keepdims=True) * gamma
        nstate_ref[0, h] = r_old * gamma + k_col * v
        mu = jnp.mean(o, axis=1, keepdims=True)
        d = o - mu
        var = jnp.mean(d * d, axis=1, keepdims=True)
        ret_ref[0, :, h * RET_V_DIM:(h + 1) * RET_V_DIM] = gate * (d * lax.rsqrt(var + GN_EPS))


def _sample_mixers(dense, qkvs, caches, state):
    Bs = dense.shape[0]
    row = lambda w: pl.BlockSpec((1, 1, w), lambda b: (b, 0, 0))
    cache_specs = [pl.BlockSpec((1,) + a.shape[1:], lambda b: (b, 0, 0)) for a in caches]
    st = pl.BlockSpec((1, RET_HEADS, RET_QK_DIM, RET_V_DIM), lambda b: (b, 0, 0, 0))
    return pl.pallas_call(
        _sample_mixer_kernel,
        grid=(Bs,),
        in_specs=[row(DENSE_WIDTH)] + [row(ATTN_WIDTH)] * N_GROUPS + cache_specs + [st],
        out_specs=[row(ATTN_WIDTH), row(ATTN_WIDTH), row(RET_V_WIDTH), st] + cache_specs,
        out_shape=[jax.ShapeDtypeStruct((Bs, 1, ATTN_WIDTH), F32),
                   jax.ShapeDtypeStruct((Bs, 1, ATTN_WIDTH), F32),
                   jax.ShapeDtypeStruct((Bs, 1, RET_V_WIDTH), F32),
                   jax.ShapeDtypeStruct(state.shape, F32)]
                  + [jax.ShapeDtypeStruct(a.shape, F32) for a in caches],
        compiler_params=_cparams(("arbitrary",)),
        name="sample_mixers",
    )(dense, *qkvs, *caches, state)


def _feature_major(a):
    _, b, w, _, _ = a.shape
    return jnp.transpose(a[0], (0, 2, 3, 1)).reshape(b, GROUP_WIDTH, w)


def _row_major(a):
    b, _, w = a.shape
    return jnp.transpose(a.reshape(b, HEADS_PER_GROUP, HEAD_DIM, w), (0, 3, 1, 2))[None]


def kernel(x_prompt, x_sample, cache_k_w128, cache_v_w128, cache_k_w512, cache_v_w512, cache_k_w2048,
           cache_v_w2048, state_retention, c_prompt, c_sample, w_ada, b_ada, w_in, w_att_out, w_ret_out,
           w_o, ln1_g, ln1_b, w_ffn_in, w_ffn_out, ln2_g, ln2_b):
    assert w_in.shape[0] == 1 and x_sample.shape[1] == 1
    B, S, D = x_prompt.shape
    Bs = x_sample.shape[0]
    dils = tuple(d for _, d in DILATED_PATTERNS)

    w_in_b = jnp.concatenate([w_in[0][:, W_IN_ROTATE:], w_in[0][:, :W_IN_ROTATE]], axis=1).astype(BF16)
    weights = tuple(w[0].astype(BF16) for w in (w_att_out, w_ret_out, w_o, w_ffn_in, w_ffn_out))
    ln = (ln1_g, ln1_b, ln2_g, ln2_b)

    mod = _modulation(jnp.concatenate([c_prompt, c_sample], axis=0), w_ada[0], b_ada[0])
    mod_p = mod[:B].reshape(B, 1, 6 * D)
    mod_s = mod[B:].reshape(1, Bs, 6 * D)

    tabs_p = _rope_tables(jnp.arange(S, dtype=jnp.int32))
    win_rows = tuple(min(w, S) for w, _ in DILATED_PATTERNS)
    dense_p, *rest = _in_projection(x_prompt, mod_p, w_in_b, tabs_p, tm=256, dils=dils, win_rows=win_rows,
                                    out_dtype=BF16)
    qkv_p, wins_p = rest[:N_GROUPS], rest[N_GROUPS:]
    o_p, lse_p = [], []
    for g, d in enumerate(dils):
        o, lse = _prompt_attention(qkv_p[g], g, tl=min(S // d, 1024))
        o_p.append(o)
        lse_p.append(lse)
    ret_p, state_p = _prompt_retention(dense_p, tr=512)
    y_p = _merge_ffn(x_prompt, dense_p, ret_p, o_p, lse_p, mod_p, weights, ln, tm=256, dils=dils)

    xs = x_sample.reshape(1, Bs, D)
    tabs_s = _rope_tables(PAST_LEN + jnp.arange(1, dtype=jnp.int32))
    unit = (1,) * N_GROUPS
    dense_s, *qkv_s = _in_projection(xs, mod_s, w_in_b, tabs_s, tm=Bs, dils=unit, win_rows=None, out_dtype=F32)
    caches = [_feature_major(c) for c in (cache_k_w128, cache_v_w128, cache_k_w512, cache_v_w512,
                                          cache_k_w2048, cache_v_w2048)]
    o_s, lse_s, ret_s, state_s, *rolled = _sample_mixers(
        dense_s.reshape(Bs, 1, DENSE_WIDTH), [a.reshape(Bs, 1, ATTN_WIDTH) for a in qkv_s], caches,
        state_retention[0])
    o_s = o_s.reshape(1, 1, Bs, ATTN_WIDTH)
    lse_s = lse_s.reshape(1, 1, Bs, ATTN_WIDTH)
    o_sg = [o_s[..., g * SEG:(g + 1) * SEG] for g in range(N_GROUPS)]
    lse_sg = [lse_s[..., g * SEG:(g + 1) * SEG] for g in range(N_GROUPS)]
    y_s = _merge_ffn(xs, dense_s, ret_s.reshape(1, Bs, RET_V_WIDTH), o_sg, lse_sg, mod_s, weights, ln,
                     tm=Bs, dils=unit)

    return (y_p, y_s.reshape(Bs, 1, D),
            *[_row_major(w) for w in wins_p], state_p[None],
            *[_row_major(r) for r in rolled], state_s[None])
```

```python
import functools
import math

import jax
import jax.numpy as jnp
from jax import lax
from jax.experimental import pallas as pl
from jax.experimental.pallas import tpu as pltpu

F32 = jnp.float32
BF16 = jnp.bfloat16

D_MODEL = 1024
PAST_LEN = 16384
DILATED_PATTERNS = ((128, 1), (512, 4), (2048, 16))
N_GROUPS = 3
HEADS_PER_GROUP = 4
HEAD_DIM = 64
GROUP_WIDTH = HEADS_PER_GROUP * HEAD_DIM
ATTN_WIDTH = N_GROUPS * GROUP_WIDTH
ATTN_BAND = 128
ATTN_SCALE = HEAD_DIM ** -0.5
ROPE_DIM = HEAD_DIM // 4
ROPE_THETA = 500000.0
RET_HEADS = 4
RET_QK_DIM = 128
RET_V_DIM = 256
RET_QK_WIDTH = RET_HEADS * RET_QK_DIM
RET_V_WIDTH = RET_HEADS * RET_V_DIM
RET_CHUNK = 128
RET_ROPE_THETA = 10000.0
IN_WIDTH = 3 * ATTN_WIDTH + 2 * RET_QK_WIDTH + 2 * RET_V_WIDTH + 2 * D_MODEL
D_FF = 2816
DEEPNORM_ALPHA = 2.0 ** 0.25
LN_EPS = 1e-5
GN_EPS = 1e-6

LANES = 128
SEG = 256
N_SEG = IN_WIDTH // SEG
OFF_RQ, OFF_RK, OFF_RV, OFF_RG, OFF_GA, OFF_GB = 0, 512, 1024, 2048, 3072, 4096
OFF_QA, OFF_KA, OFF_VA = 5120, 5888, 6656
DENSE_WIDTH = OFF_QA
W_IN_ROTATE = 3 * ATTN_WIDTH
NEG_BIG = -1e30
VMEM_LIMIT = 56 * 1024 * 1024


def _cparams(sem):
    return pltpu.CompilerParams(dimension_semantics=sem, vmem_limit_bytes=VMEM_LIMIT)


def _resident(shape):
    nd = len(shape)
    return pl.BlockSpec(shape, lambda *_: (0,) * nd, pipeline_mode=pl.Buffered(1))


def _sigmoid(x):
    return 1.0 / (1.0 + jnp.exp(-x))


def _ret_gamma(h):
    return 1.0 - 2.0 ** (-5.0 - h)


def _ada_kernel(c_ref, w_ref, b_ref, o_ref):
    c = c_ref[...]
    a = (c * _sigmoid(c)).astype(BF16)
    o_ref[...] = jnp.dot(a, w_ref[...].astype(BF16), preferred_element_type=F32) + b_ref[...]


def _modulation(c_all, w_ada, b_ada):
    rows = c_all.shape[0]
    tn = 1024
    return pl.pallas_call(
        _ada_kernel,
        grid=(6 * D_MODEL // tn,),
        in_specs=[pl.BlockSpec((rows, D_MODEL), lambda j: (0, 0)),
                  pl.BlockSpec((D_MODEL, tn), lambda j: (0, j)),
                  pl.BlockSpec((1, tn), lambda j: (0, j))],
        out_specs=pl.BlockSpec((rows, tn), lambda j: (0, j)),
        out_shape=jax.ShapeDtypeStruct((rows, 6 * D_MODEL), F32),
        compiler_params=_cparams(("arbitrary",)),
        name="modulation",
    )(c_all, w_ada, b_ada.reshape(1, -1))


PAIR_SLOT = HEAD_DIM // 2


def _pair_source():
    src = []
    for half_dims in ((0, 16), (8, 40)):
        for h in range(HEADS_PER_GROUP):
            rot, rest = half_dims
            dims = list(range(rot, rot + ROPE_DIM // 2)) + list(range(rest, rest + PAIR_SLOT - ROPE_DIM // 2))
            src += [h * HEAD_DIM + d for d in dims]
    return src


def _rope_tables(pos, paired=False):
    posf = pos.astype(F32)[:, None]
    lane = jnp.arange(LANES)
    half = ROPE_DIM // 2
    inv = jnp.exp(-math.log(ROPE_THETA) * jnp.arange(half, dtype=F32) * (2.0 / ROPE_DIM))
    ang = posf * inv[None, :]
    cos, sin = jnp.cos(ang), jnp.sin(ang)
    if paired:
        in_slot = lane % PAIR_SLOT
        f = in_slot % half
        att_c = jnp.where(in_slot < half, cos[:, f], 1.0)
        att_lo = jnp.where(in_slot < half, sin[:, f], 0.0)
        att_hi = att_lo
    else:
        in_head = lane % HEAD_DIM
        f = in_head % half
        att_c = jnp.where(in_head < ROPE_DIM, cos[:, f], 1.0)
        att_lo = jnp.where(in_head < half, -sin[:, f], 0.0)
        att_hi = jnp.where((in_head >= half) & (in_head < ROPE_DIM), sin[:, f], 0.0)
    rhalf = RET_QK_DIM // 2
    rinv = jnp.exp(-math.log(RET_ROPE_THETA) * jnp.arange(rhalf, dtype=F32) * (2.0 / RET_QK_DIM))
    rang = posf * rinv[None, :]
    rcos, rsin = jnp.cos(rang), jnp.sin(rang)
    ret_c = jnp.concatenate([rcos, rcos], axis=-1)
    ret_s = jnp.concatenate([-rsin, rsin], axis=-1)
    return att_c, att_lo, att_hi, ret_c, ret_s


def _inproj_kernel(x_ref, shift_ref, scale_ref, w_ref, wqk_ref, *refs, tm, dils, paired, win_blocks, win_first):
    n_tab = 3 * (N_GROUPS + 1) + 2
    tabs = [r[...] for r in refs[:n_tab]]
    att_tabs = [tabs[3 * g:3 * g + 3] for g in range(N_GROUPS + 1)]
    rc, rs = tabs[-2:]
    refs = refs[n_tab:]
    n_perm = sum(d > 1 for d in dils)
    dense_ref = refs[0]
    qkv_refs = refs[1:1 + N_GROUPS]
    n_out = len(refs) - (1 + n_perm if n_perm else 0)
    win_refs = refs[1 + N_GROUPS:n_out]
    i = pl.program_id(1)
    hf = x_ref[0] * (1.0 + scale_ref[0]) + shift_ref[0]
    h = hf.astype(BF16)

    h_class = {}
    if n_perm:
        stage_ref = refs[n_out]
        for u in range(D_MODEL // LANES):
            stage_ref[u] = hf[:, u * LANES:(u + 1) * LANES]
        for g, hp_ref in zip([g for g in range(N_GROUPS) if dils[g] > 1], refs[n_out + 1:]):
            d = dils[g]
            n = tm // d
            for r in range(d):
                for u in range(D_MODEL // LANES):
                    hp_ref[r * n:(r + 1) * n, u * LANES:(u + 1) * LANES] = (
                        stage_ref[u, pl.ds(r, n, stride=d), :].astype(BF16))
            h_class[g] = hp_ref

    def att_rope(a, tab, rows=None):
        ac, alo, ahi = [t if rows is None or t.shape[0] == 1 else t[rows:, :] for t in tab]
        if paired:
            lo, hi = a[:, :LANES], a[:, LANES:]
            return jnp.concatenate([lo * ac - hi * alo, hi * ac + lo * alo], axis=1)
        parts = []
        for u in range(SEG // LANES):
            xs = a[:, u * LANES:(u + 1) * LANES]
            parts.append(xs * ac + pltpu.roll(xs, LANES - ROPE_DIM // 2, 1) * alo
                         + pltpu.roll(xs, ROPE_DIM // 2, 1) * ahi)
        return jnp.concatenate(parts, axis=1)

    def ret_rope(a):
        parts = []
        for u in range(SEG // LANES):
            xs = a[:, u * LANES:(u + 1) * LANES]
            parts.append(xs * rc + pltpu.roll(xs, RET_QK_DIM // 2, 1) * rs)
        return jnp.concatenate(parts, axis=1)

    def emit_group(g, c, a):
        ref, d = qkv_refs[g], dils[g]
        n = tm // d
        for r in range(d):
            ref[0, r, :, c * SEG:(c + 1) * SEG] = a[r * n:(r + 1) * n, :].astype(ref.dtype)

    def wk_cols(g):
        return wqk_ref[:, ATTN_WIDTH + g * SEG:ATTN_WIDTH + (g + 1) * SEG]

    def wv_cols(g):
        return w_ref[:, DENSE_WIDTH + g * SEG:DENSE_WIDTH + (g + 1) * SEG]

    def emit_windows(g):
        wb = win_blocks[g]
        hw = h[tm - wb:, :]
        at = att_rope(jnp.dot(hw, wk_cols(g), preferred_element_type=F32), att_tabs[N_GROUPS], tm - wb).T
        if paired:
            dst = {f: lane for lane, f in enumerate(_pair_source())}
            at = jnp.concatenate([at[dst[f]:dst[f] + 8, :] for f in range(0, GROUP_WIDTH, 8)], axis=0)
        win_refs[2 * g][0] = at
        win_refs[2 * g + 1][0] = jnp.dot(hw, wv_cols(g), preferred_element_type=F32).T

    for seg in range(DENSE_WIDTH // SEG):
        c0 = seg * SEG
        a = jnp.dot(h, w_ref[:, c0:c0 + SEG], preferred_element_type=F32)
        if c0 < OFF_RK:
            a = ret_rope(a)
        elif c0 < OFF_RV:
            a = ret_rope(a) * (RET_QK_DIM ** -0.5)
        elif c0 < OFF_RG:
            pass
        elif c0 < OFF_GA:
            a = a * _sigmoid(a)
        else:
            a = _sigmoid(a)
        dense_ref[0, :, c0:c0 + SEG] = a.astype(dense_ref.dtype)

    for g in range(N_GROUPS):
        hg = h_class[g][...] if g in h_class else h
        wq = wqk_ref[:, g * SEG:(g + 1) * SEG]
        emit_group(g, 0, att_rope(jnp.dot(hg, wq, preferred_element_type=F32), att_tabs[g]) * ATTN_SCALE)
        emit_group(g, 1, att_rope(jnp.dot(hg, wk_cols(g), preferred_element_type=F32), att_tabs[g]))
        emit_group(g, 2, jnp.dot(hg, wv_cols(g), preferred_element_type=F32))

    if win_refs:
        order = sorted(range(N_GROUPS), key=lambda g: win_first[g])

        def guarded(pos):
            if pos == len(order):
                return
            g = order[pos]

            @pl.when(i >= win_first[g])
            def _():
                emit_windows(g)
                guarded(pos + 1)

        guarded(0)


def _class_major_rows(tab, tm, d):
    if d == 1 or tab.shape[0] == 1:
        return tab
    r = tab.shape[0]
    return tab.reshape(r // tm, tm // d, d, LANES).transpose(0, 2, 1, 3).reshape(r, LANES)


def _in_projection(x, mod, w_main, w_qk, tables, *, tm, dils, paired, win_rows, out_dtype):
    G, R, _ = x.shape
    n_tiles = R // tm
    mod_blk = tm if mod.shape[1] == R else 1
    tab_blk = tm if tables[0].shape[0] == R else 1

    def mod_spec(col):
        if mod_blk == 1:
            return pl.BlockSpec((1, 1, D_MODEL), lambda b, i: (b, 0, col))
        return pl.BlockSpec((1, tm, D_MODEL), lambda b, i: (b, i, col))

    def tab_spec(tab):
        if tab.shape[0] == 1:
            return pl.BlockSpec((1, LANES), lambda b, i: (0, 0))
        return pl.BlockSpec((tm, LANES), lambda b, i: (i, 0))

    att = list(tables[:3])
    if paired:
        att[2] = jnp.zeros((1, LANES), F32)
    all_tabs = []
    for d in tuple(dils) + (1,):
        all_tabs += [_class_major_rows(t, tm, d) for t in att]
    all_tabs += list(tables[3:])

    out_shapes = [jax.ShapeDtypeStruct((G, R, DENSE_WIDTH), out_dtype)]
    out_specs = [pl.BlockSpec((1, tm, DENSE_WIDTH), lambda b, i: (b, i, 0))]
    for d in dils:
        out_shapes.append(jax.ShapeDtypeStruct((G, d, R // d, ATTN_WIDTH), out_dtype))
        out_specs.append(pl.BlockSpec((1, d, tm // d, ATTN_WIDTH), lambda b, i: (b, 0, i, 0)))
    win_blocks = win_first = ()
    if win_rows:
        win_blocks = tuple(min(w, tm) for w in win_rows)
        win_first = tuple(n_tiles - w // wb for w, wb in zip(win_rows, win_blocks))
        for g in range(N_GROUPS):
            first = win_first[g]
            for _ in range(2):
                out_shapes.append(jax.ShapeDtypeStruct((G, GROUP_WIDTH, win_rows[g]), F32))
                out_specs.append(pl.BlockSpec((1, GROUP_WIDTH, win_blocks[g]),
                                              lambda b, i, first=first: (b, 0, jnp.maximum(i - first, 0))))

    scratch = []
    if any(d > 1 for d in dils):
        scratch.append(pltpu.VMEM((D_MODEL // LANES, tm, LANES), F32))
        scratch += [pltpu.VMEM((tm, D_MODEL), BF16) for d in dils if d > 1]
    kern = functools.partial(_inproj_kernel, tm=tm, dils=dils, paired=paired, win_blocks=win_blocks,
                             win_first=win_first)
    return pl.pallas_call(
        kern,
        grid=(G, n_tiles),
        in_specs=[pl.BlockSpec((1, tm, D_MODEL), lambda b, i: (b, i, 0)),
                  mod_spec(0), mod_spec(1),
                  _resident(w_main.shape), _resident(w_qk.shape)] + [tab_spec(t) for t in all_tabs],
        out_specs=out_specs,
        out_shape=out_shapes,
        scratch_shapes=scratch,
        compiler_params=_cparams(("arbitrary", "arbitrary")),
        name="in_projection",
    )(x, mod, mod, w_main, w_qk, *all_tabs)


def _attn_kernel(q_ref, k_ref, v_ref, kp_ref, vp_ref, o_ref, lse_ref, *, tl, nc):
    t = pl.program_id(2)
    row = lax.broadcasted_iota(jnp.int32, (ATTN_BAND, ATTN_BAND), 0)
    col = lax.broadcasted_iota(jnp.int32, (ATTN_BAND, ATTN_BAND), 1)
    lane = lax.broadcasted_iota(jnp.int32, (1, GROUP_WIDTH), 1)
    cur_ok = col <= row
    qk_lanes = [(lane % LANES) // PAIR_SLOT == h for h in range(HEADS_PER_GROUP)]
    v_lanes = [lane // HEAD_DIM == h for h in range(HEADS_PER_GROUP)]

    def block(q, k_prev, v_prev, k_cur, v_cur, prev_shift):
        prev_ok = col >= row + prev_shift
        mask = jnp.concatenate([prev_ok, cur_ok], axis=1)
        mask = jnp.concatenate([mask] * HEADS_PER_GROUP, axis=0)
        kk = jnp.concatenate([k_prev, k_cur], axis=0)
        vv = jnp.concatenate([v_prev, v_cur], axis=0)
        qs = jnp.concatenate([jnp.where(sel, q, jnp.zeros_like(q)) for sel in qk_lanes], axis=0)
        s = lax.dot_general(qs, kk, (((1,), (1,)), ((), ())), preferred_element_type=F32)
        s = jnp.where(mask, s, NEG_BIG)
        m = jnp.max(s, axis=1, keepdims=True)
        e = jnp.exp(s - m)
        l = jnp.sum(e, axis=1, keepdims=True)
        oh = jnp.dot((e / l).astype(BF16), vv, preferred_element_type=F32)
        lse = m + jnp.log(l)
        o = jnp.zeros((ATTN_BAND, GROUP_WIDTH), F32)
        lo = jnp.zeros((ATTN_BAND, GROUP_WIDTH), F32)
        for h, sel in enumerate(v_lanes):
            rows = slice(h * ATTN_BAND, (h + 1) * ATTN_BAND)
            o = jnp.where(sel, oh[rows], o)
            lo = jnp.where(sel, lse[rows], lo)
        return o, lo

    first = jnp.where(t > 0, 0, ATTN_BAND)
    for c in range(nc):
        for j in range(tl // ATTN_BAND):
            j0, jp = j * ATTN_BAND, (j - 1) * ATTN_BAND
            rows = slice(j0, j0 + ATTN_BAND)
            if j == 0:
                k_prev, v_prev, shift = kp_ref[0, c], vp_ref[0, c], first
            else:
                k_prev, v_prev, shift = k_ref[0, c, jp:j0, :], v_ref[0, c, jp:j0, :], 0
            o, lse = block(q_ref[0, c, rows, :], k_prev, v_prev, k_ref[0, c, rows, :], v_ref[0, c, rows, :], shift)
            o_ref[0, c, rows, :] = o.astype(o_ref.dtype)
            lse_ref[0, c, rows, :] = lse


def _prompt_attention(qkv, g, *, tl, nc):
    B, dil, L, _ = qkv.shape
    sub = tl // ATTN_BAND

    def cur(c):
        return pl.BlockSpec((1, nc, tl, SEG), lambda b, r, t: (b, r, t, c))

    def prev(c):
        return pl.BlockSpec((1, nc, ATTN_BAND, SEG), lambda b, r, t: (b, r, jnp.maximum(t * sub - 1, 0), c))

    out_spec = pl.BlockSpec((1, nc, tl, SEG), lambda b, r, t: (b, r, t, 0))
    return pl.pallas_call(
        functools.partial(_attn_kernel, tl=tl, nc=nc),
        grid=(B, dil // nc, L // tl),
        in_specs=[cur(0), cur(1), cur(2), prev(1), prev(2)],
        out_specs=[out_spec, out_spec],
        out_shape=[jax.ShapeDtypeStruct((B, dil, L, SEG), BF16),
                   jax.ShapeDtypeStruct((B, dil, L, SEG), F32)],
        compiler_params=_cparams(("arbitrary", "arbitrary", "arbitrary")),
        name="prompt_attention_g%d" % g,
    )(qkv, qkv, qkv, qkv, qkv)


def _retention_tables():
    C = RET_CHUNK
    lg = jnp.log(1.0 - 2.0 ** (-5.0 - jnp.arange(RET_HEADS, dtype=F32)))
    idx = jnp.arange(C, dtype=F32)
    diff = idx[:, None] - idx[None, :]
    decay_in = jnp.where(diff >= 0, jnp.exp(lg[:, None, None] * jnp.maximum(diff, 0.0)), 0.0)
    decay_q = jnp.exp(lg[:, None] * (idx[None, :] + 1.0))[:, :, None]
    decay_k = jnp.exp(lg[:, None] * (C - 1.0 - idx[None, :]))[:, :, None]
    return decay_in, decay_q, decay_k


def _retention_kernel(qk_ref, v_ref, g_ref, din_ref, dq_ref, dk_ref, o_ref, state_ref, r_scr, *, tr):
    t = pl.program_id(1)

    @pl.when(t == 0)
    def _():
        r_scr[...] = jnp.zeros_like(r_scr)

    for c in range(tr // RET_CHUNK):
        rows = slice(c * RET_CHUNK, (c + 1) * RET_CHUNK)
        for h in range(RET_HEADS):
            q = qk_ref[0, rows, OFF_RQ + h * RET_QK_DIM:OFF_RQ + (h + 1) * RET_QK_DIM]
            k = qk_ref[0, rows, OFF_RK + h * RET_QK_DIM:OFF_RK + (h + 1) * RET_QK_DIM]
            v = v_ref[0, rows, h * RET_V_DIM:(h + 1) * RET_V_DIM]
            r_old = r_scr[h]
            s = lax.dot_general(q, k, (((1,), (1,)), ((), ())), preferred_element_type=F32) * din_ref[h]
            o = (jnp.dot(s.astype(BF16), v, preferred_element_type=F32)
                 + jnp.dot(q, r_old.astype(BF16), preferred_element_type=F32) * dq_ref[h])
            kd = (k.astype(F32) * dk_ref[h]).astype(BF16)
            upd = lax.dot_general(kd, v, (((0,), (0,)), ((), ())), preferred_element_type=F32)
            r_scr[h] = r_old * (_ret_gamma(h) ** RET_CHUNK) + upd
            mu = jnp.mean(o, axis=1, keepdims=True)
            d = o - mu
            var = jnp.mean(d * d, axis=1, keepdims=True)
            gn = d * lax.rsqrt(var + GN_EPS)
            gate = g_ref[0, rows, h * RET_V_DIM:(h + 1) * RET_V_DIM].astype(F32)
            o_ref[0, rows, h * RET_V_DIM:(h + 1) * RET_V_DIM] = (gate * gn).astype(o_ref.dtype)

    @pl.when(t == pl.num_programs(1) - 1)
    def _():
        state_ref[0] = r_scr[...]


def _prompt_retention(dense, *, tr):
    B, S, _ = dense.shape
    tabs = _retention_tables()
    blk = RET_V_WIDTH

    def colblk(c):
        return pl.BlockSpec((1, tr, blk), lambda b, t: (b, t, c))

    return pl.pallas_call(
        functools.partial(_retention_kernel, tr=tr),
        grid=(B, S // tr),
        in_specs=[colblk(0), colblk(OFF_RV // blk), colblk(OFF_RG // blk)]
                 + [pl.BlockSpec(tab.shape, lambda b, t: (0, 0, 0)) for tab in tabs],
        out_specs=[pl.BlockSpec((1, tr, RET_V_WIDTH), lambda b, t: (b, t, 0)),
                   pl.BlockSpec((1, RET_HEADS, RET_QK_DIM, RET_V_DIM), lambda b, t: (b, 0, 0, 0))],
        out_shape=[jax.ShapeDtypeStruct((B, S, RET_V_WIDTH), BF16),
                   jax.ShapeDtypeStruct((B, RET_HEADS, RET_QK_DIM, RET_V_DIM), F32)],
        scratch_shapes=[pltpu.VMEM((RET_HEADS, RET_QK_DIM, RET_V_DIM), F32)],
        compiler_params=_cparams(("arbitrary", "arbitrary")),
        name="prompt_retention",
    )(dense, dense, dense, *tabs)


def _layer_norm(u, g, b):
    mu = jnp.mean(u, axis=1, keepdims=True)
    d = u - mu
    var = jnp.mean(d * d, axis=1, keepdims=True)
    return d * lax.rsqrt(var + LN_EPS) * g + b


def _merge_ffn_kernel(x_ref, ga_ref, gb_ref, ret_ref, o0_ref, o1_ref, o2_ref, l0_ref, l1_ref, l2_ref,
                      gate1_ref, shift2_ref, scale2_ref, gate2_ref,
                      watt_ref, wret_ref, wo_ref, wfi_ref, wfo_ref,
                      ln1g_ref, ln1b_ref, ln2g_ref, ln2b_ref, y_ref, act_scr, *stage_refs, tm, dils):
    def natural(ref, g, stage_ref):
        d = dils[g]
        if d == 1:
            return ref[0, 0].astype(F32)
        for r in range(d):
            for u in range(GROUP_WIDTH // LANES):
                stage_ref[u, pl.ds(r, tm // d, stride=d), :] = ref[0, r, :, u * LANES:(u + 1) * LANES].astype(F32)
        return jnp.concatenate([stage_ref[u] for u in range(GROUP_WIDTH // LANES)], axis=1)

    lses = [natural(ref, g, stage_refs[g]) for g, ref in enumerate((l0_ref, l1_ref, l2_ref))]
    outs = [natural(ref, g, stage_refs[N_GROUPS + g]) for g, ref in enumerate((o0_ref, o1_ref, o2_ref))]
    m = jnp.maximum(jnp.maximum(lses[0], lses[1]), lses[2])
    es = [jnp.exp(l - m) for l in lses]
    att = (es[0] * outs[0] + es[1] * outs[1] + es[2] * outs[2]) / (es[0] + es[1] + es[2])
    y_att = jnp.dot(att.astype(BF16), watt_ref[...], preferred_element_type=F32)
    y_ret = jnp.dot(ret_ref[0].astype(BF16), wret_ref[...], preferred_element_type=F32)
    mix = (ga_ref[0].astype(F32) * y_att + gb_ref[0].astype(F32) * y_ret).astype(BF16)
    mixed = jnp.dot(mix, wo_ref[...], preferred_element_type=F32)
    x1 = _layer_norm(DEEPNORM_ALPHA * x_ref[0] + gate1_ref[0] * mixed, ln1g_ref[...], ln1b_ref[...])
    h2 = (x1 * (1.0 + scale2_ref[0]) + shift2_ref[0]).astype(BF16)
    for c0 in range(0, D_FF, SEG):
        fg = jnp.dot(h2, wfi_ref[:, c0:c0 + SEG], preferred_element_type=F32)
        fu = jnp.dot(h2, wfi_ref[:, D_FF + c0:D_FF + c0 + SEG], preferred_element_type=F32)
        act_scr[:, c0:c0 + SEG] = (fg * _sigmoid(fg) * fu).astype(BF16)
    ff = jnp.dot(act_scr[...], wfo_ref[...], preferred_element_type=F32)
    y_ref[0] = _layer_norm(DEEPNORM_ALPHA * x1 + gate2_ref[0] * ff, ln2g_ref[...], ln2b_ref[...])


def _merge_ffn(x, dense, ret, o_groups, lse_groups, mod, weights, ln, *, tm, dils):
    G, R, _ = x.shape
    mod_rows = mod.shape[1]

    def mod_spec(col):
        if mod_rows == 1:
            return pl.BlockSpec((1, 1, D_MODEL), lambda b, i: (b, 0, col))
        return pl.BlockSpec((1, tm, D_MODEL), lambda b, i: (b, i, col))

    def rows(width, col=0):
        return pl.BlockSpec((1, tm, width), lambda b, i: (b, i, col))

    group_specs = [pl.BlockSpec((1, d, tm // d, GROUP_WIDTH), lambda b, i: (b, 0, i, 0)) for d in dils]
    vec = pl.BlockSpec((1, D_MODEL), lambda b, i: (0, 0))
    return pl.pallas_call(
        functools.partial(_merge_ffn_kernel, tm=tm, dils=dils),
        grid=(G, R // tm),
        in_specs=[rows(D_MODEL), rows(D_MODEL, OFF_GA // D_MODEL), rows(D_MODEL, OFF_GB // D_MODEL),
                  rows(RET_V_WIDTH)] + group_specs + group_specs
                 + [mod_spec(2), mod_spec(3), mod_spec(4), mod_spec(5)]
                 + [_resident(w.shape) for w in weights] + [vec] * 4,
        out_specs=rows(D_MODEL),
        out_shape=jax.ShapeDtypeStruct((G, R, D_MODEL), F32),
        scratch_shapes=[pltpu.VMEM((tm, D_FF), BF16)] + [pltpu.VMEM((GROUP_WIDTH // LANES, tm, LANES), F32)] * (2 * N_GROUPS),
        compiler_params=_cparams(("arbitrary", "arbitrary")),
        name="merge_ffn",
    )(x, dense, dense, ret, *o_groups, *lse_groups, mod, mod, mod, mod, *weights, *ln)


def _column(row):
    return jnp.broadcast_to(row, (LANES, LANES)).T


def _sample_mixer_kernel(dense_ref, qkv0_ref, qkv1_ref, qkv2_ref, ck0_ref, cv0_ref, ck1_ref, cv1_ref,
                         ck2_ref, cv2_ref, state_ref, o_ref, lse_ref, ret_ref, nstate_ref,
                         nk0_ref, nv0_ref, nk1_ref, nv1_ref, nk2_ref, nv2_ref):
    pad = 16
    row_h = lax.broadcasted_iota(jnp.int32, (pad, GROUP_WIDTH), 0)
    lane_h = lax.broadcasted_iota(jnp.int32, (pad, GROUP_WIDTH), 1) // HEAD_DIM
    own = row_h == lane_h
    lane = lax.broadcasted_iota(jnp.int32, (GROUP_WIDTH, LANES), 1)

    def rolled(c_ref, n_ref, new_row):
        w = c_ref.shape[2]
        new_col = jnp.concatenate([_column(new_row[:, u * LANES:(u + 1) * LANES])
                                   for u in range(GROUP_WIDTH // LANES)], axis=0)
        nblk = w // LANES
        cur = pltpu.roll(c_ref[0, :, 0:LANES], LANES - 1, 1)
        for j in range(nblk):
            if j + 1 < nblk:
                nxt = pltpu.roll(c_ref[0, :, (j + 1) * LANES:(j + 2) * LANES], LANES - 1, 1)
            else:
                nxt = new_col
            n_ref[0, :, j * LANES:(j + 1) * LANES] = jnp.where(lane < LANES - 1, cur, nxt)
            cur = nxt

    groups = ((qkv0_ref, ck0_ref, cv0_ref, nk0_ref, nv0_ref),
              (qkv1_ref, ck1_ref, cv1_ref, nk1_ref, nv1_ref),
              (qkv2_ref, ck2_ref, cv2_ref, nk2_ref, nv2_ref))
    for g, (qkv_ref, ck_ref, cv_ref, nk_ref, nv_ref) in enumerate(groups):
        dil = DILATED_PATTERNS[g][1]
        w = ck_ref.shape[2]
        q = qkv_ref[0, :, 0:SEG]
        kn = qkv_ref[0, :, SEG:2 * SEG]
        vn = qkv_ref[0, :, 2 * SEG:3 * SEG]
        qsel = jnp.where(own, jnp.broadcast_to(q, (pad, GROUP_WIDTH)), 0.0)
        s = jnp.dot(qsel.astype(BF16), ck_ref[0].astype(BF16), preferred_element_type=F32)
        pos = lax.broadcasted_iota(jnp.int32, (pad, w), 1)
        s = jnp.where((pos & (dil - 1)) == 0, s, NEG_BIG)
        sn = jnp.sum(qsel * kn, axis=1, keepdims=True)
        m = jnp.maximum(jnp.max(s, axis=1, keepdims=True), sn)
        e = jnp.exp(s - m)
        en = jnp.exp(sn - m)
        l = jnp.sum(e, axis=1, keepdims=True) + en
        o_all = lax.dot_general((e / l).astype(BF16), cv_ref[0].astype(BF16), (((1,), (1,)), ((), ())),
                                preferred_element_type=F32) + (en / l) * vn
        o_ref[0, :, g * SEG:(g + 1) * SEG] = jnp.sum(jnp.where(own, o_all, 0.0), axis=0, keepdims=True)
        lse_ref[0, :, g * SEG:(g + 1) * SEG] = jnp.sum(jnp.where(own, m + jnp.log(l), 0.0), axis=0, keepdims=True)
        rolled(ck_ref, nk_ref, kn)
        rolled(cv_ref, nv_ref, vn)

    for h in range(RET_HEADS):
        gamma = _ret_gamma(h)
        q = dense_ref[0, :, OFF_RQ + h * RET_QK_DIM:OFF_RQ + (h + 1) * RET_QK_DIM]
        k = dense_ref[0, :, OFF_RK + h * RET_QK_DIM:OFF_RK + (h + 1) * RET_QK_DIM]
        v = dense_ref[0, :, OFF_RV + h * RET_V_DIM:OFF_RV + (h + 1) * RET_V_DIM]
        gate = dense_ref[0, :, OFF_RG + h * RET_V_DIM:OFF_RG + (h + 1) * RET_V_DIM]
        r_old = state_ref[0, h]
        q_col = jnp.concatenate([_column(q)] * (RET_V_DIM // LANES), axis=1)
        k_col = jnp.concatenate([_column(k)] * (RET_V_DIM // LANES), axis=1)
        qk = jnp.sum(q * k, axis=1, keepdims=True)
        o = qk * v + jnp.sum(q_col * r_old, axis=0, keepdims=True) * gamma
        nstate_ref[0, h] = r_old * gamma + k_col * v
        mu = jnp.mean(o, axis=1, keepdims=True)
        d = o - mu
        var = jnp.mean(d * d, axis=1, keepdims=True)
        ret_ref[0, :, h * RET_V_DIM:(h + 1) * RET_V_DIM] = gate * (d * lax.rsqrt(var + GN_EPS))


def _sample_mixers(dense, qkvs, caches, state):
    Bs = dense.shape[0]
    row = lambda w: pl.BlockSpec((1, 1, w), lambda b: (b, 0, 0))
    cache_specs = [pl.BlockSpec((1,) + a.shape[1:], lambda b: (b, 0, 0)) for a in caches]
    st = pl.BlockSpec((1, RET_HEADS, RET_QK_DIM, RET_V_DIM), lambda b: (b, 0, 0, 0))
    return pl.pallas_call(
        _sample_mixer_kernel,
        grid=(Bs,),
        in_specs=[row(DENSE_WIDTH)] + [row(ATTN_WIDTH)] * N_GROUPS + cache_specs + [st],
        out_specs=[row(ATTN_WIDTH), row(ATTN_WIDTH), row(RET_V_WIDTH), st] + cache_specs,
        out_shape=[jax.ShapeDtypeStruct((Bs, 1, ATTN_WIDTH), F32),
                   jax.ShapeDtypeStruct((Bs, 1, ATTN_WIDTH), F32),
                   jax.ShapeDtypeStruct((Bs, 1, RET_V_WIDTH), F32),
                   jax.ShapeDtypeStruct(state.shape, F32)]
                  + [jax.ShapeDtypeStruct(a.shape, F32) for a in caches],
        compiler_params=_cparams(("arbitrary",)),
        name="sample_mixers",
    )(dense, *qkvs, *caches, state)


def _feature_major(a):
    _, b, w, _, _ = a.shape
    return jnp.transpose(a[0], (0, 2, 3, 1)).reshape(b, GROUP_WIDTH, w)


def _row_major(a):
    b, _, w = a.shape
    return jnp.transpose(a.reshape(b, HEADS_PER_GROUP, HEAD_DIM, w), (0, 3, 1, 2))[None]


def kernel(x_prompt, x_sample, cache_k_w128, cache_v_w128, cache_k_w512, cache_v_w512, cache_k_w2048,
           cache_v_w2048, state_retention, c_prompt, c_sample, w_ada, b_ada, w_in, w_att_out, w_ret_out,
           w_o, ln1_g, ln1_b, w_ffn_in, w_ffn_out, ln2_g, ln2_b):
    assert w_in.shape[0] == 1 and x_sample.shape[1] == 1
    B, S, D = x_prompt.shape
    Bs = x_sample.shape[0]
    dils = tuple(d for _, d in DILATED_PATTERNS)

    w_main = jnp.concatenate([w_in[0][:, W_IN_ROTATE:], w_in[0][:, 2 * ATTN_WIDTH:W_IN_ROTATE]], axis=1).astype(BF16)
    w_qk = w_in[0][:, :2 * ATTN_WIDTH].astype(BF16)
    pair_cols = jnp.asarray([s * SEG + f for s in range(2 * N_GROUPS) for f in _pair_source()], jnp.int32)
    w_qk_paired = w_qk[:, pair_cols]
    weights = tuple(w[0].astype(BF16) for w in (w_att_out, w_ret_out, w_o, w_ffn_in, w_ffn_out))
    ln = (ln1_g, ln1_b, ln2_g, ln2_b)

    mod = _modulation(jnp.concatenate([c_prompt, c_sample], axis=0), w_ada[0], b_ada[0])
    mod_p = mod[:B].reshape(B, 1, 6 * D)
    mod_s = mod[B:].reshape(1, Bs, 6 * D)

    tabs_p = _rope_tables(jnp.arange(S, dtype=jnp.int32), paired=True)
    win_rows = tuple(min(w, S) for w, _ in DILATED_PATTERNS)
    dense_p, *rest = _in_projection(x_prompt, mod_p, w_main, w_qk_paired, tabs_p, tm=256, dils=dils, paired=True,
                                    win_rows=win_rows, out_dtype=BF16)
    qkv_p, wins_p = rest[:N_GROUPS], rest[N_GROUPS:]
    o_p, lse_p = [], []
    for g, d in enumerate(dils):
        tl = min(S // d, 1024)
        o, lse = _prompt_attention(qkv_p[g], g, tl=tl, nc=min(d, 1024 // tl))
        o_p.append(o)
        lse_p.append(lse)
    ret_p, state_p = _prompt_retention(dense_p, tr=512)
    y_p = _merge_ffn(x_prompt, dense_p, ret_p, o_p, lse_p, mod_p, weights, ln, tm=256, dils=dils)

    xs = x_sample.reshape(1, Bs, D)
    tabs_s = _rope_tables(PAST_LEN + jnp.arange(1, dtype=jnp.int32))
    unit = (1,) * N_GROUPS
    dense_s, *qkv_s = _in_projection(xs, mod_s, w_main, w_qk, tabs_s, tm=Bs, dils=unit, paired=False,
                                     win_rows=None, out_dtype=F32)
    caches = [_feature_major(c) for c in (cache_k_w128, cache_v_w128, cache_k_w512, cache_v_w512,
                                          cache_k_w2048, cache_v_w2048)]
    o_s, lse_s, ret_s, state_s, *rolled = _sample_mixers(
        dense_s.reshape(Bs, 1, DENSE_WIDTH), [a.reshape(Bs, 1, ATTN_WIDTH) for a in qkv_s], caches,
        state_retention[0])
    o_s = o_s.reshape(1, 1, Bs, ATTN_WIDTH)
    lse_s = lse_s.reshape(1, 1, Bs, ATTN_WIDTH)
    o_sg = [o_s[..., g * SEG:(g + 1) * SEG] for g in range(N_GROUPS)]
    lse_sg = [lse_s[..., g * SEG:(g + 1) * SEG] for g in range(N_GROUPS)]
    y_s = _merge_ffn(xs, dense_s, ret_s.reshape(1, Bs, RET_V_WIDTH), o_sg, lse_sg, mod_s, weights, ln,
                     tm=Bs, dils=unit)

    return (y_p, y_s.reshape(Bs, 1, D),
            *[_row_major(w) for w in wins_p], state_p[None],
            *[_row_major(r) for r in rolled], state_s[None])
```

```python
import functools
import math

import jax
import jax.numpy as jnp
from jax import lax
from jax.experimental import pallas as pl
from jax.experimental.pallas import tpu as pltpu

F32 = jnp.float32
BF16 = jnp.bfloat16

D_MODEL = 1024
PAST_LEN = 16384
DILATED_PATTERNS = ((128, 1), (512, 4), (2048, 16))
N_GROUPS = 3
HEADS_PER_GROUP = 4
HEAD_DIM = 64
GROUP_WIDTH = HEADS_PER_GROUP * HEAD_DIM
ATTN_WIDTH = N_GROUPS * GROUP_WIDTH
ATTN_BAND = 128
ATTN_SCALE = HEAD_DIM ** -0.5
ROPE_DIM = HEAD_DIM // 4
ROPE_THETA = 500000.0
RET_HEADS = 4
RET_QK_DIM = 128
RET_V_DIM = 256
RET_QK_WIDTH = RET_HEADS * RET_QK_DIM
RET_V_WIDTH = RET_HEADS * RET_V_DIM
RET_CHUNK = 128
RET_ROPE_THETA = 10000.0
IN_WIDTH = 3 * ATTN_WIDTH + 2 * RET_QK_WIDTH + 2 * RET_V_WIDTH + 2 * D_MODEL
D_FF = 2816
DEEPNORM_ALPHA = 2.0 ** 0.25
LN_EPS = 1e-5
GN_EPS = 1e-6

LANES = 128
SEG = 256
N_SEG = IN_WIDTH // SEG
OFF_RQ, OFF_RK, OFF_RV, OFF_RG, OFF_GA, OFF_GB = 0, 512, 1024, 2048, 3072, 4096
OFF_QA, OFF_KA, OFF_VA = 5120, 5888, 6656
DENSE_WIDTH = OFF_QA
W_IN_ROTATE = 3 * ATTN_WIDTH
NEG_BIG = -1e30
VMEM_LIMIT = 56 * 1024 * 1024


def _cparams(sem):
    return pltpu.CompilerParams(dimension_semantics=sem, vmem_limit_bytes=VMEM_LIMIT)


def _resident(shape):
    nd = len(shape)
    return pl.BlockSpec(shape, lambda *_: (0,) * nd, pipeline_mode=pl.Buffered(1))


def _sigmoid(x):
    return 1.0 / (1.0 + jnp.exp(-x))


def _ret_gamma(h):
    return 1.0 - 2.0 ** (-5.0 - h)


def _ada_kernel(c_ref, w_ref, b_ref, o_ref):
    c = c_ref[...]
    a = (c * _sigmoid(c)).astype(BF16)
    o_ref[...] = jnp.dot(a, w_ref[...].astype(BF16), preferred_element_type=F32) + b_ref[...]


def _modulation(c_all, w_ada, b_ada):
    rows = c_all.shape[0]
    tn = 1024
    return pl.pallas_call(
        _ada_kernel,
        grid=(6 * D_MODEL // tn,),
        in_specs=[pl.BlockSpec((rows, D_MODEL), lambda j: (0, 0)),
                  pl.BlockSpec((D_MODEL, tn), lambda j: (0, j)),
                  pl.BlockSpec((1, tn), lambda j: (0, j))],
        out_specs=pl.BlockSpec((rows, tn), lambda j: (0, j)),
        out_shape=jax.ShapeDtypeStruct((rows, 6 * D_MODEL), F32),
        compiler_params=_cparams(("arbitrary",)),
        name="modulation",
    )(c_all, w_ada, b_ada.reshape(1, -1))


PAIR_SLOT = HEAD_DIM // 2


def _pair_source():
    src = []
    for half_dims in ((0, 16), (8, 40)):
        for h in range(HEADS_PER_GROUP):
            rot, rest = half_dims
            dims = list(range(rot, rot + ROPE_DIM // 2)) + list(range(rest, rest + PAIR_SLOT - ROPE_DIM // 2))
            src += [h * HEAD_DIM + d for d in dims]
    return src


def _rope_tables(pos, paired=False):
    posf = pos.astype(F32)[:, None]
    lane = jnp.arange(LANES)
    half = ROPE_DIM // 2
    inv = jnp.exp(-math.log(ROPE_THETA) * jnp.arange(half, dtype=F32) * (2.0 / ROPE_DIM))
    ang = posf * inv[None, :]
    cos, sin = jnp.cos(ang), jnp.sin(ang)
    if paired:
        in_slot = lane % PAIR_SLOT
        f = in_slot % half
        att_c = jnp.where(in_slot < half, cos[:, f], 1.0)
        att_lo = jnp.where(in_slot < half, sin[:, f], 0.0)
        att_hi = att_lo
    else:
        in_head = lane % HEAD_DIM
        f = in_head % half
        att_c = jnp.where(in_head < ROPE_DIM, cos[:, f], 1.0)
        att_lo = jnp.where(in_head < half, -sin[:, f], 0.0)
        att_hi = jnp.where((in_head >= half) & (in_head < ROPE_DIM), sin[:, f], 0.0)
    rhalf = RET_QK_DIM // 2
    rinv = jnp.exp(-math.log(RET_ROPE_THETA) * jnp.arange(rhalf, dtype=F32) * (2.0 / RET_QK_DIM))
    rang = posf * rinv[None, :]
    rcos, rsin = jnp.cos(rang), jnp.sin(rang)
    ret_c = jnp.concatenate([rcos, rcos], axis=-1)
    ret_s = jnp.concatenate([-rsin, rsin], axis=-1)
    return att_c, att_lo, att_hi, ret_c, ret_s


def _inproj_kernel(x_ref, shift_ref, scale_ref, w_ref, wqk_ref, *refs, tm, dils, paired, keys_t, win_blocks,
                   win_first):
    n_tab = 3 * (N_GROUPS + 1) + 2
    tabs = [r[...] for r in refs[:n_tab]]
    att_tabs = [tabs[3 * g:3 * g + 3] for g in range(N_GROUPS + 1)]
    rc, rs = tabs[-2:]
    refs = refs[n_tab:]
    n_perm = sum(d > 1 for d in dils)
    dense_ref = refs[0]
    qkv_refs = refs[1:1 + N_GROUPS]
    n_out = len(refs) - (1 + n_perm if n_perm else 0)
    kt_ref = refs[1 + N_GROUPS] if keys_t else None
    win_refs = refs[1 + N_GROUPS + bool(keys_t):n_out]
    i = pl.program_id(1)
    hf = x_ref[0] * (1.0 + scale_ref[0]) + shift_ref[0]
    h = hf.astype(BF16)

    h_class = {}
    if n_perm:
        stage_ref = refs[n_out]
        for u in range(D_MODEL // LANES):
            stage_ref[u] = hf[:, u * LANES:(u + 1) * LANES]
        for g, hp_ref in zip([g for g in range(N_GROUPS) if dils[g] > 1], refs[n_out + 1:]):
            d = dils[g]
            n = tm // d
            for r in range(d):
                for u in range(D_MODEL // LANES):
                    hp_ref[r * n:(r + 1) * n, u * LANES:(u + 1) * LANES] = (
                        stage_ref[u, pl.ds(r, n, stride=d), :].astype(BF16))
            h_class[g] = hp_ref

    def att_rope(a, tab, rows=None):
        ac, alo, ahi = [t if rows is None or t.shape[0] == 1 else t[rows:, :] for t in tab]
        if paired:
            lo, hi = a[:, :LANES], a[:, LANES:]
            return jnp.concatenate([lo * ac - hi * alo, hi * ac + lo * alo], axis=1)
        parts = []
        for u in range(SEG // LANES):
            xs = a[:, u * LANES:(u + 1) * LANES]
            parts.append(xs * ac + pltpu.roll(xs, LANES - ROPE_DIM // 2, 1) * alo
                         + pltpu.roll(xs, ROPE_DIM // 2, 1) * ahi)
        return jnp.concatenate(parts, axis=1)

    def ret_rope(a):
        parts = []
        for u in range(SEG // LANES):
            xs = a[:, u * LANES:(u + 1) * LANES]
            parts.append(xs * rc + pltpu.roll(xs, RET_QK_DIM // 2, 1) * rs)
        return jnp.concatenate(parts, axis=1)

    def emit_group(g, c, a):
        ref, d = qkv_refs[g], dils[g]
        n = tm // d
        for r in range(d):
            ref[0, r, :, c * SEG:(c + 1) * SEG] = a[r * n:(r + 1) * n, :].astype(ref.dtype)

    def wk_cols(g):
        return wqk_ref[:, ATTN_WIDTH + g * SEG:ATTN_WIDTH + (g + 1) * SEG]

    def wv_cols(g):
        return w_ref[:, DENSE_WIDTH + g * SEG:DENSE_WIDTH + (g + 1) * SEG]

    def emit_windows(g):
        wb = win_blocks[g]
        hw = h[tm - wb:, :]
        at = att_rope(jnp.dot(hw, wk_cols(g), preferred_element_type=F32), att_tabs[N_GROUPS], tm - wb).T
        if paired:
            dst = {f: lane for lane, f in enumerate(_pair_source())}
            at = jnp.concatenate([at[dst[f]:dst[f] + 8, :] for f in range(0, GROUP_WIDTH, 8)], axis=0)
        win_refs[2 * g][0] = at
        win_refs[2 * g + 1][0] = jnp.dot(hw, wv_cols(g), preferred_element_type=F32).T

    for seg in range(DENSE_WIDTH // SEG):
        c0 = seg * SEG
        a = jnp.dot(h, w_ref[:, c0:c0 + SEG], preferred_element_type=F32)
        if c0 < OFF_RK:
            a = ret_rope(a)
        elif c0 < OFF_RV:
            a = ret_rope(a) * (RET_QK_DIM ** -0.5)
            if keys_t:
                kt_ref[0, c0 - OFF_RK:c0 - OFF_RK + SEG, :] = a.T.astype(kt_ref.dtype)
        elif c0 < OFF_RG:
            pass
        elif c0 < OFF_GA:
            a = a * _sigmoid(a)
        else:
            a = _sigmoid(a)
        dense_ref[0, :, c0:c0 + SEG] = a.astype(dense_ref.dtype)

    for g in range(N_GROUPS):
        hg = h_class[g][...] if g in h_class else h
        wq = wqk_ref[:, g * SEG:(g + 1) * SEG]
        emit_group(g, 0, att_rope(jnp.dot(hg, wq, preferred_element_type=F32), att_tabs[g]) * ATTN_SCALE)
        emit_group(g, 1, att_rope(jnp.dot(hg, wk_cols(g), preferred_element_type=F32), att_tabs[g]))
        emit_group(g, 2, jnp.dot(hg, wv_cols(g), preferred_element_type=F32))

    if win_refs:
        order = sorted(range(N_GROUPS), key=lambda g: win_first[g])

        def guarded(pos):
            if pos == len(order):
                return
            g = order[pos]

            @pl.when(i >= win_first[g])
            def _():
                emit_windows(g)
                guarded(pos + 1)

        guarded(0)


def _class_major_rows(tab, tm, d):
    if d == 1 or tab.shape[0] == 1:
        return tab
    r = tab.shape[0]
    return tab.reshape(r // tm, tm // d, d, LANES).transpose(0, 2, 1, 3).reshape(r, LANES)


def _in_projection(x, mod, w_main, w_qk, tables, *, tm, dils, paired, keys_t, win_rows, out_dtype):
    G, R, _ = x.shape
    n_tiles = R // tm
    mod_blk = tm if mod.shape[1] == R else 1
    tab_blk = tm if tables[0].shape[0] == R else 1

    def mod_spec(col):
        if mod_blk == 1:
            return pl.BlockSpec((1, 1, D_MODEL), lambda b, i: (b, 0, col))
        return pl.BlockSpec((1, tm, D_MODEL), lambda b, i: (b, i, col))

    def tab_spec(tab):
        if tab.shape[0] == 1:
            return pl.BlockSpec((1, LANES), lambda b, i: (0, 0))
        return pl.BlockSpec((tm, LANES), lambda b, i: (i, 0))

    att = list(tables[:3])
    if paired:
        att[2] = jnp.zeros((1, LANES), F32)
    all_tabs = []
    for d in tuple(dils) + (1,):
        all_tabs += [_class_major_rows(t, tm, d) for t in att]
    all_tabs += list(tables[3:])

    out_shapes = [jax.ShapeDtypeStruct((G, R, DENSE_WIDTH), out_dtype)]
    out_specs = [pl.BlockSpec((1, tm, DENSE_WIDTH), lambda b, i: (b, i, 0))]
    for d in dils:
        out_shapes.append(jax.ShapeDtypeStruct((G, d, R // d, ATTN_WIDTH), out_dtype))
        out_specs.append(pl.BlockSpec((1, d, tm // d, ATTN_WIDTH), lambda b, i: (b, 0, i, 0)))
    if keys_t:
        out_shapes.append(jax.ShapeDtypeStruct((G, RET_QK_WIDTH, R), out_dtype))
        out_specs.append(pl.BlockSpec((1, RET_QK_WIDTH, tm), lambda b, i: (b, 0, i)))
    win_blocks = win_first = ()
    if win_rows:
        win_blocks = tuple(min(w, tm) for w in win_rows)
        win_first = tuple(n_tiles - w // wb for w, wb in zip(win_rows, win_blocks))
        for g in range(N_GROUPS):
            first = win_first[g]
            for _ in range(2):
                out_shapes.append(jax.ShapeDtypeStruct((G, GROUP_WIDTH, win_rows[g]), F32))
                out_specs.append(pl.BlockSpec((1, GROUP_WIDTH, win_blocks[g]),
                                              lambda b, i, first=first: (b, 0, jnp.maximum(i - first, 0))))

    scratch = []
    if any(d > 1 for d in dils):
        scratch.append(pltpu.VMEM((D_MODEL // LANES, tm, LANES), F32))
        scratch += [pltpu.VMEM((tm, D_MODEL), BF16) for d in dils if d > 1]
    kern = functools.partial(_inproj_kernel, tm=tm, dils=dils, paired=paired, keys_t=keys_t,
                             win_blocks=win_blocks, win_first=win_first)
    return pl.pallas_call(
        kern,
        grid=(G, n_tiles),
        in_specs=[pl.BlockSpec((1, tm, D_MODEL), lambda b, i: (b, i, 0)),
                  mod_spec(0), mod_spec(1),
                  _resident(w_main.shape), _resident(w_qk.shape)] + [tab_spec(t) for t in all_tabs],
        out_specs=out_specs,
        out_shape=out_shapes,
        scratch_shapes=scratch,
        compiler_params=_cparams(("arbitrary", "arbitrary")),
        name="in_projection",
    )(x, mod, mod, w_main, w_qk, *all_tabs)


def _attn_kernel(q_ref, k_ref, v_ref, kp_ref, vp_ref, o_ref, lse_ref, *, tl, nc):
    t = pl.program_id(2)
    row = lax.broadcasted_iota(jnp.int32, (ATTN_BAND, ATTN_BAND), 0)
    col = lax.broadcasted_iota(jnp.int32, (ATTN_BAND, ATTN_BAND), 1)
    lane = lax.broadcasted_iota(jnp.int32, (1, GROUP_WIDTH), 1)
    cur_ok = col <= row
    qk_lanes = [(lane % LANES) // PAIR_SLOT == h for h in range(HEADS_PER_GROUP)]
    v_lanes = [lane // HEAD_DIM == h for h in range(HEADS_PER_GROUP)]

    def block(q, k_prev, v_prev, k_cur, v_cur, prev_shift):
        prev_ok = col >= row + prev_shift
        mask = jnp.concatenate([prev_ok, cur_ok], axis=1)
        mask = jnp.concatenate([mask] * HEADS_PER_GROUP, axis=0)
        kk = jnp.concatenate([k_prev, k_cur], axis=0)
        vv = jnp.concatenate([v_prev, v_cur], axis=0)
        qs = jnp.concatenate([jnp.where(sel, q, jnp.zeros_like(q)) for sel in qk_lanes], axis=0)
        s = lax.dot_general(qs, kk, (((1,), (1,)), ((), ())), preferred_element_type=F32)
        s = jnp.where(mask, s, NEG_BIG)
        m = jnp.max(s, axis=1, keepdims=True)
        e = jnp.exp(s - m)
        l = jnp.sum(e, axis=1, keepdims=True)
        oh = jnp.dot((e / l).astype(BF16), vv, preferred_element_type=F32)
        lse = m + jnp.log(l)
        o = jnp.zeros((ATTN_BAND, GROUP_WIDTH), F32)
        lo = jnp.zeros((ATTN_BAND, GROUP_WIDTH), F32)
        for h, sel in enumerate(v_lanes):
            rows = slice(h * ATTN_BAND, (h + 1) * ATTN_BAND)
            o = jnp.where(sel, oh[rows], o)
            lo = jnp.where(sel, lse[rows], lo)
        return o, lo

    first = jnp.where(t > 0, 0, ATTN_BAND)
    for c in range(nc):
        for j in range(tl // ATTN_BAND):
            j0, jp = j * ATTN_BAND, (j - 1) * ATTN_BAND
            rows = slice(j0, j0 + ATTN_BAND)
            if j == 0:
                k_prev, v_prev, shift = kp_ref[0, c], vp_ref[0, c], first
            else:
                k_prev, v_prev, shift = k_ref[0, c, jp:j0, :], v_ref[0, c, jp:j0, :], 0
            o, lse = block(q_ref[0, c, rows, :], k_prev, v_prev, k_ref[0, c, rows, :], v_ref[0, c, rows, :], shift)
            o_ref[0, c, rows, :] = o.astype(o_ref.dtype)
            lse_ref[0, c, rows, :] = lse


def _prompt_attention(qkv, g, *, tl, nc):
    B, dil, L, _ = qkv.shape
    sub = tl // ATTN_BAND

    def cur(c):
        return pl.BlockSpec((1, nc, tl, SEG), lambda b, r, t: (b, r, t, c))

    def prev(c):
        return pl.BlockSpec((1, nc, ATTN_BAND, SEG), lambda b, r, t: (b, r, jnp.maximum(t * sub - 1, 0), c))

    out_spec = pl.BlockSpec((1, nc, tl, SEG), lambda b, r, t: (b, r, t, 0))
    return pl.pallas_call(
        functools.partial(_attn_kernel, tl=tl, nc=nc),
        grid=(B, dil // nc, L // tl),
        in_specs=[cur(0), cur(1), cur(2), prev(1), prev(2)],
        out_specs=[out_spec, out_spec],
        out_shape=[jax.ShapeDtypeStruct((B, dil, L, SEG), BF16),
                   jax.ShapeDtypeStruct((B, dil, L, SEG), F32)],
        compiler_params=_cparams(("arbitrary", "arbitrary", "arbitrary")),
        name="prompt_attention_g%d" % g,
    )(qkv, qkv, qkv, qkv, qkv)


def _retention_tables():
    C = RET_CHUNK
    lg = jnp.log(1.0 - 2.0 ** (-5.0 - jnp.arange(RET_HEADS, dtype=F32)))
    idx = jnp.arange(C, dtype=F32)
    diff = idx[:, None] - idx[None, :]
    decay_in = jnp.where(diff >= 0, jnp.exp(lg[:, None, None] * jnp.maximum(diff, 0.0)), 0.0)
    decay_q = jnp.exp(lg[:, None] * (idx[None, :] + 1.0))[:, :, None]
    decay_k = jnp.exp(lg[:, None] * (C - 1.0 - idx[None, :]))[:, None, :]
    return decay_in, decay_q, decay_k


def _retention_kernel(qk_ref, kt_ref, v_ref, g_ref, din_ref, dq_ref, dk_ref, o_ref, state_ref, r_scr, *, tr):
    t = pl.program_id(1)
    n_chunks = tr // RET_CHUNK

    @pl.when(t == 0)
    def _():
        r_scr[...] = jnp.zeros_like(r_scr)

    def key_t(c, h):
        return kt_ref[0, h * RET_QK_DIM:(h + 1) * RET_QK_DIM, c * RET_CHUNK:(c + 1) * RET_CHUNK]

    def val(c, h):
        return v_ref[0, c * RET_CHUNK:(c + 1) * RET_CHUNK, h * RET_V_DIM:(h + 1) * RET_V_DIM]

    states = [[None] * RET_HEADS for _ in range(n_chunks)]
    for h in range(RET_HEADS):
        r = r_scr[h]
        for c in range(n_chunks):
            states[c][h] = r
            kd = (key_t(c, h).astype(F32) * dk_ref[h]).astype(BF16)
            r = r * (_ret_gamma(h) ** RET_CHUNK) + jnp.dot(kd, val(c, h), preferred_element_type=F32)
        r_scr[h] = r

    for c in range(n_chunks):
        rows = slice(c * RET_CHUNK, (c + 1) * RET_CHUNK)
        for h in range(RET_HEADS):
            q = qk_ref[0, rows, OFF_RQ + h * RET_QK_DIM:OFF_RQ + (h + 1) * RET_QK_DIM]
            v = val(c, h)
            s = jnp.dot(q, key_t(c, h), preferred_element_type=F32) * din_ref[h]
            o = (jnp.dot(s.astype(BF16), v, preferred_element_type=F32)
                 + jnp.dot(q, states[c][h].astype(BF16), preferred_element_type=F32) * dq_ref[h])
            mu = jnp.mean(o, axis=1, keepdims=True)
            d = o - mu
            var = jnp.mean(d * d, axis=1, keepdims=True)
            gn = d * lax.rsqrt(var + GN_EPS)
            gate = g_ref[0, rows, h * RET_V_DIM:(h + 1) * RET_V_DIM].astype(F32)
            o_ref[0, rows, h * RET_V_DIM:(h + 1) * RET_V_DIM] = (gate * gn).astype(o_ref.dtype)

    @pl.when(t == pl.num_programs(1) - 1)
    def _():
        state_ref[0] = r_scr[...]


def _prompt_retention(dense, keys_t, *, tr):
    B, S, _ = dense.shape
    tabs = _retention_tables()
    blk = RET_V_WIDTH

    def colblk(c):
        return pl.BlockSpec((1, tr, blk), lambda b, t: (b, t, c))

    return pl.pallas_call(
        functools.partial(_retention_kernel, tr=tr),
        grid=(B, S // tr),
        in_specs=[colblk(0), pl.BlockSpec((1, RET_QK_WIDTH, tr), lambda b, t: (b, 0, t)),
                  colblk(OFF_RV // blk), colblk(OFF_RG // blk)]
                 + [pl.BlockSpec(tab.shape, lambda b, t: (0, 0, 0)) for tab in tabs],
        out_specs=[pl.BlockSpec((1, tr, RET_V_WIDTH), lambda b, t: (b, t, 0)),
                   pl.BlockSpec((1, RET_HEADS, RET_QK_DIM, RET_V_DIM), lambda b, t: (b, 0, 0, 0))],
        out_shape=[jax.ShapeDtypeStruct((B, S, RET_V_WIDTH), BF16),
                   jax.ShapeDtypeStruct((B, RET_HEADS, RET_QK_DIM, RET_V_DIM), F32)],
        scratch_shapes=[pltpu.VMEM((RET_HEADS, RET_QK_DIM, RET_V_DIM), F32)],
        compiler_params=_cparams(("arbitrary", "arbitrary")),
        name="prompt_retention",
    )(dense, keys_t, dense, dense, *tabs)


def _layer_norm(u, g, b):
    mu = jnp.mean(u, axis=1, keepdims=True)
    d = u - mu
    var = jnp.mean(d * d, axis=1, keepdims=True)
    return d * lax.rsqrt(var + LN_EPS) * g + b


def _merge_ffn_kernel(x_ref, ga_ref, gb_ref, ret_ref, o0_ref, o1_ref, o2_ref, l0_ref, l1_ref, l2_ref,
                      gate1_ref, shift2_ref, scale2_ref, gate2_ref,
                      watt_ref, wret_ref, wo_ref, wfi_ref, wfo_ref,
                      ln1g_ref, ln1b_ref, ln2g_ref, ln2b_ref, y_ref, *scratch, tm, sub, dils):
    n_sub = tm // sub
    per_sub = 1 + 2 * N_GROUPS
    for s in range(n_sub):
        rows = slice(s * sub, (s + 1) * sub)
        act_scr = scratch[s * per_sub]
        stage_refs = scratch[s * per_sub + 1:(s + 1) * per_sub]

        def natural(ref, g, stage_ref):
            d = dils[g]
            n = sub // d
            if d == 1:
                return ref[0, 0, rows, :].astype(F32)
            for r in range(d):
                for u in range(GROUP_WIDTH // LANES):
                    stage_ref[u, pl.ds(r, n, stride=d), :] = (
                        ref[0, r, s * n:(s + 1) * n, u * LANES:(u + 1) * LANES].astype(F32))
            return jnp.concatenate([stage_ref[u] for u in range(GROUP_WIDTH // LANES)], axis=1)

        def mod_rows(ref):
            return ref[0] if ref.shape[1] == 1 else ref[0, rows, :]

        lses = [natural(ref, g, stage_refs[g]) for g, ref in enumerate((l0_ref, l1_ref, l2_ref))]
        outs = [natural(ref, g, stage_refs[N_GROUPS + g]) for g, ref in enumerate((o0_ref, o1_ref, o2_ref))]
        m = jnp.maximum(jnp.maximum(lses[0], lses[1]), lses[2])
        es = [jnp.exp(l - m) for l in lses]
        att = (es[0] * outs[0] + es[1] * outs[1] + es[2] * outs[2]) / (es[0] + es[1] + es[2])
        y_att = jnp.dot(att.astype(BF16), watt_ref[...], preferred_element_type=F32)
        y_ret = jnp.dot(ret_ref[0, rows, :].astype(BF16), wret_ref[...], preferred_element_type=F32)
        mix = (ga_ref[0, rows, :].astype(F32) * y_att + gb_ref[0, rows, :].astype(F32) * y_ret).astype(BF16)
        mixed = jnp.dot(mix, wo_ref[...], preferred_element_type=F32)
        x1 = _layer_norm(DEEPNORM_ALPHA * x_ref[0, rows, :] + mod_rows(gate1_ref) * mixed,
                         ln1g_ref[...], ln1b_ref[...])
        h2 = (x1 * (1.0 + mod_rows(scale2_ref)) + mod_rows(shift2_ref)).astype(BF16)
        for c0 in range(0, D_FF, SEG):
            fg = jnp.dot(h2, wfi_ref[:, c0:c0 + SEG], preferred_element_type=F32)
            fu = jnp.dot(h2, wfi_ref[:, D_FF + c0:D_FF + c0 + SEG], preferred_element_type=F32)
            act_scr[:, c0:c0 + SEG] = (fg * _sigmoid(fg) * fu).astype(BF16)
        ff = jnp.dot(act_scr[...], wfo_ref[...], preferred_element_type=F32)
        y_ref[0, rows, :] = _layer_norm(DEEPNORM_ALPHA * x1 + mod_rows(gate2_ref) * ff, ln2g_ref[...], ln2b_ref[...])


def _merge_ffn(x, dense, ret, o_groups, lse_groups, mod, weights, ln, *, tm, sub, dils):
    G, R, _ = x.shape
    mod_rows = mod.shape[1]
    sub_scratch = [pltpu.VMEM((sub, D_FF), BF16)] + [pltpu.VMEM((GROUP_WIDTH // LANES, sub, LANES), F32)] * (2 * N_GROUPS)

    def mod_spec(col):
        if mod_rows == 1:
            return pl.BlockSpec((1, 1, D_MODEL), lambda b, i: (b, 0, col))
        return pl.BlockSpec((1, tm, D_MODEL), lambda b, i: (b, i, col))

    def rows(width, col=0):
        return pl.BlockSpec((1, tm, width), lambda b, i: (b, i, col))

    group_specs = [pl.BlockSpec((1, d, tm // d, GROUP_WIDTH), lambda b, i: (b, 0, i, 0)) for d in dils]
    vec = pl.BlockSpec((1, D_MODEL), lambda b, i: (0, 0))
    return pl.pallas_call(
        functools.partial(_merge_ffn_kernel, tm=tm, sub=sub, dils=dils),
        grid=(G, R // tm),
        in_specs=[rows(D_MODEL), rows(D_MODEL, OFF_GA // D_MODEL), rows(D_MODEL, OFF_GB // D_MODEL),
                  rows(RET_V_WIDTH)] + group_specs + group_specs
                 + [mod_spec(2), mod_spec(3), mod_spec(4), mod_spec(5)]
                 + [_resident(w.shape) for w in weights] + [vec] * 4,
        out_specs=rows(D_MODEL),
        out_shape=jax.ShapeDtypeStruct((G, R, D_MODEL), F32),
        scratch_shapes=sub_scratch * (tm // sub),
        compiler_params=_cparams(("arbitrary", "arbitrary")),
        name="merge_ffn",
    )(x, dense, dense, ret, *o_groups, *lse_groups, mod, mod, mod, mod, *weights, *ln)


def _column(row):
    return jnp.broadcast_to(row, (LANES, LANES)).T


def _sample_mixer_kernel(dense_ref, qkv0_ref, qkv1_ref, qkv2_ref, ck0_ref, cv0_ref, ck1_ref, cv1_ref,
                         ck2_ref, cv2_ref, state_ref, o_ref, lse_ref, ret_ref, nstate_ref,
                         nk0_ref, nv0_ref, nk1_ref, nv1_ref, nk2_ref, nv2_ref):
    pad = 16
    row_h = lax.broadcasted_iota(jnp.int32, (pad, GROUP_WIDTH), 0)
    lane_h = lax.broadcasted_iota(jnp.int32, (pad, GROUP_WIDTH), 1) // HEAD_DIM
    own = row_h == lane_h
    lane = lax.broadcasted_iota(jnp.int32, (GROUP_WIDTH, LANES), 1)

    def rolled(c_ref, n_ref, new_row):
        w = c_ref.shape[2]
        new_col = jnp.concatenate([_column(new_row[:, u * LANES:(u + 1) * LANES])
                                   for u in range(GROUP_WIDTH // LANES)], axis=0)
        nblk = w // LANES
        cur = pltpu.roll(c_ref[0, :, 0:LANES], LANES - 1, 1)
        for j in range(nblk):
            if j + 1 < nblk:
                nxt = pltpu.roll(c_ref[0, :, (j + 1) * LANES:(j + 2) * LANES], LANES - 1, 1)
            else:
                nxt = new_col
            n_ref[0, :, j * LANES:(j + 1) * LANES] = jnp.where(lane < LANES - 1, cur, nxt)
            cur = nxt

    groups = ((qkv0_ref, ck0_ref, cv0_ref, nk0_ref, nv0_ref),
              (qkv1_ref, ck1_ref, cv1_ref, nk1_ref, nv1_ref),
              (qkv2_ref, ck2_ref, cv2_ref, nk2_ref, nv2_ref))
    for g, (qkv_ref, ck_ref, cv_ref, nk_ref, nv_ref) in enumerate(groups):
        dil = DILATED_PATTERNS[g][1]
        w = ck_ref.shape[2]
        q = qkv_ref[0, :, 0:SEG]
        kn = qkv_ref[0, :, SEG:2 * SEG]
        vn = qkv_ref[0, :, 2 * SEG:3 * SEG]
        qsel = jnp.where(own, jnp.broadcast_to(q, (pad, GROUP_WIDTH)), 0.0)
        s = jnp.dot(qsel.astype(BF16), ck_ref[0].astype(BF16), preferred_element_type=F32)
        pos = lax.broadcasted_iota(jnp.int32, (pad, w), 1)
        s = jnp.where((pos & (dil - 1)) == 0, s, NEG_BIG)
        sn = jnp.sum(qsel * kn, axis=1, keepdims=True)
        m = jnp.maximum(jnp.max(s, axis=1, keepdims=True), sn)
        e = jnp.exp(s - m)
        en = jnp.exp(sn - m)
        l = jnp.sum(e, axis=1, keepdims=True) + en
        o_all = lax.dot_general((e / l).astype(BF16), cv_ref[0].astype(BF16), (((1,), (1,)), ((), ())),
                                preferred_element_type=F32) + (en / l) * vn
        o_ref[0, :, g * SEG:(g + 1) * SEG] = jnp.sum(jnp.where(own, o_all, 0.0), axis=0, keepdims=True)
        lse_ref[0, :, g * SEG:(g + 1) * SEG] = jnp.sum(jnp.where(own, m + jnp.log(l), 0.0), axis=0, keepdims=True)
        rolled(ck_ref, nk_ref, kn)
        rolled(cv_ref, nv_ref, vn)

    for h in range(RET_HEADS):
        gamma = _ret_gamma(h)
        q = dense_ref[0, :, OFF_RQ + h * RET_QK_DIM:OFF_RQ + (h + 1) * RET_QK_DIM]
        k = dense_ref[0, :, OFF_RK + h * RET_QK_DIM:OFF_RK + (h + 1) * RET_QK_DIM]
        v = dense_ref[0, :, OFF_RV + h * RET_V_DIM:OFF_RV + (h + 1) * RET_V_DIM]
        gate = dense_ref[0, :, OFF_RG + h * RET_V_DIM:OFF_RG + (h + 1) * RET_V_DIM]
        r_old = state_ref[0, h]
        q_col = jnp.concatenate([_column(q)] * (RET_V_DIM // LANES), axis=1)
        k_col = jnp.concatenate([_column(k)] * (RET_V_DIM // LANES), axis=1)
        qk = jnp.sum(q * k, axis=1, keepdims=True)
        o = qk * v + jnp.sum(q_col * r_old, axis=0, keepdims=True) * gamma
        nstate_ref[0, h] = r_old * gamma + k_col * v
        mu = jnp.mean(o, axis=1, keepdims=True)
        d = o - mu
        var = jnp.mean(d * d, axis=1, keepdims=True)
        ret_ref[0, :, h * RET_V_DIM:(h + 1) * RET_V_DIM] = gate * (d * lax.rsqrt(var + GN_EPS))


def _sample_mixers(dense, qkvs, caches, state):
    Bs = dense.shape[0]
    row = lambda w: pl.BlockSpec((1, 1, w), lambda b: (b, 0, 0))
    cache_specs = [pl.BlockSpec((1,) + a.shape[1:], lambda b: (b, 0, 0)) for a in caches]
    st = pl.BlockSpec((1, RET_HEADS, RET_QK_DIM, RET_V_DIM), lambda b: (b, 0, 0, 0))
    return pl.pallas_call(
        _sample_mixer_kernel,
        grid=(Bs,),
        in_specs=[row(DENSE_WIDTH)] + [row(ATTN_WIDTH)] * N_GROUPS + cache_specs + [st],
        out_specs=[row(ATTN_WIDTH), row(ATTN_WIDTH), row(RET_V_WIDTH), st] + cache_specs,
        out_shape=[jax.ShapeDtypeStruct((Bs, 1, ATTN_WIDTH), F32),
                   jax.ShapeDtypeStruct((Bs, 1, ATTN_WIDTH), F32),
                   jax.ShapeDtypeStruct((Bs, 1, RET_V_WIDTH), F32),
                   jax.ShapeDtypeStruct(state.shape, F32)]
                  + [jax.ShapeDtypeStruct(a.shape, F32) for a in caches],
        compiler_params=_cparams(("arbitrary",)),
        name="sample_mixers",
    )(dense, *qkvs, *caches, state)


def _feature_major(a):
    _, b, w, _, _ = a.shape
    return jnp.transpose(a[0], (0, 2, 3, 1)).reshape(b, GROUP_WIDTH, w)


def _row_major(a):
    b, _, w = a.shape
    return jnp.transpose(a.reshape(b, HEADS_PER_GROUP, HEAD_DIM, w), (0, 3, 1, 2))[None]


def kernel(x_prompt, x_sample, cache_k_w128, cache_v_w128, cache_k_w512, cache_v_w512, cache_k_w2048,
           cache_v_w2048, state_retention, c_prompt, c_sample, w_ada, b_ada, w_in, w_att_out, w_ret_out,
           w_o, ln1_g, ln1_b, w_ffn_in, w_ffn_out, ln2_g, ln2_b):
    assert w_in.shape[0] == 1 and x_sample.shape[1] == 1
    B, S, D = x_prompt.shape
    Bs = x_sample.shape[0]
    dils = tuple(d for _, d in DILATED_PATTERNS)

    w_main = jnp.concatenate([w_in[0][:, W_IN_ROTATE:], w_in[0][:, 2 * ATTN_WIDTH:W_IN_ROTATE]], axis=1).astype(BF16)
    w_qk = w_in[0][:, :2 * ATTN_WIDTH].astype(BF16)
    pair_cols = jnp.asarray([s * SEG + f for s in range(2 * N_GROUPS) for f in _pair_source()], jnp.int32)
    w_qk_paired = w_qk[:, pair_cols]
    weights = tuple(w[0].astype(BF16) for w in (w_att_out, w_ret_out, w_o, w_ffn_in, w_ffn_out))
    ln = (ln1_g, ln1_b, ln2_g, ln2_b)

    mod = _modulation(jnp.concatenate([c_prompt, c_sample], axis=0), w_ada[0], b_ada[0])
    mod_p = mod[:B].reshape(B, 1, 6 * D)
    mod_s = mod[B:].reshape(1, Bs, 6 * D)

    tabs_p = _rope_tables(jnp.arange(S, dtype=jnp.int32), paired=True)
    win_rows = tuple(min(w, S) for w, _ in DILATED_PATTERNS)
    dense_p, *rest = _in_projection(x_prompt, mod_p, w_main, w_qk_paired, tabs_p, tm=256, dils=dils, paired=True,
                                    keys_t=True, win_rows=win_rows, out_dtype=BF16)
    qkv_p, keys_t_p, wins_p = rest[:N_GROUPS], rest[N_GROUPS], rest[N_GROUPS + 1:]
    o_p, lse_p = [], []
    for g, d in enumerate(dils):
        tl = min(S // d, 1024)
        o, lse = _prompt_attention(qkv_p[g], g, tl=tl, nc=min(d, 1024 // tl))
        o_p.append(o)
        lse_p.append(lse)
    ret_p, state_p = _prompt_retention(dense_p, keys_t_p, tr=512)
    y_p = _merge_ffn(x_prompt, dense_p, ret_p, o_p, lse_p, mod_p, weights, ln, tm=512, sub=256, dils=dils)

    xs = x_sample.reshape(1, Bs, D)
    tabs_s = _rope_tables(PAST_LEN + jnp.arange(1, dtype=jnp.int32))
    unit = (1,) * N_GROUPS
    dense_s, *qkv_s = _in_projection(xs, mod_s, w_main, w_qk, tabs_s, tm=Bs, dils=unit, paired=False,
                                     keys_t=False, win_rows=None, out_dtype=F32)
    caches = [_feature_major(c) for c in (cache_k_w128, cache_v_w128, cache_k_w512, cache_v_w512,
                                          cache_k_w2048, cache_v_w2048)]
    o_s, lse_s, ret_s, state_s, *rolled = _sample_mixers(
        dense_s.reshape(Bs, 1, DENSE_WIDTH), [a.reshape(Bs, 1, ATTN_WIDTH) for a in qkv_s], caches,
        state_retention[0])
    o_s = o_s.reshape(1, 1, Bs, ATTN_WIDTH)
    lse_s = lse_s.reshape(1, 1, Bs, ATTN_WIDTH)
    o_sg = [o_s[..., g * SEG:(g + 1) * SEG] for g in range(N_GROUPS)]
    lse_sg = [lse_s[..., g * SEG:(g + 1) * SEG] for g in range(N_GROUPS)]
    y_s = _merge_ffn(xs, dense_s, ret_s.reshape(1, Bs, RET_V_WIDTH), o_sg, lse_sg, mod_s, weights, ln,
                     tm=Bs, sub=Bs, dils=unit)

    return (y_p, y_s.reshape(Bs, 1, D),
            *[_row_major(w) for w in wins_p], state_p[None],
            *[_row_major(r) for r in rolled], state_s[None])
```

```python
import functools
import math

import jax
import jax.numpy as jnp
import numpy as np
from jax import lax
from jax.experimental import pallas as pl
from jax.experimental.pallas import tpu as pltpu

F32 = jnp.float32
BF16 = jnp.bfloat16

D_MODEL = 1024
PAST_LEN = 16384
DILATED_PATTERNS = ((128, 1), (512, 4), (2048, 16))
N_GROUPS = 3
HEADS_PER_GROUP = 4
HEAD_DIM = 64
GROUP_WIDTH = HEADS_PER_GROUP * HEAD_DIM
ATTN_WIDTH = N_GROUPS * GROUP_WIDTH
ATTN_BAND = 128
ATTN_SCALE = HEAD_DIM ** -0.5
ROPE_DIM = HEAD_DIM // 4
ROPE_THETA = 500000.0
RET_HEADS = 4
RET_QK_DIM = 128
RET_V_DIM = 256
RET_QK_WIDTH = RET_HEADS * RET_QK_DIM
RET_V_WIDTH = RET_HEADS * RET_V_DIM
RET_CHUNK = 128
RET_ROPE_THETA = 10000.0
IN_WIDTH = 3 * ATTN_WIDTH + 2 * RET_QK_WIDTH + 2 * RET_V_WIDTH + 2 * D_MODEL
D_FF = 2816
DEEPNORM_ALPHA = 2.0 ** 0.25
LN_EPS = 1e-5
GN_EPS = 1e-6

LANES = 128
SEG = 256
N_SEG = IN_WIDTH // SEG
FFN_CHUNK = 256
OFF_RQ, OFF_RK, OFF_RV, OFF_RG, OFF_GA, OFF_GB = 0, 512, 1024, 2048, 3072, 4096
OFF_QA, OFF_KA, OFF_VA = 5120, 5888, 6656
DENSE_WIDTH = OFF_QA
W_IN_ROTATE = 3 * ATTN_WIDTH
NEG_BIG = -1e30
VMEM_LIMIT = 56 * 1024 * 1024


def _cparams(sem):
    return pltpu.CompilerParams(dimension_semantics=sem, vmem_limit_bytes=VMEM_LIMIT)


def _resident(shape):
    nd = len(shape)
    return pl.BlockSpec(shape, lambda *_: (0,) * nd, pipeline_mode=pl.Buffered(1))


def _sigmoid(x):
    return 1.0 / (1.0 + jnp.exp(-x))


def _ret_gamma(h):
    return 1.0 - 2.0 ** (-5.0 - h)


def _ada_kernel(c_ref, w_ref, b_ref, o_ref):
    c = c_ref[...]
    a = (c * _sigmoid(c)).astype(BF16)
    o_ref[...] = jnp.dot(a, w_ref[...].astype(BF16), preferred_element_type=F32) + b_ref[...]


def _modulation(c_all, w_ada, b_ada):
    rows = c_all.shape[0]
    tn = 1024
    return pl.pallas_call(
        _ada_kernel,
        grid=(6 * D_MODEL // tn,),
        in_specs=[pl.BlockSpec((rows, D_MODEL), lambda j: (0, 0)),
                  pl.BlockSpec((D_MODEL, tn), lambda j: (0, j)),
                  pl.BlockSpec((1, tn), lambda j: (0, j))],
        out_specs=pl.BlockSpec((rows, tn), lambda j: (0, j)),
        out_shape=jax.ShapeDtypeStruct((rows, 6 * D_MODEL), F32),
        compiler_params=_cparams(("arbitrary",)),
        name="modulation",
    )(c_all, w_ada, b_ada.reshape(1, -1))


PAIR_SLOT = HEAD_DIM // 2


def _pair_source():
    src = []
    for half_dims in ((0, 16), (8, 40)):
        for h in range(HEADS_PER_GROUP):
            rot, rest = half_dims
            dims = list(range(rot, rot + ROPE_DIM // 2)) + list(range(rest, rest + PAIR_SLOT - ROPE_DIM // 2))
            src += [h * HEAD_DIM + d for d in dims]
    return src


def _rope_tables(pos, paired=False):
    posf = jnp.asarray(np.asarray(pos), jnp.int32).astype(F32)[:, None]
    rows = posf.shape[0]
    half = ROPE_DIM // 2
    inv = jnp.exp(-math.log(ROPE_THETA) * jnp.arange(half, dtype=F32) * (2.0 / ROPE_DIM))
    ang = posf * inv[None, :]
    cos, sin = jnp.cos(ang), jnp.sin(ang)
    ones = jnp.ones((rows, 1), F32)
    zeros = jnp.zeros((rows, 1), F32)

    def lanes(*pieces):
        slot = jnp.concatenate([jnp.broadcast_to(p, (rows, n)) for p, n in pieces], axis=1)
        return jnp.tile(slot, (1, LANES // slot.shape[1]))

    if paired:
        rest = PAIR_SLOT - half
        att_c = lanes((cos, half), (ones, rest))
        att_lo = lanes((sin, half), (zeros, rest))
        att_hi = att_lo
    else:
        rest = HEAD_DIM - ROPE_DIM
        att_c = lanes((cos, half), (cos, half), (ones, rest))
        att_lo = lanes((-sin, half), (zeros, half), (zeros, rest))
        att_hi = lanes((zeros, half), (sin, half), (zeros, rest))
    rhalf = RET_QK_DIM // 2
    rinv = jnp.exp(-math.log(RET_ROPE_THETA) * jnp.arange(rhalf, dtype=F32) * (2.0 / RET_QK_DIM))
    rang = posf * rinv[None, :]
    rcos, rsin = jnp.cos(rang), jnp.sin(rang)
    ret_c = jnp.concatenate([rcos, rcos], axis=-1)
    ret_s = jnp.concatenate([-rsin, rsin], axis=-1)
    return att_c, att_lo, att_hi, ret_c, ret_s


def _inproj_kernel(x_ref, shift_ref, scale_ref, w_ref, wqk_ref, *refs, tm, dils, paired, keys_t, win_blocks,
                   win_first):
    n_tab = 3 * (N_GROUPS + 1) + 2
    tabs = [r[...] for r in refs[:n_tab]]
    att_tabs = [tabs[3 * g:3 * g + 3] for g in range(N_GROUPS + 1)]
    rc, rs = tabs[-2:]
    refs = refs[n_tab:]
    n_perm = sum(d > 1 for d in dils)
    dense_ref = refs[0]
    qkv_refs = refs[1:1 + N_GROUPS]
    n_out = len(refs) - (1 + n_perm if n_perm else 0)
    kt_ref = refs[1 + N_GROUPS] if keys_t else None
    win_refs = refs[1 + N_GROUPS + bool(keys_t):n_out]
    i = pl.program_id(1)
    hf = x_ref[0] * (1.0 + scale_ref[0]) + shift_ref[0]
    h = hf.astype(BF16)

    h_class = {}
    if n_perm:
        stage_ref = refs[n_out]
        for u in range(D_MODEL // LANES):
            stage_ref[u] = hf[:, u * LANES:(u + 1) * LANES]
        for g, hp_ref in zip([g for g in range(N_GROUPS) if dils[g] > 1], refs[n_out + 1:]):
            d = dils[g]
            n = tm // d
            for r in range(d):
                for u in range(D_MODEL // LANES):
                    hp_ref[r * n:(r + 1) * n, u * LANES:(u + 1) * LANES] = (
                        stage_ref[u, pl.ds(r, n, stride=d), :].astype(BF16))
            h_class[g] = hp_ref

    def att_rope(a, tab, rows=None):
        ac, alo, ahi = [t if rows is None or t.shape[0] == 1 else t[rows:, :] for t in tab]
        if paired:
            lo, hi = a[:, :LANES], a[:, LANES:]
            return jnp.concatenate([lo * ac - hi * alo, hi * ac + lo * alo], axis=1)
        parts = []
        for u in range(SEG // LANES):
            xs = a[:, u * LANES:(u + 1) * LANES]
            parts.append(xs * ac + pltpu.roll(xs, LANES - ROPE_DIM // 2, 1) * alo
                         + pltpu.roll(xs, ROPE_DIM // 2, 1) * ahi)
        return jnp.concatenate(parts, axis=1)

    def ret_rope(a):
        parts = []
        for u in range(SEG // LANES):
            xs = a[:, u * LANES:(u + 1) * LANES]
            parts.append(xs * rc + pltpu.roll(xs, RET_QK_DIM // 2, 1) * rs)
        return jnp.concatenate(parts, axis=1)

    def emit_group(g, c, a):
        ref, d = qkv_refs[g], dils[g]
        n = tm // d
        for r in range(d):
            ref[0, r, :, c * SEG:(c + 1) * SEG] = a[r * n:(r + 1) * n, :].astype(ref.dtype)

    def wk_cols(g):
        return wqk_ref[:, ATTN_WIDTH + g * SEG:ATTN_WIDTH + (g + 1) * SEG]

    def wv_cols(g):
        return w_ref[:, DENSE_WIDTH + g * SEG:DENSE_WIDTH + (g + 1) * SEG]

    def emit_windows(g):
        wb = win_blocks[g]
        hw = h[tm - wb:, :]
        at = att_rope(jnp.dot(hw, wk_cols(g), preferred_element_type=F32), att_tabs[N_GROUPS], tm - wb).T
        if paired:
            dst = {f: lane for lane, f in enumerate(_pair_source())}
            at = jnp.concatenate([at[dst[f]:dst[f] + 8, :] for f in range(0, GROUP_WIDTH, 8)], axis=0)
        win_refs[2 * g][0] = at
        win_refs[2 * g + 1][0] = jnp.dot(hw, wv_cols(g), preferred_element_type=F32).T

    for seg in range(DENSE_WIDTH // SEG):
        c0 = seg * SEG
        a = jnp.dot(h, w_ref[:, c0:c0 + SEG], preferred_element_type=F32)
        if c0 < OFF_RK:
            a = ret_rope(a)
        elif c0 < OFF_RV:
            a = ret_rope(a) * (RET_QK_DIM ** -0.5)
            if keys_t:
                kt_ref[0, c0 - OFF_RK:c0 - OFF_RK + SEG, :] = a.T.astype(kt_ref.dtype)
        elif c0 < OFF_RG:
            pass
        elif c0 < OFF_GA:
            a = a * _sigmoid(a)
        else:
            a = _sigmoid(a)
        dense_ref[0, :, c0:c0 + SEG] = a.astype(dense_ref.dtype)

    for g in range(N_GROUPS):
        hg = h_class[g][...] if g in h_class else h
        wq = wqk_ref[:, g * SEG:(g + 1) * SEG]
        emit_group(g, 0, att_rope(jnp.dot(hg, wq, preferred_element_type=F32), att_tabs[g]) * ATTN_SCALE)
        emit_group(g, 1, att_rope(jnp.dot(hg, wk_cols(g), preferred_element_type=F32), att_tabs[g]))
        emit_group(g, 2, jnp.dot(hg, wv_cols(g), preferred_element_type=F32))

    if win_refs:
        order = sorted(range(N_GROUPS), key=lambda g: win_first[g])

        def guarded(pos):
            if pos == len(order):
                return
            g = order[pos]

            @pl.when(i >= win_first[g])
            def _():
                emit_windows(g)
                guarded(pos + 1)

        guarded(0)


def _class_major_rows(tab, tm, d):
    if d == 1 or tab.shape[0] == 1:
        return tab
    r = tab.shape[0]
    return tab.reshape(r // tm, tm // d, d, LANES).transpose(0, 2, 1, 3).reshape(r, LANES)


def _in_projection(x, mod, w_main, w_qk, tables, *, tm, dils, paired, keys_t, win_rows, out_dtype):
    G, R, _ = x.shape
    n_tiles = R // tm
    mod_blk = tm if mod.shape[1] == R else 1
    tab_blk = tm if tables[0].shape[0] == R else 1

    def mod_spec(col):
        if mod_blk == 1:
            return pl.BlockSpec((1, 1, D_MODEL), lambda b, i: (b, 0, col))
        return pl.BlockSpec((1, tm, D_MODEL), lambda b, i: (b, i, col))

    def tab_spec(tab):
        if tab.shape[0] == 1:
            return pl.BlockSpec((1, LANES), lambda b, i: (0, 0))
        return pl.BlockSpec((tm, LANES), lambda b, i: (i, 0))

    att = list(tables[:3])
    if paired:
        att[2] = np.zeros((1, LANES), np.float32)
    all_tabs = []
    for d in tuple(dils) + (1,):
        all_tabs += [_class_major_rows(t, tm, d) for t in att]
    all_tabs += list(tables[3:])

    out_shapes = [jax.ShapeDtypeStruct((G, R, DENSE_WIDTH), out_dtype)]
    out_specs = [pl.BlockSpec((1, tm, DENSE_WIDTH), lambda b, i: (b, i, 0))]
    for d in dils:
        out_shapes.append(jax.ShapeDtypeStruct((G, d, R // d, ATTN_WIDTH), out_dtype))
        out_specs.append(pl.BlockSpec((1, d, tm // d, ATTN_WIDTH), lambda b, i: (b, 0, i, 0)))
    if keys_t:
        out_shapes.append(jax.ShapeDtypeStruct((G, RET_QK_WIDTH, R), out_dtype))
        out_specs.append(pl.BlockSpec((1, RET_QK_WIDTH, tm), lambda b, i: (b, 0, i)))
    win_blocks = win_first = ()
    if win_rows:
        win_blocks = tuple(min(w, tm) for w in win_rows)
        win_first = tuple(n_tiles - w // wb for w, wb in zip(win_rows, win_blocks))
        for g in range(N_GROUPS):
            first = win_first[g]
            for _ in range(2):
                out_shapes.append(jax.ShapeDtypeStruct((G, GROUP_WIDTH, win_rows[g]), F32))
                out_specs.append(pl.BlockSpec((1, GROUP_WIDTH, win_blocks[g]),
                                              lambda b, i, first=first: (b, 0, jnp.maximum(i - first, 0))))

    scratch = []
    if any(d > 1 for d in dils):
        scratch.append(pltpu.VMEM((D_MODEL // LANES, tm, LANES), F32))
        scratch += [pltpu.VMEM((tm, D_MODEL), BF16) for d in dils if d > 1]
    kern = functools.partial(_inproj_kernel, tm=tm, dils=dils, paired=paired, keys_t=keys_t,
                             win_blocks=win_blocks, win_first=win_first)
    return pl.pallas_call(
        kern,
        grid=(G, n_tiles),
        in_specs=[pl.BlockSpec((1, tm, D_MODEL), lambda b, i: (b, i, 0)),
                  mod_spec(0), mod_spec(1),
                  _resident(w_main.shape), _resident(w_qk.shape)] + [tab_spec(t) for t in all_tabs],
        out_specs=out_specs,
        out_shape=out_shapes,
        scratch_shapes=scratch,
        compiler_params=_cparams(("arbitrary", "arbitrary")),
        name="in_projection",
    )(x, mod, mod, w_main, w_qk, *all_tabs)


def _attn_kernel(q_ref, k_ref, v_ref, kp_ref, vp_ref, o_ref, lse_ref, *, tl, nc):
    t = pl.program_id(2)
    row = lax.broadcasted_iota(jnp.int32, (ATTN_BAND, ATTN_BAND), 0)
    col = lax.broadcasted_iota(jnp.int32, (ATTN_BAND, ATTN_BAND), 1)
    lane = lax.broadcasted_iota(jnp.int32, (1, GROUP_WIDTH), 1)
    cur_ok = col <= row
    qk_lanes = [(lane % LANES) // PAIR_SLOT == h for h in range(HEADS_PER_GROUP)]
    v_lanes = [lane // HEAD_DIM == h for h in range(HEADS_PER_GROUP)]

    def block(q, k_prev, v_prev, k_cur, v_cur, prev_shift):
        prev_ok = col >= row + prev_shift
        mask = jnp.concatenate([prev_ok, cur_ok], axis=1)
        mask = jnp.concatenate([mask] * HEADS_PER_GROUP, axis=0)
        kk = jnp.concatenate([k_prev, k_cur], axis=0)
        vv = jnp.concatenate([v_prev, v_cur], axis=0)
        qs = jnp.concatenate([jnp.where(sel, q, jnp.zeros_like(q)) for sel in qk_lanes], axis=0)
        s = lax.dot_general(qs, kk, (((1,), (1,)), ((), ())), preferred_element_type=F32)
        s = jnp.where(mask, s, NEG_BIG)
        m = jnp.max(s, axis=1, keepdims=True)
        e = jnp.exp(s - m)
        l = jnp.sum(e, axis=1, keepdims=True)
        oh = jnp.dot((e / l).astype(BF16), vv, preferred_element_type=F32)
        lse = m + jnp.log(l)
        o = jnp.zeros((ATTN_BAND, GROUP_WIDTH), F32)
        lo = jnp.zeros((ATTN_BAND, GROUP_WIDTH), F32)
        for h, sel in enumerate(v_lanes):
            rows = slice(h * ATTN_BAND, (h + 1) * ATTN_BAND)
            o = jnp.where(sel, oh[rows], o)
            lo = jnp.where(sel, lse[rows], lo)
        return o, lo

    first = jnp.where(t > 0, 0, ATTN_BAND)
    for c in range(nc):
        for j in range(tl // ATTN_BAND):
            j0, jp = j * ATTN_BAND, (j - 1) * ATTN_BAND
            rows = slice(j0, j0 + ATTN_BAND)
            if j == 0:
                k_prev, v_prev, shift = kp_ref[0, c], vp_ref[0, c], first
            else:
                k_prev, v_prev, shift = k_ref[0, c, jp:j0, :], v_ref[0, c, jp:j0, :], 0
            o, lse = block(q_ref[0, c, rows, :], k_prev, v_prev, k_ref[0, c, rows, :], v_ref[0, c, rows, :], shift)
            o_ref[0, c, rows, :] = o.astype(o_ref.dtype)
            lse_ref[0, c, rows, :] = lse


def _prompt_attention(qkv, g, *, tl, nc):
    B, dil, L, _ = qkv.shape
    sub = tl // ATTN_BAND

    def cur(c):
        return pl.BlockSpec((1, nc, tl, SEG), lambda b, r, t: (b, r, t, c))

    def prev(c):
        return pl.BlockSpec((1, nc, ATTN_BAND, SEG), lambda b, r, t: (b, r, jnp.maximum(t * sub - 1, 0), c))

    out_spec = pl.BlockSpec((1, nc, tl, SEG), lambda b, r, t: (b, r, t, 0))
    return pl.pallas_call(
        functools.partial(_attn_kernel, tl=tl, nc=nc),
        grid=(B, dil // nc, L // tl),
        in_specs=[cur(0), cur(1), cur(2), prev(1), prev(2)],
        out_specs=[out_spec, out_spec],
        out_shape=[jax.ShapeDtypeStruct((B, dil, L, SEG), BF16),
                   jax.ShapeDtypeStruct((B, dil, L, SEG), F32)],
        compiler_params=_cparams(("arbitrary", "arbitrary", "arbitrary")),
        name="prompt_attention_g%d" % g,
    )(qkv, qkv, qkv, qkv, qkv)


def _retention_tables():
    f32 = np.float32
    C = RET_CHUNK
    lg = np.log(f32(1.0) - f32(2.0) ** (f32(-5.0) - np.arange(RET_HEADS, dtype=f32)))
    idx = np.arange(C, dtype=f32)
    diff = idx[:, None] - idx[None, :]
    decay_in = np.where(diff >= 0, np.exp(lg[:, None, None] * np.maximum(diff, f32(0.0))), f32(0.0))
    decay_q = np.exp(lg[:, None] * (idx[None, :] + f32(1.0)))[:, :, None]
    decay_k = np.exp(lg[:, None] * (f32(C - 1.0) - idx[None, :]))[:, None, :]
    return tuple(t.astype(f32) for t in (decay_in, decay_q, decay_k))


def _retention_kernel(qk_ref, kt_ref, v_ref, g_ref, din_ref, dq_ref, dk_ref, o_ref, state_ref, r_scr, *, tr):
    t = pl.program_id(1)
    n_chunks = tr // RET_CHUNK

    @pl.when(t == 0)
    def _():
        r_scr[...] = jnp.zeros_like(r_scr)

    def key_t(c, h):
        return kt_ref[0, h * RET_QK_DIM:(h + 1) * RET_QK_DIM, c * RET_CHUNK:(c + 1) * RET_CHUNK]

    def val(c, h):
        return v_ref[0, c * RET_CHUNK:(c + 1) * RET_CHUNK, h * RET_V_DIM:(h + 1) * RET_V_DIM]

    states = [[None] * RET_HEADS for _ in range(n_chunks)]
    for h in range(RET_HEADS):
        r = r_scr[h]
        for c in range(n_chunks):
            states[c][h] = r
            kd = (key_t(c, h).astype(F32) * dk_ref[h]).astype(BF16)
            r = r * (_ret_gamma(h) ** RET_CHUNK) + jnp.dot(kd, val(c, h), preferred_element_type=F32)
        r_scr[h] = r

    for c in range(n_chunks):
        rows = slice(c * RET_CHUNK, (c + 1) * RET_CHUNK)
        for h in range(RET_HEADS):
            q = qk_ref[0, rows, OFF_RQ + h * RET_QK_DIM:OFF_RQ + (h + 1) * RET_QK_DIM]
            v = val(c, h)
            s = jnp.dot(q, key_t(c, h), preferred_element_type=F32) * din_ref[h]
            o = (jnp.dot(s.astype(BF16), v, preferred_element_type=F32)
                 + jnp.dot(q, states[c][h].astype(BF16), preferred_element_type=F32) * dq_ref[h])
            mu = jnp.mean(o, axis=1, keepdims=True)
            d = o - mu
            var = jnp.mean(d * d, axis=1, keepdims=True)
            gn = d * lax.rsqrt(var + GN_EPS)
            gate = g_ref[0, rows, h * RET_V_DIM:(h + 1) * RET_V_DIM].astype(F32)
            o_ref[0, rows, h * RET_V_DIM:(h + 1) * RET_V_DIM] = (gate * gn).astype(o_ref.dtype)

    @pl.when(t == pl.num_programs(1) - 1)
    def _():
        state_ref[0] = r_scr[...]


def _prompt_retention(dense, keys_t, *, tr):
    B, S, _ = dense.shape
    tabs = _retention_tables()
    blk = RET_V_WIDTH

    def colblk(c):
        return pl.BlockSpec((1, tr, blk), lambda b, t: (b, t, c))

    return pl.pallas_call(
        functools.partial(_retention_kernel, tr=tr),
        grid=(B, S // tr),
        in_specs=[pl.BlockSpec((1, tr, RET_QK_WIDTH), lambda b, t: (b, t, OFF_RQ // RET_QK_WIDTH)),
                  pl.BlockSpec((1, RET_QK_WIDTH, tr), lambda b, t: (b, 0, t)),
                  colblk(OFF_RV // blk), colblk(OFF_RG // blk)]
                 + [pl.BlockSpec(tab.shape, lambda b, t: (0, 0, 0)) for tab in tabs],
        out_specs=[pl.BlockSpec((1, tr, RET_V_WIDTH), lambda b, t: (b, t, 0)),
                   pl.BlockSpec((1, RET_HEADS, RET_QK_DIM, RET_V_DIM), lambda b, t: (b, 0, 0, 0))],
        out_shape=[jax.ShapeDtypeStruct((B, S, RET_V_WIDTH), BF16),
                   jax.ShapeDtypeStruct((B, RET_HEADS, RET_QK_DIM, RET_V_DIM), F32)],
        scratch_shapes=[pltpu.VMEM((RET_HEADS, RET_QK_DIM, RET_V_DIM), F32)],
        compiler_params=_cparams(("arbitrary", "arbitrary")),
        name="prompt_retention",
    )(dense, keys_t, dense, dense, *tabs)


def _layer_norm(u, g, b):
    mu = jnp.mean(u, axis=1, keepdims=True)
    d = u - mu
    var = jnp.mean(d * d, axis=1, keepdims=True)
    return d * lax.rsqrt(var + LN_EPS) * g + b


def _merge_ffn_kernel(x_ref, ga_ref, gb_ref, ret_ref, o0_ref, o1_ref, o2_ref, l0_ref, l1_ref, l2_ref,
                      gate1_ref, shift2_ref, scale2_ref, gate2_ref,
                      watt_ref, wret_ref, wo_ref, wfi_ref, wfo_ref,
                      ln1g_ref, ln1b_ref, ln2g_ref, ln2b_ref, y_ref, *scratch, tm, sub, dils):
    n_sub = tm // sub
    per_sub = 1 + 2 * N_GROUPS
    for s in range(n_sub):
        rows = slice(s * sub, (s + 1) * sub)
        act_scr = scratch[s * per_sub]
        stage_refs = scratch[s * per_sub + 1:(s + 1) * per_sub]

        def natural(ref, g, stage_ref):
            d = dils[g]
            n = sub // d
            if d == 1:
                return ref[0, 0, rows, :].astype(F32)
            for r in range(d):
                for u in range(GROUP_WIDTH // LANES):
                    stage_ref[u, pl.ds(r, n, stride=d), :] = (
                        ref[0, r, s * n:(s + 1) * n, u * LANES:(u + 1) * LANES].astype(F32))
            return jnp.concatenate([stage_ref[u] for u in range(GROUP_WIDTH // LANES)], axis=1)

        def mod_rows(ref):
            return ref[0] if ref.shape[1] == 1 else ref[0, rows, :]

        lses = [natural(ref, g, stage_refs[g]) for g, ref in enumerate((l0_ref, l1_ref, l2_ref))]
        outs = [natural(ref, g, stage_refs[N_GROUPS + g]) for g, ref in enumerate((o0_ref, o1_ref, o2_ref))]
        m = jnp.maximum(jnp.maximum(lses[0], lses[1]), lses[2])
        es = [jnp.exp(l - m) for l in lses]
        att = (es[0] * outs[0] + es[1] * outs[1] + es[2] * outs[2]) / (es[0] + es[1] + es[2])
        y_att = jnp.dot(att.astype(BF16), watt_ref[...], preferred_element_type=F32)
        y_ret = jnp.dot(ret_ref[0, rows, :].astype(BF16), wret_ref[...], preferred_element_type=F32)
        mix = (ga_ref[0, rows, :].astype(F32) * y_att + gb_ref[0, rows, :].astype(F32) * y_ret).astype(BF16)
        mixed = jnp.dot(mix, wo_ref[...], preferred_element_type=F32)
        x1 = _layer_norm(DEEPNORM_ALPHA * x_ref[0, rows, :] + mod_rows(gate1_ref) * mixed,
                         ln1g_ref[...], ln1b_ref[...])
        h2 = (x1 * (1.0 + mod_rows(scale2_ref)) + mod_rows(shift2_ref)).astype(BF16)
        for c0 in range(0, D_FF, FFN_CHUNK):
            c1 = min(c0 + FFN_CHUNK, D_FF)
            fg = jnp.dot(h2, wfi_ref[:, c0:c1], preferred_element_type=F32)
            fu = jnp.dot(h2, wfi_ref[:, D_FF + c0:D_FF + c1], preferred_element_type=F32)
            act_scr[:, c0:c1] = (fg * _sigmoid(fg) * fu).astype(BF16)
        ff = jnp.dot(act_scr[...], wfo_ref[...], preferred_element_type=F32)
        y_ref[0, rows, :] = _layer_norm(DEEPNORM_ALPHA * x1 + mod_rows(gate2_ref) * ff, ln2g_ref[...], ln2b_ref[...])


def _merge_ffn(x, dense, ret, o_groups, lse_groups, mod, weights, ln, *, tm, sub, dils):
    G, R, _ = x.shape
    mod_rows = mod.shape[1]
    sub_scratch = [pltpu.VMEM((sub, D_FF), BF16)] + [pltpu.VMEM((GROUP_WIDTH // LANES, sub, LANES), F32)] * (2 * N_GROUPS)

    def mod_spec(col):
        if mod_rows == 1:
            return pl.BlockSpec((1, 1, D_MODEL), lambda b, i: (b, 0, col))
        return pl.BlockSpec((1, tm, D_MODEL), lambda b, i: (b, i, col))

    def rows(width, col=0):
        return pl.BlockSpec((1, tm, width), lambda b, i: (b, i, col))

    group_specs = [pl.BlockSpec((1, d, tm // d, GROUP_WIDTH), lambda b, i: (b, 0, i, 0)) for d in dils]
    vec = pl.BlockSpec((1, D_MODEL), lambda b, i: (0, 0))
    return pl.pallas_call(
        functools.partial(_merge_ffn_kernel, tm=tm, sub=sub, dils=dils),
        grid=(G, R // tm),
        in_specs=[rows(D_MODEL), rows(D_MODEL, OFF_GA // D_MODEL), rows(D_MODEL, OFF_GB // D_MODEL),
                  rows(RET_V_WIDTH)] + group_specs + group_specs
                 + [mod_spec(2), mod_spec(3), mod_spec(4), mod_spec(5)]
                 + [_resident(w.shape) for w in weights] + [vec] * 4,
        out_specs=rows(D_MODEL),
        out_shape=jax.ShapeDtypeStruct((G, R, D_MODEL), F32),
        scratch_shapes=sub_scratch * (tm // sub),
        compiler_params=_cparams(("arbitrary", "arbitrary")),
        name="merge_ffn",
    )(x, dense, dense, ret, *o_groups, *lse_groups, mod, mod, mod, mod, *weights, *ln)


def _column(row):
    return jnp.broadcast_to(row, (LANES, LANES)).T


def _sample_mixer_kernel(dense_ref, qkv0_ref, qkv1_ref, qkv2_ref, ck0_ref, cv0_ref, ck1_ref, cv1_ref,
                         ck2_ref, cv2_ref, state_ref, o_ref, lse_ref, ret_ref, nstate_ref,
                         nk0_ref, nv0_ref, nk1_ref, nv1_ref, nk2_ref, nv2_ref):
    pad = 16
    row_h = lax.broadcasted_iota(jnp.int32, (pad, GROUP_WIDTH), 0)
    lane_h = lax.broadcasted_iota(jnp.int32, (pad, GROUP_WIDTH), 1) // HEAD_DIM
    own = row_h == lane_h
    lane = lax.broadcasted_iota(jnp.int32, (GROUP_WIDTH, LANES), 1)

    def rolled(c_ref, n_ref, new_row):
        w = c_ref.shape[2]
        new_col = jnp.concatenate([_column(new_row[:, u * LANES:(u + 1) * LANES])
                                   for u in range(GROUP_WIDTH // LANES)], axis=0)
        nblk = w // LANES
        cur = pltpu.roll(c_ref[0, :, 0:LANES], LANES - 1, 1)
        for j in range(nblk):
            if j + 1 < nblk:
                nxt = pltpu.roll(c_ref[0, :, (j + 1) * LANES:(j + 2) * LANES], LANES - 1, 1)
            else:
                nxt = new_col
            n_ref[0, :, j * LANES:(j + 1) * LANES] = jnp.where(lane < LANES - 1, cur, nxt)
            cur = nxt

    groups = ((qkv0_ref, ck0_ref, cv0_ref, nk0_ref, nv0_ref),
              (qkv1_ref, ck1_ref, cv1_ref, nk1_ref, nv1_ref),
              (qkv2_ref, ck2_ref, cv2_ref, nk2_ref, nv2_ref))
    for g, (qkv_ref, ck_ref, cv_ref, nk_ref, nv_ref) in enumerate(groups):
        dil = DILATED_PATTERNS[g][1]
        w = ck_ref.shape[2]
        q = qkv_ref[0, :, 0:SEG]
        kn = qkv_ref[0, :, SEG:2 * SEG]
        vn = qkv_ref[0, :, 2 * SEG:3 * SEG]
        qsel = jnp.where(own, jnp.broadcast_to(q, (pad, GROUP_WIDTH)), 0.0)
        s = jnp.dot(qsel.astype(BF16), ck_ref[0].astype(BF16), preferred_element_type=F32)
        pos = lax.broadcasted_iota(jnp.int32, (pad, w), 1)
        s = jnp.where((pos & (dil - 1)) == 0, s, NEG_BIG)
        sn = jnp.sum(qsel * kn, axis=1, keepdims=True)
        m = jnp.maximum(jnp.max(s, axis=1, keepdims=True), sn)
        e = jnp.exp(s - m)
        en = jnp.exp(sn - m)
        l = jnp.sum(e, axis=1, keepdims=True) + en
        o_all = lax.dot_general((e / l).astype(BF16), cv_ref[0].astype(BF16), (((1,), (1,)), ((), ())),
                                preferred_element_type=F32) + (en / l) * vn
        o_ref[0, :, g * SEG:(g + 1) * SEG] = jnp.sum(jnp.where(own, o_all, 0.0), axis=0, keepdims=True)
        lse_ref[0, :, g * SEG:(g + 1) * SEG] = jnp.sum(jnp.where(own, m + jnp.log(l), 0.0), axis=0, keepdims=True)
        rolled(ck_ref, nk_ref, kn)
        rolled(cv_ref, nv_ref, vn)

    for h in range(RET_HEADS):
        gamma = _ret_gamma(h)
        q = dense_ref[0, :, OFF_RQ + h * RET_QK_DIM:OFF_RQ + (h + 1) * RET_QK_DIM]
        k = dense_ref[0, :, OFF_RK + h * RET_QK_DIM:OFF_RK + (h + 1) * RET_QK_DIM]
        v = dense_ref[0, :, OFF_RV + h * RET_V_DIM:OFF_RV + (h + 1) * RET_V_DIM]
        gate = dense_ref[0, :, OFF_RG + h * RET_V_DIM:OFF_RG + (h + 1) * RET_V_DIM]
        r_old = state_ref[0, h]
        q_col = jnp.concatenate([_column(q)] * (RET_V_DIM // LANES), axis=1)
        k_col = jnp.concatenate([_column(k)] * (RET_V_DIM // LANES), axis=1)
        qk = jnp.sum(q * k, axis=1, keepdims=True)
        o = qk * v + jnp.sum(q_col * r_old, axis=0, keepdims=True) * gamma
        nstate_ref[0, h] = r_old * gamma + k_col * v
        mu = jnp.mean(o, axis=1, keepdims=True)
        d = o - mu
        var = jnp.mean(d * d, axis=1, keepdims=True)
        ret_ref[0, :, h * RET_V_DIM:(h + 1) * RET_V_DIM] = gate * (d * lax.rsqrt(var + GN_EPS))


def _sample_mixers(dense, qkvs, caches, state):
    Bs = dense.shape[0]
    row = lambda w: pl.BlockSpec((1, 1, w), lambda b: (b, 0, 0))
    cache_specs = [pl.BlockSpec((1,) + a.shape[1:], lambda b: (b, 0, 0)) for a in caches]
    st = pl.BlockSpec((1, RET_HEADS, RET_QK_DIM, RET_V_DIM), lambda b: (b, 0, 0, 0))
    return pl.pallas_call(
        _sample_mixer_kernel,
        grid=(Bs,),
        in_specs=[row(DENSE_WIDTH)] + [row(ATTN_WIDTH)] * N_GROUPS + cache_specs + [st],
        out_specs=[row(ATTN_WIDTH), row(ATTN_WIDTH), row(RET_V_WIDTH), st] + cache_specs,
        out_shape=[jax.ShapeDtypeStruct((Bs, 1, ATTN_WIDTH), F32),
                   jax.ShapeDtypeStruct((Bs, 1, ATTN_WIDTH), F32),
                   jax.ShapeDtypeStruct((Bs, 1, RET_V_WIDTH), F32),
                   jax.ShapeDtypeStruct(state.shape, F32)]
                  + [jax.ShapeDtypeStruct(a.shape, F32) for a in caches],
        compiler_params=_cparams(("arbitrary",)),
        name="sample_mixers",
    )(dense, *qkvs, *caches, state)


def _feature_major(a):
    _, b, w, _, _ = a.shape
    return jnp.transpose(a[0], (0, 2, 3, 1)).reshape(b, GROUP_WIDTH, w)


def _row_major(a):
    b, _, w = a.shape
    return jnp.transpose(a.reshape(b, HEADS_PER_GROUP, HEAD_DIM, w), (0, 3, 1, 2))[None]


def kernel(x_prompt, x_sample, cache_k_w128, cache_v_w128, cache_k_w512, cache_v_w512, cache_k_w2048,
           cache_v_w2048, state_retention, c_prompt, c_sample, w_ada, b_ada, w_in, w_att_out, w_ret_out,
           w_o, ln1_g, ln1_b, w_ffn_in, w_ffn_out, ln2_g, ln2_b):
    assert w_in.shape[0] == 1 and x_sample.shape[1] == 1
    B, S, D = x_prompt.shape
    Bs = x_sample.shape[0]
    dils = tuple(d for _, d in DILATED_PATTERNS)

    w_main = jnp.concatenate([w_in[0][:, W_IN_ROTATE:], w_in[0][:, 2 * ATTN_WIDTH:W_IN_ROTATE]], axis=1).astype(BF16)
    w_qk = w_in[0][:, :2 * ATTN_WIDTH].astype(BF16)
    pair_cols = jnp.asarray([s * SEG + f for s in range(2 * N_GROUPS) for f in _pair_source()], jnp.int32)
    w_qk_paired = w_qk[:, pair_cols]
    weights = tuple(w[0].astype(BF16) for w in (w_att_out, w_ret_out, w_o, w_ffn_in, w_ffn_out))
    ln = (ln1_g, ln1_b, ln2_g, ln2_b)

    mod = _modulation(jnp.concatenate([c_prompt, c_sample], axis=0), w_ada[0], b_ada[0])
    mod_p = mod[:B].reshape(B, 1, 6 * D)
    mod_s = mod[B:].reshape(1, Bs, 6 * D)

    tabs_p = _rope_tables(np.arange(S), paired=True)
    win_rows = tuple(min(w, S) for w, _ in DILATED_PATTERNS)
    dense_p, *rest = _in_projection(x_prompt, mod_p, w_main, w_qk_paired, tabs_p, tm=512, dils=dils, paired=True,
                                    keys_t=True, win_rows=win_rows, out_dtype=BF16)
    qkv_p, keys_t_p, wins_p = rest[:N_GROUPS], rest[N_GROUPS], rest[N_GROUPS + 1:]
    o_p, lse_p = [], []
    for g, d in enumerate(dils):
        tl = min(S // d, 1024)
        o, lse = _prompt_attention(qkv_p[g], g, tl=tl, nc=min(d, 1024 // tl))
        o_p.append(o)
        lse_p.append(lse)
    ret_p, state_p = _prompt_retention(dense_p, keys_t_p, tr=512)
    y_p = _merge_ffn(x_prompt, dense_p, ret_p, o_p, lse_p, mod_p, weights, ln, tm=512, sub=256, dils=dils)

    xs = x_sample.reshape(1, Bs, D)
    tabs_s = _rope_tables(PAST_LEN + np.arange(1))
    unit = (1,) * N_GROUPS
    dense_s, *qkv_s = _in_projection(xs, mod_s, w_main, w_qk, tabs_s, tm=Bs, dils=unit, paired=False,
                                     keys_t=False, win_rows=None, out_dtype=F32)
    caches = [_feature_major(c) for c in (cache_k_w128, cache_v_w128, cache_k_w512, cache_v_w512,
                                          cache_k_w2048, cache_v_w2048)]
    o_s, lse_s, ret_s, state_s, *rolled = _sample_mixers(
        dense_s.reshape(Bs, 1, DENSE_WIDTH), [a.reshape(Bs, 1, ATTN_WIDTH) for a in qkv_s], caches,
        state_retention[0])
    o_s = o_s.reshape(1, 1, Bs, ATTN_WIDTH)
    lse_s = lse_s.reshape(1, 1, Bs, ATTN_WIDTH)
    o_sg = [o_s[..., g * SEG:(g + 1) * SEG] for g in range(N_GROUPS)]
    lse_sg = [lse_s[..., g * SEG:(g + 1) * SEG] for g in range(N_GROUPS)]
    y_s = _merge_ffn(xs, dense_s, ret_s.reshape(1, Bs, RET_V_WIDTH), o_sg, lse_sg, mod_s, weights, ln,
                     tm=Bs, sub=Bs, dils=unit)

    return (y_p, y_s.reshape(Bs, 1, D),
            *[_row_major(w) for w in wins_p], state_p[None],
            *[_row_major(r) for r in rolled], state_s[None])
```

```python
import functools
import math

import jax
import jax.numpy as jnp
import numpy as np
from jax import lax
from jax.experimental import pallas as pl
from jax.experimental.pallas import tpu as pltpu

F32 = jnp.float32
BF16 = jnp.bfloat16

D_MODEL = 1024
PAST_LEN = 16384
DILATED_PATTERNS = ((128, 1), (512, 4), (2048, 16))
N_GROUPS = 3
HEADS_PER_GROUP = 4
HEAD_DIM = 64
GROUP_WIDTH = HEADS_PER_GROUP * HEAD_DIM
ATTN_WIDTH = N_GROUPS * GROUP_WIDTH
ATTN_BAND = 128
ATTN_SCALE = HEAD_DIM ** -0.5
ROPE_DIM = HEAD_DIM // 4
ROPE_THETA = 500000.0
RET_HEADS = 4
RET_QK_DIM = 128
RET_V_DIM = 256
RET_QK_WIDTH = RET_HEADS * RET_QK_DIM
RET_V_WIDTH = RET_HEADS * RET_V_DIM
RET_CHUNK = 128
RET_ROPE_THETA = 10000.0
IN_WIDTH = 3 * ATTN_WIDTH + 2 * RET_QK_WIDTH + 2 * RET_V_WIDTH + 2 * D_MODEL
D_FF = 2816
DEEPNORM_ALPHA = 2.0 ** 0.25
LN_EPS = 1e-5
GN_EPS = 1e-6

LANES = 128
SEG = 256
N_SEG = IN_WIDTH // SEG
FFN_CHUNK = 256
OFF_RQ, OFF_RK, OFF_RV, OFF_RG, OFF_GA, OFF_GB = 0, 512, 1024, 2048, 3072, 4096
OFF_QA, OFF_KA, OFF_VA = 5120, 5888, 6656
DENSE_WIDTH = OFF_QA
W_IN_ROTATE = 3 * ATTN_WIDTH
NEG_BIG = -1e30
VMEM_LIMIT = 60 * 1024 * 1024


def _cparams(sem):
    return pltpu.CompilerParams(dimension_semantics=sem, vmem_limit_bytes=VMEM_LIMIT)


def _resident(shape):
    nd = len(shape)
    return pl.BlockSpec(shape, lambda *_: (0,) * nd, pipeline_mode=pl.Buffered(1))


def _sigmoid(x):
    return 1.0 / (1.0 + jnp.exp(-x))


def _ret_gamma(h):
    return 1.0 - 2.0 ** (-5.0 - h)


def _ada_kernel(c_ref, w_ref, b_ref, o_ref):
    c = c_ref[...]
    a = (c * _sigmoid(c)).astype(BF16)
    o_ref[...] = jnp.dot(a, w_ref[...].astype(BF16), preferred_element_type=F32) + b_ref[...]


def _modulation(c_all, w_ada, b_ada):
    rows = c_all.shape[0]
    tn = 1024
    return pl.pallas_call(
        _ada_kernel,
        grid=(6 * D_MODEL // tn,),
        in_specs=[pl.BlockSpec((rows, D_MODEL), lambda j: (0, 0)),
                  pl.BlockSpec((D_MODEL, tn), lambda j: (0, j)),
                  pl.BlockSpec((1, tn), lambda j: (0, j))],
        out_specs=pl.BlockSpec((rows, tn), lambda j: (0, j)),
        out_shape=jax.ShapeDtypeStruct((rows, 6 * D_MODEL), F32),
        compiler_params=_cparams(("arbitrary",)),
        name="modulation",
    )(c_all, w_ada, b_ada.reshape(1, -1))


PAIR_SLOT = HEAD_DIM // 2


def _pair_source():
    src = []
    for half_dims in ((0, 16), (8, 40)):
        for h in range(HEADS_PER_GROUP):
            rot, rest = half_dims
            dims = list(range(rot, rot + ROPE_DIM // 2)) + list(range(rest, rest + PAIR_SLOT - ROPE_DIM // 2))
            src += [h * HEAD_DIM + d for d in dims]
    return src


def _rope_tables(pos, paired=False):
    posf = jnp.asarray(np.asarray(pos), jnp.int32).astype(F32)[:, None]
    lane = np.arange(LANES)
    half = ROPE_DIM // 2
    inv = jnp.exp(-math.log(ROPE_THETA) * jnp.arange(half, dtype=F32) * (2.0 / ROPE_DIM))
    ang = posf * inv[None, :]
    cos, sin = jnp.cos(ang), jnp.sin(ang)
    if paired:
        in_slot = lane % PAIR_SLOT
        f = in_slot % half
        att_c = jnp.where(in_slot < half, cos[:, f], 1.0)
        att_lo = jnp.where(in_slot < half, sin[:, f], 0.0)
        att_hi = att_lo
    else:
        in_head = lane % HEAD_DIM
        f = in_head % half
        att_c = jnp.where(in_head < ROPE_DIM, cos[:, f], 1.0)
        att_lo = jnp.where(in_head < half, -sin[:, f], 0.0)
        att_hi = jnp.where((in_head >= half) & (in_head < ROPE_DIM), sin[:, f], 0.0)
    rhalf = RET_QK_DIM // 2
    rinv = jnp.exp(-math.log(RET_ROPE_THETA) * jnp.arange(rhalf, dtype=F32) * (2.0 / RET_QK_DIM))
    rang = posf * rinv[None, :]
    rcos, rsin = jnp.cos(rang), jnp.sin(rang)
    ret_c = jnp.concatenate([rcos, rcos], axis=-1)
    ret_s = jnp.concatenate([-rsin, rsin], axis=-1)
    return att_c, att_lo, att_hi, ret_c, ret_s


def _inproj_kernel(x_ref, shift_ref, scale_ref, w_ref, wqk_ref, *refs, tm, dils, paired, keys_t, win_blocks,
                   win_first):
    n_tab = 3 * (N_GROUPS + 1) + 2
    tabs = [r[...] for r in refs[:n_tab]]
    att_tabs = [tabs[3 * g:3 * g + 3] for g in range(N_GROUPS + 1)]
    rc, rs = tabs[-2:]
    refs = refs[n_tab:]
    n_perm = sum(d > 1 for d in dils)
    dense_ref = refs[0]
    qkv_refs = refs[1:1 + N_GROUPS]
    n_out = len(refs) - (1 + n_perm if n_perm else 0)
    kt_ref = refs[1 + N_GROUPS] if keys_t else None
    win_refs = refs[1 + N_GROUPS + bool(keys_t):n_out]
    i = pl.program_id(1)
    hf = x_ref[0] * (1.0 + scale_ref[0]) + shift_ref[0]
    h = hf.astype(BF16)

    h_class = {}
    if n_perm:
        stage_ref = refs[n_out]
        for u in range(D_MODEL // LANES):
            stage_ref[u] = hf[:, u * LANES:(u + 1) * LANES]
        for g, hp_ref in zip([g for g in range(N_GROUPS) if dils[g] > 1], refs[n_out + 1:]):
            d = dils[g]
            n = tm // d
            for r in range(d):
                for u in range(D_MODEL // LANES):
                    hp_ref[r * n:(r + 1) * n, u * LANES:(u + 1) * LANES] = (
                        stage_ref[u, pl.ds(r, n, stride=d), :].astype(BF16))
            h_class[g] = hp_ref

    def att_rope(a, tab, rows=None):
        ac, alo, ahi = [t if rows is None or t.shape[0] == 1 else t[rows:, :] for t in tab]
        if paired:
            lo, hi = a[:, :LANES], a[:, LANES:]
            return jnp.concatenate([lo * ac - hi * alo, hi * ac + lo * alo], axis=1)
        parts = []
        for u in range(SEG // LANES):
            xs = a[:, u * LANES:(u + 1) * LANES]
            parts.append(xs * ac + pltpu.roll(xs, LANES - ROPE_DIM // 2, 1) * alo
                         + pltpu.roll(xs, ROPE_DIM // 2, 1) * ahi)
        return jnp.concatenate(parts, axis=1)

    def ret_rope(a):
        parts = []
        for u in range(SEG // LANES):
            xs = a[:, u * LANES:(u + 1) * LANES]
            parts.append(xs * rc + pltpu.roll(xs, RET_QK_DIM // 2, 1) * rs)
        return jnp.concatenate(parts, axis=1)

    def emit_group(g, c, a):
        ref, d = qkv_refs[g], dils[g]
        n = tm // d
        for r in range(d):
            ref[0, r, :, c * SEG:(c + 1) * SEG] = a[r * n:(r + 1) * n, :].astype(ref.dtype)

    def wk_cols(g):
        return wqk_ref[:, ATTN_WIDTH + g * SEG:ATTN_WIDTH + (g + 1) * SEG]

    def wv_cols(g):
        return w_ref[:, DENSE_WIDTH + g * SEG:DENSE_WIDTH + (g + 1) * SEG]

    def emit_windows(g):
        wb = win_blocks[g]
        hw = h[tm - wb:, :]
        at = att_rope(jnp.dot(hw, wk_cols(g), preferred_element_type=F32), att_tabs[N_GROUPS], tm - wb).T
        if paired:
            dst = {f: lane for lane, f in enumerate(_pair_source())}
            at = jnp.concatenate([at[dst[f]:dst[f] + 8, :] for f in range(0, GROUP_WIDTH, 8)], axis=0)
        win_refs[2 * g][0] = at
        win_refs[2 * g + 1][0] = jnp.dot(hw, wv_cols(g), preferred_element_type=F32).T

    for seg in range(DENSE_WIDTH // SEG):
        c0 = seg * SEG
        a = jnp.dot(h, w_ref[:, c0:c0 + SEG], preferred_element_type=F32)
        if c0 < OFF_RK:
            a = ret_rope(a)
        elif c0 < OFF_RV:
            a = ret_rope(a) * (RET_QK_DIM ** -0.5)
            if keys_t:
                kt_ref[0, c0 - OFF_RK:c0 - OFF_RK + SEG, :] = a.T.astype(kt_ref.dtype)
        elif c0 < OFF_RG:
            pass
        elif c0 < OFF_GA:
            a = a * _sigmoid(a)
        else:
            a = _sigmoid(a)
        dense_ref[0, :, c0:c0 + SEG] = a.astype(dense_ref.dtype)

    for g in range(N_GROUPS):
        hg = h_class[g][...] if g in h_class else h
        wq = wqk_ref[:, g * SEG:(g + 1) * SEG]
        emit_group(g, 0, att_rope(jnp.dot(hg, wq, preferred_element_type=F32), att_tabs[g]) * ATTN_SCALE)
        emit_group(g, 1, att_rope(jnp.dot(hg, wk_cols(g), preferred_element_type=F32), att_tabs[g]))
        emit_group(g, 2, jnp.dot(hg, wv_cols(g), preferred_element_type=F32))

    if win_refs:
        order = sorted(range(N_GROUPS), key=lambda g: win_first[g])

        def guarded(pos):
            if pos == len(order):
                return
            g = order[pos]

            @pl.when(i >= win_first[g])
            def _():
                emit_windows(g)
                guarded(pos + 1)

        guarded(0)


def _class_major_rows(tab, tm, d):
    if d == 1 or tab.shape[0] == 1:
        return tab
    r = tab.shape[0]
    return tab.reshape(r // tm, tm // d, d, LANES).transpose(0, 2, 1, 3).reshape(r, LANES)


def _in_projection(x, mod, w_main, w_qk, tables, *, tm, dils, paired, keys_t, win_rows, out_dtype):
    G, R, _ = x.shape
    n_tiles = R // tm
    mod_blk = tm if mod.shape[1] == R else 1
    tab_blk = tm if tables[0].shape[0] == R else 1

    def mod_spec(col):
        if mod_blk == 1:
            return pl.BlockSpec((1, 1, D_MODEL), lambda b, i: (b, 0, col))
        return pl.BlockSpec((1, tm, D_MODEL), lambda b, i: (b, i, col))

    def tab_spec(tab):
        if tab.shape[0] == 1:
            return pl.BlockSpec((1, LANES), lambda b, i: (0, 0))
        return pl.BlockSpec((tm, LANES), lambda b, i: (i, 0))

    att = list(tables[:3])
    if paired:
        att[2] = np.zeros((1, LANES), np.float32)
    all_tabs = []
    for d in tuple(dils) + (1,):
        all_tabs += [_class_major_rows(t, tm, d) for t in att]
    all_tabs += list(tables[3:])

    out_shapes = [jax.ShapeDtypeStruct((G, R, DENSE_WIDTH), out_dtype)]
    out_specs = [pl.BlockSpec((1, tm, DENSE_WIDTH), lambda b, i: (b, i, 0))]
    for d in dils:
        out_shapes.append(jax.ShapeDtypeStruct((G, d, R // d, ATTN_WIDTH), out_dtype))
        out_specs.append(pl.BlockSpec((1, d, tm // d, ATTN_WIDTH), lambda b, i: (b, 0, i, 0)))
    if keys_t:
        out_shapes.append(jax.ShapeDtypeStruct((G, RET_QK_WIDTH, R), out_dtype))
        out_specs.append(pl.BlockSpec((1, RET_QK_WIDTH, tm), lambda b, i: (b, 0, i)))
    win_blocks = win_first = ()
    if win_rows:
        win_blocks = tuple(min(w, tm) for w in win_rows)
        win_first = tuple(n_tiles - w // wb for w, wb in zip(win_rows, win_blocks))
        for g in range(N_GROUPS):
            first = win_first[g]
            for _ in range(2):
                out_shapes.append(jax.ShapeDtypeStruct((G, GROUP_WIDTH, win_rows[g]), F32))
                out_specs.append(pl.BlockSpec((1, GROUP_WIDTH, win_blocks[g]),
                                              lambda b, i, first=first: (b, 0, jnp.maximum(i - first, 0))))

    scratch = []
    if any(d > 1 for d in dils):
        scratch.append(pltpu.VMEM((D_MODEL // LANES, tm, LANES), F32))
        scratch += [pltpu.VMEM((tm, D_MODEL), BF16) for d in dils if d > 1]
    kern = functools.partial(_inproj_kernel, tm=tm, dils=dils, paired=paired, keys_t=keys_t,
                             win_blocks=win_blocks, win_first=win_first)
    return pl.pallas_call(
        kern,
        grid=(G, n_tiles),
        in_specs=[pl.BlockSpec((1, tm, D_MODEL), lambda b, i: (b, i, 0)),
                  mod_spec(0), mod_spec(1),
                  _resident(w_main.shape), _resident(w_qk.shape)] + [tab_spec(t) for t in all_tabs],
        out_specs=out_specs,
        out_shape=out_shapes,
        scratch_shapes=scratch,
        compiler_params=_cparams(("arbitrary", "arbitrary")),
        name="in_projection",
    )(x, mod, mod, w_main, w_qk, *all_tabs)


def _attn_kernel(q_ref, k_ref, v_ref, kp_ref, vp_ref, o_ref, lse_ref, *, tl, nc):
    t = pl.program_id(2)
    row = lax.broadcasted_iota(jnp.int32, (ATTN_BAND, ATTN_BAND), 0)
    col = lax.broadcasted_iota(jnp.int32, (ATTN_BAND, ATTN_BAND), 1)
    lane = lax.broadcasted_iota(jnp.int32, (1, GROUP_WIDTH), 1)
    cur_ok = col <= row
    qk_lanes = [(lane % LANES) // PAIR_SLOT == h for h in range(HEADS_PER_GROUP)]
    v_lanes = [lane // HEAD_DIM == h for h in range(HEADS_PER_GROUP)]

    def block(q, k_prev, v_prev, k_cur, v_cur, prev_shift):
        prev_ok = col >= row + prev_shift
        mask = jnp.concatenate([prev_ok, cur_ok], axis=1)
        mask = jnp.concatenate([mask] * HEADS_PER_GROUP, axis=0)
        kk = jnp.concatenate([k_prev, k_cur], axis=0)
        vv = jnp.concatenate([v_prev, v_cur], axis=0)
        qs = jnp.concatenate([jnp.where(sel, q, jnp.zeros_like(q)) for sel in qk_lanes], axis=0)
        s = lax.dot_general(qs, kk, (((1,), (1,)), ((), ())), preferred_element_type=F32)
        s = jnp.where(mask, s, NEG_BIG)
        m = jnp.max(s, axis=1, keepdims=True)
        e = jnp.exp(s - m)
        l = jnp.sum(e, axis=1, keepdims=True)
        oh = jnp.dot((e / l).astype(BF16), vv, preferred_element_type=F32)
        lse = m + jnp.log(l)
        o = jnp.zeros((ATTN_BAND, GROUP_WIDTH), F32)
        lo = jnp.zeros((ATTN_BAND, GROUP_WIDTH), F32)
        for h, sel in enumerate(v_lanes):
            rows = slice(h * ATTN_BAND, (h + 1) * ATTN_BAND)
            o = jnp.where(sel, oh[rows], o)
            lo = jnp.where(sel, lse[rows], lo)
        return o, lo

    first = jnp.where(t > 0, 0, ATTN_BAND)
    for c in range(nc):
        for j in range(tl // ATTN_BAND):
            j0, jp = j * ATTN_BAND, (j - 1) * ATTN_BAND
            rows = slice(j0, j0 + ATTN_BAND)
            if j == 0:
                k_prev, v_prev, shift = kp_ref[0, c], vp_ref[0, c], first
            else:
                k_prev, v_prev, shift = k_ref[0, c, jp:j0, :], v_ref[0, c, jp:j0, :], 0
            o, lse = block(q_ref[0, c, rows, :], k_prev, v_prev, k_ref[0, c, rows, :], v_ref[0, c, rows, :], shift)
            o_ref[0, c, rows, :] = o.astype(o_ref.dtype)
            lse_ref[0, c, rows, :] = lse


def _prompt_attention(qkv, g, *, tl, nc):
    B, dil, L, _ = qkv.shape
    sub = tl // ATTN_BAND

    def cur(c):
        return pl.BlockSpec((1, nc, tl, SEG), lambda b, r, t: (b, r, t, c))

    def prev(c):
        return pl.BlockSpec((1, nc, ATTN_BAND, SEG), lambda b, r, t: (b, r, jnp.maximum(t * sub - 1, 0), c))

    out_spec = pl.BlockSpec((1, nc, tl, SEG), lambda b, r, t: (b, r, t, 0))
    return pl.pallas_call(
        functools.partial(_attn_kernel, tl=tl, nc=nc),
        grid=(B, dil // nc, L // tl),
        in_specs=[cur(0), cur(1), cur(2), prev(1), prev(2)],
        out_specs=[out_spec, out_spec],
        out_shape=[jax.ShapeDtypeStruct((B, dil, L, SEG), BF16),
                   jax.ShapeDtypeStruct((B, dil, L, SEG), F32)],
        compiler_params=_cparams(("arbitrary", "arbitrary", "arbitrary")),
        name="prompt_attention_g%d" % g,
    )(qkv, qkv, qkv, qkv, qkv)


def _retention_tables():
    f32 = np.float32
    C = RET_CHUNK
    lg = np.log(f32(1.0) - f32(2.0) ** (f32(-5.0) - np.arange(RET_HEADS, dtype=f32)))
    idx = np.arange(C, dtype=f32)
    diff = idx[:, None] - idx[None, :]
    decay_in = np.where(diff >= 0, np.exp(lg[:, None, None] * np.maximum(diff, f32(0.0))), f32(0.0))
    decay_q = np.exp(lg[:, None] * (idx[None, :] + f32(1.0)))[:, :, None]
    decay_k = np.exp(lg[:, None] * (f32(C - 1.0) - idx[None, :]))[:, None, :]
    return tuple(t.astype(f32) for t in (decay_in, decay_q, decay_k))


def _retention_kernel(qk_ref, kt_ref, v_ref, g_ref, din_ref, dq_ref, dk_ref, o_ref, state_ref, r_scr, *, tr):
    t = pl.program_id(1)
    n_chunks = tr // RET_CHUNK

    @pl.when(t == 0)
    def _():
        r_scr[...] = jnp.zeros_like(r_scr)

    def key_t(c, h):
        return kt_ref[0, h * RET_QK_DIM:(h + 1) * RET_QK_DIM, c * RET_CHUNK:(c + 1) * RET_CHUNK]

    def val(c, h):
        return v_ref[0, c * RET_CHUNK:(c + 1) * RET_CHUNK, h * RET_V_DIM:(h + 1) * RET_V_DIM]

    states = [[None] * RET_HEADS for _ in range(n_chunks)]
    for h in range(RET_HEADS):
        r = r_scr[h]
        for c in range(n_chunks):
            states[c][h] = r
            kd = (key_t(c, h).astype(F32) * dk_ref[h]).astype(BF16)
            r = r * (_ret_gamma(h) ** RET_CHUNK) + jnp.dot(kd, val(c, h), preferred_element_type=F32)
        r_scr[h] = r

    for c in range(n_chunks):
        rows = slice(c * RET_CHUNK, (c + 1) * RET_CHUNK)
        for h in range(RET_HEADS):
            q = qk_ref[0, rows, OFF_RQ + h * RET_QK_DIM:OFF_RQ + (h + 1) * RET_QK_DIM]
            v = val(c, h)
            s = jnp.dot(q, key_t(c, h), preferred_element_type=F32) * din_ref[h]
            o = (jnp.dot(s.astype(BF16), v, preferred_element_type=F32)
                 + jnp.dot(q, states[c][h].astype(BF16), preferred_element_type=F32) * dq_ref[h])
            mu = jnp.mean(o, axis=1, keepdims=True)
            d = o - mu
            var = jnp.mean(d * d, axis=1, keepdims=True)
            gn = d * lax.rsqrt(var + GN_EPS)
            gate = g_ref[0, rows, h * RET_V_DIM:(h + 1) * RET_V_DIM].astype(F32)
            o_ref[0, rows, h * RET_V_DIM:(h + 1) * RET_V_DIM] = (gate * gn).astype(o_ref.dtype)

    @pl.when(t == pl.num_programs(1) - 1)
    def _():
        state_ref[0] = r_scr[...]


def _prompt_retention(dense, keys_t, *, tr):
    B, S, _ = dense.shape
    tabs = _retention_tables()
    blk = RET_V_WIDTH

    def colblk(c):
        return pl.BlockSpec((1, tr, blk), lambda b, t: (b, t, c))

    return pl.pallas_call(
        functools.partial(_retention_kernel, tr=tr),
        grid=(B, S // tr),
        in_specs=[pl.BlockSpec((1, tr, RET_QK_WIDTH), lambda b, t: (b, t, OFF_RQ // RET_QK_WIDTH)),
                  pl.BlockSpec((1, RET_QK_WIDTH, tr), lambda b, t: (b, 0, t)),
                  colblk(OFF_RV // blk), colblk(OFF_RG // blk)]
                 + [pl.BlockSpec(tab.shape, lambda b, t: (0, 0, 0)) for tab in tabs],
        out_specs=[pl.BlockSpec((1, tr, RET_V_WIDTH), lambda b, t: (b, t, 0)),
                   pl.BlockSpec((1, RET_HEADS, RET_QK_DIM, RET_V_DIM), lambda b, t: (b, 0, 0, 0))],
        out_shape=[jax.ShapeDtypeStruct((B, S, RET_V_WIDTH), BF16),
                   jax.ShapeDtypeStruct((B, RET_HEADS, RET_QK_DIM, RET_V_DIM), F32)],
        scratch_shapes=[pltpu.VMEM((RET_HEADS, RET_QK_DIM, RET_V_DIM), F32)],
        compiler_params=_cparams(("arbitrary", "arbitrary")),
        name="prompt_retention",
    )(dense, keys_t, dense, dense, *tabs)


def _layer_norm(u, g, b):
    mu = jnp.mean(u, axis=1, keepdims=True)
    d = u - mu
    var = jnp.mean(d * d, axis=1, keepdims=True)
    return d * lax.rsqrt(var + LN_EPS) * g + b


def _column(row):
    return jnp.broadcast_to(row, (LANES, LANES)).T


def _window_half_step(first_half, qkv_refs, cache_refs, o_ref, lse_ref, new_refs):
    pad = 16
    row_h = lax.broadcasted_iota(jnp.int32, (pad, LANES), 0)
    lane_h = lax.broadcasted_iota(jnp.int32, (pad, LANES), 1) // HEAD_DIM
    own = row_h == lane_h
    lane = lax.broadcasted_iota(jnp.int32, (LANES, LANES), 1)

    def pick(row):
        return jnp.where(first_half, row[:, :LANES], row[:, LANES:])

    def rolled(c_ref, n_ref, new_row):
        w = c_ref.shape[3]
        new_col = _column(new_row)
        nblk = w // LANES
        cur = pltpu.roll(c_ref[0, 0, :, 0:LANES], LANES - 1, 1)
        for j in range(nblk):
            nxt = pltpu.roll(c_ref[0, 0, :, (j + 1) * LANES:(j + 2) * LANES], LANES - 1, 1) if j + 1 < nblk else new_col
            n_ref[0, 0, :, j * LANES:(j + 1) * LANES] = jnp.where(lane < LANES - 1, cur, nxt)
            cur = nxt

    for g in range(N_GROUPS):
        dil = DILATED_PATTERNS[g][1]
        ck_ref, cv_ref = cache_refs[2 * g:2 * g + 2]
        w = ck_ref.shape[3]
        q = pick(qkv_refs[g][0, :, 0:SEG])
        kn = pick(qkv_refs[g][0, :, SEG:2 * SEG])
        vn = pick(qkv_refs[g][0, :, 2 * SEG:3 * SEG])
        qsel = jnp.where(own, jnp.broadcast_to(q, (pad, LANES)), 0.0)
        s = jnp.dot(qsel.astype(BF16), ck_ref[0, 0].astype(BF16), preferred_element_type=F32)
        pos = lax.broadcasted_iota(jnp.int32, (pad, w), 1)
        s = jnp.where((pos & (dil - 1)) == 0, s, NEG_BIG)
        sn = jnp.sum(qsel * kn, axis=1, keepdims=True)
        m = jnp.maximum(jnp.max(s, axis=1, keepdims=True), sn)
        e = jnp.exp(s - m)
        en = jnp.exp(sn - m)
        l = jnp.sum(e, axis=1, keepdims=True) + en
        o_all = lax.dot_general((e / l).astype(BF16), cv_ref[0, 0].astype(BF16), (((1,), (1,)), ((), ())),
                                preferred_element_type=F32) + (en / l) * vn
        o_ref[0, 0, :, g * LANES:(g + 1) * LANES] = jnp.sum(jnp.where(own, o_all, 0.0), axis=0, keepdims=True)
        lse_ref[0, 0, :, g * LANES:(g + 1) * LANES] = jnp.sum(jnp.where(own, m + jnp.log(l), 0.0), axis=0,
                                                              keepdims=True)
        rolled(ck_ref, new_refs[2 * g], kn)
        rolled(cv_ref, new_refs[2 * g + 1], vn)


N_MERGE_INPUTS = 23
N_SIDE_INPUTS = N_GROUPS + 2 * N_GROUPS
N_SIDE_OUTPUTS = 2 + 2 * N_GROUPS


def _merge_ffn_kernel(*refs, tm, sub, dils, side, steps_per_batch):
    (x_ref, ga_ref, gb_ref, ret_ref, o0_ref, o1_ref, o2_ref, l0_ref, l1_ref, l2_ref,
     gate1_ref, shift2_ref, scale2_ref, gate2_ref, watt_ref, wret_ref, wo_ref, wfi_ref, wfo_ref,
     ln1g_ref, ln1b_ref, ln2g_ref, ln2b_ref) = refs[:N_MERGE_INPUTS]
    n_in = N_MERGE_INPUTS + (N_SIDE_INPUTS if side else 0)
    y_ref = refs[n_in]
    n_out = 1 + (N_SIDE_OUTPUTS if side else 0)
    scratch = refs[n_in + n_out:]
    if side:
        side_in = refs[N_MERGE_INPUTS:n_in]
        side_out = refs[n_in + 1:n_in + n_out]
        step = pl.program_id(0) * steps_per_batch + pl.program_id(1)
        _window_half_step(step % 2 == 0, side_in[:N_GROUPS], side_in[N_GROUPS:], side_out[0], side_out[1],
                          side_out[2:])
    n_sub = tm // sub
    per_sub = 1 + 2 * N_GROUPS
    for s in range(n_sub):
        rows = slice(s * sub, (s + 1) * sub)
        act_scr = scratch[s * per_sub]
        stage_refs = scratch[s * per_sub + 1:(s + 1) * per_sub]

        def natural(ref, g, stage_ref):
            d = dils[g]
            n = sub // d
            if d == 1:
                return ref[0, 0, rows, :].astype(F32)
            for r in range(d):
                for u in range(GROUP_WIDTH // LANES):
                    stage_ref[u, pl.ds(r, n, stride=d), :] = (
                        ref[0, r, s * n:(s + 1) * n, u * LANES:(u + 1) * LANES].astype(F32))
            return jnp.concatenate([stage_ref[u] for u in range(GROUP_WIDTH // LANES)], axis=1)

        def mod_rows(ref):
            return ref[0] if ref.shape[1] == 1 else ref[0, rows, :]

        lses = [natural(ref, g, stage_refs[g]) for g, ref in enumerate((l0_ref, l1_ref, l2_ref))]
        outs = [natural(ref, g, stage_refs[N_GROUPS + g]) for g, ref in enumerate((o0_ref, o1_ref, o2_ref))]
        m = jnp.maximum(jnp.maximum(lses[0], lses[1]), lses[2])
        es = [jnp.exp(l - m) for l in lses]
        att = (es[0] * outs[0] + es[1] * outs[1] + es[2] * outs[2]) / (es[0] + es[1] + es[2])
        y_att = jnp.dot(att.astype(BF16), watt_ref[...], preferred_element_type=F32)
        y_ret = jnp.dot(ret_ref[0, rows, :].astype(BF16), wret_ref[...], preferred_element_type=F32)
        mix = (ga_ref[0, rows, :].astype(F32) * y_att + gb_ref[0, rows, :].astype(F32) * y_ret).astype(BF16)
        mixed = jnp.dot(mix, wo_ref[...], preferred_element_type=F32)
        x1 = _layer_norm(DEEPNORM_ALPHA * x_ref[0, rows, :] + mod_rows(gate1_ref) * mixed,
                         ln1g_ref[...], ln1b_ref[...])
        h2 = (x1 * (1.0 + mod_rows(scale2_ref)) + mod_rows(shift2_ref)).astype(BF16)
        for c0 in range(0, D_FF, FFN_CHUNK):
            c1 = min(c0 + FFN_CHUNK, D_FF)
            fg = jnp.dot(h2, wfi_ref[:, c0:c1], preferred_element_type=F32)
            fu = jnp.dot(h2, wfi_ref[:, D_FF + c0:D_FF + c1], preferred_element_type=F32)
            act_scr[:, c0:c1] = (fg * _sigmoid(fg) * fu).astype(BF16)
        ff = jnp.dot(act_scr[...], wfo_ref[...], preferred_element_type=F32)
        y_ref[0, rows, :] = _layer_norm(DEEPNORM_ALPHA * x1 + mod_rows(gate2_ref) * ff, ln2g_ref[...], ln2b_ref[...])


def _merge_ffn(x, dense, ret, o_groups, lse_groups, mod, weights, ln, *, tm, sub, dils, side=None):
    G, R, _ = x.shape
    mod_rows = mod.shape[1]
    n_tiles = R // tm
    side_args, side_in_specs, side_out_specs, side_out_shapes = [], [], [], []
    if side is not None:
        qkvs, caches = side
        n_seq = caches[0].shape[0]
        assert G * n_tiles == 2 * n_seq

        def seq_half(b, i):
            step = b * n_tiles + i
            return step // 2, step % 2

        row_spec = pl.BlockSpec((1, 1, ATTN_WIDTH), lambda b, i: (seq_half(b, i)[0], 0, 0))
        small_spec = pl.BlockSpec((1, 1, 1, N_GROUPS * LANES), lambda b, i: seq_half(b, i) + (0, 0))
        halves = [c.reshape(n_seq, 2, LANES, c.shape[2]) for c in caches]
        half_specs = [pl.BlockSpec((1, 1, LANES, c.shape[3]), lambda b, i: seq_half(b, i) + (0, 0)) for c in halves]
        side_args = list(qkvs) + halves
        side_in_specs = [row_spec] * N_GROUPS + half_specs
        side_out_specs = [small_spec, small_spec] + half_specs
        side_out_shapes = ([jax.ShapeDtypeStruct((n_seq, 2, 1, N_GROUPS * LANES), F32)] * 2
                           + [jax.ShapeDtypeStruct(c.shape, F32) for c in halves])
    sub_scratch = [pltpu.VMEM((sub, D_FF), BF16)] + [pltpu.VMEM((GROUP_WIDTH // LANES, sub, LANES), F32)] * (2 * N_GROUPS)

    def mod_spec(col):
        if mod_rows == 1:
            return pl.BlockSpec((1, 1, D_MODEL), lambda b, i: (b, 0, col))
        return pl.BlockSpec((1, tm, D_MODEL), lambda b, i: (b, i, col))

    def rows(width, col=0):
        return pl.BlockSpec((1, tm, width), lambda b, i: (b, i, col))

    group_specs = [pl.BlockSpec((1, d, tm // d, GROUP_WIDTH), lambda b, i: (b, 0, i, 0)) for d in dils]
    vec = pl.BlockSpec((1, D_MODEL), lambda b, i: (0, 0))
    outs = pl.pallas_call(
        functools.partial(_merge_ffn_kernel, tm=tm, sub=sub, dils=dils, side=side is not None,
                          steps_per_batch=n_tiles),
        grid=(G, n_tiles),
        in_specs=[rows(D_MODEL), rows(D_MODEL, OFF_GA // D_MODEL), rows(D_MODEL, OFF_GB // D_MODEL),
                  rows(RET_V_WIDTH)] + group_specs + group_specs
                 + [mod_spec(2), mod_spec(3), mod_spec(4), mod_spec(5)]
                 + [_resident(w.shape) for w in weights] + [vec] * 4 + side_in_specs,
        out_specs=[rows(D_MODEL)] + side_out_specs,
        out_shape=[jax.ShapeDtypeStruct((G, R, D_MODEL), F32)] + side_out_shapes,
        scratch_shapes=sub_scratch * (tm // sub),
        compiler_params=_cparams(("arbitrary", "arbitrary")),
        name="merge_ffn",
    )(x, dense, dense, ret, *o_groups, *lse_groups, mod, mod, mod, mod, *weights, *ln, *side_args)
    if side is None:
        return outs[0]
    y, o_s, lse_s, *rolled = outs
    return y, o_s, lse_s, [r.reshape(c.shape) for r, c in zip(rolled, side[1])]


def _sample_retention_kernel(dense_ref, state_ref, ret_ref, nstate_ref):
    for h in range(RET_HEADS):
        gamma = _ret_gamma(h)
        q = dense_ref[0, :, OFF_RQ + h * RET_QK_DIM:OFF_RQ + (h + 1) * RET_QK_DIM]
        k = dense_ref[0, :, OFF_RK + h * RET_QK_DIM:OFF_RK + (h + 1) * RET_QK_DIM]
        v = dense_ref[0, :, OFF_RV + h * RET_V_DIM:OFF_RV + (h + 1) * RET_V_DIM]
        gate = dense_ref[0, :, OFF_RG + h * RET_V_DIM:OFF_RG + (h + 1) * RET_V_DIM]
        r_old = state_ref[0, h]
        q_col = jnp.concatenate([_column(q)] * (RET_V_DIM // LANES), axis=1)
        k_col = jnp.concatenate([_column(k)] * (RET_V_DIM // LANES), axis=1)
        qk = jnp.sum(q * k, axis=1, keepdims=True)
        o = qk * v + jnp.sum(q_col * r_old, axis=0, keepdims=True) * gamma
        nstate_ref[0, h] = r_old * gamma + k_col * v
        mu = jnp.mean(o, axis=1, keepdims=True)
        d = o - mu
        var = jnp.mean(d * d, axis=1, keepdims=True)
        ret_ref[0, :, h * RET_V_DIM:(h + 1) * RET_V_DIM] = gate * (d * lax.rsqrt(var + GN_EPS))


def _sample_retention(dense, state):
    Bs = dense.shape[0]
    row = lambda w: pl.BlockSpec((1, 1, w), lambda b: (b, 0, 0))
    st = pl.BlockSpec((1, RET_HEADS, RET_QK_DIM, RET_V_DIM), lambda b: (b, 0, 0, 0))
    return pl.pallas_call(
        _sample_retention_kernel,
        grid=(Bs,),
        in_specs=[row(DENSE_WIDTH), st],
        out_specs=[row(RET_V_WIDTH), st],
        out_shape=[jax.ShapeDtypeStruct((Bs, 1, RET_V_WIDTH), F32), jax.ShapeDtypeStruct(state.shape, F32)],
        compiler_params=_cparams(("arbitrary",)),
        name="sample_retention",
    )(dense, state)


def _feature_major(a):
    _, b, w, _, _ = a.shape
    return jnp.transpose(a[0], (0, 2, 3, 1)).reshape(b, GROUP_WIDTH, w)


def _row_major(a):
    b, _, w = a.shape
    return jnp.transpose(a.reshape(b, HEADS_PER_GROUP, HEAD_DIM, w), (0, 3, 1, 2))[None]


def kernel(x_prompt, x_sample, cache_k_w128, cache_v_w128, cache_k_w512, cache_v_w512, cache_k_w2048,
           cache_v_w2048, state_retention, c_prompt, c_sample, w_ada, b_ada, w_in, w_att_out, w_ret_out,
           w_o, ln1_g, ln1_b, w_ffn_in, w_ffn_out, ln2_g, ln2_b):
    assert w_in.shape[0] == 1 and x_sample.shape[1] == 1
    B, S, D = x_prompt.shape
    Bs = x_sample.shape[0]
    dils = tuple(d for _, d in DILATED_PATTERNS)

    w_main = jnp.concatenate([w_in[0][:, W_IN_ROTATE:], w_in[0][:, 2 * ATTN_WIDTH:W_IN_ROTATE]], axis=1).astype(BF16)
    w_qk = w_in[0][:, :2 * ATTN_WIDTH].astype(BF16)
    pair_cols = jnp.asarray([s * SEG + f for s in range(2 * N_GROUPS) for f in _pair_source()], jnp.int32)
    w_qk_paired = w_qk[:, pair_cols]
    weights = tuple(w[0].astype(BF16) for w in (w_att_out, w_ret_out, w_o, w_ffn_in, w_ffn_out))
    ln = (ln1_g, ln1_b, ln2_g, ln2_b)

    mod = _modulation(jnp.concatenate([c_prompt, c_sample], axis=0), w_ada[0], b_ada[0])
    mod_p = mod[:B].reshape(B, 1, 6 * D)
    mod_s = mod[B:].reshape(1, Bs, 6 * D)

    tabs_p = _rope_tables(np.arange(S), paired=True)
    win_rows = tuple(min(w, S) for w, _ in DILATED_PATTERNS)
    dense_p, *rest = _in_projection(x_prompt, mod_p, w_main, w_qk_paired, tabs_p, tm=256, dils=dils, paired=True,
                                    keys_t=True, win_rows=win_rows, out_dtype=BF16)
    qkv_p, keys_t_p, wins_p = rest[:N_GROUPS], rest[N_GROUPS], rest[N_GROUPS + 1:]
    o_p, lse_p = [], []
    for g, d in enumerate(dils):
        tl = min(S // d, 1024)
        o, lse = _prompt_attention(qkv_p[g], g, tl=tl, nc=min(d, 1024 // tl))
        o_p.append(o)
        lse_p.append(lse)
    ret_p, state_p = _prompt_retention(dense_p, keys_t_p, tr=512)

    xs = x_sample.reshape(1, Bs, D)
    tabs_s = _rope_tables(PAST_LEN + np.arange(1))
    unit = (1,) * N_GROUPS
    dense_s, *qkv_s = _in_projection(xs, mod_s, w_main, w_qk, tabs_s, tm=Bs, dils=unit, paired=False,
                                     keys_t=False, win_rows=None, out_dtype=F32)
    caches = [_feature_major(c) for c in (cache_k_w128, cache_v_w128, cache_k_w512, cache_v_w512,
                                          cache_k_w2048, cache_v_w2048)]
    ret_s, state_s = _sample_retention(dense_s.reshape(Bs, 1, DENSE_WIDTH), state_retention[0])
    y_p, o_s, lse_s, rolled = _merge_ffn(
        x_prompt, dense_p, ret_p, o_p, lse_p, mod_p, weights, ln, tm=512, sub=256, dils=dils,
        side=([a.reshape(Bs, 1, ATTN_WIDTH) for a in qkv_s], caches))

    def by_group(a):
        return [a[:, :, 0, g * LANES:(g + 1) * LANES].reshape(1, 1, Bs, GROUP_WIDTH) for g in range(N_GROUPS)]

    o_sg, lse_sg = by_group(o_s), by_group(lse_s)
    y_s = _merge_ffn(xs, dense_s, ret_s.reshape(1, Bs, RET_V_WIDTH), o_sg, lse_sg, mod_s, weights, ln,
                     tm=Bs, sub=Bs, dils=unit)

    return (y_p, y_s.reshape(Bs, 1, D),
            *[_row_major(w) for w in wins_p], state_p[None],
            *[_row_major(r) for r in rolled], state_s[None])
```

```python
import functools
import math

import jax
import jax.numpy as jnp
import numpy as np
from jax import lax
from jax.experimental import pallas as pl
from jax.experimental.pallas import tpu as pltpu

F32 = jnp.float32
BF16 = jnp.bfloat16

D_MODEL = 1024
PAST_LEN = 16384
DILATED_PATTERNS = ((128, 1), (512, 4), (2048, 16))
N_GROUPS = 3
HEADS_PER_GROUP = 4
HEAD_DIM = 64
GROUP_WIDTH = HEADS_PER_GROUP * HEAD_DIM
ATTN_WIDTH = N_GROUPS * GROUP_WIDTH
ATTN_BAND = 128
ATTN_SCALE = HEAD_DIM ** -0.5
ROPE_DIM = HEAD_DIM // 4
ROPE_THETA = 500000.0
RET_HEADS = 4
RET_QK_DIM = 128
RET_V_DIM = 256
RET_QK_WIDTH = RET_HEADS * RET_QK_DIM
RET_V_WIDTH = RET_HEADS * RET_V_DIM
RET_CHUNK = 128
RET_ROPE_THETA = 10000.0
IN_WIDTH = 3 * ATTN_WIDTH + 2 * RET_QK_WIDTH + 2 * RET_V_WIDTH + 2 * D_MODEL
D_FF = 2816
DEEPNORM_ALPHA = 2.0 ** 0.25
LN_EPS = 1e-5
GN_EPS = 1e-6

LANES = 128
SEG = 256
N_SEG = IN_WIDTH // SEG
FFN_CHUNK = 256
OFF_RQ, OFF_RK, OFF_RV, OFF_RG, OFF_GA, OFF_GB = 0, 512, 1024, 2048, 3072, 4096
OFF_QA, OFF_KA, OFF_VA = 5120, 5888, 6656
DENSE_WIDTH = OFF_QA
W_IN_ROTATE = 3 * ATTN_WIDTH
NEG_BIG = -1e30
VMEM_LIMIT = 60 * 1024 * 1024


def _cparams(sem):
    return pltpu.CompilerParams(dimension_semantics=sem, vmem_limit_bytes=VMEM_LIMIT)


def _resident(shape):
    nd = len(shape)
    return pl.BlockSpec(shape, lambda *_: (0,) * nd, pipeline_mode=pl.Buffered(1))


def _sigmoid(x):
    return 1.0 / (1.0 + jnp.exp(-x))


def _ret_gamma(h):
    return 1.0 - 2.0 ** (-5.0 - h)


def _ada_kernel(c_ref, w_ref, b_ref, o_ref):
    c = c_ref[...]
    a = (c * _sigmoid(c)).astype(BF16)
    o_ref[...] = jnp.dot(a, w_ref[...].astype(BF16), preferred_element_type=F32) + b_ref[...]


def _modulation(c_all, w_ada, b_ada):
    rows = c_all.shape[0]
    tn = 1024
    return pl.pallas_call(
        _ada_kernel,
        grid=(6 * D_MODEL // tn,),
        in_specs=[pl.BlockSpec((rows, D_MODEL), lambda j: (0, 0)),
                  pl.BlockSpec((D_MODEL, tn), lambda j: (0, j)),
                  pl.BlockSpec((1, tn), lambda j: (0, j))],
        out_specs=pl.BlockSpec((rows, tn), lambda j: (0, j)),
        out_shape=jax.ShapeDtypeStruct((rows, 6 * D_MODEL), F32),
        compiler_params=_cparams(("arbitrary",)),
        name="modulation",
    )(c_all, w_ada, b_ada.reshape(1, -1))


PAIR_SLOT = HEAD_DIM // 2


def _pair_source():
    src = []
    for half_dims in ((0, 16), (8, 40)):
        for h in range(HEADS_PER_GROUP):
            rot, rest = half_dims
            dims = list(range(rot, rot + ROPE_DIM // 2)) + list(range(rest, rest + PAIR_SLOT - ROPE_DIM // 2))
            src += [h * HEAD_DIM + d for d in dims]
    return src


def _rope_tables(pos, paired=False):
    posf = jnp.asarray(np.asarray(pos), jnp.int32).astype(F32)[:, None]
    lane = np.arange(LANES)
    half = ROPE_DIM // 2
    inv = jnp.exp(-math.log(ROPE_THETA) * jnp.arange(half, dtype=F32) * (2.0 / ROPE_DIM))
    ang = posf * inv[None, :]
    cos, sin = jnp.cos(ang), jnp.sin(ang)
    if paired:
        in_slot = lane % PAIR_SLOT
        f = in_slot % half
        att_c = jnp.where(in_slot < half, cos[:, f], 1.0)
        att_lo = jnp.where(in_slot < half, sin[:, f], 0.0)
        att_hi = att_lo
    else:
        in_head = lane % HEAD_DIM
        f = in_head % half
        att_c = jnp.where(in_head < ROPE_DIM, cos[:, f], 1.0)
        att_lo = jnp.where(in_head < half, -sin[:, f], 0.0)
        att_hi = jnp.where((in_head >= half) & (in_head < ROPE_DIM), sin[:, f], 0.0)
    rhalf = RET_QK_DIM // 2
    rinv = jnp.exp(-math.log(RET_ROPE_THETA) * jnp.arange(rhalf, dtype=F32) * (2.0 / RET_QK_DIM))
    rang = posf * rinv[None, :]
    rcos, rsin = jnp.cos(rang), jnp.sin(rang)
    ret_c = jnp.concatenate([rcos, rcos], axis=-1)
    ret_s = jnp.concatenate([-rsin, rsin], axis=-1)
    return att_c, att_lo, att_hi, ret_c, ret_s


def _inproj_kernel(x_ref, shift_ref, scale_ref, w_ref, wqk_ref, *refs, tm, dils, paired, keys_t, win_blocks,
                   win_first):
    n_tab = 3 * N_GROUPS + 2
    tabs = [r[...] for r in refs[:n_tab]]
    att_tabs = [tabs[3 * g:3 * g + 3] for g in range(N_GROUPS)]
    rc, rs = tabs[-2:]
    refs = refs[n_tab:]
    n_perm = sum(d > 1 for d in dils)
    dense_ref = refs[0]
    qkv_refs = refs[1:1 + N_GROUPS]
    n_out = len(refs) - (1 + n_perm if n_perm else 0)
    kt_ref = refs[1 + N_GROUPS] if keys_t else None
    win_refs = refs[1 + N_GROUPS + bool(keys_t):n_out]
    i = pl.program_id(1)
    hf = x_ref[0] * (1.0 + scale_ref[0]) + shift_ref[0]
    h = hf.astype(BF16)

    h_class = {}
    if n_perm:
        stage_ref = refs[n_out]
        for u in range(D_MODEL // LANES):
            stage_ref[u] = hf[:, u * LANES:(u + 1) * LANES]
        for g, hp_ref in zip([g for g in range(N_GROUPS) if dils[g] > 1], refs[n_out + 1:]):
            d = dils[g]
            n = tm // d
            for r in range(d):
                for u in range(D_MODEL // LANES):
                    hp_ref[r * n:(r + 1) * n, u * LANES:(u + 1) * LANES] = (
                        stage_ref[u, pl.ds(r, n, stride=d), :].astype(BF16))
            h_class[g] = hp_ref

    def att_rope(a, tab):
        ac, alo, ahi = tab
        if paired:
            lo, hi = a[:, :LANES], a[:, LANES:]
            return jnp.concatenate([lo * ac - hi * alo, hi * ac + lo * alo], axis=1)
        parts = []
        for u in range(SEG // LANES):
            xs = a[:, u * LANES:(u + 1) * LANES]
            parts.append(xs * ac + pltpu.roll(xs, LANES - ROPE_DIM // 2, 1) * alo
                         + pltpu.roll(xs, ROPE_DIM // 2, 1) * ahi)
        return jnp.concatenate(parts, axis=1)

    def ret_rope(a):
        parts = []
        for u in range(SEG // LANES):
            xs = a[:, u * LANES:(u + 1) * LANES]
            parts.append(xs * rc + pltpu.roll(xs, RET_QK_DIM // 2, 1) * rs)
        return jnp.concatenate(parts, axis=1)

    def emit_group(g, c, a):
        ref, d = qkv_refs[g], dils[g]
        n = tm // d
        for r in range(d):
            ref[0, r, :, c * SEG:(c + 1) * SEG] = a[r * n:(r + 1) * n, :].astype(ref.dtype)

    def wk_cols(g):
        return wqk_ref[:, ATTN_WIDTH + g * SEG:ATTN_WIDTH + (g + 1) * SEG]

    def wv_cols(g):
        return w_ref[:, DENSE_WIDTH + g * SEG:DENSE_WIDTH + (g + 1) * SEG]

    kv_f32 = {}

    def position_order(a, g, slot):
        d = dils[g]
        if d == 1:
            return a
        n = tm // d
        for u in range(SEG // LANES):
            for r in range(d):
                stage_ref[slot + u, pl.ds(r, n, stride=d), :] = a[r * n:(r + 1) * n, u * LANES:(u + 1) * LANES]
        return jnp.concatenate([stage_ref[slot + u] for u in range(SEG // LANES)], axis=1)

    def emit_windows(g):
        wb = win_blocks[g]
        assert dils[g] == 1 or wb == tm
        k, v = kv_f32[g]
        slot = 2 * (SEG // LANES) * (g % 2)
        at = position_order(k, g, slot)[tm - wb:, :].T
        if paired:
            dst = {f: lane for lane, f in enumerate(_pair_source())}
            at = jnp.concatenate([at[dst[f]:dst[f] + 8, :] for f in range(0, GROUP_WIDTH, 8)], axis=0)
        win_refs[2 * g][0] = at
        win_refs[2 * g + 1][0] = position_order(v, g, slot + SEG // LANES)[tm - wb:, :].T

    for seg in range(DENSE_WIDTH // SEG):
        c0 = seg * SEG
        a = jnp.dot(h, w_ref[:, c0:c0 + SEG], preferred_element_type=F32)
        if c0 < OFF_RK:
            a = ret_rope(a)
        elif c0 < OFF_RV:
            a = ret_rope(a) * (RET_QK_DIM ** -0.5)
            if keys_t:
                kt_ref[0, c0 - OFF_RK:c0 - OFF_RK + SEG, :] = a.T.astype(kt_ref.dtype)
        elif c0 < OFF_RG:
            pass
        elif c0 < OFF_GA:
            a = a * _sigmoid(a)
        else:
            a = _sigmoid(a)
        dense_ref[0, :, c0:c0 + SEG] = a.astype(dense_ref.dtype)

    for g in range(N_GROUPS):
        hg = h_class[g][...] if g in h_class else h
        wq = wqk_ref[:, g * SEG:(g + 1) * SEG]
        emit_group(g, 0, att_rope(jnp.dot(hg, wq, preferred_element_type=F32), att_tabs[g]) * ATTN_SCALE)
        k = att_rope(jnp.dot(hg, wk_cols(g), preferred_element_type=F32), att_tabs[g])
        v = jnp.dot(hg, wv_cols(g), preferred_element_type=F32)
        emit_group(g, 1, k)
        emit_group(g, 2, v)
        kv_f32[g] = (k, v)

    if win_refs:
        order = sorted(range(N_GROUPS), key=lambda g: win_first[g])

        def guarded(pos):
            if pos == len(order):
                return
            g = order[pos]

            @pl.when(i >= win_first[g])
            def _():
                emit_windows(g)
                guarded(pos + 1)

        guarded(0)


def _class_major_rows(tab, tm, d):
    if d == 1 or tab.shape[0] == 1:
        return tab
    r = tab.shape[0]
    return tab.reshape(r // tm, tm // d, d, LANES).transpose(0, 2, 1, 3).reshape(r, LANES)


def _in_projection(x, mod, w_main, w_qk, tables, *, tm, dils, paired, keys_t, win_rows, out_dtype):
    G, R, _ = x.shape
    n_tiles = R // tm
    mod_blk = tm if mod.shape[1] == R else 1
    tab_blk = tm if tables[0].shape[0] == R else 1

    def mod_spec(col):
        if mod_blk == 1:
            return pl.BlockSpec((1, 1, D_MODEL), lambda b, i: (b, 0, col))
        return pl.BlockSpec((1, tm, D_MODEL), lambda b, i: (b, i, col))

    def tab_spec(tab):
        if tab.shape[0] == 1:
            return pl.BlockSpec((1, LANES), lambda b, i: (0, 0))
        return pl.BlockSpec((tm, LANES), lambda b, i: (i, 0))

    att = list(tables[:3])
    if paired:
        att[2] = np.zeros((1, LANES), np.float32)
    all_tabs = []
    for d in dils:
        all_tabs += [_class_major_rows(t, tm, d) for t in att]
    all_tabs += list(tables[3:])

    out_shapes = [jax.ShapeDtypeStruct((G, R, DENSE_WIDTH), out_dtype)]
    out_specs = [pl.BlockSpec((1, tm, DENSE_WIDTH), lambda b, i: (b, i, 0))]
    for d in dils:
        out_shapes.append(jax.ShapeDtypeStruct((G, d, R // d, ATTN_WIDTH), out_dtype))
        out_specs.append(pl.BlockSpec((1, d, tm // d, ATTN_WIDTH), lambda b, i: (b, 0, i, 0)))
    if keys_t:
        out_shapes.append(jax.ShapeDtypeStruct((G, RET_QK_WIDTH, R), out_dtype))
        out_specs.append(pl.BlockSpec((1, RET_QK_WIDTH, tm), lambda b, i: (b, 0, i)))
    win_blocks = win_first = ()
    if win_rows:
        win_blocks = tuple(min(w, tm) for w in win_rows)
        win_first = tuple(n_tiles - w // wb for w, wb in zip(win_rows, win_blocks))
        for g in range(N_GROUPS):
            first = win_first[g]
            for _ in range(2):
                out_shapes.append(jax.ShapeDtypeStruct((G, GROUP_WIDTH, win_rows[g]), F32))
                out_specs.append(pl.BlockSpec((1, GROUP_WIDTH, win_blocks[g]),
                                              lambda b, i, first=first: (b, 0, jnp.maximum(i - first, 0))))

    scratch = []
    if any(d > 1 for d in dils):
        scratch.append(pltpu.VMEM((D_MODEL // LANES, tm, LANES), F32))
        scratch += [pltpu.VMEM((tm, D_MODEL), BF16) for d in dils if d > 1]
    kern = functools.partial(_inproj_kernel, tm=tm, dils=dils, paired=paired, keys_t=keys_t,
                             win_blocks=win_blocks, win_first=win_first)
    return pl.pallas_call(
        kern,
        grid=(G, n_tiles),
        in_specs=[pl.BlockSpec((1, tm, D_MODEL), lambda b, i: (b, i, 0)),
                  mod_spec(0), mod_spec(1),
                  _resident(w_main.shape), _resident(w_qk.shape)] + [tab_spec(t) for t in all_tabs],
        out_specs=out_specs,
        out_shape=out_shapes,
        scratch_shapes=scratch,
        compiler_params=_cparams(("arbitrary", "arbitrary")),
        name="in_projection",
    )(x, mod, mod, w_main, w_qk, *all_tabs)


def _attn_kernel(q_ref, k_ref, v_ref, kp_ref, vp_ref, o_ref, lse_ref, *, tl, nc):
    t = pl.program_id(2)
    row = lax.broadcasted_iota(jnp.int32, (ATTN_BAND, ATTN_BAND), 0)
    col = lax.broadcasted_iota(jnp.int32, (ATTN_BAND, ATTN_BAND), 1)
    lane = lax.broadcasted_iota(jnp.int32, (1, GROUP_WIDTH), 1)
    cur_ok = col <= row
    qk_lanes = [(lane % LANES) // PAIR_SLOT == h for h in range(HEADS_PER_GROUP)]
    v_lanes = [lane // HEAD_DIM == h for h in range(HEADS_PER_GROUP)]

    def block(q, k_prev, v_prev, k_cur, v_cur, prev_shift):
        prev_ok = col >= row + prev_shift
        mask = jnp.concatenate([prev_ok, cur_ok], axis=1)
        mask = jnp.concatenate([mask] * HEADS_PER_GROUP, axis=0)
        kk = jnp.concatenate([k_prev, k_cur], axis=0)
        vv = jnp.concatenate([v_prev, v_cur], axis=0)
        qs = jnp.concatenate([jnp.where(sel, q, jnp.zeros_like(q)) for sel in qk_lanes], axis=0)
        s = lax.dot_general(qs, kk, (((1,), (1,)), ((), ())), preferred_element_type=F32)
        s = jnp.where(mask, s, NEG_BIG)
        m = jnp.max(s, axis=1, keepdims=True)
        e = jnp.exp(s - m)
        l = jnp.sum(e, axis=1, keepdims=True)
        oh = jnp.dot((e / l).astype(BF16), vv, preferred_element_type=F32)
        lse = m + jnp.log(l)
        o = jnp.zeros((ATTN_BAND, GROUP_WIDTH), F32)
        lo = jnp.zeros((ATTN_BAND, GROUP_WIDTH), F32)
        for h, sel in enumerate(v_lanes):
            rows = slice(h * ATTN_BAND, (h + 1) * ATTN_BAND)
            o = jnp.where(sel, oh[rows], o)
            lo = jnp.where(sel, lse[rows], lo)
        return o, lo

    first = jnp.where(t > 0, 0, ATTN_BAND)
    for c in range(nc):
        for j in range(tl // ATTN_BAND):
            j0, jp = j * ATTN_BAND, (j - 1) * ATTN_BAND
            rows = slice(j0, j0 + ATTN_BAND)
            if j == 0:
                k_prev, v_prev, shift = kp_ref[0, c], vp_ref[0, c], first
            else:
                k_prev, v_prev, shift = k_ref[0, c, jp:j0, :], v_ref[0, c, jp:j0, :], 0
            o, lse = block(q_ref[0, c, rows, :], k_prev, v_prev, k_ref[0, c, rows, :], v_ref[0, c, rows, :], shift)
            o_ref[0, c, rows, :] = o.astype(o_ref.dtype)
            lse_ref[0, c, rows, :] = lse


def _prompt_attention(qkv, g, *, tl, nc):
    B, dil, L, _ = qkv.shape
    sub = tl // ATTN_BAND

    def cur(c):
        return pl.BlockSpec((1, nc, tl, SEG), lambda b, r, t: (b, r, t, c))

    def prev(c):
        return pl.BlockSpec((1, nc, ATTN_BAND, SEG), lambda b, r, t: (b, r, jnp.maximum(t * sub - 1, 0), c))

    out_spec = pl.BlockSpec((1, nc, tl, SEG), lambda b, r, t: (b, r, t, 0))
    return pl.pallas_call(
        functools.partial(_attn_kernel, tl=tl, nc=nc),
        grid=(B, dil // nc, L // tl),
        in_specs=[cur(0), cur(1), cur(2), prev(1), prev(2)],
        out_specs=[out_spec, out_spec],
        out_shape=[jax.ShapeDtypeStruct((B, dil, L, SEG), BF16),
                   jax.ShapeDtypeStruct((B, dil, L, SEG), F32)],
        compiler_params=_cparams(("arbitrary", "arbitrary", "arbitrary")),
        name="prompt_attention_g%d" % g,
    )(qkv, qkv, qkv, qkv, qkv)


def _retention_tables():
    f32 = np.float32
    C = RET_CHUNK
    lg = np.log(f32(1.0) - f32(2.0) ** (f32(-5.0) - np.arange(RET_HEADS, dtype=f32)))
    idx = np.arange(C, dtype=f32)
    diff = idx[:, None] - idx[None, :]
    decay_in = np.where(diff >= 0, np.exp(lg[:, None, None] * np.maximum(diff, f32(0.0))), f32(0.0))
    decay_q = np.exp(lg[:, None] * (idx[None, :] + f32(1.0)))[:, :, None]
    decay_k = np.exp(lg[:, None] * (f32(C - 1.0) - idx[None, :]))[:, None, :]
    return tuple(t.astype(f32) for t in (decay_in, decay_q, decay_k))


def _retention_kernel(qk_ref, kt_ref, v_ref, g_ref, din_ref, dq_ref, dk_ref, o_ref, state_ref, r_scr, *, tr):
    t = pl.program_id(1)
    n_chunks = tr // RET_CHUNK

    @pl.when(t == 0)
    def _():
        r_scr[...] = jnp.zeros_like(r_scr)

    def key_t(c, h):
        return kt_ref[0, h * RET_QK_DIM:(h + 1) * RET_QK_DIM, c * RET_CHUNK:(c + 1) * RET_CHUNK]

    def val(c, h):
        return v_ref[0, c * RET_CHUNK:(c + 1) * RET_CHUNK, h * RET_V_DIM:(h + 1) * RET_V_DIM]

    states = [[None] * RET_HEADS for _ in range(n_chunks)]
    for h in range(RET_HEADS):
        r = r_scr[h]
        for c in range(n_chunks):
            states[c][h] = r
            kd = (key_t(c, h).astype(F32) * dk_ref[h]).astype(BF16)
            r = r * (_ret_gamma(h) ** RET_CHUNK) + jnp.dot(kd, val(c, h), preferred_element_type=F32)
        r_scr[h] = r

    for c in range(n_chunks):
        rows = slice(c * RET_CHUNK, (c + 1) * RET_CHUNK)
        for h in range(RET_HEADS):
            q = qk_ref[0, rows, OFF_RQ + h * RET_QK_DIM:OFF_RQ + (h + 1) * RET_QK_DIM]
            v = val(c, h)
            s = jnp.dot(q, key_t(c, h), preferred_element_type=F32) * din_ref[h]
            o = (jnp.dot(s.astype(BF16), v, preferred_element_type=F32)
                 + jnp.dot(q, states[c][h].astype(BF16), preferred_element_type=F32) * dq_ref[h])
            mu = jnp.mean(o, axis=1, keepdims=True)
            d = o - mu
            var = jnp.mean(d * d, axis=1, keepdims=True)
            gn = d * lax.rsqrt(var + GN_EPS)
            gate = g_ref[0, rows, h * RET_V_DIM:(h + 1) * RET_V_DIM].astype(F32)
            o_ref[0, rows, h * RET_V_DIM:(h + 1) * RET_V_DIM] = (gate * gn).astype(o_ref.dtype)

    @pl.when(t == pl.num_programs(1) - 1)
    def _():
        state_ref[0] = r_scr[...]


def _prompt_retention(dense, keys_t, *, tr):
    B, S, _ = dense.shape
    tabs = _retention_tables()
    blk = RET_V_WIDTH

    def colblk(c):
        return pl.BlockSpec((1, tr, blk), lambda b, t: (b, t, c))

    return pl.pallas_call(
        functools.partial(_retention_kernel, tr=tr),
        grid=(B, S // tr),
        in_specs=[pl.BlockSpec((1, tr, RET_QK_WIDTH), lambda b, t: (b, t, OFF_RQ // RET_QK_WIDTH)),
                  pl.BlockSpec((1, RET_QK_WIDTH, tr), lambda b, t: (b, 0, t)),
                  colblk(OFF_RV // blk), colblk(OFF_RG // blk)]
                 + [pl.BlockSpec(tab.shape, lambda b, t: (0, 0, 0)) for tab in tabs],
        out_specs=[pl.BlockSpec((1, tr, RET_V_WIDTH), lambda b, t: (b, t, 0)),
                   pl.BlockSpec((1, RET_HEADS, RET_QK_DIM, RET_V_DIM), lambda b, t: (b, 0, 0, 0))],
        out_shape=[jax.ShapeDtypeStruct((B, S, RET_V_WIDTH), BF16),
                   jax.ShapeDtypeStruct((B, RET_HEADS, RET_QK_DIM, RET_V_DIM), F32)],
        scratch_shapes=[pltpu.VMEM((RET_HEADS, RET_QK_DIM, RET_V_DIM), F32)],
        compiler_params=_cparams(("arbitrary", "arbitrary")),
        name="prompt_retention",
    )(dense, keys_t, dense, dense, *tabs)


def _layer_norm(u, g, b):
    mu = jnp.mean(u, axis=1, keepdims=True)
    d = u - mu
    var = jnp.mean(d * d, axis=1, keepdims=True)
    return d * lax.rsqrt(var + LN_EPS) * g + b


def _column(row):
    return jnp.broadcast_to(row, (LANES, LANES)).T


def _window_half_step(first_half, qkv_refs, cache_refs, o_ref, lse_ref, new_refs):
    pad = 16
    row_h = lax.broadcasted_iota(jnp.int32, (pad, LANES), 0)
    lane_h = lax.broadcasted_iota(jnp.int32, (pad, LANES), 1) // HEAD_DIM
    own = row_h == lane_h
    lane = lax.broadcasted_iota(jnp.int32, (LANES, LANES), 1)

    def pick(row):
        return jnp.where(first_half, row[:, :LANES], row[:, LANES:])

    def rolled(c_ref, n_ref, new_row):
        w = c_ref.shape[3]
        new_col = _column(new_row)
        nblk = w // LANES
        cur = pltpu.roll(c_ref[0, 0, :, 0:LANES], LANES - 1, 1)
        for j in range(nblk):
            nxt = pltpu.roll(c_ref[0, 0, :, (j + 1) * LANES:(j + 2) * LANES], LANES - 1, 1) if j + 1 < nblk else new_col
            n_ref[0, 0, :, j * LANES:(j + 1) * LANES] = jnp.where(lane < LANES - 1, cur, nxt)
            cur = nxt

    for g in range(N_GROUPS):
        dil = DILATED_PATTERNS[g][1]
        ck_ref, cv_ref = cache_refs[2 * g:2 * g + 2]
        w = ck_ref.shape[3]
        q = pick(qkv_refs[g][0, :, 0:SEG])
        kn = pick(qkv_refs[g][0, :, SEG:2 * SEG])
        vn = pick(qkv_refs[g][0, :, 2 * SEG:3 * SEG])
        qsel = jnp.where(own, jnp.broadcast_to(q, (pad, LANES)), 0.0)
        s = jnp.dot(qsel.astype(BF16), ck_ref[0, 0].astype(BF16), preferred_element_type=F32)
        pos = lax.broadcasted_iota(jnp.int32, (pad, w), 1)
        s = jnp.where((pos & (dil - 1)) == 0, s, NEG_BIG)
        sn = jnp.sum(qsel * kn, axis=1, keepdims=True)
        m = jnp.maximum(jnp.max(s, axis=1, keepdims=True), sn)
        e = jnp.exp(s - m)
        en = jnp.exp(sn - m)
        l = jnp.sum(e, axis=1, keepdims=True) + en
        o_all = lax.dot_general((e / l).astype(BF16), cv_ref[0, 0].astype(BF16), (((1,), (1,)), ((), ())),
                                preferred_element_type=F32) + (en / l) * vn
        o_ref[0, 0, :, g * LANES:(g + 1) * LANES] = jnp.sum(jnp.where(own, o_all, 0.0), axis=0, keepdims=True)
        lse_ref[0, 0, :, g * LANES:(g + 1) * LANES] = jnp.sum(jnp.where(own, m + jnp.log(l), 0.0), axis=0,
                                                              keepdims=True)
        rolled(ck_ref, new_refs[2 * g], kn)
        rolled(cv_ref, new_refs[2 * g + 1], vn)


N_MERGE_INPUTS = 23
N_SIDE_INPUTS = N_GROUPS + 2 * N_GROUPS
N_SIDE_OUTPUTS = 2 + 2 * N_GROUPS


def _merge_ffn_kernel(*refs, tm, sub, dils, side, steps_per_batch):
    (x_ref, ga_ref, gb_ref, ret_ref, o0_ref, o1_ref, o2_ref, l0_ref, l1_ref, l2_ref,
     gate1_ref, shift2_ref, scale2_ref, gate2_ref, watt_ref, wret_ref, wo_ref, wfi_ref, wfo_ref,
     ln1g_ref, ln1b_ref, ln2g_ref, ln2b_ref) = refs[:N_MERGE_INPUTS]
    n_in = N_MERGE_INPUTS + (N_SIDE_INPUTS if side else 0)
    y_ref = refs[n_in]
    n_out = 1 + (N_SIDE_OUTPUTS if side else 0)
    scratch = refs[n_in + n_out:]
    if side:
        side_in = refs[N_MERGE_INPUTS:n_in]
        side_out = refs[n_in + 1:n_in + n_out]
        step = pl.program_id(0) * steps_per_batch + pl.program_id(1)
        _window_half_step(step % 2 == 0, side_in[:N_GROUPS], side_in[N_GROUPS:], side_out[0], side_out[1],
                          side_out[2:])
    n_sub = tm // sub
    per_sub = 1 + 2 * N_GROUPS
    for s in range(n_sub):
        rows = slice(s * sub, (s + 1) * sub)
        act_scr = scratch[s * per_sub]
        stage_refs = scratch[s * per_sub + 1:(s + 1) * per_sub]

        def natural(ref, g, stage_ref):
            d = dils[g]
            n = sub // d
            if d == 1:
                return ref[0, 0, rows, :].astype(F32)
            for r in range(d):
                for u in range(GROUP_WIDTH // LANES):
                    stage_ref[u, pl.ds(r, n, stride=d), :] = (
                        ref[0, r, s * n:(s + 1) * n, u * LANES:(u + 1) * LANES].astype(F32))
            return jnp.concatenate([stage_ref[u] for u in range(GROUP_WIDTH // LANES)], axis=1)

        def mod_rows(ref):
            return ref[0] if ref.shape[1] == 1 else ref[0, rows, :]

        lses = [natural(ref, g, stage_refs[g]) for g, ref in enumerate((l0_ref, l1_ref, l2_ref))]
        outs = [natural(ref, g, stage_refs[N_GROUPS + g]) for g, ref in enumerate((o0_ref, o1_ref, o2_ref))]
        m = jnp.maximum(jnp.maximum(lses[0], lses[1]), lses[2])
        es = [jnp.exp(l - m) for l in lses]
        att = (es[0] * outs[0] + es[1] * outs[1] + es[2] * outs[2]) / (es[0] + es[1] + es[2])
        y_att = jnp.dot(att.astype(BF16), watt_ref[...], preferred_element_type=F32)
        y_ret = jnp.dot(ret_ref[0, rows, :].astype(BF16), wret_ref[...], preferred_element_type=F32)
        mix = (ga_ref[0, rows, :].astype(F32) * y_att + gb_ref[0, rows, :].astype(F32) * y_ret).astype(BF16)
        mixed = jnp.dot(mix, wo_ref[...], preferred_element_type=F32)
        x1 = _layer_norm(DEEPNORM_ALPHA * x_ref[0, rows, :] + mod_rows(gate1_ref) * mixed,
                         ln1g_ref[...], ln1b_ref[...])
        h2 = (x1 * (1.0 + mod_rows(scale2_ref)) + mod_rows(shift2_ref)).astype(BF16)
        for c0 in range(0, D_FF, FFN_CHUNK):
            c1 = min(c0 + FFN_CHUNK, D_FF)
            fg = jnp.dot(h2, wfi_ref[:, c0:c1], preferred_element_type=F32)
            fu = jnp.dot(h2, wfi_ref[:, D_FF + c0:D_FF + c1], preferred_element_type=F32)
            act_scr[:, c0:c1] = (fg * _sigmoid(fg) * fu).astype(BF16)
        ff = jnp.dot(act_scr[...], wfo_ref[...], preferred_element_type=F32)
        y_ref[0, rows, :] = _layer_norm(DEEPNORM_ALPHA * x1 + mod_rows(gate2_ref) * ff, ln2g_ref[...], ln2b_ref[...])


def _merge_ffn(x, dense, ret, o_groups, lse_groups, mod, weights, ln, *, tm, sub, dils, side=None):
    G, R, _ = x.shape
    mod_rows = mod.shape[1]
    n_tiles = R // tm
    side_args, side_in_specs, side_out_specs, side_out_shapes = [], [], [], []
    if side is not None:
        qkvs, caches = side
        n_seq = caches[0].shape[0]
        assert G * n_tiles == 2 * n_seq

        def seq_half(b, i):
            step = b * n_tiles + i
            return step // 2, step % 2

        row_spec = pl.BlockSpec((1, 1, ATTN_WIDTH), lambda b, i: (seq_half(b, i)[0], 0, 0))
        small_spec = pl.BlockSpec((1, 1, 1, N_GROUPS * LANES), lambda b, i: seq_half(b, i) + (0, 0))
        halves = [c.reshape(n_seq, 2, LANES, c.shape[2]) for c in caches]
        half_specs = [pl.BlockSpec((1, 1, LANES, c.shape[3]), lambda b, i: seq_half(b, i) + (0, 0)) for c in halves]
        side_args = list(qkvs) + halves
        side_in_specs = [row_spec] * N_GROUPS + half_specs
        side_out_specs = [small_spec, small_spec] + half_specs
        side_out_shapes = ([jax.ShapeDtypeStruct((n_seq, 2, 1, N_GROUPS * LANES), F32)] * 2
                           + [jax.ShapeDtypeStruct(c.shape, F32) for c in halves])
    sub_scratch = [pltpu.VMEM((sub, D_FF), BF16)] + [pltpu.VMEM((GROUP_WIDTH // LANES, sub, LANES), F32)] * (2 * N_GROUPS)

    def mod_spec(col):
        if mod_rows == 1:
            return pl.BlockSpec((1, 1, D_MODEL), lambda b, i: (b, 0, col))
        return pl.BlockSpec((1, tm, D_MODEL), lambda b, i: (b, i, col))

    def rows(width, col=0):
        return pl.BlockSpec((1, tm, width), lambda b, i: (b, i, col))

    group_specs = [pl.BlockSpec((1, d, tm // d, GROUP_WIDTH), lambda b, i: (b, 0, i, 0)) for d in dils]
    vec = pl.BlockSpec((1, D_MODEL), lambda b, i: (0, 0))
    outs = pl.pallas_call(
        functools.partial(_merge_ffn_kernel, tm=tm, sub=sub, dils=dils, side=side is not None,
                          steps_per_batch=n_tiles),
        grid=(G, n_tiles),
        in_specs=[rows(D_MODEL), rows(D_MODEL, OFF_GA // D_MODEL), rows(D_MODEL, OFF_GB // D_MODEL),
                  rows(RET_V_WIDTH)] + group_specs + group_specs
                 + [mod_spec(2), mod_spec(3), mod_spec(4), mod_spec(5)]
                 + [_resident(w.shape) for w in weights] + [vec] * 4 + side_in_specs,
        out_specs=[rows(D_MODEL)] + side_out_specs,
        out_shape=[jax.ShapeDtypeStruct((G, R, D_MODEL), F32)] + side_out_shapes,
        scratch_shapes=sub_scratch * (tm // sub),
        compiler_params=_cparams(("arbitrary", "arbitrary")),
        name="merge_ffn",
    )(x, dense, dense, ret, *o_groups, *lse_groups, mod, mod, mod, mod, *weights, *ln, *side_args)
    if side is None:
        return outs[0]
    y, o_s, lse_s, *rolled = outs
    return y, o_s, lse_s, [r.reshape(c.shape) for r, c in zip(rolled, side[1])]


def _sample_retention_kernel(dense_ref, state_ref, ret_ref, nstate_ref):
    for b in range(dense_ref.shape[0]):
        for h in range(RET_HEADS):
            gamma = _ret_gamma(h)
            q = dense_ref[b, :, OFF_RQ + h * RET_QK_DIM:OFF_RQ + (h + 1) * RET_QK_DIM]
            k = dense_ref[b, :, OFF_RK + h * RET_QK_DIM:OFF_RK + (h + 1) * RET_QK_DIM]
            v = dense_ref[b, :, OFF_RV + h * RET_V_DIM:OFF_RV + (h + 1) * RET_V_DIM]
            gate = dense_ref[b, :, OFF_RG + h * RET_V_DIM:OFF_RG + (h + 1) * RET_V_DIM]
            r_old = state_ref[b, h]
            q_col = jnp.concatenate([_column(q)] * (RET_V_DIM // LANES), axis=1)
            k_col = jnp.concatenate([_column(k)] * (RET_V_DIM // LANES), axis=1)
            qk = jnp.sum(q * k, axis=1, keepdims=True)
            o = qk * v + jnp.sum(q_col * r_old, axis=0, keepdims=True) * gamma
            nstate_ref[b, h] = r_old * gamma + k_col * v
            mu = jnp.mean(o, axis=1, keepdims=True)
            d = o - mu
            var = jnp.mean(d * d, axis=1, keepdims=True)
            ret_ref[b, :, h * RET_V_DIM:(h + 1) * RET_V_DIM] = gate * (d * lax.rsqrt(var + GN_EPS))


def _sample_retention(dense, state, *, nb=4):
    Bs = dense.shape[0]
    row = lambda w: pl.BlockSpec((nb, 1, w), lambda b: (b, 0, 0))
    st = pl.BlockSpec((nb, RET_HEADS, RET_QK_DIM, RET_V_DIM), lambda b: (b, 0, 0, 0))
    return pl.pallas_call(
        _sample_retention_kernel,
        grid=(Bs // nb,),
        in_specs=[row(DENSE_WIDTH), st],
        out_specs=[row(RET_V_WIDTH), st],
        out_shape=[jax.ShapeDtypeStruct((Bs, 1, RET_V_WIDTH), F32), jax.ShapeDtypeStruct(state.shape, F32)],
        compiler_params=_cparams(("arbitrary",)),
        name="sample_retention",
    )(dense, state)


def _feature_major(a):
    _, b, w, _, _ = a.shape
    return jnp.transpose(a[0], (0, 2, 3, 1)).reshape(b, GROUP_WIDTH, w)


def _row_major(a):
    b, _, w = a.shape
    return jnp.transpose(a.reshape(b, HEADS_PER_GROUP, HEAD_DIM, w), (0, 3, 1, 2))[None]


def kernel(x_prompt, x_sample, cache_k_w128, cache_v_w128, cache_k_w512, cache_v_w512, cache_k_w2048,
           cache_v_w2048, state_retention, c_prompt, c_sample, w_ada, b_ada, w_in, w_att_out, w_ret_out,
           w_o, ln1_g, ln1_b, w_ffn_in, w_ffn_out, ln2_g, ln2_b):
    assert w_in.shape[0] == 1 and x_sample.shape[1] == 1
    B, S, D = x_prompt.shape
    Bs = x_sample.shape[0]
    dils = tuple(d for _, d in DILATED_PATTERNS)

    w_main = jnp.concatenate([w_in[0][:, W_IN_ROTATE:], w_in[0][:, 2 * ATTN_WIDTH:W_IN_ROTATE]], axis=1).astype(BF16)
    w_qk = w_in[0][:, :2 * ATTN_WIDTH].astype(BF16)
    pair_cols = jnp.asarray([s * SEG + f for s in range(2 * N_GROUPS) for f in _pair_source()], jnp.int32)
    w_qk_paired = w_qk[:, pair_cols]
    weights = tuple(w[0].astype(BF16) for w in (w_att_out, w_ret_out, w_o, w_ffn_in, w_ffn_out))
    ln = (ln1_g, ln1_b, ln2_g, ln2_b)

    mod = _modulation(jnp.concatenate([c_prompt, c_sample], axis=0), w_ada[0], b_ada[0])
    mod_p = mod[:B].reshape(B, 1, 6 * D)
    mod_s = mod[B:].reshape(1, Bs, 6 * D)

    tabs_p = _rope_tables(np.arange(S), paired=True)
    win_rows = tuple(min(w, S) for w, _ in DILATED_PATTERNS)
    dense_p, *rest = _in_projection(x_prompt, mod_p, w_main, w_qk_paired, tabs_p, tm=256, dils=dils, paired=True,
                                    keys_t=True, win_rows=win_rows, out_dtype=BF16)
    qkv_p, keys_t_p, wins_p = rest[:N_GROUPS], rest[N_GROUPS], rest[N_GROUPS + 1:]
    o_p, lse_p = [], []
    for g, d in enumerate(dils):
        tl = min(S // d, 1024)
        o, lse = _prompt_attention(qkv_p[g], g, tl=tl, nc=min(d, 1024 // tl))
        o_p.append(o)
        lse_p.append(lse)
    ret_p, state_p = _prompt_retention(dense_p, keys_t_p, tr=512)

    xs = x_sample.reshape(1, Bs, D)
    tabs_s = _rope_tables(PAST_LEN + np.arange(1))
    unit = (1,) * N_GROUPS
    dense_s, *qkv_s = _in_projection(xs, mod_s, w_main, w_qk, tabs_s, tm=Bs, dils=unit, paired=False,
                                     keys_t=False, win_rows=None, out_dtype=F32)
    caches = [_feature_major(c) for c in (cache_k_w128, cache_v_w128, cache_k_w512, cache_v_w512,
                                          cache_k_w2048, cache_v_w2048)]
    ret_s, state_s = _sample_retention(dense_s.reshape(Bs, 1, DENSE_WIDTH), state_retention[0])
    y_p, o_s, lse_s, rolled = _merge_ffn(
        x_prompt, dense_p, ret_p, o_p, lse_p, mod_p, weights, ln, tm=512, sub=256, dils=dils,
        side=([a.reshape(Bs, 1, ATTN_WIDTH) for a in qkv_s], caches))

    def by_group(a):
        return [a[:, :, 0, g * LANES:(g + 1) * LANES].reshape(1, 1, Bs, GROUP_WIDTH) for g in range(N_GROUPS)]

    o_sg, lse_sg = by_group(o_s), by_group(lse_s)
    y_s = _merge_ffn(xs, dense_s, ret_s.reshape(1, Bs, RET_V_WIDTH), o_sg, lse_sg, mod_s, weights, ln,
                     tm=Bs, sub=Bs, dils=unit)

    return (y_p, y_s.reshape(Bs, 1, D),
            *[_row_major(w) for w in wins_p], state_p[None],
            *[_row_major(r) for r in rolled], state_s[None])
```

```python
import functools
import math

import jax
import jax.numpy as jnp
import numpy as np
from jax import lax
from jax.experimental import pallas as pl
from jax.experimental.pallas import tpu as pltpu

F32 = jnp.float32
BF16 = jnp.bfloat16

D_MODEL = 1024
PAST_LEN = 16384
DILATED_PATTERNS = ((128, 1), (512, 4), (2048, 16))
N_GROUPS = 3
HEADS_PER_GROUP = 4
HEAD_DIM = 64
GROUP_WIDTH = HEADS_PER_GROUP * HEAD_DIM
ATTN_WIDTH = N_GROUPS * GROUP_WIDTH
ATTN_BAND = 128
ATTN_SCALE = HEAD_DIM ** -0.5
ROPE_DIM = HEAD_DIM // 4
ROPE_THETA = 500000.0
RET_HEADS = 4
RET_QK_DIM = 128
RET_V_DIM = 256
RET_QK_WIDTH = RET_HEADS * RET_QK_DIM
RET_V_WIDTH = RET_HEADS * RET_V_DIM
RET_CHUNK = 128
RET_ROPE_THETA = 10000.0
IN_WIDTH = 3 * ATTN_WIDTH + 2 * RET_QK_WIDTH + 2 * RET_V_WIDTH + 2 * D_MODEL
D_FF = 2816
DEEPNORM_ALPHA = 2.0 ** 0.25
LN_EPS = 1e-5
GN_EPS = 1e-6

LANES = 128
SEG = 256
N_SEG = IN_WIDTH // SEG
FFN_CHUNK = 256
OFF_RQ, OFF_RK, OFF_RV, OFF_RG, OFF_GA, OFF_GB = 0, 512, 1024, 2048, 3072, 4096
OFF_QA, OFF_KA, OFF_VA = 5120, 5888, 6656
DENSE_WIDTH = OFF_QA
W_IN_ROTATE = 3 * ATTN_WIDTH
NEG_BIG = -1e30
VMEM_LIMIT = 60 * 1024 * 1024


def _cparams(sem):
    return pltpu.CompilerParams(dimension_semantics=sem, vmem_limit_bytes=VMEM_LIMIT)


def _resident(shape):
    nd = len(shape)
    return pl.BlockSpec(shape, lambda *_: (0,) * nd, pipeline_mode=pl.Buffered(1))


def _sigmoid(x):
    return 1.0 / (1.0 + jnp.exp(-x))


def _ret_gamma(h):
    return 1.0 - 2.0 ** (-5.0 - h)


def _ada_kernel(c_ref, w_ref, b_ref, o_ref):
    c = c_ref[...]
    a = (c * _sigmoid(c)).astype(BF16)
    o_ref[...] = jnp.dot(a, w_ref[...].astype(BF16), preferred_element_type=F32) + b_ref[...]


def _modulation(c_all, w_ada, b_ada):
    rows = c_all.shape[0]
    tn = 1024
    return pl.pallas_call(
        _ada_kernel,
        grid=(6 * D_MODEL // tn,),
        in_specs=[pl.BlockSpec((rows, D_MODEL), lambda j: (0, 0)),
                  pl.BlockSpec((D_MODEL, tn), lambda j: (0, j)),
                  pl.BlockSpec((1, tn), lambda j: (0, j))],
        out_specs=pl.BlockSpec((rows, tn), lambda j: (0, j)),
        out_shape=jax.ShapeDtypeStruct((rows, 6 * D_MODEL), F32),
        compiler_params=_cparams(("arbitrary",)),
        name="modulation",
    )(c_all, w_ada, b_ada.reshape(1, -1))


PAIR_SLOT = HEAD_DIM // 2


def _pair_source():
    src = []
    for half_dims in ((0, 16), (8, 40)):
        for h in range(HEADS_PER_GROUP):
            rot, rest = half_dims
            dims = list(range(rot, rot + ROPE_DIM // 2)) + list(range(rest, rest + PAIR_SLOT - ROPE_DIM // 2))
            src += [h * HEAD_DIM + d for d in dims]
    return src


def _pair_columns(w):
    run = ROPE_DIM // 2
    w5 = w.reshape(w.shape[0], -1, HEADS_PER_GROUP, HEAD_DIM // run, run)
    half_a = jnp.concatenate([w5[:, :, :, 0:1], w5[:, :, :, 2:5]], axis=3)
    half_b = jnp.concatenate([w5[:, :, :, 1:2], w5[:, :, :, 5:8]], axis=3)
    return jnp.stack([half_a, half_b], axis=2).reshape(w.shape)


def _rope_tables(pos, paired=False):
    posf = jnp.asarray(np.asarray(pos), jnp.int32).astype(F32)[:, None]
    lane = np.arange(LANES)
    half = ROPE_DIM // 2
    inv = jnp.exp(-math.log(ROPE_THETA) * jnp.arange(half, dtype=F32) * (2.0 / ROPE_DIM))
    ang = posf * inv[None, :]
    cos, sin = jnp.cos(ang), jnp.sin(ang)
    if paired:
        in_slot = lane % PAIR_SLOT
        f = in_slot % half
        att_c = jnp.where(in_slot < half, cos[:, f], 1.0)
        att_lo = jnp.where(in_slot < half, sin[:, f], 0.0)
        att_hi = att_lo
    else:
        in_head = lane % HEAD_DIM
        f = in_head % half
        att_c = jnp.where(in_head < ROPE_DIM, cos[:, f], 1.0)
        att_lo = jnp.where(in_head < half, -sin[:, f], 0.0)
        att_hi = jnp.where((in_head >= half) & (in_head < ROPE_DIM), sin[:, f], 0.0)
    rhalf = RET_QK_DIM // 2
    rinv = jnp.exp(-math.log(RET_ROPE_THETA) * jnp.arange(rhalf, dtype=F32) * (2.0 / RET_QK_DIM))
    rang = posf * rinv[None, :]
    rcos, rsin = jnp.cos(rang), jnp.sin(rang)
    ret_c = jnp.concatenate([rcos, rcos], axis=-1)
    ret_s = jnp.concatenate([-rsin, rsin], axis=-1)
    return att_c, att_lo, att_hi, ret_c, ret_s


def _inproj_kernel(x_ref, shift_ref, scale_ref, w_ref, wqk_ref, *refs, tm, dils, paired, keys_t, win_blocks,
                   win_first):
    n_tab = 3 * N_GROUPS + 2
    tabs = [r[...] for r in refs[:n_tab]]
    att_tabs = [tabs[3 * g:3 * g + 3] for g in range(N_GROUPS)]
    rc, rs = tabs[-2:]
    refs = refs[n_tab:]
    n_perm = sum(d > 1 for d in dils)
    dense_ref = refs[0]
    qkv_refs = refs[1:1 + N_GROUPS]
    n_out = len(refs) - (1 + n_perm if n_perm else 0)
    kt_ref = refs[1 + N_GROUPS] if keys_t else None
    win_refs = refs[1 + N_GROUPS + bool(keys_t):n_out]
    i = pl.program_id(1)
    hf = x_ref[0] * (1.0 + scale_ref[0]) + shift_ref[0]
    h = hf.astype(BF16)

    h_class = {}
    if n_perm:
        stage_ref = refs[n_out]
        for u in range(D_MODEL // LANES):
            stage_ref[u] = hf[:, u * LANES:(u + 1) * LANES]
        for g, hp_ref in zip([g for g in range(N_GROUPS) if dils[g] > 1], refs[n_out + 1:]):
            d = dils[g]
            n = tm // d
            for r in range(d):
                for u in range(D_MODEL // LANES):
                    hp_ref[r * n:(r + 1) * n, u * LANES:(u + 1) * LANES] = (
                        stage_ref[u, pl.ds(r, n, stride=d), :].astype(BF16))
            h_class[g] = hp_ref

    def att_rope(a, tab):
        ac, alo, ahi = tab
        if paired:
            lo, hi = a[:, :LANES], a[:, LANES:]
            return jnp.concatenate([lo * ac - hi * alo, hi * ac + lo * alo], axis=1)
        parts = []
        for u in range(SEG // LANES):
            xs = a[:, u * LANES:(u + 1) * LANES]
            parts.append(xs * ac + pltpu.roll(xs, LANES - ROPE_DIM // 2, 1) * alo
                         + pltpu.roll(xs, ROPE_DIM // 2, 1) * ahi)
        return jnp.concatenate(parts, axis=1)

    def ret_rope(a):
        parts = []
        for u in range(SEG // LANES):
            xs = a[:, u * LANES:(u + 1) * LANES]
            parts.append(xs * rc + pltpu.roll(xs, RET_QK_DIM // 2, 1) * rs)
        return jnp.concatenate(parts, axis=1)

    def emit_group(g, c, a):
        ref, d = qkv_refs[g], dils[g]
        n = tm // d
        for r in range(d):
            ref[0, r, :, c * SEG:(c + 1) * SEG] = a[r * n:(r + 1) * n, :].astype(ref.dtype)

    def wk_cols(g):
        return wqk_ref[:, ATTN_WIDTH + g * SEG:ATTN_WIDTH + (g + 1) * SEG]

    def wv_cols(g):
        return w_ref[:, DENSE_WIDTH + g * SEG:DENSE_WIDTH + (g + 1) * SEG]

    kv_f32 = {}

    def position_order(a, g, slot):
        d = dils[g]
        if d == 1:
            return a
        n = tm // d
        for u in range(SEG // LANES):
            for r in range(d):
                stage_ref[slot + u, pl.ds(r, n, stride=d), :] = a[r * n:(r + 1) * n, u * LANES:(u + 1) * LANES]
        return jnp.concatenate([stage_ref[slot + u] for u in range(SEG // LANES)], axis=1)

    def emit_windows(g):
        wb = win_blocks[g]
        assert dils[g] == 1 or wb == tm
        k, v = kv_f32[g]
        slot = 2 * (SEG // LANES) * (g % 2)
        at = position_order(k, g, slot)[tm - wb:, :].T
        if paired:
            dst = {f: lane for lane, f in enumerate(_pair_source())}
            at = jnp.concatenate([at[dst[f]:dst[f] + 8, :] for f in range(0, GROUP_WIDTH, 8)], axis=0)
        win_refs[2 * g][0] = at
        win_refs[2 * g + 1][0] = position_order(v, g, slot + SEG // LANES)[tm - wb:, :].T

    for seg in range(DENSE_WIDTH // SEG):
        c0 = seg * SEG
        a = jnp.dot(h, w_ref[:, c0:c0 + SEG], preferred_element_type=F32)
        if c0 < OFF_RK:
            a = ret_rope(a)
        elif c0 < OFF_RV:
            a = ret_rope(a) * (RET_QK_DIM ** -0.5)
            if keys_t:
                kt_ref[0, c0 - OFF_RK:c0 - OFF_RK + SEG, :] = a.T.astype(kt_ref.dtype)
        elif c0 < OFF_RG:
            pass
        elif c0 < OFF_GA:
            a = a * _sigmoid(a)
        else:
            a = _sigmoid(a)
        dense_ref[0, :, c0:c0 + SEG] = a.astype(dense_ref.dtype)

    for g in range(N_GROUPS):
        hg = h_class[g][...] if g in h_class else h
        wq = wqk_ref[:, g * SEG:(g + 1) * SEG]
        emit_group(g, 0, att_rope(jnp.dot(hg, wq, preferred_element_type=F32), att_tabs[g]) * ATTN_SCALE)
        k = att_rope(jnp.dot(hg, wk_cols(g), preferred_element_type=F32), att_tabs[g])
        v = jnp.dot(hg, wv_cols(g), preferred_element_type=F32)
        emit_group(g, 1, k)
        emit_group(g, 2, v)
        kv_f32[g] = (k, v)

    if win_refs:
        order = sorted(range(N_GROUPS), key=lambda g: win_first[g])

        def guarded(pos):
            if pos == len(order):
                return
            g = order[pos]

            @pl.when(i >= win_first[g])
            def _():
                emit_windows(g)
                guarded(pos + 1)

        guarded(0)


def _class_major_rows(tab, tm, d):
    if d == 1 or tab.shape[0] == 1:
        return tab
    r = tab.shape[0]
    return tab.reshape(r // tm, tm // d, d, LANES).transpose(0, 2, 1, 3).reshape(r, LANES)


def _in_projection(x, mod, w_main, w_qk, tables, *, tm, dils, paired, keys_t, win_rows, out_dtype):
    G, R, _ = x.shape
    n_tiles = R // tm
    mod_blk = tm if mod.shape[1] == R else 1
    tab_blk = tm if tables[0].shape[0] == R else 1

    def mod_spec(col):
        if mod_blk == 1:
            return pl.BlockSpec((1, 1, D_MODEL), lambda b, i: (b, 0, col))
        return pl.BlockSpec((1, tm, D_MODEL), lambda b, i: (b, i, col))

    def tab_spec(tab):
        if tab.shape[0] == 1:
            return pl.BlockSpec((1, LANES), lambda b, i: (0, 0))
        return pl.BlockSpec((tm, LANES), lambda b, i: (i, 0))

    att = list(tables[:3])
    if paired:
        att[2] = np.zeros((1, LANES), np.float32)
    all_tabs = []
    for d in dils:
        all_tabs += [_class_major_rows(t, tm, d) for t in att]
    all_tabs += list(tables[3:])

    out_shapes = [jax.ShapeDtypeStruct((G, R, DENSE_WIDTH), out_dtype)]
    out_specs = [pl.BlockSpec((1, tm, DENSE_WIDTH), lambda b, i: (b, i, 0))]
    for d in dils:
        out_shapes.append(jax.ShapeDtypeStruct((G, d, R // d, ATTN_WIDTH), out_dtype))
        out_specs.append(pl.BlockSpec((1, d, tm // d, ATTN_WIDTH), lambda b, i: (b, 0, i, 0)))
    if keys_t:
        out_shapes.append(jax.ShapeDtypeStruct((G, RET_QK_WIDTH, R), out_dtype))
        out_specs.append(pl.BlockSpec((1, RET_QK_WIDTH, tm), lambda b, i: (b, 0, i)))
    win_blocks = win_first = ()
    if win_rows:
        win_blocks = tuple(min(w, tm) for w in win_rows)
        win_first = tuple(n_tiles - w // wb for w, wb in zip(win_rows, win_blocks))
        for g in range(N_GROUPS):
            first = win_first[g]
            for _ in range(2):
                out_shapes.append(jax.ShapeDtypeStruct((G, GROUP_WIDTH, win_rows[g]), F32))
                out_specs.append(pl.BlockSpec((1, GROUP_WIDTH, win_blocks[g]),
                                              lambda b, i, first=first: (b, 0, jnp.maximum(i - first, 0))))

    scratch = []
    if any(d > 1 for d in dils):
        scratch.append(pltpu.VMEM((D_MODEL // LANES, tm, LANES), F32))
        scratch += [pltpu.VMEM((tm, D_MODEL), BF16) for d in dils if d > 1]
    kern = functools.partial(_inproj_kernel, tm=tm, dils=dils, paired=paired, keys_t=keys_t,
                             win_blocks=win_blocks, win_first=win_first)
    return pl.pallas_call(
        kern,
        grid=(G, n_tiles),
        in_specs=[pl.BlockSpec((1, tm, D_MODEL), lambda b, i: (b, i, 0)),
                  mod_spec(0), mod_spec(1),
                  _resident(w_main.shape), _resident(w_qk.shape)] + [tab_spec(t) for t in all_tabs],
        out_specs=out_specs,
        out_shape=out_shapes,
        scratch_shapes=scratch,
        compiler_params=_cparams(("arbitrary", "arbitrary")),
        name="in_projection",
    )(x, mod, mod, w_main, w_qk, *all_tabs)


def _attn_kernel(q_ref, k_ref, v_ref, kp_ref, vp_ref, o_ref, lse_ref, *, tl, nc):
    t = pl.program_id(2)
    row = lax.broadcasted_iota(jnp.int32, (ATTN_BAND, ATTN_BAND), 0)
    col = lax.broadcasted_iota(jnp.int32, (ATTN_BAND, ATTN_BAND), 1)
    lane = lax.broadcasted_iota(jnp.int32, (1, GROUP_WIDTH), 1)
    cur_ok = col <= row
    qk_lanes = [(lane % LANES) // PAIR_SLOT == h for h in range(HEADS_PER_GROUP)]
    v_lanes = [lane // HEAD_DIM == h for h in range(HEADS_PER_GROUP)]

    def block(q, k_prev, v_prev, k_cur, v_cur, prev_shift):
        prev_ok = col >= row + prev_shift
        mask = jnp.concatenate([prev_ok, cur_ok], axis=1)
        mask = jnp.concatenate([mask] * HEADS_PER_GROUP, axis=0)
        kk = jnp.concatenate([k_prev, k_cur], axis=0)
        vv = jnp.concatenate([v_prev, v_cur], axis=0)
        qs = jnp.concatenate([jnp.where(sel, q, jnp.zeros_like(q)) for sel in qk_lanes], axis=0)
        s = lax.dot_general(qs, kk, (((1,), (1,)), ((), ())), preferred_element_type=F32)
        s = jnp.where(mask, s, NEG_BIG)
        m = jnp.max(s, axis=1, keepdims=True)
        e = jnp.exp(s - m)
        l = jnp.sum(e, axis=1, keepdims=True)
        oh = jnp.dot((e / l).astype(BF16), vv, preferred_element_type=F32)
        lse = m + jnp.log(l)
        o = jnp.zeros((ATTN_BAND, GROUP_WIDTH), F32)
        lo = jnp.zeros((ATTN_BAND, GROUP_WIDTH), F32)
        for h, sel in enumerate(v_lanes):
            rows = slice(h * ATTN_BAND, (h + 1) * ATTN_BAND)
            o = jnp.where(sel, oh[rows], o)
            lo = jnp.where(sel, lse[rows], lo)
        return o, lo

    first = jnp.where(t > 0, 0, ATTN_BAND)
    for c in range(nc):
        for j in range(tl // ATTN_BAND):
            j0, jp = j * ATTN_BAND, (j - 1) * ATTN_BAND
            rows = slice(j0, j0 + ATTN_BAND)
            if j == 0:
                k_prev, v_prev, shift = kp_ref[0, c], vp_ref[0, c], first
            else:
                k_prev, v_prev, shift = k_ref[0, c, jp:j0, :], v_ref[0, c, jp:j0, :], 0
            o, lse = block(q_ref[0, c, rows, :], k_prev, v_prev, k_ref[0, c, rows, :], v_ref[0, c, rows, :], shift)
            o_ref[0, c, rows, :] = o.astype(o_ref.dtype)
            lse_ref[0, c, rows, :] = lse


def _prompt_attention(qkv, g, *, tl, nc):
    B, dil, L, _ = qkv.shape
    sub = tl // ATTN_BAND

    def cur(c):
        return pl.BlockSpec((1, nc, tl, SEG), lambda b, r, t: (b, r, t, c))

    def prev(c):
        return pl.BlockSpec((1, nc, ATTN_BAND, SEG), lambda b, r, t: (b, r, jnp.maximum(t * sub - 1, 0), c))

    out_spec = pl.BlockSpec((1, nc, tl, SEG), lambda b, r, t: (b, r, t, 0))
    return pl.pallas_call(
        functools.partial(_attn_kernel, tl=tl, nc=nc),
        grid=(B, dil // nc, L // tl),
        in_specs=[cur(0), cur(1), cur(2), prev(1), prev(2)],
        out_specs=[out_spec, out_spec],
        out_shape=[jax.ShapeDtypeStruct((B, dil, L, SEG), BF16),
                   jax.ShapeDtypeStruct((B, dil, L, SEG), F32)],
        compiler_params=_cparams(("arbitrary", "arbitrary", "arbitrary")),
        name="prompt_attention_g%d" % g,
    )(qkv, qkv, qkv, qkv, qkv)


def _retention_tables():
    f32 = np.float32
    C = RET_CHUNK
    lg = np.log(f32(1.0) - f32(2.0) ** (f32(-5.0) - np.arange(RET_HEADS, dtype=f32)))
    idx = np.arange(C, dtype=f32)
    diff = idx[:, None] - idx[None, :]
    decay_in = np.where(diff >= 0, np.exp(lg[:, None, None] * np.maximum(diff, f32(0.0))), f32(0.0))
    decay_q = np.exp(lg[:, None] * (idx[None, :] + f32(1.0)))[:, :, None]
    decay_k = np.exp(lg[:, None] * (f32(C - 1.0) - idx[None, :]))[:, None, :]
    return tuple(t.astype(f32) for t in (decay_in, decay_q, decay_k))


def _retention_kernel(qk_ref, kt_ref, v_ref, g_ref, din_ref, dq_ref, dk_ref, o_ref, state_ref, r_scr, *, tr):
    t = pl.program_id(1)
    n_chunks = tr // RET_CHUNK

    @pl.when(t == 0)
    def _():
        r_scr[...] = jnp.zeros_like(r_scr)

    def key_t(c, h):
        return kt_ref[0, h * RET_QK_DIM:(h + 1) * RET_QK_DIM, c * RET_CHUNK:(c + 1) * RET_CHUNK]

    def val(c, h):
        return v_ref[0, c * RET_CHUNK:(c + 1) * RET_CHUNK, h * RET_V_DIM:(h + 1) * RET_V_DIM]

    states = [[None] * RET_HEADS for _ in range(n_chunks)]
    for h in range(RET_HEADS):
        r = r_scr[h]
        for c in range(n_chunks):
            states[c][h] = r
            kd = (key_t(c, h).astype(F32) * dk_ref[h]).astype(BF16)
            r = r * (_ret_gamma(h) ** RET_CHUNK) + jnp.dot(kd, val(c, h), preferred_element_type=F32)
        r_scr[h] = r

    for c in range(n_chunks):
        rows = slice(c * RET_CHUNK, (c + 1) * RET_CHUNK)
        for h in range(RET_HEADS):
            q = qk_ref[0, rows, OFF_RQ + h * RET_QK_DIM:OFF_RQ + (h + 1) * RET_QK_DIM]
            v = val(c, h)
            s = jnp.dot(q, key_t(c, h), preferred_element_type=F32) * din_ref[h]
            o = (jnp.dot(s.astype(BF16), v, preferred_element_type=F32)
                 + jnp.dot(q, states[c][h].astype(BF16), preferred_element_type=F32) * dq_ref[h])
            mu = jnp.mean(o, axis=1, keepdims=True)
            d = o - mu
            var = jnp.mean(d * d, axis=1, keepdims=True)
            gn = d * lax.rsqrt(var + GN_EPS)
            gate = g_ref[0, rows, h * RET_V_DIM:(h + 1) * RET_V_DIM].astype(F32)
            o_ref[0, rows, h * RET_V_DIM:(h + 1) * RET_V_DIM] = (gate * gn).astype(o_ref.dtype)

    @pl.when(t == pl.num_programs(1) - 1)
    def _():
        state_ref[0] = r_scr[...]


def _prompt_retention(dense, keys_t, *, tr):
    B, S, _ = dense.shape
    tabs = _retention_tables()
    blk = RET_V_WIDTH

    def colblk(c):
        return pl.BlockSpec((1, tr, blk), lambda b, t: (b, t, c))

    return pl.pallas_call(
        functools.partial(_retention_kernel, tr=tr),
        grid=(B, S // tr),
        in_specs=[pl.BlockSpec((1, tr, RET_QK_WIDTH), lambda b, t: (b, t, OFF_RQ // RET_QK_WIDTH)),
                  pl.BlockSpec((1, RET_QK_WIDTH, tr), lambda b, t: (b, 0, t)),
                  colblk(OFF_RV // blk), colblk(OFF_RG // blk)]
                 + [pl.BlockSpec(tab.shape, lambda b, t: (0, 0, 0)) for tab in tabs],
        out_specs=[pl.BlockSpec((1, tr, RET_V_WIDTH), lambda b, t: (b, t, 0)),
                   pl.BlockSpec((1, RET_HEADS, RET_QK_DIM, RET_V_DIM), lambda b, t: (b, 0, 0, 0))],
        out_shape=[jax.ShapeDtypeStruct((B, S, RET_V_WIDTH), BF16),
                   jax.ShapeDtypeStruct((B, RET_HEADS, RET_QK_DIM, RET_V_DIM), F32)],
        scratch_shapes=[pltpu.VMEM((RET_HEADS, RET_QK_DIM, RET_V_DIM), F32)],
        compiler_params=_cparams(("arbitrary", "arbitrary")),
        name="prompt_retention",
    )(dense, keys_t, dense, dense, *tabs)


def _layer_norm(u, g, b):
    mu = jnp.mean(u, axis=1, keepdims=True)
    d = u - mu
    var = jnp.mean(d * d, axis=1, keepdims=True)
    return d * lax.rsqrt(var + LN_EPS) * g + b


def _column(row):
    return jnp.broadcast_to(row, (LANES, LANES)).T


def _window_half_step(first_half, qkv_refs, cache_refs, o_ref, lse_ref, new_refs):
    pad = 16
    row_h = lax.broadcasted_iota(jnp.int32, (pad, LANES), 0)
    lane_h = lax.broadcasted_iota(jnp.int32, (pad, LANES), 1) // HEAD_DIM
    own = row_h == lane_h
    lane = lax.broadcasted_iota(jnp.int32, (LANES, LANES), 1)

    def pick(row):
        return jnp.where(first_half, row[:, :LANES], row[:, LANES:])

    def rolled(c_ref, n_ref, new_row):
        w = c_ref.shape[3]
        new_col = _column(new_row)
        nblk = w // LANES
        cur = pltpu.roll(c_ref[0, 0, :, 0:LANES], LANES - 1, 1)
        for j in range(nblk):
            nxt = pltpu.roll(c_ref[0, 0, :, (j + 1) * LANES:(j + 2) * LANES], LANES - 1, 1) if j + 1 < nblk else new_col
            n_ref[0, 0, :, j * LANES:(j + 1) * LANES] = jnp.where(lane < LANES - 1, cur, nxt)
            cur = nxt

    for g in range(N_GROUPS):
        dil = DILATED_PATTERNS[g][1]
        ck_ref, cv_ref = cache_refs[2 * g:2 * g + 2]
        w = ck_ref.shape[3]
        q = pick(qkv_refs[g][0, :, 0:SEG])
        kn = pick(qkv_refs[g][0, :, SEG:2 * SEG])
        vn = pick(qkv_refs[g][0, :, 2 * SEG:3 * SEG])
        qsel = jnp.where(own, jnp.broadcast_to(q, (pad, LANES)), 0.0)
        s = jnp.dot(qsel.astype(BF16), ck_ref[0, 0].astype(BF16), preferred_element_type=F32)
        pos = lax.broadcasted_iota(jnp.int32, (pad, w), 1)
        s = jnp.where((pos & (dil - 1)) == 0, s, NEG_BIG)
        sn = jnp.sum(qsel * kn, axis=1, keepdims=True)
        m = jnp.maximum(jnp.max(s, axis=1, keepdims=True), sn)
        e = jnp.exp(s - m)
        en = jnp.exp(sn - m)
        l = jnp.sum(e, axis=1, keepdims=True) + en
        o_all = lax.dot_general((e / l).astype(BF16), cv_ref[0, 0].astype(BF16), (((1,), (1,)), ((), ())),
                                preferred_element_type=F32) + (en / l) * vn
        o_ref[0, 0, :, g * LANES:(g + 1) * LANES] = jnp.sum(jnp.where(own, o_all, 0.0), axis=0, keepdims=True)
        lse_ref[0, 0, :, g * LANES:(g + 1) * LANES] = jnp.sum(jnp.where(own, m + jnp.log(l), 0.0), axis=0,
                                                              keepdims=True)
        rolled(ck_ref, new_refs[2 * g], kn)
        rolled(cv_ref, new_refs[2 * g + 1], vn)


N_MERGE_INPUTS = 23
N_SIDE_INPUTS = N_GROUPS + 2 * N_GROUPS
N_SIDE_OUTPUTS = 2 + 2 * N_GROUPS


def _merge_ffn_kernel(*refs, tm, sub, dils, side, steps_per_batch):
    (x_ref, ga_ref, gb_ref, ret_ref, o0_ref, o1_ref, o2_ref, l0_ref, l1_ref, l2_ref,
     gate1_ref, shift2_ref, scale2_ref, gate2_ref, watt_ref, wret_ref, wo_ref, wfi_ref, wfo_ref,
     ln1g_ref, ln1b_ref, ln2g_ref, ln2b_ref) = refs[:N_MERGE_INPUTS]
    n_in = N_MERGE_INPUTS + (N_SIDE_INPUTS if side else 0)
    y_ref = refs[n_in]
    n_out = 1 + (N_SIDE_OUTPUTS if side else 0)
    scratch = refs[n_in + n_out:]
    if side:
        side_in = refs[N_MERGE_INPUTS:n_in]
        side_out = refs[n_in + 1:n_in + n_out]
        step = pl.program_id(0) * steps_per_batch + pl.program_id(1)
        _window_half_step(step % 2 == 0, side_in[:N_GROUPS], side_in[N_GROUPS:], side_out[0], side_out[1],
                          side_out[2:])
    n_sub = tm // sub
    per_sub = 1 + 2 * N_GROUPS
    for s in range(n_sub):
        rows = slice(s * sub, (s + 1) * sub)
        act_scr = scratch[s * per_sub]
        stage_refs = scratch[s * per_sub + 1:(s + 1) * per_sub]

        def natural(ref, g, stage_ref):
            d = dils[g]
            n = sub // d
            if d == 1:
                return ref[0, 0, rows, :].astype(F32)
            for r in range(d):
                for u in range(GROUP_WIDTH // LANES):
                    stage_ref[u, pl.ds(r, n, stride=d), :] = (
                        ref[0, r, s * n:(s + 1) * n, u * LANES:(u + 1) * LANES].astype(F32))
            return jnp.concatenate([stage_ref[u] for u in range(GROUP_WIDTH // LANES)], axis=1)

        def mod_rows(ref):
            return ref[0] if ref.shape[1] == 1 else ref[0, rows, :]

        lses = [natural(ref, g, stage_refs[g]) for g, ref in enumerate((l0_ref, l1_ref, l2_ref))]
        outs = [natural(ref, g, stage_refs[N_GROUPS + g]) for g, ref in enumerate((o0_ref, o1_ref, o2_ref))]
        m = jnp.maximum(jnp.maximum(lses[0], lses[1]), lses[2])
        es = [jnp.exp(l - m) for l in lses]
        att = (es[0] * outs[0] + es[1] * outs[1] + es[2] * outs[2]) / (es[0] + es[1] + es[2])
        y_att = jnp.dot(att.astype(BF16), watt_ref[...], preferred_element_type=F32)
        y_ret = jnp.dot(ret_ref[0, rows, :].astype(BF16), wret_ref[...], preferred_element_type=F32)
        mix = (ga_ref[0, rows, :].astype(F32) * y_att + gb_ref[0, rows, :].astype(F32) * y_ret).astype(BF16)
        mixed = jnp.dot(mix, wo_ref[...], preferred_element_type=F32)
        x1 = _layer_norm(DEEPNORM_ALPHA * x_ref[0, rows, :] + mod_rows(gate1_ref) * mixed,
                         ln1g_ref[...], ln1b_ref[...])
        h2 = (x1 * (1.0 + mod_rows(scale2_ref)) + mod_rows(shift2_ref)).astype(BF16)
        for c0 in range(0, D_FF, FFN_CHUNK):
            c1 = min(c0 + FFN_CHUNK, D_FF)
            fg = jnp.dot(h2, wfi_ref[:, c0:c1], preferred_element_type=F32)
            fu = jnp.dot(h2, wfi_ref[:, D_FF + c0:D_FF + c1], preferred_element_type=F32)
            act_scr[:, c0:c1] = (fg * _sigmoid(fg) * fu).astype(BF16)
        ff = jnp.dot(act_scr[...], wfo_ref[...], preferred_element_type=F32)
        y_ref[0, rows, :] = _layer_norm(DEEPNORM_ALPHA * x1 + mod_rows(gate2_ref) * ff, ln2g_ref[...], ln2b_ref[...])


def _merge_ffn(x, dense, ret, o_groups, lse_groups, mod, weights, ln, *, tm, sub, dils, side=None):
    G, R, _ = x.shape
    mod_rows = mod.shape[1]
    n_tiles = R // tm
    side_args, side_in_specs, side_out_specs, side_out_shapes = [], [], [], []
    if side is not None:
        qkvs, caches = side
        n_seq = caches[0].shape[0]
        assert G * n_tiles == 2 * n_seq

        def seq_half(b, i):
            step = b * n_tiles + i
            return step // 2, step % 2

        row_spec = pl.BlockSpec((1, 1, ATTN_WIDTH), lambda b, i: (seq_half(b, i)[0], 0, 0))
        small_spec = pl.BlockSpec((1, 1, 1, N_GROUPS * LANES), lambda b, i: seq_half(b, i) + (0, 0))
        halves = [c.reshape(n_seq, 2, LANES, c.shape[2]) for c in caches]
        half_specs = [pl.BlockSpec((1, 1, LANES, c.shape[3]), lambda b, i: seq_half(b, i) + (0, 0)) for c in halves]
        side_args = list(qkvs) + halves
        side_in_specs = [row_spec] * N_GROUPS + half_specs
        side_out_specs = [small_spec, small_spec] + half_specs
        side_out_shapes = ([jax.ShapeDtypeStruct((n_seq, 2, 1, N_GROUPS * LANES), F32)] * 2
                           + [jax.ShapeDtypeStruct(c.shape, F32) for c in halves])
    sub_scratch = [pltpu.VMEM((sub, D_FF), BF16)] + [pltpu.VMEM((GROUP_WIDTH // LANES, sub, LANES), F32)] * (2 * N_GROUPS)

    def mod_spec(col):
        if mod_rows == 1:
            return pl.BlockSpec((1, 1, D_MODEL), lambda b, i: (b, 0, col))
        return pl.BlockSpec((1, tm, D_MODEL), lambda b, i: (b, i, col))

    def rows(width, col=0):
        return pl.BlockSpec((1, tm, width), lambda b, i: (b, i, col))

    group_specs = [pl.BlockSpec((1, d, tm // d, GROUP_WIDTH), lambda b, i: (b, 0, i, 0)) for d in dils]
    vec = pl.BlockSpec((1, D_MODEL), lambda b, i: (0, 0))
    outs = pl.pallas_call(
        functools.partial(_merge_ffn_kernel, tm=tm, sub=sub, dils=dils, side=side is not None,
                          steps_per_batch=n_tiles),
        grid=(G, n_tiles),
        in_specs=[rows(D_MODEL), rows(D_MODEL, OFF_GA // D_MODEL), rows(D_MODEL, OFF_GB // D_MODEL),
                  rows(RET_V_WIDTH)] + group_specs + group_specs
                 + [mod_spec(2), mod_spec(3), mod_spec(4), mod_spec(5)]
                 + [_resident(w.shape) for w in weights] + [vec] * 4 + side_in_specs,
        out_specs=[rows(D_MODEL)] + side_out_specs,
        out_shape=[jax.ShapeDtypeStruct((G, R, D_MODEL), F32)] + side_out_shapes,
        scratch_shapes=sub_scratch * (tm // sub),
        compiler_params=_cparams(("arbitrary", "arbitrary")),
        name="merge_ffn",
    )(x, dense, dense, ret, *o_groups, *lse_groups, mod, mod, mod, mod, *weights, *ln, *side_args)
    if side is None:
        return outs[0]
    y, o_s, lse_s, *rolled = outs
    return y, o_s, lse_s, [r.reshape(c.shape) for r, c in zip(rolled, side[1])]


def _sample_retention_kernel(dense_ref, state_ref, ret_ref, nstate_ref):
    for b in range(dense_ref.shape[0]):
        for h in range(RET_HEADS):
            gamma = _ret_gamma(h)
            q = dense_ref[b, :, OFF_RQ + h * RET_QK_DIM:OFF_RQ + (h + 1) * RET_QK_DIM]
            k = dense_ref[b, :, OFF_RK + h * RET_QK_DIM:OFF_RK + (h + 1) * RET_QK_DIM]
            v = dense_ref[b, :, OFF_RV + h * RET_V_DIM:OFF_RV + (h + 1) * RET_V_DIM]
            gate = dense_ref[b, :, OFF_RG + h * RET_V_DIM:OFF_RG + (h + 1) * RET_V_DIM]
            r_old = state_ref[b, h]
            q_col = jnp.concatenate([_column(q)] * (RET_V_DIM // LANES), axis=1)
            k_col = jnp.concatenate([_column(k)] * (RET_V_DIM // LANES), axis=1)
            qk = jnp.sum(q * k, axis=1, keepdims=True)
            o = qk * v + jnp.sum(q_col * r_old, axis=0, keepdims=True) * gamma
            nstate_ref[b, h] = r_old * gamma + k_col * v
            mu = jnp.mean(o, axis=1, keepdims=True)
            d = o - mu
            var = jnp.mean(d * d, axis=1, keepdims=True)
            ret_ref[b, :, h * RET_V_DIM:(h + 1) * RET_V_DIM] = gate * (d * lax.rsqrt(var + GN_EPS))


def _sample_retention(dense, state, *, nb=4):
    Bs = dense.shape[0]
    row = lambda w: pl.BlockSpec((nb, 1, w), lambda b: (b, 0, 0))
    st = pl.BlockSpec((nb, RET_HEADS, RET_QK_DIM, RET_V_DIM), lambda b: (b, 0, 0, 0))
    return pl.pallas_call(
        _sample_retention_kernel,
        grid=(Bs // nb,),
        in_specs=[row(DENSE_WIDTH), st],
        out_specs=[row(RET_V_WIDTH), st],
        out_shape=[jax.ShapeDtypeStruct((Bs, 1, RET_V_WIDTH), F32), jax.ShapeDtypeStruct(state.shape, F32)],
        compiler_params=_cparams(("arbitrary",)),
        name="sample_retention",
    )(dense, state)


def _feature_major(a):
    _, b, w, _, _ = a.shape
    return jnp.transpose(a[0], (0, 2, 3, 1)).reshape(b, GROUP_WIDTH, w)


def _row_major(a):
    b, _, w = a.shape
    return jnp.transpose(a.reshape(b, HEADS_PER_GROUP, HEAD_DIM, w), (0, 3, 1, 2))[None]


def kernel(x_prompt, x_sample, cache_k_w128, cache_v_w128, cache_k_w512, cache_v_w512, cache_k_w2048,
           cache_v_w2048, state_retention, c_prompt, c_sample, w_ada, b_ada, w_in, w_att_out, w_ret_out,
           w_o, ln1_g, ln1_b, w_ffn_in, w_ffn_out, ln2_g, ln2_b):
    assert w_in.shape[0] == 1 and x_sample.shape[1] == 1
    B, S, D = x_prompt.shape
    Bs = x_sample.shape[0]
    dils = tuple(d for _, d in DILATED_PATTERNS)

    w_main = jnp.concatenate([w_in[0][:, W_IN_ROTATE:], w_in[0][:, 2 * ATTN_WIDTH:W_IN_ROTATE]], axis=1).astype(BF16)
    w_qk = w_in[0][:, :2 * ATTN_WIDTH].astype(BF16)
    w_qk_paired = _pair_columns(w_qk)
    weights = tuple(w[0].astype(BF16) for w in (w_att_out, w_ret_out, w_o, w_ffn_in, w_ffn_out))
    ln = (ln1_g, ln1_b, ln2_g, ln2_b)

    mod = _modulation(jnp.concatenate([c_prompt, c_sample], axis=0), w_ada[0], b_ada[0])
    mod_p = mod[:B].reshape(B, 1, 6 * D)
    mod_s = mod[B:].reshape(1, Bs, 6 * D)

    tabs_p = _rope_tables(np.arange(S), paired=True)
    win_rows = tuple(min(w, S) for w, _ in DILATED_PATTERNS)
    dense_p, *rest = _in_projection(x_prompt, mod_p, w_main, w_qk_paired, tabs_p, tm=256, dils=dils, paired=True,
                                    keys_t=True, win_rows=win_rows, out_dtype=BF16)
    qkv_p, keys_t_p, wins_p = rest[:N_GROUPS], rest[N_GROUPS], rest[N_GROUPS + 1:]
    o_p, lse_p = [], []
    for g, d in enumerate(dils):
        tl = min(S // d, 1024)
        o, lse = _prompt_attention(qkv_p[g], g, tl=tl, nc=min(d, 1024 // tl))
        o_p.append(o)
        lse_p.append(lse)
    ret_p, state_p = _prompt_retention(dense_p, keys_t_p, tr=1024)

    xs = x_sample.reshape(1, Bs, D)
    tabs_s = _rope_tables(PAST_LEN + np.arange(1))
    unit = (1,) * N_GROUPS
    dense_s, *qkv_s = _in_projection(xs, mod_s, w_main, w_qk, tabs_s, tm=Bs, dils=unit, paired=False,
                                     keys_t=False, win_rows=None, out_dtype=F32)
    caches = [_feature_major(c) for c in (cache_k_w128, cache_v_w128, cache_k_w512, cache_v_w512,
                                          cache_k_w2048, cache_v_w2048)]
    ret_s, state_s = _sample_retention(dense_s.reshape(Bs, 1, DENSE_WIDTH), state_retention[0])
    y_p, o_s, lse_s, rolled = _merge_ffn(
        x_prompt, dense_p, ret_p, o_p, lse_p, mod_p, weights, ln, tm=512, sub=256, dils=dils,
        side=([a.reshape(Bs, 1, ATTN_WIDTH) for a in qkv_s], caches))

    def by_group(a):
        return [a[:, :, 0, g * LANES:(g + 1) * LANES].reshape(1, 1, Bs, GROUP_WIDTH) for g in range(N_GROUPS)]

    o_sg, lse_sg = by_group(o_s), by_group(lse_s)
    y_s = _merge_ffn(xs, dense_s, ret_s.reshape(1, Bs, RET_V_WIDTH), o_sg, lse_sg, mod_s, weights, ln,
                     tm=Bs, sub=Bs, dils=unit)

    return (y_p, y_s.reshape(Bs, 1, D),
            *[_row_major(w) for w in wins_p], state_p[None],
            *[_row_major(r) for r in rolled], state_s[None])
```

```python
import functools
import math

import jax
import jax.numpy as jnp
import numpy as np
from jax import lax
from jax.experimental import pallas as pl
from jax.experimental.pallas import tpu as pltpu

F32 = jnp.float32
BF16 = jnp.bfloat16

D_MODEL = 1024
PAST_LEN = 16384
DILATED_PATTERNS = ((128, 1), (512, 4), (2048, 16))
N_GROUPS = 3
HEADS_PER_GROUP = 4
HEAD_DIM = 64
GROUP_WIDTH = HEADS_PER_GROUP * HEAD_DIM
ATTN_WIDTH = N_GROUPS * GROUP_WIDTH
ATTN_BAND = 128
ATTN_SCALE = HEAD_DIM ** -0.5
ROPE_DIM = HEAD_DIM // 4
ROPE_THETA = 500000.0
RET_HEADS = 4
RET_QK_DIM = 128
RET_V_DIM = 256
RET_QK_WIDTH = RET_HEADS * RET_QK_DIM
RET_V_WIDTH = RET_HEADS * RET_V_DIM
RET_CHUNK = 128
RET_ROPE_THETA = 10000.0
IN_WIDTH = 3 * ATTN_WIDTH + 2 * RET_QK_WIDTH + 2 * RET_V_WIDTH + 2 * D_MODEL
D_FF = 2816
DEEPNORM_ALPHA = 2.0 ** 0.25
LN_EPS = 1e-5
GN_EPS = 1e-6

LANES = 128
SEG = 256
N_SEG = IN_WIDTH // SEG
FFN_CHUNK = 256
OFF_RQ, OFF_RK, OFF_RV, OFF_RG, OFF_GA, OFF_GB = 0, 512, 1024, 2048, 3072, 4096
OFF_QA, OFF_KA, OFF_VA = 5120, 5888, 6656
DENSE_WIDTH = OFF_QA
W_IN_ROTATE = 3 * ATTN_WIDTH
NEG_BIG = -1e30
VMEM_LIMIT = 60 * 1024 * 1024


def _cparams(sem):
    return pltpu.CompilerParams(dimension_semantics=sem, vmem_limit_bytes=VMEM_LIMIT)


def _resident(shape):
    nd = len(shape)
    return pl.BlockSpec(shape, lambda *_: (0,) * nd, pipeline_mode=pl.Buffered(1))


def _sigmoid(x):
    return 1.0 / (1.0 + jnp.exp(-x))


def _ret_gamma(h):
    return 1.0 - 2.0 ** (-5.0 - h)


def _ada_kernel(c_ref, w_ref, b_ref, o_ref):
    c = c_ref[...]
    a = (c * _sigmoid(c)).astype(BF16)
    o_ref[...] = jnp.dot(a, w_ref[...].astype(BF16), preferred_element_type=F32) + b_ref[...]


def _modulation(c_all, w_ada, b_ada):
    rows = c_all.shape[0]
    tn = 1024
    return pl.pallas_call(
        _ada_kernel,
        grid=(6 * D_MODEL // tn,),
        in_specs=[pl.BlockSpec((rows, D_MODEL), lambda j: (0, 0)),
                  pl.BlockSpec((D_MODEL, tn), lambda j: (0, j)),
                  pl.BlockSpec((1, tn), lambda j: (0, j))],
        out_specs=pl.BlockSpec((rows, tn), lambda j: (0, j)),
        out_shape=jax.ShapeDtypeStruct((rows, 6 * D_MODEL), F32),
        compiler_params=_cparams(("arbitrary",)),
        name="modulation",
    )(c_all, w_ada, b_ada.reshape(1, -1))


PAIR_SLOT = HEAD_DIM // 2


def _pair_source():
    src = []
    for half_dims in ((0, 16), (8, 40)):
        for h in range(HEADS_PER_GROUP):
            rot, rest = half_dims
            dims = list(range(rot, rot + ROPE_DIM // 2)) + list(range(rest, rest + PAIR_SLOT - ROPE_DIM // 2))
            src += [h * HEAD_DIM + d for d in dims]
    return src


def _pair_columns(w):
    run = ROPE_DIM // 2
    w5 = w.reshape(w.shape[0], -1, HEADS_PER_GROUP, HEAD_DIM // run, run)
    half_a = jnp.concatenate([w5[:, :, :, 0:1], w5[:, :, :, 2:5]], axis=3)
    half_b = jnp.concatenate([w5[:, :, :, 1:2], w5[:, :, :, 5:8]], axis=3)
    return jnp.stack([half_a, half_b], axis=2).reshape(w.shape)


def _rope_tables(pos, paired=False):
    posf = jnp.asarray(np.asarray(pos), jnp.int32).astype(F32)[:, None]
    lane = np.arange(LANES)
    half = ROPE_DIM // 2
    inv = jnp.exp(-math.log(ROPE_THETA) * jnp.arange(half, dtype=F32) * (2.0 / ROPE_DIM))
    ang = posf * inv[None, :]
    cos, sin = jnp.cos(ang), jnp.sin(ang)
    if paired:
        in_slot = lane % PAIR_SLOT
        f = in_slot % half
        att_c = jnp.where(in_slot < half, cos[:, f], 1.0)
        att_lo = jnp.where(in_slot < half, sin[:, f], 0.0)
        att_hi = att_lo
    else:
        in_head = lane % HEAD_DIM
        f = in_head % half
        att_c = jnp.where(in_head < ROPE_DIM, cos[:, f], 1.0)
        att_lo = jnp.where(in_head < half, -sin[:, f], 0.0)
        att_hi = jnp.where((in_head >= half) & (in_head < ROPE_DIM), sin[:, f], 0.0)
    rhalf = RET_QK_DIM // 2
    rinv = jnp.exp(-math.log(RET_ROPE_THETA) * jnp.arange(rhalf, dtype=F32) * (2.0 / RET_QK_DIM))
    rang = posf * rinv[None, :]
    rcos, rsin = jnp.cos(rang), jnp.sin(rang)
    ret_c = jnp.concatenate([rcos, rcos], axis=-1)
    ret_s = jnp.concatenate([-rsin, rsin], axis=-1)
    return att_c, att_lo, att_hi, ret_c, ret_s


def _inproj_kernel(x_ref, shift_ref, scale_ref, w_ref, wqk_ref, *refs, tm, dils, paired, keys_t, win_blocks,
                   win_first):
    n_tab = 5
    att_tab = [r[...] for r in refs[:3]]
    rc, rs = refs[3][...], refs[4][...]
    refs = refs[n_tab:]
    n_perm = sum(d > 1 for d in dils)
    dense_ref = refs[0]
    qkv_refs = refs[1:1 + N_GROUPS]
    n_out = len(refs) - (1 + n_perm if n_perm else 0)
    kt_ref = refs[1 + N_GROUPS] if keys_t else None
    win_refs = refs[1 + N_GROUPS + bool(keys_t):n_out]
    i = pl.program_id(1)
    hf = x_ref[0] * (1.0 + scale_ref[0]) + shift_ref[0]
    h = hf.astype(BF16)

    h_class = {}
    att_tabs = [att_tab] * N_GROUPS
    if n_perm:
        stage_ref = refs[n_out]
        n_x = D_MODEL // LANES
        for u in range(n_x):
            stage_ref[u] = hf[:, u * LANES:(u + 1) * LANES]
        moving = [t for t in att_tab if t.shape[0] == tm]
        for u, t in enumerate(moving):
            stage_ref[n_x + u] = t
        for g, hp_ref in zip([g for g in range(N_GROUPS) if dils[g] > 1], refs[n_out + 1:]):
            d = dils[g]
            n = tm // d
            for r in range(d):
                for u in range(n_x):
                    hp_ref[r * n:(r + 1) * n, u * LANES:(u + 1) * LANES] = (
                        stage_ref[u, pl.ds(r, n, stride=d), :].astype(BF16))
            h_class[g] = hp_ref
            regrouped = iter([jnp.concatenate([stage_ref[n_x + u, pl.ds(r, n, stride=d), :] for r in range(d)], axis=0)
                              for u in range(len(moving))])
            att_tabs[g] = [next(regrouped) if t.shape[0] == tm else t for t in att_tab]

    def att_rope(a, tab):
        ac, alo, ahi = tab
        if paired:
            lo, hi = a[:, :LANES], a[:, LANES:]
            return jnp.concatenate([lo * ac - hi * alo, hi * ac + lo * alo], axis=1)
        parts = []
        for u in range(SEG // LANES):
            xs = a[:, u * LANES:(u + 1) * LANES]
            parts.append(xs * ac + pltpu.roll(xs, LANES - ROPE_DIM // 2, 1) * alo
                         + pltpu.roll(xs, ROPE_DIM // 2, 1) * ahi)
        return jnp.concatenate(parts, axis=1)

    def ret_rope(a):
        parts = []
        for u in range(SEG // LANES):
            xs = a[:, u * LANES:(u + 1) * LANES]
            parts.append(xs * rc + pltpu.roll(xs, RET_QK_DIM // 2, 1) * rs)
        return jnp.concatenate(parts, axis=1)

    def emit_group(g, c, a):
        ref, d = qkv_refs[g], dils[g]
        n = tm // d
        for r in range(d):
            ref[0, r, :, c * SEG:(c + 1) * SEG] = a[r * n:(r + 1) * n, :].astype(ref.dtype)

    def wk_cols(g):
        return wqk_ref[:, ATTN_WIDTH + g * SEG:ATTN_WIDTH + (g + 1) * SEG]

    def wv_cols(g):
        return w_ref[:, DENSE_WIDTH + g * SEG:DENSE_WIDTH + (g + 1) * SEG]

    kv_f32 = {}

    def position_order(a, g, slot):
        d = dils[g]
        if d == 1:
            return a
        n = tm // d
        for u in range(SEG // LANES):
            for r in range(d):
                stage_ref[slot + u, pl.ds(r, n, stride=d), :] = a[r * n:(r + 1) * n, u * LANES:(u + 1) * LANES]
        return jnp.concatenate([stage_ref[slot + u] for u in range(SEG // LANES)], axis=1)

    def emit_windows(g):
        wb = win_blocks[g]
        assert dils[g] == 1 or wb == tm
        k, v = kv_f32[g]
        slot = 2 * (SEG // LANES) * (g % 2)
        at = position_order(k, g, slot)[tm - wb:, :].T
        if paired:
            dst = {f: lane for lane, f in enumerate(_pair_source())}
            at = jnp.concatenate([at[dst[f]:dst[f] + 8, :] for f in range(0, GROUP_WIDTH, 8)], axis=0)
        win_refs[2 * g][0] = at
        win_refs[2 * g + 1][0] = position_order(v, g, slot + SEG // LANES)[tm - wb:, :].T

    for seg in range(DENSE_WIDTH // SEG):
        c0 = seg * SEG
        a = jnp.dot(h, w_ref[:, c0:c0 + SEG], preferred_element_type=F32)
        if c0 < OFF_RK:
            a = ret_rope(a)
        elif c0 < OFF_RV:
            a = ret_rope(a) * (RET_QK_DIM ** -0.5)
            if keys_t:
                kt_ref[0, c0 - OFF_RK:c0 - OFF_RK + SEG, :] = a.T.astype(kt_ref.dtype)
        elif c0 < OFF_RG:
            pass
        elif c0 < OFF_GA:
            a = a * _sigmoid(a)
        else:
            a = _sigmoid(a)
        dense_ref[0, :, c0:c0 + SEG] = a.astype(dense_ref.dtype)

    for g in range(N_GROUPS):
        hg = h_class[g][...] if g in h_class else h
        wq = wqk_ref[:, g * SEG:(g + 1) * SEG]
        emit_group(g, 0, att_rope(jnp.dot(hg, wq, preferred_element_type=F32), att_tabs[g]) * ATTN_SCALE)
        k = att_rope(jnp.dot(hg, wk_cols(g), preferred_element_type=F32), att_tabs[g])
        v = jnp.dot(hg, wv_cols(g), preferred_element_type=F32)
        emit_group(g, 1, k)
        emit_group(g, 2, v)
        kv_f32[g] = (k, v)

    if win_refs:
        order = sorted(range(N_GROUPS), key=lambda g: win_first[g])

        def guarded(pos):
            if pos == len(order):
                return
            g = order[pos]

            @pl.when(i >= win_first[g])
            def _():
                emit_windows(g)
                guarded(pos + 1)

        guarded(0)


def _in_projection(x, mod, w_main, w_qk, tables, *, tm, dils, paired, keys_t, win_rows, out_dtype):
    G, R, _ = x.shape
    n_tiles = R // tm
    mod_blk = tm if mod.shape[1] == R else 1
    tab_blk = tm if tables[0].shape[0] == R else 1

    def mod_spec(col):
        if mod_blk == 1:
            return pl.BlockSpec((1, 1, D_MODEL), lambda b, i: (b, 0, col))
        return pl.BlockSpec((1, tm, D_MODEL), lambda b, i: (b, i, col))

    def tab_spec(tab):
        if tab.shape[0] == 1:
            return pl.BlockSpec((1, LANES), lambda b, i: (0, 0))
        return pl.BlockSpec((tm, LANES), lambda b, i: (i, 0))

    att = list(tables[:3])
    if paired:
        att[2] = np.zeros((1, LANES), np.float32)
    all_tabs = att + list(tables[3:])

    out_shapes = [jax.ShapeDtypeStruct((G, R, DENSE_WIDTH), out_dtype)]
    out_specs = [pl.BlockSpec((1, tm, DENSE_WIDTH), lambda b, i: (b, i, 0))]
    for d in dils:
        out_shapes.append(jax.ShapeDtypeStruct((G, d, R // d, ATTN_WIDTH), out_dtype))
        out_specs.append(pl.BlockSpec((1, d, tm // d, ATTN_WIDTH), lambda b, i: (b, 0, i, 0)))
    if keys_t:
        out_shapes.append(jax.ShapeDtypeStruct((G, RET_QK_WIDTH, R), out_dtype))
        out_specs.append(pl.BlockSpec((1, RET_QK_WIDTH, tm), lambda b, i: (b, 0, i)))
    win_blocks = win_first = ()
    if win_rows:
        win_blocks = tuple(min(w, tm) for w in win_rows)
        win_first = tuple(n_tiles - w // wb for w, wb in zip(win_rows, win_blocks))
        for g in range(N_GROUPS):
            first = win_first[g]
            for _ in range(2):
                out_shapes.append(jax.ShapeDtypeStruct((G, GROUP_WIDTH, win_rows[g]), F32))
                out_specs.append(pl.BlockSpec((1, GROUP_WIDTH, win_blocks[g]),
                                              lambda b, i, first=first: (b, 0, jnp.maximum(i - first, 0))))

    scratch = []
    if any(d > 1 for d in dils):
        scratch.append(pltpu.VMEM((D_MODEL // LANES + 2, tm, LANES), F32))
        scratch += [pltpu.VMEM((tm, D_MODEL), BF16) for d in dils if d > 1]
    kern = functools.partial(_inproj_kernel, tm=tm, dils=dils, paired=paired, keys_t=keys_t,
                             win_blocks=win_blocks, win_first=win_first)
    return pl.pallas_call(
        kern,
        grid=(G, n_tiles),
        in_specs=[pl.BlockSpec((1, tm, D_MODEL), lambda b, i: (b, i, 0)),
                  mod_spec(0), mod_spec(1),
                  _resident(w_main.shape), _resident(w_qk.shape)] + [tab_spec(t) for t in all_tabs],
        out_specs=out_specs,
        out_shape=out_shapes,
        scratch_shapes=scratch,
        compiler_params=_cparams(("arbitrary", "arbitrary")),
        name="in_projection",
    )(x, mod, mod, w_main, w_qk, *all_tabs)


def _attn_kernel(q_ref, k_ref, v_ref, kp_ref, vp_ref, o_ref, lse_ref, *, tl, nc):
    t = pl.program_id(2)
    row = lax.broadcasted_iota(jnp.int32, (ATTN_BAND, ATTN_BAND), 0)
    col = lax.broadcasted_iota(jnp.int32, (ATTN_BAND, ATTN_BAND), 1)
    lane = lax.broadcasted_iota(jnp.int32, (1, GROUP_WIDTH), 1)
    cur_ok = col <= row
    qk_lanes = [(lane % LANES) // PAIR_SLOT == h for h in range(HEADS_PER_GROUP)]
    v_lanes = [lane // HEAD_DIM == h for h in range(HEADS_PER_GROUP)]

    def block(q, k_prev, v_prev, k_cur, v_cur, prev_shift):
        prev_ok = col >= row + prev_shift
        mask = jnp.concatenate([prev_ok, cur_ok], axis=1)
        mask = jnp.concatenate([mask] * HEADS_PER_GROUP, axis=0)
        kk = jnp.concatenate([k_prev, k_cur], axis=0)
        vv = jnp.concatenate([v_prev, v_cur], axis=0)
        qs = jnp.concatenate([jnp.where(sel, q, jnp.zeros_like(q)) for sel in qk_lanes], axis=0)
        s = lax.dot_general(qs, kk, (((1,), (1,)), ((), ())), preferred_element_type=F32)
        s = jnp.where(mask, s, NEG_BIG)
        m = jnp.max(s, axis=1, keepdims=True)
        e = jnp.exp(s - m)
        l = jnp.sum(e, axis=1, keepdims=True)
        oh = jnp.dot((e / l).astype(BF16), vv, preferred_element_type=F32)
        lse = m + jnp.log(l)
        o = jnp.zeros((ATTN_BAND, GROUP_WIDTH), F32)
        lo = jnp.zeros((ATTN_BAND, GROUP_WIDTH), F32)
        for h, sel in enumerate(v_lanes):
            rows = slice(h * ATTN_BAND, (h + 1) * ATTN_BAND)
            o = jnp.where(sel, oh[rows], o)
            lo = jnp.where(sel, lse[rows], lo)
        return o, lo

    first = jnp.where(t > 0, 0, ATTN_BAND)
    for c in range(nc):
        for j in range(tl // ATTN_BAND):
            j0, jp = j * ATTN_BAND, (j - 1) * ATTN_BAND
            rows = slice(j0, j0 + ATTN_BAND)
            if j == 0:
                k_prev, v_prev, shift = kp_ref[0, c], vp_ref[0, c], first
            else:
                k_prev, v_prev, shift = k_ref[0, c, jp:j0, :], v_ref[0, c, jp:j0, :], 0
            o, lse = block(q_ref[0, c, rows, :], k_prev, v_prev, k_ref[0, c, rows, :], v_ref[0, c, rows, :], shift)
            o_ref[0, c, rows, :] = o.astype(o_ref.dtype)
            lse_ref[0, c, rows, :] = lse


def _prompt_attention(qkv, g, *, tl, nc):
    B, dil, L, _ = qkv.shape
    sub = tl // ATTN_BAND

    def cur(c):
        return pl.BlockSpec((1, nc, tl, SEG), lambda b, r, t: (b, r, t, c))

    def prev(c):
        return pl.BlockSpec((1, nc, ATTN_BAND, SEG), lambda b, r, t: (b, r, jnp.maximum(t * sub - 1, 0), c))

    out_spec = pl.BlockSpec((1, nc, tl, SEG), lambda b, r, t: (b, r, t, 0))
    return pl.pallas_call(
        functools.partial(_attn_kernel, tl=tl, nc=nc),
        grid=(B, dil // nc, L // tl),
        in_specs=[cur(0), cur(1), cur(2), prev(1), prev(2)],
        out_specs=[out_spec, out_spec],
        out_shape=[jax.ShapeDtypeStruct((B, dil, L, SEG), BF16),
                   jax.ShapeDtypeStruct((B, dil, L, SEG), F32)],
        compiler_params=_cparams(("arbitrary", "arbitrary", "arbitrary")),
        name="prompt_attention_g%d" % g,
    )(qkv, qkv, qkv, qkv, qkv)


def _retention_tables():
    f32 = np.float32
    C = RET_CHUNK
    lg = np.log(f32(1.0) - f32(2.0) ** (f32(-5.0) - np.arange(RET_HEADS, dtype=f32)))
    idx = np.arange(C, dtype=f32)
    diff = idx[:, None] - idx[None, :]
    decay_in = np.where(diff >= 0, np.exp(lg[:, None, None] * np.maximum(diff, f32(0.0))), f32(0.0))
    decay_q = np.exp(lg[:, None] * (idx[None, :] + f32(1.0)))[:, :, None]
    decay_k = np.exp(lg[:, None] * (f32(C - 1.0) - idx[None, :]))[:, None, :]
    return tuple(t.astype(f32) for t in (decay_in, decay_q, decay_k))


def _retention_kernel(qk_ref, kt_ref, v_ref, din_ref, dq_ref, dk_ref, o_ref, state_ref, r_scr, *, tr):
    t = pl.program_id(1)
    n_chunks = tr // RET_CHUNK

    @pl.when(t == 0)
    def _():
        r_scr[...] = jnp.zeros_like(r_scr)

    def key_t(c, h):
        return kt_ref[0, h * RET_QK_DIM:(h + 1) * RET_QK_DIM, c * RET_CHUNK:(c + 1) * RET_CHUNK]

    def val(c, h):
        return v_ref[0, c * RET_CHUNK:(c + 1) * RET_CHUNK, h * RET_V_DIM:(h + 1) * RET_V_DIM]

    states = [[None] * RET_HEADS for _ in range(n_chunks)]
    for h in range(RET_HEADS):
        r = r_scr[h]
        for c in range(n_chunks):
            states[c][h] = r
            kd = (key_t(c, h).astype(F32) * dk_ref[h]).astype(BF16)
            r = r * (_ret_gamma(h) ** RET_CHUNK) + jnp.dot(kd, val(c, h), preferred_element_type=F32)
        r_scr[h] = r

    for c in range(n_chunks):
        rows = slice(c * RET_CHUNK, (c + 1) * RET_CHUNK)
        for h in range(RET_HEADS):
            q = qk_ref[0, rows, OFF_RQ + h * RET_QK_DIM:OFF_RQ + (h + 1) * RET_QK_DIM]
            v = val(c, h)
            s = jnp.dot(q, key_t(c, h), preferred_element_type=F32) * din_ref[h]
            o = (jnp.dot(s.astype(BF16), v, preferred_element_type=F32)
                 + jnp.dot(q, states[c][h].astype(BF16), preferred_element_type=F32) * dq_ref[h])
            mu = jnp.mean(o, axis=1, keepdims=True)
            d = o - mu
            var = jnp.mean(d * d, axis=1, keepdims=True)
            o_ref[0, rows, h * RET_V_DIM:(h + 1) * RET_V_DIM] = (d * lax.rsqrt(var + GN_EPS)).astype(o_ref.dtype)

    @pl.when(t == pl.num_programs(1) - 1)
    def _():
        state_ref[0] = r_scr[...]


def _prompt_retention(dense, keys_t, *, tr):
    B, S, _ = dense.shape
    tabs = _retention_tables()
    blk = RET_V_WIDTH

    def colblk(c):
        return pl.BlockSpec((1, tr, blk), lambda b, t: (b, t, c))

    return pl.pallas_call(
        functools.partial(_retention_kernel, tr=tr),
        grid=(B, S // tr),
        in_specs=[pl.BlockSpec((1, tr, RET_QK_WIDTH), lambda b, t: (b, t, OFF_RQ // RET_QK_WIDTH)),
                  pl.BlockSpec((1, RET_QK_WIDTH, tr), lambda b, t: (b, 0, t)),
                  colblk(OFF_RV // blk)]
                 + [pl.BlockSpec(tab.shape, lambda b, t: (0, 0, 0)) for tab in tabs],
        out_specs=[pl.BlockSpec((1, tr, RET_V_WIDTH), lambda b, t: (b, t, 0)),
                   pl.BlockSpec((1, RET_HEADS, RET_QK_DIM, RET_V_DIM), lambda b, t: (b, 0, 0, 0))],
        out_shape=[jax.ShapeDtypeStruct((B, S, RET_V_WIDTH), BF16),
                   jax.ShapeDtypeStruct((B, RET_HEADS, RET_QK_DIM, RET_V_DIM), F32)],
        scratch_shapes=[pltpu.VMEM((RET_HEADS, RET_QK_DIM, RET_V_DIM), F32)],
        compiler_params=_cparams(("arbitrary", "arbitrary")),
        name="prompt_retention",
    )(dense, keys_t, dense, *tabs)


def _layer_norm(u, g, b):
    mu = jnp.mean(u, axis=1, keepdims=True)
    d = u - mu
    var = jnp.mean(d * d, axis=1, keepdims=True)
    return d * lax.rsqrt(var + LN_EPS) * g + b


def _column(row):
    return jnp.broadcast_to(row, (LANES, LANES)).T


def _window_half_step(first_half, qkv_refs, cache_refs, o_ref, lse_ref, new_refs):
    pad = 16
    row_h = lax.broadcasted_iota(jnp.int32, (pad, LANES), 0)
    lane_h = lax.broadcasted_iota(jnp.int32, (pad, LANES), 1) // HEAD_DIM
    own = row_h == lane_h
    lane = lax.broadcasted_iota(jnp.int32, (LANES, LANES), 1)

    def pick(row):
        return jnp.where(first_half, row[:, :LANES], row[:, LANES:])

    def rolled(c_ref, n_ref, new_row):
        w = c_ref.shape[3]
        new_col = _column(new_row)
        nblk = w // LANES
        cur = pltpu.roll(c_ref[0, 0, :, 0:LANES], LANES - 1, 1)
        for j in range(nblk):
            nxt = pltpu.roll(c_ref[0, 0, :, (j + 1) * LANES:(j + 2) * LANES], LANES - 1, 1) if j + 1 < nblk else new_col
            n_ref[0, 0, :, j * LANES:(j + 1) * LANES] = jnp.where(lane < LANES - 1, cur, nxt)
            cur = nxt

    for g in range(N_GROUPS):
        dil = DILATED_PATTERNS[g][1]
        ck_ref, cv_ref = cache_refs[2 * g:2 * g + 2]
        w = ck_ref.shape[3]
        q = pick(qkv_refs[g][0, :, 0:SEG])
        kn = pick(qkv_refs[g][0, :, SEG:2 * SEG])
        vn = pick(qkv_refs[g][0, :, 2 * SEG:3 * SEG])
        qsel = jnp.where(own, jnp.broadcast_to(q, (pad, LANES)), 0.0)
        s = jnp.dot(qsel.astype(BF16), ck_ref[0, 0].astype(BF16), preferred_element_type=F32)
        pos = lax.broadcasted_iota(jnp.int32, (pad, w), 1)
        s = jnp.where((pos & (dil - 1)) == 0, s, NEG_BIG)
        sn = jnp.sum(qsel * kn, axis=1, keepdims=True)
        m = jnp.maximum(jnp.max(s, axis=1, keepdims=True), sn)
        e = jnp.exp(s - m)
        en = jnp.exp(sn - m)
        l = jnp.sum(e, axis=1, keepdims=True) + en
        o_all = lax.dot_general((e / l).astype(BF16), cv_ref[0, 0].astype(BF16), (((1,), (1,)), ((), ())),
                                preferred_element_type=F32) + (en / l) * vn
        o_ref[0, 0, :, g * LANES:(g + 1) * LANES] = jnp.sum(jnp.where(own, o_all, 0.0), axis=0, keepdims=True)
        lse_ref[0, 0, :, g * LANES:(g + 1) * LANES] = jnp.sum(jnp.where(own, m + jnp.log(l), 0.0), axis=0,
                                                              keepdims=True)
        rolled(ck_ref, new_refs[2 * g], kn)
        rolled(cv_ref, new_refs[2 * g + 1], vn)


N_MERGE_INPUTS = 24
N_SIDE_INPUTS = N_GROUPS + 2 * N_GROUPS
N_SIDE_OUTPUTS = 2 + 2 * N_GROUPS


def _merge_ffn_kernel(*refs, tm, sub, dils, side, steps_per_batch):
    (x_ref, ga_ref, gb_ref, rg_ref, ret_ref, o0_ref, o1_ref, o2_ref, l0_ref, l1_ref, l2_ref,
     gate1_ref, shift2_ref, scale2_ref, gate2_ref, watt_ref, wret_ref, wo_ref, wfi_ref, wfo_ref,
     ln1g_ref, ln1b_ref, ln2g_ref, ln2b_ref) = refs[:N_MERGE_INPUTS]
    n_in = N_MERGE_INPUTS + (N_SIDE_INPUTS if side else 0)
    y_ref = refs[n_in]
    n_out = 1 + (N_SIDE_OUTPUTS if side else 0)
    scratch = refs[n_in + n_out:]
    if side:
        side_in = refs[N_MERGE_INPUTS:n_in]
        side_out = refs[n_in + 1:n_in + n_out]
        step = pl.program_id(0) * steps_per_batch + pl.program_id(1)
        _window_half_step(step % 2 == 0, side_in[:N_GROUPS], side_in[N_GROUPS:], side_out[0], side_out[1],
                          side_out[2:])
    n_sub = tm // sub
    per_sub = 1 + 2 * N_GROUPS
    for s in range(n_sub):
        rows = slice(s * sub, (s + 1) * sub)
        act_scr = scratch[s * per_sub]
        stage_refs = scratch[s * per_sub + 1:(s + 1) * per_sub]

        def natural(ref, g, stage_ref):
            d = dils[g]
            n = sub // d
            if d == 1:
                return ref[0, 0, rows, :].astype(F32)
            for r in range(d):
                for u in range(GROUP_WIDTH // LANES):
                    stage_ref[u, pl.ds(r, n, stride=d), :] = (
                        ref[0, r, s * n:(s + 1) * n, u * LANES:(u + 1) * LANES].astype(F32))
            return jnp.concatenate([stage_ref[u] for u in range(GROUP_WIDTH // LANES)], axis=1)

        def mod_rows(ref):
            return ref[0] if ref.shape[1] == 1 else ref[0, rows, :]

        lses = [natural(ref, g, stage_refs[g]) for g, ref in enumerate((l0_ref, l1_ref, l2_ref))]
        outs = [natural(ref, g, stage_refs[N_GROUPS + g]) for g, ref in enumerate((o0_ref, o1_ref, o2_ref))]
        m = jnp.maximum(jnp.maximum(lses[0], lses[1]), lses[2])
        es = [jnp.exp(l - m) for l in lses]
        att = (es[0] * outs[0] + es[1] * outs[1] + es[2] * outs[2]) / (es[0] + es[1] + es[2])
        y_att = jnp.dot(att.astype(BF16), watt_ref[...], preferred_element_type=F32)
        gated = rg_ref[0, rows, :].astype(F32) * ret_ref[0, rows, :].astype(F32)
        y_ret = jnp.dot(gated.astype(BF16), wret_ref[...], preferred_element_type=F32)
        mix = (ga_ref[0, rows, :].astype(F32) * y_att + gb_ref[0, rows, :].astype(F32) * y_ret).astype(BF16)
        mixed = jnp.dot(mix, wo_ref[...], preferred_element_type=F32)
        x1 = _layer_norm(DEEPNORM_ALPHA * x_ref[0, rows, :] + mod_rows(gate1_ref) * mixed,
                         ln1g_ref[...], ln1b_ref[...])
        h2 = (x1 * (1.0 + mod_rows(scale2_ref)) + mod_rows(shift2_ref)).astype(BF16)
        for c0 in range(0, D_FF, FFN_CHUNK):
            c1 = min(c0 + FFN_CHUNK, D_FF)
            fg = jnp.dot(h2, wfi_ref[:, c0:c1], preferred_element_type=F32)
            fu = jnp.dot(h2, wfi_ref[:, D_FF + c0:D_FF + c1], preferred_element_type=F32)
            act_scr[:, c0:c1] = (fg * _sigmoid(fg) * fu).astype(BF16)
        ff = jnp.dot(act_scr[...], wfo_ref[...], preferred_element_type=F32)
        y_ref[0, rows, :] = _layer_norm(DEEPNORM_ALPHA * x1 + mod_rows(gate2_ref) * ff, ln2g_ref[...], ln2b_ref[...])


def _merge_ffn(x, dense, ret, o_groups, lse_groups, mod, weights, ln, *, tm, sub, dils, side=None):
    G, R, _ = x.shape
    mod_rows = mod.shape[1]
    n_tiles = R // tm
    side_args, side_in_specs, side_out_specs, side_out_shapes = [], [], [], []
    if side is not None:
        qkvs, caches = side
        n_seq = caches[0].shape[0]
        assert G * n_tiles == 2 * n_seq

        def seq_half(b, i):
            step = b * n_tiles + i
            return step // 2, step % 2

        row_spec = pl.BlockSpec((1, 1, ATTN_WIDTH), lambda b, i: (seq_half(b, i)[0], 0, 0))
        small_spec = pl.BlockSpec((1, 1, 1, N_GROUPS * LANES), lambda b, i: seq_half(b, i) + (0, 0))
        halves = [c.reshape(n_seq, 2, LANES, c.shape[2]) for c in caches]
        half_specs = [pl.BlockSpec((1, 1, LANES, c.shape[3]), lambda b, i: seq_half(b, i) + (0, 0)) for c in halves]
        side_args = list(qkvs) + halves
        side_in_specs = [row_spec] * N_GROUPS + half_specs
        side_out_specs = [small_spec, small_spec] + half_specs
        side_out_shapes = ([jax.ShapeDtypeStruct((n_seq, 2, 1, N_GROUPS * LANES), F32)] * 2
                           + [jax.ShapeDtypeStruct(c.shape, F32) for c in halves])
    sub_scratch = [pltpu.VMEM((sub, D_FF), BF16)] + [pltpu.VMEM((GROUP_WIDTH // LANES, sub, LANES), F32)] * (2 * N_GROUPS)

    def mod_spec(col):
        if mod_rows == 1:
            return pl.BlockSpec((1, 1, D_MODEL), lambda b, i: (b, 0, col))
        return pl.BlockSpec((1, tm, D_MODEL), lambda b, i: (b, i, col))

    def rows(width, col=0):
        return pl.BlockSpec((1, tm, width), lambda b, i: (b, i, col))

    group_specs = [pl.BlockSpec((1, d, tm // d, GROUP_WIDTH), lambda b, i: (b, 0, i, 0)) for d in dils]
    vec = pl.BlockSpec((1, D_MODEL), lambda b, i: (0, 0))
    outs = pl.pallas_call(
        functools.partial(_merge_ffn_kernel, tm=tm, sub=sub, dils=dils, side=side is not None,
                          steps_per_batch=n_tiles),
        grid=(G, n_tiles),
        in_specs=[rows(D_MODEL), rows(D_MODEL, OFF_GA // D_MODEL), rows(D_MODEL, OFF_GB // D_MODEL),
                  rows(D_MODEL, OFF_RG // D_MODEL), rows(RET_V_WIDTH)] + group_specs + group_specs
                 + [mod_spec(2), mod_spec(3), mod_spec(4), mod_spec(5)]
                 + [_resident(w.shape) for w in weights] + [vec] * 4 + side_in_specs,
        out_specs=[rows(D_MODEL)] + side_out_specs,
        out_shape=[jax.ShapeDtypeStruct((G, R, D_MODEL), F32)] + side_out_shapes,
        scratch_shapes=sub_scratch * (tm // sub),
        compiler_params=_cparams(("arbitrary", "arbitrary")),
        name="merge_ffn",
    )(x, dense, dense, dense, ret, *o_groups, *lse_groups, mod, mod, mod, mod, *weights, *ln, *side_args)
    if side is None:
        return outs[0]
    y, o_s, lse_s, *rolled = outs
    return y, o_s, lse_s, [r.reshape(c.shape) for r, c in zip(rolled, side[1])]


def _sample_retention_kernel(dense_ref, state_ref, ret_ref, nstate_ref):
    for b in range(dense_ref.shape[0]):
        for h in range(RET_HEADS):
            gamma = _ret_gamma(h)
            q = dense_ref[b, :, OFF_RQ + h * RET_QK_DIM:OFF_RQ + (h + 1) * RET_QK_DIM]
            k = dense_ref[b, :, OFF_RK + h * RET_QK_DIM:OFF_RK + (h + 1) * RET_QK_DIM]
            v = dense_ref[b, :, OFF_RV + h * RET_V_DIM:OFF_RV + (h + 1) * RET_V_DIM]
            r_old = state_ref[b, h]
            q_col = jnp.concatenate([_column(q)] * (RET_V_DIM // LANES), axis=1)
            k_col = jnp.concatenate([_column(k)] * (RET_V_DIM // LANES), axis=1)
            qk = jnp.sum(q * k, axis=1, keepdims=True)
            o = qk * v + jnp.sum(q_col * r_old, axis=0, keepdims=True) * gamma
            nstate_ref[b, h] = r_old * gamma + k_col * v
            mu = jnp.mean(o, axis=1, keepdims=True)
            d = o - mu
            var = jnp.mean(d * d, axis=1, keepdims=True)
            ret_ref[b, :, h * RET_V_DIM:(h + 1) * RET_V_DIM] = d * lax.rsqrt(var + GN_EPS)


def _sample_retention(dense, state, *, nb=4):
    Bs = dense.shape[0]
    row = lambda w: pl.BlockSpec((nb, 1, w), lambda b: (b, 0, 0))
    st = pl.BlockSpec((nb, RET_HEADS, RET_QK_DIM, RET_V_DIM), lambda b: (b, 0, 0, 0))
    return pl.pallas_call(
        _sample_retention_kernel,
        grid=(Bs // nb,),
        in_specs=[row(DENSE_WIDTH), st],
        out_specs=[row(RET_V_WIDTH), st],
        out_shape=[jax.ShapeDtypeStruct((Bs, 1, RET_V_WIDTH), F32), jax.ShapeDtypeStruct(state.shape, F32)],
        compiler_params=_cparams(("arbitrary",)),
        name="sample_retention",
    )(dense, state)


def _feature_major(a):
    _, b, w, _, _ = a.shape
    return jnp.transpose(a[0], (0, 2, 3, 1)).reshape(b, GROUP_WIDTH, w)


def _row_major(a):
    b, _, w = a.shape
    return jnp.transpose(a.reshape(b, HEADS_PER_GROUP, HEAD_DIM, w), (0, 3, 1, 2))[None]


def kernel(x_prompt, x_sample, cache_k_w128, cache_v_w128, cache_k_w512, cache_v_w512, cache_k_w2048,
           cache_v_w2048, state_retention, c_prompt, c_sample, w_ada, b_ada, w_in, w_att_out, w_ret_out,
           w_o, ln1_g, ln1_b, w_ffn_in, w_ffn_out, ln2_g, ln2_b):
    assert w_in.shape[0] == 1 and x_sample.shape[1] == 1
    B, S, D = x_prompt.shape
    Bs = x_sample.shape[0]
    dils = tuple(d for _, d in DILATED_PATTERNS)

    w_main = jnp.concatenate([w_in[0][:, W_IN_ROTATE:], w_in[0][:, 2 * ATTN_WIDTH:W_IN_ROTATE]], axis=1).astype(BF16)
    w_qk = w_in[0][:, :2 * ATTN_WIDTH].astype(BF16)
    w_qk_paired = _pair_columns(w_qk)
    weights = tuple(w[0].astype(BF16) for w in (w_att_out, w_ret_out, w_o, w_ffn_in, w_ffn_out))
    ln = (ln1_g, ln1_b, ln2_g, ln2_b)

    mod = _modulation(jnp.concatenate([c_prompt, c_sample], axis=0), w_ada[0], b_ada[0])
    mod_p = mod[:B].reshape(B, 1, 6 * D)
    mod_s = mod[B:].reshape(1, Bs, 6 * D)

    tabs_p = _rope_tables(np.arange(S), paired=True)
    win_rows = tuple(min(w, S) for w, _ in DILATED_PATTERNS)
    dense_p, *rest = _in_projection(x_prompt, mod_p, w_main, w_qk_paired, tabs_p, tm=256, dils=dils, paired=True,
                                    keys_t=True, win_rows=win_rows, out_dtype=BF16)
    qkv_p, keys_t_p, wins_p = rest[:N_GROUPS], rest[N_GROUPS], rest[N_GROUPS + 1:]
    o_p, lse_p = [], []
    for g, d in enumerate(dils):
        tl = min(S // d, 1024)
        o, lse = _prompt_attention(qkv_p[g], g, tl=tl, nc=min(d, 1024 // tl))
        o_p.append(o)
        lse_p.append(lse)
    ret_p, state_p = _prompt_retention(dense_p, keys_t_p, tr=1024)

    xs = x_sample.reshape(1, Bs, D)
    tabs_s = _rope_tables(PAST_LEN + np.arange(1))
    unit = (1,) * N_GROUPS
    dense_s, *qkv_s = _in_projection(xs, mod_s, w_main, w_qk, tabs_s, tm=Bs, dils=unit, paired=False,
                                     keys_t=False, win_rows=None, out_dtype=F32)
    caches = [_feature_major(c) for c in (cache_k_w128, cache_v_w128, cache_k_w512, cache_v_w512,
                                          cache_k_w2048, cache_v_w2048)]
    ret_s, state_s = _sample_retention(dense_s.reshape(Bs, 1, DENSE_WIDTH), state_retention[0])
    y_p, o_s, lse_s, rolled = _merge_ffn(
        x_prompt, dense_p, ret_p, o_p, lse_p, mod_p, weights, ln, tm=512, sub=256, dils=dils,
        side=([a.reshape(Bs, 1, ATTN_WIDTH) for a in qkv_s], caches))

    def by_group(a):
        return [a[:, :, 0, g * LANES:(g + 1) * LANES].reshape(1, 1, Bs, GROUP_WIDTH) for g in range(N_GROUPS)]

    o_sg, lse_sg = by_group(o_s), by_group(lse_s)
    y_s = _merge_ffn(xs, dense_s, ret_s.reshape(1, Bs, RET_V_WIDTH), o_sg, lse_sg, mod_s, weights, ln,
                     tm=Bs, sub=Bs, dils=unit)

    return (y_p, y_s.reshape(Bs, 1, D),
            *[_row_major(w) for w in wins_p], state_p[None],
            *[_row_major(r) for r in rolled], state_s[None])
```

```python
import functools
import math

import jax
import jax.numpy as jnp
import numpy as np
from jax import lax
from jax.experimental import pallas as pl
from jax.experimental.pallas import tpu as pltpu

F32 = jnp.float32
BF16 = jnp.bfloat16

D_MODEL = 1024
PAST_LEN = 16384
DILATED_PATTERNS = ((128, 1), (512, 4), (2048, 16))
N_GROUPS = 3
HEADS_PER_GROUP = 4
HEAD_DIM = 64
GROUP_WIDTH = HEADS_PER_GROUP * HEAD_DIM
ATTN_WIDTH = N_GROUPS * GROUP_WIDTH
ATTN_BAND = 128
ATTN_SCALE = HEAD_DIM ** -0.5
ROPE_DIM = HEAD_DIM // 4
ROPE_THETA = 500000.0
RET_HEADS = 4
RET_QK_DIM = 128
RET_V_DIM = 256
RET_QK_WIDTH = RET_HEADS * RET_QK_DIM
RET_V_WIDTH = RET_HEADS * RET_V_DIM
RET_CHUNK = 128
RET_ROPE_THETA = 10000.0
IN_WIDTH = 3 * ATTN_WIDTH + 2 * RET_QK_WIDTH + 2 * RET_V_WIDTH + 2 * D_MODEL
D_FF = 2816
DEEPNORM_ALPHA = 2.0 ** 0.25
LN_EPS = 1e-5
GN_EPS = 1e-6

LANES = 128
SEG = 256
N_SEG = IN_WIDTH // SEG
FFN_CHUNK = 256
SCORE_LOOKAHEAD = 2
OFF_RQ, OFF_RK, OFF_RV, OFF_RG, OFF_GA, OFF_GB = 0, 512, 1024, 2048, 3072, 4096
OFF_QA, OFF_KA, OFF_VA = 5120, 5888, 6656
DENSE_WIDTH = OFF_QA
W_IN_ROTATE = 3 * ATTN_WIDTH
NEG_BIG = -1e30
VMEM_LIMIT = 60 * 1024 * 1024


def _cparams(sem):
    return pltpu.CompilerParams(dimension_semantics=sem, vmem_limit_bytes=VMEM_LIMIT)


def _resident(shape):
    nd = len(shape)
    return pl.BlockSpec(shape, lambda *_: (0,) * nd, pipeline_mode=pl.Buffered(1))


def _sigmoid(x):
    return 1.0 / (1.0 + jnp.exp(-x))


def _ret_gamma(h):
    return 1.0 - 2.0 ** (-5.0 - h)


def _ada_kernel(c_ref, w_ref, b_ref, o_ref):
    c = c_ref[...]
    a = (c * _sigmoid(c)).astype(BF16)
    o_ref[...] = jnp.dot(a, w_ref[...].astype(BF16), preferred_element_type=F32) + b_ref[...]


def _modulation(c_all, w_ada, b_ada):
    rows = c_all.shape[0]
    tn = 1024
    return pl.pallas_call(
        _ada_kernel,
        grid=(6 * D_MODEL // tn,),
        in_specs=[pl.BlockSpec((rows, D_MODEL), lambda j: (0, 0)),
                  pl.BlockSpec((D_MODEL, tn), lambda j: (0, j)),
                  pl.BlockSpec((1, tn), lambda j: (0, j))],
        out_specs=pl.BlockSpec((rows, tn), lambda j: (0, j)),
        out_shape=jax.ShapeDtypeStruct((rows, 6 * D_MODEL), F32),
        compiler_params=_cparams(("arbitrary",)),
        name="modulation",
    )(c_all, w_ada, b_ada.reshape(1, -1))


PAIR_SLOT = HEAD_DIM // 2


def _pair_source():
    src = []
    for half_dims in ((0, 16), (8, 40)):
        for h in range(HEADS_PER_GROUP):
            rot, rest = half_dims
            dims = list(range(rot, rot + ROPE_DIM // 2)) + list(range(rest, rest + PAIR_SLOT - ROPE_DIM // 2))
            src += [h * HEAD_DIM + d for d in dims]
    return src


def _pair_columns(w):
    run = ROPE_DIM // 2
    w5 = w.reshape(w.shape[0], -1, HEADS_PER_GROUP, HEAD_DIM // run, run)
    half_a = jnp.concatenate([w5[:, :, :, 0:1], w5[:, :, :, 2:5]], axis=3)
    half_b = jnp.concatenate([w5[:, :, :, 1:2], w5[:, :, :, 5:8]], axis=3)
    return jnp.stack([half_a, half_b], axis=2).reshape(w.shape)


def _rope_tables(pos, paired=False):
    posf = jnp.asarray(np.asarray(pos), jnp.int32).astype(F32)[:, None]
    lane = np.arange(LANES)
    half = ROPE_DIM // 2
    inv = jnp.exp(-math.log(ROPE_THETA) * jnp.arange(half, dtype=F32) * (2.0 / ROPE_DIM))
    ang = posf * inv[None, :]
    cos, sin = jnp.cos(ang), jnp.sin(ang)
    if paired:
        in_slot = lane % PAIR_SLOT
        f = in_slot % half
        att_c = jnp.where(in_slot < half, cos[:, f], 1.0)
        att_lo = jnp.where(in_slot < half, sin[:, f], 0.0)
        att_hi = att_lo
    else:
        in_head = lane % HEAD_DIM
        f = in_head % half
        att_c = jnp.where(in_head < ROPE_DIM, cos[:, f], 1.0)
        att_lo = jnp.where(in_head < half, -sin[:, f], 0.0)
        att_hi = jnp.where((in_head >= half) & (in_head < ROPE_DIM), sin[:, f], 0.0)
    rhalf = RET_QK_DIM // 2
    rinv = jnp.exp(-math.log(RET_ROPE_THETA) * jnp.arange(rhalf, dtype=F32) * (2.0 / RET_QK_DIM))
    rang = posf * rinv[None, :]
    rcos, rsin = jnp.cos(rang), jnp.sin(rang)
    ret_c = jnp.concatenate([rcos, rcos], axis=-1)
    ret_s = jnp.concatenate([-rsin, rsin], axis=-1)
    return att_c, att_lo, att_hi, ret_c, ret_s


def _inproj_kernel(x_ref, shift_ref, scale_ref, w_ref, wqk_ref, *refs, tm, dils, paired, keys_t, win_blocks,
                   win_first):
    n_tab = 5
    att_tab = [r[...] for r in refs[:3]]
    rc, rs = refs[3][...], refs[4][...]
    refs = refs[n_tab:]
    n_perm = sum(d > 1 for d in dils)
    dense_ref = refs[0]
    qkv_refs = refs[1:1 + N_GROUPS]
    n_out = len(refs) - (1 + n_perm if n_perm else 0)
    kt_ref = refs[1 + N_GROUPS] if keys_t else None
    win_refs = refs[1 + N_GROUPS + bool(keys_t):n_out]
    i = pl.program_id(1)
    hf = x_ref[0] * (1.0 + scale_ref[0]) + shift_ref[0]
    h = hf.astype(BF16)

    h_class = {}
    att_tabs = [att_tab] * N_GROUPS
    if n_perm:
        stage_ref = refs[n_out]
        n_x = D_MODEL // LANES
        for u in range(n_x):
            stage_ref[u] = hf[:, u * LANES:(u + 1) * LANES]
        moving = [t for t in att_tab if t.shape[0] == tm]
        for u, t in enumerate(moving):
            stage_ref[n_x + u] = t
        for g, hp_ref in zip([g for g in range(N_GROUPS) if dils[g] > 1], refs[n_out + 1:]):
            d = dils[g]
            n = tm // d
            for r in range(d):
                for u in range(n_x):
                    hp_ref[r * n:(r + 1) * n, u * LANES:(u + 1) * LANES] = (
                        stage_ref[u, pl.ds(r, n, stride=d), :].astype(BF16))
            h_class[g] = hp_ref
            regrouped = iter([jnp.concatenate([stage_ref[n_x + u, pl.ds(r, n, stride=d), :] for r in range(d)], axis=0)
                              for u in range(len(moving))])
            att_tabs[g] = [next(regrouped) if t.shape[0] == tm else t for t in att_tab]

    def att_rope(a, tab):
        ac, alo, ahi = tab
        if paired:
            lo, hi = a[:, :LANES], a[:, LANES:]
            return jnp.concatenate([lo * ac - hi * alo, hi * ac + lo * alo], axis=1)
        parts = []
        for u in range(SEG // LANES):
            xs = a[:, u * LANES:(u + 1) * LANES]
            parts.append(xs * ac + pltpu.roll(xs, LANES - ROPE_DIM // 2, 1) * alo
                         + pltpu.roll(xs, ROPE_DIM // 2, 1) * ahi)
        return jnp.concatenate(parts, axis=1)

    def ret_rope(a):
        parts = []
        for u in range(SEG // LANES):
            xs = a[:, u * LANES:(u + 1) * LANES]
            parts.append(xs * rc + pltpu.roll(xs, RET_QK_DIM // 2, 1) * rs)
        return jnp.concatenate(parts, axis=1)

    def emit_group(g, c, a):
        ref, d = qkv_refs[g], dils[g]
        n = tm // d
        for r in range(d):
            ref[0, r, :, c * SEG:(c + 1) * SEG] = a[r * n:(r + 1) * n, :].astype(ref.dtype)

    def wk_cols(g):
        return wqk_ref[:, ATTN_WIDTH + g * SEG:ATTN_WIDTH + (g + 1) * SEG]

    def wv_cols(g):
        return w_ref[:, DENSE_WIDTH + g * SEG:DENSE_WIDTH + (g + 1) * SEG]

    kv_f32 = {}

    def position_order(a, g, slot):
        d = dils[g]
        if d == 1:
            return a
        n = tm // d
        for u in range(SEG // LANES):
            for r in range(d):
                stage_ref[slot + u, pl.ds(r, n, stride=d), :] = a[r * n:(r + 1) * n, u * LANES:(u + 1) * LANES]
        return jnp.concatenate([stage_ref[slot + u] for u in range(SEG // LANES)], axis=1)

    def emit_windows(g):
        wb = win_blocks[g]
        assert dils[g] == 1 or wb == tm
        k, v = kv_f32[g]
        slot = 2 * (SEG // LANES) * (g % 2)
        at = position_order(k, g, slot)[tm - wb:, :].T
        if paired:
            dst = {f: lane for lane, f in enumerate(_pair_source())}
            at = jnp.concatenate([at[dst[f]:dst[f] + 8, :] for f in range(0, GROUP_WIDTH, 8)], axis=0)
        win_refs[2 * g][0] = at
        win_refs[2 * g + 1][0] = position_order(v, g, slot + SEG // LANES)[tm - wb:, :].T

    for seg in range(DENSE_WIDTH // SEG):
        c0 = seg * SEG
        a = jnp.dot(h, w_ref[:, c0:c0 + SEG], preferred_element_type=F32)
        if c0 < OFF_RK:
            a = ret_rope(a)
        elif c0 < OFF_RV:
            a = ret_rope(a) * (RET_QK_DIM ** -0.5)
            if keys_t:
                kt_ref[0, c0 - OFF_RK:c0 - OFF_RK + SEG, :] = a.T.astype(kt_ref.dtype)
        elif c0 < OFF_RG:
            pass
        elif c0 < OFF_GA:
            a = a * _sigmoid(a)
        else:
            a = _sigmoid(a)
        dense_ref[0, :, c0:c0 + SEG] = a.astype(dense_ref.dtype)

    for g in range(N_GROUPS):
        hg = h_class[g][...] if g in h_class else h
        wq = wqk_ref[:, g * SEG:(g + 1) * SEG]
        emit_group(g, 0, att_rope(jnp.dot(hg, wq, preferred_element_type=F32), att_tabs[g]) * ATTN_SCALE)
        k = att_rope(jnp.dot(hg, wk_cols(g), preferred_element_type=F32), att_tabs[g])
        v = jnp.dot(hg, wv_cols(g), preferred_element_type=F32)
        emit_group(g, 1, k)
        emit_group(g, 2, v)
        kv_f32[g] = (k, v)

    if win_refs:
        order = sorted(range(N_GROUPS), key=lambda g: win_first[g])

        def guarded(pos):
            if pos == len(order):
                return
            g = order[pos]

            @pl.when(i >= win_first[g])
            def _():
                emit_windows(g)
                guarded(pos + 1)

        guarded(0)


def _in_projection(x, mod, w_main, w_qk, tables, *, tm, dils, paired, keys_t, win_rows, out_dtype):
    G, R, _ = x.shape
    n_tiles = R // tm
    mod_blk = tm if mod.shape[1] == R else 1
    tab_blk = tm if tables[0].shape[0] == R else 1

    def mod_spec(col):
        if mod_blk == 1:
            return pl.BlockSpec((1, 1, D_MODEL), lambda b, i: (b, 0, col))
        return pl.BlockSpec((1, tm, D_MODEL), lambda b, i: (b, i, col))

    def tab_spec(tab):
        if tab.shape[0] == 1:
            return pl.BlockSpec((1, LANES), lambda b, i: (0, 0))
        return pl.BlockSpec((tm, LANES), lambda b, i: (i, 0))

    att = list(tables[:3])
    if paired:
        att[2] = np.zeros((1, LANES), np.float32)
    all_tabs = att + list(tables[3:])

    out_shapes = [jax.ShapeDtypeStruct((G, R, DENSE_WIDTH), out_dtype)]
    out_specs = [pl.BlockSpec((1, tm, DENSE_WIDTH), lambda b, i: (b, i, 0))]
    for d in dils:
        out_shapes.append(jax.ShapeDtypeStruct((G, d, R // d, ATTN_WIDTH), out_dtype))
        out_specs.append(pl.BlockSpec((1, d, tm // d, ATTN_WIDTH), lambda b, i: (b, 0, i, 0)))
    if keys_t:
        out_shapes.append(jax.ShapeDtypeStruct((G, RET_QK_WIDTH, R), out_dtype))
        out_specs.append(pl.BlockSpec((1, RET_QK_WIDTH, tm), lambda b, i: (b, 0, i)))
    win_blocks = win_first = ()
    if win_rows:
        win_blocks = tuple(min(w, tm) for w in win_rows)
        win_first = tuple(n_tiles - w // wb for w, wb in zip(win_rows, win_blocks))
        for g in range(N_GROUPS):
            first = win_first[g]
            for _ in range(2):
                out_shapes.append(jax.ShapeDtypeStruct((G, GROUP_WIDTH, win_rows[g]), F32))
                out_specs.append(pl.BlockSpec((1, GROUP_WIDTH, win_blocks[g]),
                                              lambda b, i, first=first: (b, 0, jnp.maximum(i - first, 0))))

    scratch = []
    if any(d > 1 for d in dils):
        scratch.append(pltpu.VMEM((D_MODEL // LANES + 2, tm, LANES), F32))
        scratch += [pltpu.VMEM((tm, D_MODEL), BF16) for d in dils if d > 1]
    kern = functools.partial(_inproj_kernel, tm=tm, dils=dils, paired=paired, keys_t=keys_t,
                             win_blocks=win_blocks, win_first=win_first)
    return pl.pallas_call(
        kern,
        grid=(G, n_tiles),
        in_specs=[pl.BlockSpec((1, tm, D_MODEL), lambda b, i: (b, i, 0)),
                  mod_spec(0), mod_spec(1),
                  _resident(w_main.shape), _resident(w_qk.shape)] + [tab_spec(t) for t in all_tabs],
        out_specs=out_specs,
        out_shape=out_shapes,
        scratch_shapes=scratch,
        compiler_params=_cparams(("arbitrary", "arbitrary")),
        name="in_projection",
    )(x, mod, mod, w_main, w_qk, *all_tabs)


def _attn_kernel(q_ref, k_ref, v_ref, kp_ref, vp_ref, o_ref, lse_ref, *, tl, nc):
    t = pl.program_id(2)
    row = lax.broadcasted_iota(jnp.int32, (ATTN_BAND, ATTN_BAND), 0)
    col = lax.broadcasted_iota(jnp.int32, (ATTN_BAND, ATTN_BAND), 1)
    lane = lax.broadcasted_iota(jnp.int32, (1, GROUP_WIDTH), 1)
    cur_ok = col <= row
    qk_lanes = [(lane % LANES) // PAIR_SLOT == h for h in range(HEADS_PER_GROUP)]
    v_lanes = [lane // HEAD_DIM == h for h in range(HEADS_PER_GROUP)]

    def scores(q, k_prev, k_cur):
        kk = jnp.concatenate([k_prev, k_cur], axis=0)
        qs = jnp.concatenate([jnp.where(sel, q, jnp.zeros_like(q)) for sel in qk_lanes], axis=0)
        return lax.dot_general(qs, kk, (((1,), (1,)), ((), ())), preferred_element_type=F32)

    def finish(s, v_prev, v_cur, prev_shift):
        prev_ok = col >= row + prev_shift
        mask = jnp.concatenate([prev_ok, cur_ok], axis=1)
        mask = jnp.concatenate([mask] * HEADS_PER_GROUP, axis=0)
        vv = jnp.concatenate([v_prev, v_cur], axis=0)
        s = jnp.where(mask, s, NEG_BIG)
        m = jnp.max(s, axis=1, keepdims=True)
        e = jnp.exp(s - m)
        l = jnp.sum(e, axis=1, keepdims=True)
        oh = jnp.dot((e / l).astype(BF16), vv, preferred_element_type=F32)
        lse = m + jnp.log(l)
        o = jnp.zeros((ATTN_BAND, GROUP_WIDTH), F32)
        lo = jnp.zeros((ATTN_BAND, GROUP_WIDTH), F32)
        for h, sel in enumerate(v_lanes):
            rows = slice(h * ATTN_BAND, (h + 1) * ATTN_BAND)
            o = jnp.where(sel, oh[rows], o)
            lo = jnp.where(sel, lse[rows], lo)
        return o, lo

    first = jnp.where(t > 0, 0, ATTN_BAND)
    blocks = [(c, j) for c in range(nc) for j in range(tl // ATTN_BAND)]

    def prev_rows(ref, prev_ref, c, j):
        return prev_ref[0, c] if j == 0 else ref[0, c, (j - 1) * ATTN_BAND:j * ATTN_BAND, :]

    def cur_rows(ref, c, j):
        return ref[0, c, j * ATTN_BAND:(j + 1) * ATTN_BAND, :]

    def block_scores(c, j):
        return scores(cur_rows(q_ref, c, j), prev_rows(k_ref, kp_ref, c, j), cur_rows(k_ref, c, j))

    ahead = [block_scores(*blk) for blk in blocks[:SCORE_LOOKAHEAD]]
    for n, (c, j) in enumerate(blocks):
        if n + SCORE_LOOKAHEAD < len(blocks):
            ahead.append(block_scores(*blocks[n + SCORE_LOOKAHEAD]))
        o, lse = finish(ahead[n], prev_rows(v_ref, vp_ref, c, j), cur_rows(v_ref, c, j), first if j == 0 else 0)
        rows = slice(j * ATTN_BAND, (j + 1) * ATTN_BAND)
        o_ref[0, c, rows, :] = o.astype(o_ref.dtype)
        lse_ref[0, c, rows, :] = lse


def _prompt_attention(qkv, g, *, tl, nc):
    B, dil, L, _ = qkv.shape
    sub = tl // ATTN_BAND

    def cur(c):
        return pl.BlockSpec((1, nc, tl, SEG), lambda b, r, t: (b, r, t, c))

    def prev(c):
        return pl.BlockSpec((1, nc, ATTN_BAND, SEG), lambda b, r, t: (b, r, jnp.maximum(t * sub - 1, 0), c))

    out_spec = pl.BlockSpec((1, nc, tl, SEG), lambda b, r, t: (b, r, t, 0))
    return pl.pallas_call(
        functools.partial(_attn_kernel, tl=tl, nc=nc),
        grid=(B, dil // nc, L // tl),
        in_specs=[cur(0), cur(1), cur(2), prev(1), prev(2)],
        out_specs=[out_spec, out_spec],
        out_shape=[jax.ShapeDtypeStruct((B, dil, L, SEG), BF16),
                   jax.ShapeDtypeStruct((B, dil, L, SEG), F32)],
        compiler_params=_cparams(("arbitrary", "arbitrary", "arbitrary")),
        name="prompt_attention_g%d" % g,
    )(qkv, qkv, qkv, qkv, qkv)


def _retention_tables():
    f32 = np.float32
    C = RET_CHUNK
    lg = np.log(f32(1.0) - f32(2.0) ** (f32(-5.0) - np.arange(RET_HEADS, dtype=f32)))
    idx = np.arange(C, dtype=f32)
    diff = idx[:, None] - idx[None, :]
    decay_in = np.where(diff >= 0, np.exp(lg[:, None, None] * np.maximum(diff, f32(0.0))), f32(0.0))
    decay_q = np.exp(lg[:, None] * (idx[None, :] + f32(1.0)))[:, :, None]
    decay_k = np.exp(lg[:, None] * (f32(C - 1.0) - idx[None, :]))[:, None, :]
    return tuple(t.astype(f32) for t in (decay_in, decay_q, decay_k))


def _retention_kernel(qk_ref, kt_ref, v_ref, din_ref, dq_ref, dk_ref, o_ref, state_ref, r_scr, *, tr):
    t = pl.program_id(1)
    n_chunks = tr // RET_CHUNK

    @pl.when(t == 0)
    def _():
        r_scr[...] = jnp.zeros_like(r_scr)

    def key_t(c, h):
        return kt_ref[0, h * RET_QK_DIM:(h + 1) * RET_QK_DIM, c * RET_CHUNK:(c + 1) * RET_CHUNK]

    def val(c, h):
        return v_ref[0, c * RET_CHUNK:(c + 1) * RET_CHUNK, h * RET_V_DIM:(h + 1) * RET_V_DIM]

    states = [[None] * RET_HEADS for _ in range(n_chunks)]
    for h in range(RET_HEADS):
        r = r_scr[h]
        for c in range(n_chunks):
            states[c][h] = r
            kd = (key_t(c, h).astype(F32) * dk_ref[h]).astype(BF16)
            r = r * (_ret_gamma(h) ** RET_CHUNK) + jnp.dot(kd, val(c, h), preferred_element_type=F32)
        r_scr[h] = r

    def query(c, h):
        return qk_ref[0, c * RET_CHUNK:(c + 1) * RET_CHUNK, OFF_RQ + h * RET_QK_DIM:OFF_RQ + (h + 1) * RET_QK_DIM]

    scores = [[(jnp.dot(query(c, h), key_t(c, h), preferred_element_type=F32) * din_ref[h]).astype(BF16)
               for h in range(RET_HEADS)] for c in range(n_chunks)]

    for c in range(n_chunks):
        rows = slice(c * RET_CHUNK, (c + 1) * RET_CHUNK)
        for h in range(RET_HEADS):
            q = query(c, h)
            o = (jnp.dot(scores[c][h], val(c, h), preferred_element_type=F32)
                 + jnp.dot(q, states[c][h].astype(BF16), preferred_element_type=F32) * dq_ref[h])
            mu = jnp.mean(o, axis=1, keepdims=True)
            d = o - mu
            var = jnp.mean(d * d, axis=1, keepdims=True)
            o_ref[0, rows, h * RET_V_DIM:(h + 1) * RET_V_DIM] = (d * lax.rsqrt(var + GN_EPS)).astype(o_ref.dtype)

    @pl.when(t == pl.num_programs(1) - 1)
    def _():
        state_ref[0] = r_scr[...]


def _prompt_retention(dense, keys_t, *, tr):
    B, S, _ = dense.shape
    tabs = _retention_tables()
    blk = RET_V_WIDTH

    def colblk(c):
        return pl.BlockSpec((1, tr, blk), lambda b, t: (b, t, c))

    return pl.pallas_call(
        functools.partial(_retention_kernel, tr=tr),
        grid=(B, S // tr),
        in_specs=[pl.BlockSpec((1, tr, RET_QK_WIDTH), lambda b, t: (b, t, OFF_RQ // RET_QK_WIDTH)),
                  pl.BlockSpec((1, RET_QK_WIDTH, tr), lambda b, t: (b, 0, t)),
                  colblk(OFF_RV // blk)]
                 + [pl.BlockSpec(tab.shape, lambda b, t: (0, 0, 0)) for tab in tabs],
        out_specs=[pl.BlockSpec((1, tr, RET_V_WIDTH), lambda b, t: (b, t, 0)),
                   pl.BlockSpec((1, RET_HEADS, RET_QK_DIM, RET_V_DIM), lambda b, t: (b, 0, 0, 0))],
        out_shape=[jax.ShapeDtypeStruct((B, S, RET_V_WIDTH), BF16),
                   jax.ShapeDtypeStruct((B, RET_HEADS, RET_QK_DIM, RET_V_DIM), F32)],
        scratch_shapes=[pltpu.VMEM((RET_HEADS, RET_QK_DIM, RET_V_DIM), F32)],
        compiler_params=_cparams(("arbitrary", "arbitrary")),
        name="prompt_retention",
    )(dense, keys_t, dense, *tabs)


def _layer_norm(u, g, b):
    mu = jnp.mean(u, axis=1, keepdims=True)
    d = u - mu
    var = jnp.mean(d * d, axis=1, keepdims=True)
    return d * lax.rsqrt(var + LN_EPS) * g + b


def _column(row):
    return jnp.broadcast_to(row, (LANES, LANES)).T


def _window_half_step(first_half, qkv_refs, cache_refs, o_ref, lse_ref, new_refs):
    pad = 16
    row_h = lax.broadcasted_iota(jnp.int32, (pad, LANES), 0)
    lane_h = lax.broadcasted_iota(jnp.int32, (pad, LANES), 1) // HEAD_DIM
    own = row_h == lane_h
    lane = lax.broadcasted_iota(jnp.int32, (LANES, LANES), 1)

    def pick(row):
        return jnp.where(first_half, row[:, :LANES], row[:, LANES:])

    def rolled(c_ref, n_ref, new_row):
        w = c_ref.shape[3]
        new_col = _column(new_row)
        nblk = w // LANES
        cur = pltpu.roll(c_ref[0, 0, :, 0:LANES], LANES - 1, 1)
        for j in range(nblk):
            nxt = pltpu.roll(c_ref[0, 0, :, (j + 1) * LANES:(j + 2) * LANES], LANES - 1, 1) if j + 1 < nblk else new_col
            n_ref[0, 0, :, j * LANES:(j + 1) * LANES] = jnp.where(lane < LANES - 1, cur, nxt)
            cur = nxt

    for g in range(N_GROUPS):
        dil = DILATED_PATTERNS[g][1]
        ck_ref, cv_ref = cache_refs[2 * g:2 * g + 2]
        w = ck_ref.shape[3]
        q = pick(qkv_refs[g][0, :, 0:SEG])
        kn = pick(qkv_refs[g][0, :, SEG:2 * SEG])
        vn = pick(qkv_refs[g][0, :, 2 * SEG:3 * SEG])
        qsel = jnp.where(own, jnp.broadcast_to(q, (pad, LANES)), 0.0)
        s = jnp.dot(qsel.astype(BF16), ck_ref[0, 0].astype(BF16), preferred_element_type=F32)
        pos = lax.broadcasted_iota(jnp.int32, (pad, w), 1)
        s = jnp.where((pos & (dil - 1)) == 0, s, NEG_BIG)
        sn = jnp.sum(qsel * kn, axis=1, keepdims=True)
        m = jnp.maximum(jnp.max(s, axis=1, keepdims=True), sn)
        e = jnp.exp(s - m)
        en = jnp.exp(sn - m)
        l = jnp.sum(e, axis=1, keepdims=True) + en
        o_all = lax.dot_general((e / l).astype(BF16), cv_ref[0, 0].astype(BF16), (((1,), (1,)), ((), ())),
                                preferred_element_type=F32) + (en / l) * vn
        o_ref[0, 0, :, g * LANES:(g + 1) * LANES] = jnp.sum(jnp.where(own, o_all, 0.0), axis=0, keepdims=True)
        lse_ref[0, 0, :, g * LANES:(g + 1) * LANES] = jnp.sum(jnp.where(own, m + jnp.log(l), 0.0), axis=0,
                                                              keepdims=True)
        rolled(ck_ref, new_refs[2 * g], kn)
        rolled(cv_ref, new_refs[2 * g + 1], vn)


N_MERGE_INPUTS = 24
N_SIDE_INPUTS = N_GROUPS + 2 * N_GROUPS
N_SIDE_OUTPUTS = 2 + 2 * N_GROUPS


def _merge_ffn_kernel(*refs, tm, sub, dils, side, steps_per_batch):
    (x_ref, ga_ref, gb_ref, rg_ref, ret_ref, o0_ref, o1_ref, o2_ref, l0_ref, l1_ref, l2_ref,
     gate1_ref, shift2_ref, scale2_ref, gate2_ref, watt_ref, wret_ref, wo_ref, wfi_ref, wfo_ref,
     ln1g_ref, ln1b_ref, ln2g_ref, ln2b_ref) = refs[:N_MERGE_INPUTS]
    n_in = N_MERGE_INPUTS + (N_SIDE_INPUTS if side else 0)
    y_ref = refs[n_in]
    n_out = 1 + (N_SIDE_OUTPUTS if side else 0)
    scratch = refs[n_in + n_out:]
    if side:
        side_in = refs[N_MERGE_INPUTS:n_in]
        side_out = refs[n_in + 1:n_in + n_out]
        step = pl.program_id(0) * steps_per_batch + pl.program_id(1)
        _window_half_step(step % 2 == 0, side_in[:N_GROUPS], side_in[N_GROUPS:], side_out[0], side_out[1],
                          side_out[2:])
    n_sub = tm // sub
    per_sub = 1 + 2 * N_GROUPS
    for s in range(n_sub):
        rows = slice(s * sub, (s + 1) * sub)
        act_scr = scratch[s * per_sub]
        stage_refs = scratch[s * per_sub + 1:(s + 1) * per_sub]

        def natural(ref, g, stage_ref):
            d = dils[g]
            n = sub // d
            if d == 1:
                return ref[0, 0, rows, :].astype(F32)
            for r in range(d):
                for u in range(GROUP_WIDTH // LANES):
                    stage_ref[u, pl.ds(r, n, stride=d), :] = (
                        ref[0, r, s * n:(s + 1) * n, u * LANES:(u + 1) * LANES].astype(F32))
            return jnp.concatenate([stage_ref[u] for u in range(GROUP_WIDTH // LANES)], axis=1)

        def mod_rows(ref):
            return ref[0] if ref.shape[1] == 1 else ref[0, rows, :]

        lses = [natural(ref, g, stage_refs[g]) for g, ref in enumerate((l0_ref, l1_ref, l2_ref))]
        outs = [natural(ref, g, stage_refs[N_GROUPS + g]) for g, ref in enumerate((o0_ref, o1_ref, o2_ref))]
        m = jnp.maximum(jnp.maximum(lses[0], lses[1]), lses[2])
        es = [jnp.exp(l - m) for l in lses]
        att = (es[0] * outs[0] + es[1] * outs[1] + es[2] * outs[2]) / (es[0] + es[1] + es[2])
        y_att = jnp.dot(att.astype(BF16), watt_ref[...], preferred_element_type=F32)
        gated = rg_ref[0, rows, :].astype(F32) * ret_ref[0, rows, :].astype(F32)
        y_ret = jnp.dot(gated.astype(BF16), wret_ref[...], preferred_element_type=F32)
        mix = (ga_ref[0, rows, :].astype(F32) * y_att + gb_ref[0, rows, :].astype(F32) * y_ret).astype(BF16)
        mixed = jnp.dot(mix, wo_ref[...], preferred_element_type=F32)
        x1 = _layer_norm(DEEPNORM_ALPHA * x_ref[0, rows, :] + mod_rows(gate1_ref) * mixed,
                         ln1g_ref[...], ln1b_ref[...])
        h2 = (x1 * (1.0 + mod_rows(scale2_ref)) + mod_rows(shift2_ref)).astype(BF16)
        for c0 in range(0, D_FF, FFN_CHUNK):
            c1 = min(c0 + FFN_CHUNK, D_FF)
            fg = jnp.dot(h2, wfi_ref[:, c0:c1], preferred_element_type=F32)
            fu = jnp.dot(h2, wfi_ref[:, D_FF + c0:D_FF + c1], preferred_element_type=F32)
            act_scr[:, c0:c1] = (fg * _sigmoid(fg) * fu).astype(BF16)
        ff = jnp.dot(act_scr[...], wfo_ref[...], preferred_element_type=F32)
        y_ref[0, rows, :] = _layer_norm(DEEPNORM_ALPHA * x1 + mod_rows(gate2_ref) * ff, ln2g_ref[...], ln2b_ref[...])


def _merge_ffn(x, dense, ret, o_groups, lse_groups, mod, weights, ln, *, tm, sub, dils, side=None):
    G, R, _ = x.shape
    mod_rows = mod.shape[1]
    n_tiles = R // tm
    side_args, side_in_specs, side_out_specs, side_out_shapes = [], [], [], []
    if side is not None:
        qkvs, caches = side
        n_seq = caches[0].shape[0]
        assert G * n_tiles == 2 * n_seq

        def seq_half(b, i):
            step = b * n_tiles + i
            return step // 2, step % 2

        row_spec = pl.BlockSpec((1, 1, ATTN_WIDTH), lambda b, i: (seq_half(b, i)[0], 0, 0))
        small_spec = pl.BlockSpec((1, 1, 1, N_GROUPS * LANES), lambda b, i: seq_half(b, i) + (0, 0))
        halves = [c.reshape(n_seq, 2, LANES, c.shape[2]) for c in caches]
        half_specs = [pl.BlockSpec((1, 1, LANES, c.shape[3]), lambda b, i: seq_half(b, i) + (0, 0)) for c in halves]
        side_args = list(qkvs) + halves
        side_in_specs = [row_spec] * N_GROUPS + half_specs
        side_out_specs = [small_spec, small_spec] + half_specs
        side_out_shapes = ([jax.ShapeDtypeStruct((n_seq, 2, 1, N_GROUPS * LANES), F32)] * 2
                           + [jax.ShapeDtypeStruct(c.shape, F32) for c in halves])
    sub_scratch = [pltpu.VMEM((sub, D_FF), BF16)] + [pltpu.VMEM((GROUP_WIDTH // LANES, sub, LANES), F32)] * (2 * N_GROUPS)

    def mod_spec(col):
        if mod_rows == 1:
            return pl.BlockSpec((1, 1, D_MODEL), lambda b, i: (b, 0, col))
        return pl.BlockSpec((1, tm, D_MODEL), lambda b, i: (b, i, col))

    def rows(width, col=0):
        return pl.BlockSpec((1, tm, width), lambda b, i: (b, i, col))

    group_specs = [pl.BlockSpec((1, d, tm // d, GROUP_WIDTH), lambda b, i: (b, 0, i, 0)) for d in dils]
    vec = pl.BlockSpec((1, D_MODEL), lambda b, i: (0, 0))
    outs = pl.pallas_call(
        functools.partial(_merge_ffn_kernel, tm=tm, sub=sub, dils=dils, side=side is not None,
                          steps_per_batch=n_tiles),
        grid=(G, n_tiles),
        in_specs=[rows(D_MODEL), rows(D_MODEL, OFF_GA // D_MODEL), rows(D_MODEL, OFF_GB // D_MODEL),
                  rows(D_MODEL, OFF_RG // D_MODEL), rows(RET_V_WIDTH)] + group_specs + group_specs
                 + [mod_spec(2), mod_spec(3), mod_spec(4), mod_spec(5)]
                 + [_resident(w.shape) for w in weights] + [vec] * 4 + side_in_specs,
        out_specs=[rows(D_MODEL)] + side_out_specs,
        out_shape=[jax.ShapeDtypeStruct((G, R, D_MODEL), F32)] + side_out_shapes,
        scratch_shapes=sub_scratch * (tm // sub),
        compiler_params=_cparams(("arbitrary", "arbitrary")),
        name="merge_ffn",
    )(x, dense, dense, dense, ret, *o_groups, *lse_groups, mod, mod, mod, mod, *weights, *ln, *side_args)
    if side is None:
        return outs[0]
    y, o_s, lse_s, *rolled = outs
    return y, o_s, lse_s, [r.reshape(c.shape) for r, c in zip(rolled, side[1])]


def _sample_retention_kernel(dense_ref, state_ref, ret_ref, nstate_ref):
    for b in range(dense_ref.shape[0]):
        for h in range(RET_HEADS):
            gamma = _ret_gamma(h)
            q = dense_ref[b, :, OFF_RQ + h * RET_QK_DIM:OFF_RQ + (h + 1) * RET_QK_DIM]
            k = dense_ref[b, :, OFF_RK + h * RET_QK_DIM:OFF_RK + (h + 1) * RET_QK_DIM]
            v = dense_ref[b, :, OFF_RV + h * RET_V_DIM:OFF_RV + (h + 1) * RET_V_DIM]
            r_old = state_ref[b, h]
            q_col = jnp.concatenate([_column(q)] * (RET_V_DIM // LANES), axis=1)
            k_col = jnp.concatenate([_column(k)] * (RET_V_DIM // LANES), axis=1)
            qk = jnp.sum(q * k, axis=1, keepdims=True)
            o = qk * v + jnp.sum(q_col * r_old, axis=0, keepdims=True) * gamma
            nstate_ref[b, h] = r_old * gamma + k_col * v
            mu = jnp.mean(o, axis=1, keepdims=True)
            d = o - mu
            var = jnp.mean(d * d, axis=1, keepdims=True)
            ret_ref[b, :, h * RET_V_DIM:(h + 1) * RET_V_DIM] = d * lax.rsqrt(var + GN_EPS)


def _sample_retention(dense, state, *, nb=4):
    Bs = dense.shape[0]
    row = lambda w: pl.BlockSpec((nb, 1, w), lambda b: (b, 0, 0))
    st = pl.BlockSpec((nb, RET_HEADS, RET_QK_DIM, RET_V_DIM), lambda b: (b, 0, 0, 0))
    return pl.pallas_call(
        _sample_retention_kernel,
        grid=(Bs // nb,),
        in_specs=[row(DENSE_WIDTH), st],
        out_specs=[row(RET_V_WIDTH), st],
        out_shape=[jax.ShapeDtypeStruct((Bs, 1, RET_V_WIDTH), F32), jax.ShapeDtypeStruct(state.shape, F32)],
        compiler_params=_cparams(("arbitrary",)),
        name="sample_retention",
    )(dense, state)


def _feature_major(a):
    _, b, w, _, _ = a.shape
    return jnp.transpose(a[0], (0, 2, 3, 1)).reshape(b, GROUP_WIDTH, w)


def _row_major(a):
    b, _, w = a.shape
    return jnp.transpose(a.reshape(b, HEADS_PER_GROUP, HEAD_DIM, w), (0, 3, 1, 2))[None]


def kernel(x_prompt, x_sample, cache_k_w128, cache_v_w128, cache_k_w512, cache_v_w512, cache_k_w2048,
           cache_v_w2048, state_retention, c_prompt, c_sample, w_ada, b_ada, w_in, w_att_out, w_ret_out,
           w_o, ln1_g, ln1_b, w_ffn_in, w_ffn_out, ln2_g, ln2_b):
    assert w_in.shape[0] == 1 and x_sample.shape[1] == 1
    B, S, D = x_prompt.shape
    Bs = x_sample.shape[0]
    dils = tuple(d for _, d in DILATED_PATTERNS)

    w_main = jnp.concatenate([w_in[0][:, W_IN_ROTATE:], w_in[0][:, 2 * ATTN_WIDTH:W_IN_ROTATE]], axis=1).astype(BF16)
    w_qk = w_in[0][:, :2 * ATTN_WIDTH].astype(BF16)
    w_qk_paired = _pair_columns(w_qk)
    weights = tuple(w[0].astype(BF16) for w in (w_att_out, w_ret_out, w_o, w_ffn_in, w_ffn_out))
    ln = (ln1_g, ln1_b, ln2_g, ln2_b)

    mod = _modulation(jnp.concatenate([c_prompt, c_sample], axis=0), w_ada[0], b_ada[0])
    mod_p = mod[:B].reshape(B, 1, 6 * D)
    mod_s = mod[B:].reshape(1, Bs, 6 * D)

    tabs_p = _rope_tables(np.arange(S), paired=True)
    win_rows = tuple(min(w, S) for w, _ in DILATED_PATTERNS)
    dense_p, *rest = _in_projection(x_prompt, mod_p, w_main, w_qk_paired, tabs_p, tm=256, dils=dils, paired=True,
                                    keys_t=True, win_rows=win_rows, out_dtype=BF16)
    qkv_p, keys_t_p, wins_p = rest[:N_GROUPS], rest[N_GROUPS], rest[N_GROUPS + 1:]
    o_p, lse_p = [], []
    for g, d in enumerate(dils):
        tl = min(S // d, 1024)
        o, lse = _prompt_attention(qkv_p[g], g, tl=tl, nc=min(d, 1024 // tl))
        o_p.append(o)
        lse_p.append(lse)
    ret_p, state_p = _prompt_retention(dense_p, keys_t_p, tr=1024)

    xs = x_sample.reshape(1, Bs, D)
    tabs_s = _rope_tables(PAST_LEN + np.arange(1))
    unit = (1,) * N_GROUPS
    dense_s, *qkv_s = _in_projection(xs, mod_s, w_main, w_qk, tabs_s, tm=Bs, dils=unit, paired=False,
                                     keys_t=False, win_rows=None, out_dtype=F32)
    caches = [_feature_major(c) for c in (cache_k_w128, cache_v_w128, cache_k_w512, cache_v_w512,
                                          cache_k_w2048, cache_v_w2048)]
    ret_s, state_s = _sample_retention(dense_s.reshape(Bs, 1, DENSE_WIDTH), state_retention[0])
    y_p, o_s, lse_s, rolled = _merge_ffn(
        x_prompt, dense_p, ret_p, o_p, lse_p, mod_p, weights, ln, tm=512, sub=256, dils=dils,
        side=([a.reshape(Bs, 1, ATTN_WIDTH) for a in qkv_s], caches))

    def by_group(a):
        return [a[:, :, 0, g * LANES:(g + 1) * LANES].reshape(1, 1, Bs, GROUP_WIDTH) for g in range(N_GROUPS)]

    o_sg, lse_sg = by_group(o_s), by_group(lse_s)
    y_s = _merge_ffn(xs, dense_s, ret_s.reshape(1, Bs, RET_V_WIDTH), o_sg, lse_sg, mod_s, weights, ln,
                     tm=Bs, sub=Bs, dils=unit)

    return (y_p, y_s.reshape(Bs, 1, D),
            *[_row_major(w) for w in wins_p], state_p[None],
            *[_row_major(r) for r in rolled], state_s[None])
```

```python
import functools
import math

import jax
import jax.numpy as jnp
import numpy as np
from jax import lax
from jax.experimental import pallas as pl
from jax.experimental.pallas import tpu as pltpu

F32 = jnp.float32
BF16 = jnp.bfloat16

D_MODEL = 1024
PAST_LEN = 16384
DILATED_PATTERNS = ((128, 1), (512, 4), (2048, 16))
N_GROUPS = 3
HEADS_PER_GROUP = 4
HEAD_DIM = 64
GROUP_WIDTH = HEADS_PER_GROUP * HEAD_DIM
ATTN_WIDTH = N_GROUPS * GROUP_WIDTH
ATTN_BAND = 128
ATTN_SCALE = HEAD_DIM ** -0.5
ROPE_DIM = HEAD_DIM // 4
ROPE_THETA = 500000.0
RET_HEADS = 4
RET_QK_DIM = 128
RET_V_DIM = 256
RET_QK_WIDTH = RET_HEADS * RET_QK_DIM
RET_V_WIDTH = RET_HEADS * RET_V_DIM
RET_CHUNK = 128
RET_ROPE_THETA = 10000.0
IN_WIDTH = 3 * ATTN_WIDTH + 2 * RET_QK_WIDTH + 2 * RET_V_WIDTH + 2 * D_MODEL
D_FF = 2816
DEEPNORM_ALPHA = 2.0 ** 0.25
LN_EPS = 1e-5
GN_EPS = 1e-6

LANES = 128
SEG = 256
N_SEG = IN_WIDTH // SEG
FFN_CHUNK = 256
SCORE_LOOKAHEAD = 2
OFF_RQ, OFF_RK, OFF_RV, OFF_RG, OFF_GA, OFF_GB = 0, 512, 1024, 2048, 3072, 4096
OFF_QA, OFF_KA, OFF_VA = 5120, 5888, 6656
DENSE_WIDTH = OFF_QA
W_IN_ROTATE = 3 * ATTN_WIDTH
NEG_BIG = -1e30
VMEM_LIMIT = 60 * 1024 * 1024


def _cparams(sem):
    return pltpu.CompilerParams(dimension_semantics=sem, vmem_limit_bytes=VMEM_LIMIT)


def _resident(shape):
    nd = len(shape)
    return pl.BlockSpec(shape, lambda *_: (0,) * nd, pipeline_mode=pl.Buffered(1))


def _sigmoid(x):
    return 1.0 / (1.0 + jnp.exp(-x))


def _ret_gamma(h):
    return 1.0 - 2.0 ** (-5.0 - h)


def _ada_kernel(c_ref, w_ref, b_ref, o_ref):
    c = c_ref[...]
    a = (c * _sigmoid(c)).astype(BF16)
    o_ref[...] = jnp.dot(a, w_ref[...].astype(BF16), preferred_element_type=F32) + b_ref[...]


def _modulation(c_all, w_ada, b_ada):
    rows = c_all.shape[0]
    tn = 1024
    return pl.pallas_call(
        _ada_kernel,
        grid=(6 * D_MODEL // tn,),
        in_specs=[pl.BlockSpec((rows, D_MODEL), lambda j: (0, 0)),
                  pl.BlockSpec((D_MODEL, tn), lambda j: (0, j)),
                  pl.BlockSpec((1, tn), lambda j: (0, j))],
        out_specs=pl.BlockSpec((rows, tn), lambda j: (0, j)),
        out_shape=jax.ShapeDtypeStruct((rows, 6 * D_MODEL), F32),
        compiler_params=_cparams(("arbitrary",)),
        name="modulation",
    )(c_all, w_ada, b_ada.reshape(1, -1))


PAIR_SLOT = HEAD_DIM // 2


def _pair_source():
    src = []
    for half_dims in ((0, 16), (8, 40)):
        for h in range(HEADS_PER_GROUP):
            rot, rest = half_dims
            dims = list(range(rot, rot + ROPE_DIM // 2)) + list(range(rest, rest + PAIR_SLOT - ROPE_DIM // 2))
            src += [h * HEAD_DIM + d for d in dims]
    return src


def _pair_columns(w):
    run = ROPE_DIM // 2
    w5 = w.reshape(w.shape[0], -1, HEADS_PER_GROUP, HEAD_DIM // run, run)
    half_a = jnp.concatenate([w5[:, :, :, 0:1], w5[:, :, :, 2:5]], axis=3)
    half_b = jnp.concatenate([w5[:, :, :, 1:2], w5[:, :, :, 5:8]], axis=3)
    return jnp.stack([half_a, half_b], axis=2).reshape(w.shape)


def _rope_tables(pos, paired=False):
    posf = jnp.asarray(np.asarray(pos), jnp.int32).astype(F32)[:, None]
    lane = np.arange(LANES)
    half = ROPE_DIM // 2
    inv = jnp.exp(-math.log(ROPE_THETA) * jnp.arange(half, dtype=F32) * (2.0 / ROPE_DIM))
    ang = posf * inv[None, :]
    cos, sin = jnp.cos(ang), jnp.sin(ang)
    if paired:
        in_slot = lane % PAIR_SLOT
        f = in_slot % half
        att_c = jnp.where(in_slot < half, cos[:, f], 1.0)
        att_lo = jnp.where(in_slot < half, sin[:, f], 0.0)
        att_hi = att_lo
    else:
        in_head = lane % HEAD_DIM
        f = in_head % half
        att_c = jnp.where(in_head < ROPE_DIM, cos[:, f], 1.0)
        att_lo = jnp.where(in_head < half, -sin[:, f], 0.0)
        att_hi = jnp.where((in_head >= half) & (in_head < ROPE_DIM), sin[:, f], 0.0)
    rhalf = RET_QK_DIM // 2
    rinv = jnp.exp(-math.log(RET_ROPE_THETA) * jnp.arange(rhalf, dtype=F32) * (2.0 / RET_QK_DIM))
    rang = posf * rinv[None, :]
    rcos, rsin = jnp.cos(rang), jnp.sin(rang)
    ret_c = jnp.concatenate([rcos, rcos], axis=-1)
    ret_s = jnp.concatenate([-rsin, rsin], axis=-1)
    return att_c, att_lo, att_hi, ret_c, ret_s


def _inproj_kernel(x_ref, shift_ref, scale_ref, w_ref, wqk_ref, *refs, tm, dils, paired, keys_t, win_blocks,
                   win_first):
    n_tab = 5
    att_tab = [r[...] for r in refs[:3]]
    rc, rs = refs[3][...], refs[4][...]
    refs = refs[n_tab:]
    n_perm = sum(d > 1 for d in dils)
    dense_ref = refs[0]
    qkv_refs = refs[1:1 + N_GROUPS]
    n_out = len(refs) - (1 + n_perm if n_perm else 0)
    kt_ref = refs[1 + N_GROUPS] if keys_t else None
    win_refs = refs[1 + N_GROUPS + bool(keys_t):n_out]
    i = pl.program_id(1)
    hf = x_ref[0] * (1.0 + scale_ref[0]) + shift_ref[0]
    h = hf.astype(BF16)

    h_class = {}
    att_tabs = [att_tab] * N_GROUPS
    if n_perm:
        stage_ref = refs[n_out]
        n_x = D_MODEL // LANES
        for u in range(n_x):
            stage_ref[u] = hf[:, u * LANES:(u + 1) * LANES]
        moving = [t for t in att_tab if t.shape[0] == tm]
        for u, t in enumerate(moving):
            stage_ref[n_x + u] = t
        for g, hp_ref in zip([g for g in range(N_GROUPS) if dils[g] > 1], refs[n_out + 1:]):
            d = dils[g]
            n = tm // d
            for r in range(d):
                for u in range(n_x):
                    hp_ref[r * n:(r + 1) * n, u * LANES:(u + 1) * LANES] = (
                        stage_ref[u, pl.ds(r, n, stride=d), :].astype(BF16))
            h_class[g] = hp_ref
            regrouped = iter([jnp.concatenate([stage_ref[n_x + u, pl.ds(r, n, stride=d), :] for r in range(d)], axis=0)
                              for u in range(len(moving))])
            att_tabs[g] = [next(regrouped) if t.shape[0] == tm else t for t in att_tab]

    def att_rope(a, tab):
        ac, alo, ahi = tab
        if paired:
            lo, hi = a[:, :LANES], a[:, LANES:]
            return jnp.concatenate([lo * ac - hi * alo, hi * ac + lo * alo], axis=1)
        parts = []
        for u in range(SEG // LANES):
            xs = a[:, u * LANES:(u + 1) * LANES]
            parts.append(xs * ac + pltpu.roll(xs, LANES - ROPE_DIM // 2, 1) * alo
                         + pltpu.roll(xs, ROPE_DIM // 2, 1) * ahi)
        return jnp.concatenate(parts, axis=1)

    def ret_rope(a):
        parts = []
        for u in range(SEG // LANES):
            xs = a[:, u * LANES:(u + 1) * LANES]
            parts.append(xs * rc + pltpu.roll(xs, RET_QK_DIM // 2, 1) * rs)
        return jnp.concatenate(parts, axis=1)

    def emit_group(g, c, a):
        ref, d = qkv_refs[g], dils[g]
        n = tm // d
        for r in range(d):
            ref[0, r, :, c * SEG:(c + 1) * SEG] = a[r * n:(r + 1) * n, :].astype(ref.dtype)

    def wk_cols(g):
        return wqk_ref[:, ATTN_WIDTH + g * SEG:ATTN_WIDTH + (g + 1) * SEG]

    def wv_cols(g):
        return w_ref[:, DENSE_WIDTH + g * SEG:DENSE_WIDTH + (g + 1) * SEG]

    kv_f32 = {}

    def position_order(a, g, slot):
        d = dils[g]
        if d == 1:
            return a
        n = tm // d
        for u in range(SEG // LANES):
            for r in range(d):
                stage_ref[slot + u, pl.ds(r, n, stride=d), :] = a[r * n:(r + 1) * n, u * LANES:(u + 1) * LANES]
        return jnp.concatenate([stage_ref[slot + u] for u in range(SEG // LANES)], axis=1)

    def emit_windows(g):
        wb = win_blocks[g]
        assert dils[g] == 1 or wb == tm
        k, v = kv_f32[g]
        slot = 2 * (SEG // LANES) * (g % 2)
        at = position_order(k, g, slot)[tm - wb:, :].T
        if paired:
            dst = {f: lane for lane, f in enumerate(_pair_source())}
            at = jnp.concatenate([at[dst[f]:dst[f] + 8, :] for f in range(0, GROUP_WIDTH, 8)], axis=0)
        win_refs[2 * g][0] = at
        win_refs[2 * g + 1][0] = position_order(v, g, slot + SEG // LANES)[tm - wb:, :].T

    for seg in range(DENSE_WIDTH // SEG):
        c0 = seg * SEG
        a = jnp.dot(h, w_ref[:, c0:c0 + SEG], preferred_element_type=F32)
        if c0 < OFF_RK:
            a = ret_rope(a)
        elif c0 < OFF_RV:
            a = ret_rope(a) * (RET_QK_DIM ** -0.5)
            if keys_t:
                kt_ref[0, c0 - OFF_RK:c0 - OFF_RK + SEG, :] = a.T.astype(kt_ref.dtype)
        elif c0 < OFF_RG:
            pass
        elif c0 < OFF_GA:
            a = a * _sigmoid(a)
        else:
            a = _sigmoid(a)
        dense_ref[0, :, c0:c0 + SEG] = a.astype(dense_ref.dtype)

    for g in range(N_GROUPS):
        hg = h_class[g][...] if g in h_class else h
        wq = wqk_ref[:, g * SEG:(g + 1) * SEG]
        emit_group(g, 0, att_rope(jnp.dot(hg, wq, preferred_element_type=F32), att_tabs[g]) * ATTN_SCALE)
        k = att_rope(jnp.dot(hg, wk_cols(g), preferred_element_type=F32), att_tabs[g])
        v = jnp.dot(hg, wv_cols(g), preferred_element_type=F32)
        emit_group(g, 1, k)
        emit_group(g, 2, v)
        kv_f32[g] = (k, v)

    if win_refs:
        order = sorted(range(N_GROUPS), key=lambda g: win_first[g])

        def guarded(pos):
            if pos == len(order):
                return
            g = order[pos]

            @pl.when(i >= win_first[g])
            def _():
                emit_windows(g)
                guarded(pos + 1)

        guarded(0)


def _in_projection(x, mod, w_main, w_qk, tables, *, tm, dils, paired, keys_t, win_rows, out_dtype):
    G, R, _ = x.shape
    n_tiles = R // tm
    mod_blk = tm if mod.shape[1] == R else 1
    tab_blk = tm if tables[0].shape[0] == R else 1

    def mod_spec(col):
        if mod_blk == 1:
            return pl.BlockSpec((1, 1, D_MODEL), lambda b, i: (b, 0, col))
        return pl.BlockSpec((1, tm, D_MODEL), lambda b, i: (b, i, col))

    def tab_spec(tab):
        if tab.shape[0] == 1:
            return pl.BlockSpec((1, LANES), lambda b, i: (0, 0))
        return pl.BlockSpec((tm, LANES), lambda b, i: (i, 0))

    att = list(tables[:3])
    if paired:
        att[2] = np.zeros((1, LANES), np.float32)
    all_tabs = att + list(tables[3:])

    out_shapes = [jax.ShapeDtypeStruct((G, R, DENSE_WIDTH), out_dtype)]
    out_specs = [pl.BlockSpec((1, tm, DENSE_WIDTH), lambda b, i: (b, i, 0))]
    for d in dils:
        out_shapes.append(jax.ShapeDtypeStruct((G, d, R // d, ATTN_WIDTH), out_dtype))
        out_specs.append(pl.BlockSpec((1, d, tm // d, ATTN_WIDTH), lambda b, i: (b, 0, i, 0)))
    if keys_t:
        out_shapes.append(jax.ShapeDtypeStruct((G, RET_QK_WIDTH, R), out_dtype))
        out_specs.append(pl.BlockSpec((1, RET_QK_WIDTH, tm), lambda b, i: (b, 0, i)))
    win_blocks = win_first = ()
    if win_rows:
        win_blocks = tuple(min(w, tm) for w in win_rows)
        win_first = tuple(n_tiles - w // wb for w, wb in zip(win_rows, win_blocks))
        for g in range(N_GROUPS):
            first = win_first[g]
            for _ in range(2):
                out_shapes.append(jax.ShapeDtypeStruct((G, GROUP_WIDTH, win_rows[g]), F32))
                out_specs.append(pl.BlockSpec((1, GROUP_WIDTH, win_blocks[g]),
                                              lambda b, i, first=first: (b, 0, jnp.maximum(i - first, 0))))

    scratch = []
    if any(d > 1 for d in dils):
        scratch.append(pltpu.VMEM((D_MODEL // LANES + 2, tm, LANES), F32))
        scratch += [pltpu.VMEM((tm, D_MODEL), BF16) for d in dils if d > 1]
    kern = functools.partial(_inproj_kernel, tm=tm, dils=dils, paired=paired, keys_t=keys_t,
                             win_blocks=win_blocks, win_first=win_first)
    return pl.pallas_call(
        kern,
        grid=(G, n_tiles),
        in_specs=[pl.BlockSpec((1, tm, D_MODEL), lambda b, i: (b, i, 0)),
                  mod_spec(0), mod_spec(1),
                  _resident(w_main.shape), _resident(w_qk.shape)] + [tab_spec(t) for t in all_tabs],
        out_specs=out_specs,
        out_shape=out_shapes,
        scratch_shapes=scratch,
        compiler_params=_cparams(("arbitrary", "arbitrary")),
        name="in_projection",
    )(x, mod, mod, w_main, w_qk, *all_tabs)


def _attn_kernel(q_ref, k_ref, v_ref, kp_ref, vp_ref, o_ref, lse_ref, *, tl, nc):
    t = pl.program_id(2)
    row = lax.broadcasted_iota(jnp.int32, (ATTN_BAND, ATTN_BAND), 0)
    col = lax.broadcasted_iota(jnp.int32, (ATTN_BAND, ATTN_BAND), 1)
    lane = lax.broadcasted_iota(jnp.int32, (1, GROUP_WIDTH), 1)
    cur_ok = col <= row
    qk_lanes = [(lane % LANES) // PAIR_SLOT == h for h in range(HEADS_PER_GROUP)]
    v_lanes = [lane // HEAD_DIM == h for h in range(HEADS_PER_GROUP)]

    def scores(q, k_prev, k_cur):
        kk = jnp.concatenate([k_prev, k_cur], axis=0)
        qs = jnp.concatenate([jnp.where(sel, q, jnp.zeros_like(q)) for sel in qk_lanes], axis=0)
        return lax.dot_general(qs, kk, (((1,), (1,)), ((), ())), preferred_element_type=F32)

    def finish(s, v_prev, v_cur, prev_shift):
        prev_ok = col >= row + prev_shift
        mask = jnp.concatenate([prev_ok, cur_ok], axis=1)
        mask = jnp.concatenate([mask] * HEADS_PER_GROUP, axis=0)
        vv = jnp.concatenate([v_prev, v_cur], axis=0)
        s = jnp.where(mask, s, NEG_BIG)
        m = jnp.max(s, axis=1, keepdims=True)
        e = jnp.exp(s - m)
        l = jnp.sum(e, axis=1, keepdims=True)
        oh = jnp.dot((e / l).astype(BF16), vv, preferred_element_type=F32)
        lse = m + jnp.log(l)
        o = jnp.zeros((ATTN_BAND, GROUP_WIDTH), F32)
        lo = jnp.zeros((ATTN_BAND, GROUP_WIDTH), F32)
        for h, sel in enumerate(v_lanes):
            rows = slice(h * ATTN_BAND, (h + 1) * ATTN_BAND)
            o = jnp.where(sel, oh[rows], o)
            lo = jnp.where(sel, lse[rows], lo)
        return o, lo

    first = jnp.where(t > 0, 0, ATTN_BAND)
    blocks = [(c, j) for c in range(nc) for j in range(tl // ATTN_BAND)]

    def prev_rows(ref, prev_ref, c, j):
        return prev_ref[0, c] if j == 0 else ref[0, c, (j - 1) * ATTN_BAND:j * ATTN_BAND, :]

    def cur_rows(ref, c, j):
        return ref[0, c, j * ATTN_BAND:(j + 1) * ATTN_BAND, :]

    def block_scores(c, j):
        return scores(cur_rows(q_ref, c, j), prev_rows(k_ref, kp_ref, c, j), cur_rows(k_ref, c, j))

    ahead = [block_scores(*blk) for blk in blocks[:SCORE_LOOKAHEAD]]
    for n, (c, j) in enumerate(blocks):
        if n + SCORE_LOOKAHEAD < len(blocks):
            ahead.append(block_scores(*blocks[n + SCORE_LOOKAHEAD]))
        o, lse = finish(ahead[n], prev_rows(v_ref, vp_ref, c, j), cur_rows(v_ref, c, j), first if j == 0 else 0)
        rows = slice(j * ATTN_BAND, (j + 1) * ATTN_BAND)
        o_ref[0, c, rows, :] = o.astype(o_ref.dtype)
        lse_ref[0, c, rows, :] = lse


def _prompt_attention(qkv, g, *, tl, nc):
    B, dil, L, _ = qkv.shape
    sub = tl // ATTN_BAND

    def cur(c):
        return pl.BlockSpec((1, nc, tl, SEG), lambda b, r, t: (b, r, t, c))

    def prev(c):
        return pl.BlockSpec((1, nc, ATTN_BAND, SEG), lambda b, r, t: (b, r, jnp.maximum(t * sub - 1, 0), c))

    out_spec = pl.BlockSpec((1, nc, tl, SEG), lambda b, r, t: (b, r, t, 0))
    return pl.pallas_call(
        functools.partial(_attn_kernel, tl=tl, nc=nc),
        grid=(B, dil // nc, L // tl),
        in_specs=[cur(0), cur(1), cur(2), prev(1), prev(2)],
        out_specs=[out_spec, out_spec],
        out_shape=[jax.ShapeDtypeStruct((B, dil, L, SEG), BF16),
                   jax.ShapeDtypeStruct((B, dil, L, SEG), F32)],
        compiler_params=_cparams(("arbitrary", "arbitrary", "arbitrary")),
        name="prompt_attention_g%d" % g,
    )(qkv, qkv, qkv, qkv, qkv)


def _retention_tables():
    f32 = np.float32
    C = RET_CHUNK
    lg = np.log(f32(1.0) - f32(2.0) ** (f32(-5.0) - np.arange(RET_HEADS, dtype=f32)))
    idx = np.arange(C, dtype=f32)
    diff = idx[:, None] - idx[None, :]
    decay_in = np.where(diff >= 0, np.exp(lg[:, None, None] * np.maximum(diff, f32(0.0))), f32(0.0))
    decay_q = np.exp(lg[:, None] * (idx[None, :] + f32(1.0)))[:, :, None]
    decay_k = np.exp(lg[:, None] * (f32(C - 1.0) - idx[None, :]))[:, None, :]
    return tuple(t.astype(f32) for t in (decay_in, decay_q, decay_k))


def _retention_kernel(qk_ref, kt_ref, v_ref, din_ref, dq_ref, dk_ref, o_ref, state_ref, r_scr, *, tr):
    t = pl.program_id(1)
    n_chunks = tr // RET_CHUNK

    @pl.when(t == 0)
    def _():
        r_scr[...] = jnp.zeros_like(r_scr)

    def key_t(c, h):
        return kt_ref[0, h * RET_QK_DIM:(h + 1) * RET_QK_DIM, c * RET_CHUNK:(c + 1) * RET_CHUNK]

    def val(c, h):
        return v_ref[0, c * RET_CHUNK:(c + 1) * RET_CHUNK, h * RET_V_DIM:(h + 1) * RET_V_DIM]

    states = [[None] * RET_HEADS for _ in range(n_chunks)]
    for h in range(RET_HEADS):
        r = r_scr[h]
        for c in range(n_chunks):
            states[c][h] = r
            kd = (key_t(c, h).astype(F32) * dk_ref[h]).astype(BF16)
            r = r * (_ret_gamma(h) ** RET_CHUNK) + jnp.dot(kd, val(c, h), preferred_element_type=F32)
        r_scr[h] = r

    def query(c, h):
        return qk_ref[0, c * RET_CHUNK:(c + 1) * RET_CHUNK, OFF_RQ + h * RET_QK_DIM:OFF_RQ + (h + 1) * RET_QK_DIM]

    scores = [[(jnp.dot(query(c, h), key_t(c, h), preferred_element_type=F32) * din_ref[h]).astype(BF16)
               for h in range(RET_HEADS)] for c in range(n_chunks)]

    for c in range(n_chunks):
        rows = slice(c * RET_CHUNK, (c + 1) * RET_CHUNK)
        for h in range(RET_HEADS):
            q = query(c, h)
            o = (jnp.dot(scores[c][h], val(c, h), preferred_element_type=F32)
                 + jnp.dot(q, states[c][h].astype(BF16), preferred_element_type=F32) * dq_ref[h])
            mu = jnp.mean(o, axis=1, keepdims=True)
            d = o - mu
            var = jnp.mean(d * d, axis=1, keepdims=True)
            o_ref[0, rows, h * RET_V_DIM:(h + 1) * RET_V_DIM] = (d * lax.rsqrt(var + GN_EPS)).astype(o_ref.dtype)

    @pl.when(t == pl.num_programs(1) - 1)
    def _():
        state_ref[0] = r_scr[...]


def _prompt_retention(dense, keys_t, *, tr):
    B, S, _ = dense.shape
    tabs = _retention_tables()
    blk = RET_V_WIDTH

    def colblk(c):
        return pl.BlockSpec((1, tr, blk), lambda b, t: (b, t, c))

    return pl.pallas_call(
        functools.partial(_retention_kernel, tr=tr),
        grid=(B, S // tr),
        in_specs=[pl.BlockSpec((1, tr, RET_QK_WIDTH), lambda b, t: (b, t, OFF_RQ // RET_QK_WIDTH)),
                  pl.BlockSpec((1, RET_QK_WIDTH, tr), lambda b, t: (b, 0, t)),
                  colblk(OFF_RV // blk)]
                 + [pl.BlockSpec(tab.shape, lambda b, t: (0, 0, 0)) for tab in tabs],
        out_specs=[pl.BlockSpec((1, tr, RET_V_WIDTH), lambda b, t: (b, t, 0)),
                   pl.BlockSpec((1, RET_HEADS, RET_QK_DIM, RET_V_DIM), lambda b, t: (b, 0, 0, 0))],
        out_shape=[jax.ShapeDtypeStruct((B, S, RET_V_WIDTH), BF16),
                   jax.ShapeDtypeStruct((B, RET_HEADS, RET_QK_DIM, RET_V_DIM), F32)],
        scratch_shapes=[pltpu.VMEM((RET_HEADS, RET_QK_DIM, RET_V_DIM), F32)],
        compiler_params=_cparams(("arbitrary", "arbitrary")),
        name="prompt_retention",
    )(dense, keys_t, dense, *tabs)


def _layer_norm(u, g, b):
    mu = jnp.mean(u, axis=1, keepdims=True)
    d = u - mu
    var = jnp.mean(d * d, axis=1, keepdims=True)
    return d * lax.rsqrt(var + LN_EPS) * g + b


def _column(row):
    return jnp.broadcast_to(row, (LANES, LANES)).T


def _window_half_step(first_half, qkv_refs, cache_refs, o_ref, lse_ref, new_refs):
    pad = 16
    row_h = lax.broadcasted_iota(jnp.int32, (pad, LANES), 0)
    lane_h = lax.broadcasted_iota(jnp.int32, (pad, LANES), 1) // HEAD_DIM
    own = row_h == lane_h
    lane = lax.broadcasted_iota(jnp.int32, (LANES, LANES), 1)

    def pick(row):
        return jnp.where(first_half, row[:, :LANES], row[:, LANES:])

    def rolled(c_ref, n_ref, new_row):
        w = c_ref.shape[3]
        new_col = _column(new_row)
        nblk = w // LANES
        cur = pltpu.roll(c_ref[0, 0, :, 0:LANES], LANES - 1, 1)
        for j in range(nblk):
            nxt = pltpu.roll(c_ref[0, 0, :, (j + 1) * LANES:(j + 2) * LANES], LANES - 1, 1) if j + 1 < nblk else new_col
            n_ref[0, 0, :, j * LANES:(j + 1) * LANES] = jnp.where(lane < LANES - 1, cur, nxt)
            cur = nxt

    started = []
    for g in range(N_GROUPS):
        q = pick(qkv_refs[g][0, :, 0:SEG])
        qsel = jnp.where(own, jnp.broadcast_to(q, (pad, LANES)), 0.0)
        s = jnp.dot(qsel.astype(BF16), cache_refs[2 * g][0, 0].astype(BF16), preferred_element_type=F32)
        started.append((qsel, s))

    def finish():
        for g, (qsel, s) in enumerate(started):
            dil = DILATED_PATTERNS[g][1]
            ck_ref, cv_ref = cache_refs[2 * g:2 * g + 2]
            kn = pick(qkv_refs[g][0, :, SEG:2 * SEG])
            vn = pick(qkv_refs[g][0, :, 2 * SEG:3 * SEG])
            pos = lax.broadcasted_iota(jnp.int32, s.shape, 1)
            s = jnp.where((pos & (dil - 1)) == 0, s, NEG_BIG)
            sn = jnp.sum(qsel * kn, axis=1, keepdims=True)
            m = jnp.maximum(jnp.max(s, axis=1, keepdims=True), sn)
            e = jnp.exp(s - m)
            en = jnp.exp(sn - m)
            l = jnp.sum(e, axis=1, keepdims=True) + en
            o_all = lax.dot_general((e / l).astype(BF16), cv_ref[0, 0].astype(BF16), (((1,), (1,)), ((), ())),
                                    preferred_element_type=F32) + (en / l) * vn
            o_ref[0, 0, :, g * LANES:(g + 1) * LANES] = jnp.sum(jnp.where(own, o_all, 0.0), axis=0, keepdims=True)
            lse_ref[0, 0, :, g * LANES:(g + 1) * LANES] = jnp.sum(jnp.where(own, m + jnp.log(l), 0.0), axis=0,
                                                                  keepdims=True)
            rolled(ck_ref, new_refs[2 * g], kn)
            rolled(cv_ref, new_refs[2 * g + 1], vn)

    return finish


N_MERGE_INPUTS = 24
N_SIDE_INPUTS = N_GROUPS + 2 * N_GROUPS
N_SIDE_OUTPUTS = 2 + 2 * N_GROUPS


def _merge_ffn_kernel(*refs, tm, sub, dils, side, steps_per_batch):
    (x_ref, ga_ref, gb_ref, rg_ref, ret_ref, o0_ref, o1_ref, o2_ref, l0_ref, l1_ref, l2_ref,
     gate1_ref, shift2_ref, scale2_ref, gate2_ref, watt_ref, wret_ref, wo_ref, wfi_ref, wfo_ref,
     ln1g_ref, ln1b_ref, ln2g_ref, ln2b_ref) = refs[:N_MERGE_INPUTS]
    n_in = N_MERGE_INPUTS + (N_SIDE_INPUTS if side else 0)
    y_ref = refs[n_in]
    n_out = 1 + (N_SIDE_OUTPUTS if side else 0)
    scratch = refs[n_in + n_out:]
    if side:
        side_in = refs[N_MERGE_INPUTS:n_in]
        side_out = refs[n_in + 1:n_in + n_out]
        step = pl.program_id(0) * steps_per_batch + pl.program_id(1)
        finish_side = _window_half_step(step % 2 == 0, side_in[:N_GROUPS], side_in[N_GROUPS:], side_out[0],
                                        side_out[1], side_out[2:])
    n_sub = tm // sub
    per_sub = 1 + 2 * N_GROUPS
    for s in range(n_sub):
        rows = slice(s * sub, (s + 1) * sub)
        act_scr = scratch[s * per_sub]
        stage_refs = scratch[s * per_sub + 1:(s + 1) * per_sub]

        def natural(ref, g, stage_ref):
            d = dils[g]
            n = sub // d
            if d == 1:
                return ref[0, 0, rows, :].astype(F32)
            for r in range(d):
                for u in range(GROUP_WIDTH // LANES):
                    stage_ref[u, pl.ds(r, n, stride=d), :] = (
                        ref[0, r, s * n:(s + 1) * n, u * LANES:(u + 1) * LANES].astype(F32))
            return jnp.concatenate([stage_ref[u] for u in range(GROUP_WIDTH // LANES)], axis=1)

        def mod_rows(ref):
            return ref[0] if ref.shape[1] == 1 else ref[0, rows, :]

        gated = rg_ref[0, rows, :].astype(F32) * ret_ref[0, rows, :].astype(F32)
        y_ret = jnp.dot(gated.astype(BF16), wret_ref[...], preferred_element_type=F32)
        lses = [natural(ref, g, stage_refs[g]) for g, ref in enumerate((l0_ref, l1_ref, l2_ref))]
        outs = [natural(ref, g, stage_refs[N_GROUPS + g]) for g, ref in enumerate((o0_ref, o1_ref, o2_ref))]
        m = jnp.maximum(jnp.maximum(lses[0], lses[1]), lses[2])
        es = [jnp.exp(l - m) for l in lses]
        att = (es[0] * outs[0] + es[1] * outs[1] + es[2] * outs[2]) / (es[0] + es[1] + es[2])
        y_att = jnp.dot(att.astype(BF16), watt_ref[...], preferred_element_type=F32)
        mix =(ga_ref[0, rows, :].astype(F32) * y_att + gb_ref[0, rows, :].astype(F32) * y_ret).astype(BF16)
        mixed = jnp.dot(mix, wo_ref[...], preferred_element_type=F32)
        if side and s == 0:
            finish_side()
        x1 = _layer_norm(DEEPNORM_ALPHA * x_ref[0, rows, :] + mod_rows(gate1_ref) * mixed,
                         ln1g_ref[...], ln1b_ref[...])
        h2 = (x1 * (1.0 + mod_rows(scale2_ref)) + mod_rows(shift2_ref)).astype(BF16)
        for c0 in range(0, D_FF, FFN_CHUNK):
            c1 = min(c0 + FFN_CHUNK, D_FF)
            fg = jnp.dot(h2, wfi_ref[:, c0:c1], preferred_element_type=F32)
            fu = jnp.dot(h2, wfi_ref[:, D_FF + c0:D_FF + c1], preferred_element_type=F32)
            act_scr[:, c0:c1] = (fg * _sigmoid(fg) * fu).astype(BF16)
        ff = jnp.dot(act_scr[...], wfo_ref[...], preferred_element_type=F32)
        y_ref[0, rows, :] = _layer_norm(DEEPNORM_ALPHA * x1 + mod_rows(gate2_ref) * ff, ln2g_ref[...], ln2b_ref[...])


def _merge_ffn(x, dense, ret, o_groups, lse_groups, mod, weights, ln, *, tm, sub, dils, side=None):
    G, R, _ = x.shape
    mod_rows = mod.shape[1]
    n_tiles = R // tm
    side_args, side_in_specs, side_out_specs, side_out_shapes = [], [], [], []
    if side is not None:
        qkvs, caches = side
        n_seq = caches[0].shape[0]
        assert G * n_tiles == 2 * n_seq

        def seq_half(b, i):
            step = b * n_tiles + i
            return step // 2, step % 2

        row_spec = pl.BlockSpec((1, 1, ATTN_WIDTH), lambda b, i: (seq_half(b, i)[0], 0, 0))
        small_spec = pl.BlockSpec((1, 1, 1, N_GROUPS * LANES), lambda b, i: seq_half(b, i) + (0, 0))
        halves = [c.reshape(n_seq, 2, LANES, c.shape[2]) for c in caches]
        half_specs = [pl.BlockSpec((1, 1, LANES, c.shape[3]), lambda b, i: seq_half(b, i) + (0, 0)) for c in halves]
        side_args = list(qkvs) + halves
        side_in_specs = [row_spec] * N_GROUPS + half_specs
        side_out_specs = [small_spec, small_spec] + half_specs
        side_out_shapes = ([jax.ShapeDtypeStruct((n_seq, 2, 1, N_GROUPS * LANES), F32)] * 2
                           + [jax.ShapeDtypeStruct(c.shape, F32) for c in halves])
    sub_scratch = [pltpu.VMEM((sub, D_FF), BF16)] + [pltpu.VMEM((GROUP_WIDTH // LANES, sub, LANES), F32)] * (2 * N_GROUPS)

    def mod_spec(col):
        if mod_rows == 1:
            return pl.BlockSpec((1, 1, D_MODEL), lambda b, i: (b, 0, col))
        return pl.BlockSpec((1, tm, D_MODEL), lambda b, i: (b, i, col))

    def rows(width, col=0):
        return pl.BlockSpec((1, tm, width), lambda b, i: (b, i, col))

    group_specs = [pl.BlockSpec((1, d, tm // d, GROUP_WIDTH), lambda b, i: (b, 0, i, 0)) for d in dils]
    vec = pl.BlockSpec((1, D_MODEL), lambda b, i: (0, 0))
    outs = pl.pallas_call(
        functools.partial(_merge_ffn_kernel, tm=tm, sub=sub, dils=dils, side=side is not None,
                          steps_per_batch=n_tiles),
        grid=(G, n_tiles),
        in_specs=[rows(D_MODEL), rows(D_MODEL, OFF_GA // D_MODEL), rows(D_MODEL, OFF_GB // D_MODEL),
                  rows(D_MODEL, OFF_RG // D_MODEL), rows(RET_V_WIDTH)] + group_specs + group_specs
                 + [mod_spec(2), mod_spec(3), mod_spec(4), mod_spec(5)]
                 + [_resident(w.shape) for w in weights] + [vec] * 4 + side_in_specs,
        out_specs=[rows(D_MODEL)] + side_out_specs,
        out_shape=[jax.ShapeDtypeStruct((G, R, D_MODEL), F32)] + side_out_shapes,
        scratch_shapes=sub_scratch * (tm // sub),
        compiler_params=_cparams(("arbitrary", "arbitrary")),
        name="merge_ffn",
    )(x, dense, dense, dense, ret, *o_groups, *lse_groups, mod, mod, mod, mod, *weights, *ln, *side_args)
    if side is None:
        return outs[0]
    y, o_s, lse_s, *rolled = outs
    return y, o_s, lse_s, [r.reshape(c.shape) for r, c in zip(rolled, side[1])]


def _sample_retention_kernel(dense_ref, state_ref, ret_ref, nstate_ref):
    for b in range(dense_ref.shape[0]):
        for h in range(RET_HEADS):
            gamma = _ret_gamma(h)
            q = dense_ref[b, :, OFF_RQ + h * RET_QK_DIM:OFF_RQ + (h + 1) * RET_QK_DIM]
            k = dense_ref[b, :, OFF_RK + h * RET_QK_DIM:OFF_RK + (h + 1) * RET_QK_DIM]
            v = dense_ref[b, :, OFF_RV + h * RET_V_DIM:OFF_RV + (h + 1) * RET_V_DIM]
            r_old = state_ref[b, h]
            q_col = jnp.concatenate([_column(q)] * (RET_V_DIM // LANES), axis=1)
            k_col = jnp.concatenate([_column(k)] * (RET_V_DIM // LANES), axis=1)
            qk = jnp.sum(q * k, axis=1, keepdims=True)
            o = qk * v + jnp.sum(q_col * r_old, axis=0, keepdims=True) * gamma
            nstate_ref[b, h] = r_old * gamma + k_col * v
            mu = jnp.mean(o, axis=1, keepdims=True)
            d = o - mu
            var = jnp.mean(d * d, axis=1, keepdims=True)
            ret_ref[b, :, h * RET_V_DIM:(h + 1) * RET_V_DIM] = d * lax.rsqrt(var + GN_EPS)


def _sample_retention(dense, state, *, nb=4):
    Bs = dense.shape[0]
    row = lambda w: pl.BlockSpec((nb, 1, w), lambda b: (b, 0, 0))
    st = pl.BlockSpec((nb, RET_HEADS, RET_QK_DIM, RET_V_DIM), lambda b: (b, 0, 0, 0))
    return pl.pallas_call(
        _sample_retention_kernel,
        grid=(Bs // nb,),
        in_specs=[row(DENSE_WIDTH), st],
        out_specs=[row(RET_V_WIDTH), st],
        out_shape=[jax.ShapeDtypeStruct((Bs, 1, RET_V_WIDTH), F32), jax.ShapeDtypeStruct(state.shape, F32)],
        compiler_params=_cparams(("arbitrary",)),
        name="sample_retention",
    )(dense, state)


def _feature_major(a):
    _, b, w, _, _ = a.shape
    return jnp.transpose(a[0], (0, 2, 3, 1)).reshape(b, GROUP_WIDTH, w)


def _row_major(a):
    b, _, w = a.shape
    return jnp.transpose(a.reshape(b, HEADS_PER_GROUP, HEAD_DIM, w), (0, 3, 1, 2))[None]


def kernel(x_prompt, x_sample, cache_k_w128, cache_v_w128, cache_k_w512, cache_v_w512, cache_k_w2048,
           cache_v_w2048, state_retention, c_prompt, c_sample, w_ada, b_ada, w_in, w_att_out, w_ret_out,
           w_o, ln1_g, ln1_b, w_ffn_in, w_ffn_out, ln2_g, ln2_b):
    assert w_in.shape[0] == 1 and x_sample.shape[1] == 1
    B, S, D = x_prompt.shape
    Bs = x_sample.shape[0]
    dils = tuple(d for _, d in DILATED_PATTERNS)

    w_main = jnp.concatenate([w_in[0][:, W_IN_ROTATE:], w_in[0][:, 2 * ATTN_WIDTH:W_IN_ROTATE]], axis=1).astype(BF16)
    w_qk = w_in[0][:, :2 * ATTN_WIDTH].astype(BF16)
    w_qk_paired = _pair_columns(w_qk)
    weights = tuple(w[0].astype(BF16) for w in (w_att_out, w_ret_out, w_o, w_ffn_in, w_ffn_out))
    ln = (ln1_g, ln1_b, ln2_g, ln2_b)

    mod = _modulation(jnp.concatenate([c_prompt, c_sample], axis=0), w_ada[0], b_ada[0])
    mod_p = mod[:B].reshape(B, 1, 6 * D)
    mod_s = mod[B:].reshape(1, Bs, 6 * D)

    tabs_p = _rope_tables(np.arange(S), paired=True)
    win_rows = tuple(min(w, S) for w, _ in DILATED_PATTERNS)
    dense_p, *rest = _in_projection(x_prompt, mod_p, w_main, w_qk_paired, tabs_p, tm=256, dils=dils, paired=True,
                                    keys_t=True, win_rows=win_rows, out_dtype=BF16)
    qkv_p, keys_t_p, wins_p = rest[:N_GROUPS], rest[N_GROUPS], rest[N_GROUPS + 1:]
    o_p, lse_p = [], []
    for g, d in enumerate(dils):
        tl = min(S // d, 1024)
        o, lse = _prompt_attention(qkv_p[g], g, tl=tl, nc=min(d, 1024 // tl))
        o_p.append(o)
        lse_p.append(lse)
    ret_p, state_p = _prompt_retention(dense_p, keys_t_p, tr=1024)

    xs = x_sample.reshape(1, Bs, D)
    tabs_s = _rope_tables(PAST_LEN + np.arange(1))
    unit = (1,) * N_GROUPS
    dense_s, *qkv_s = _in_projection(xs, mod_s, w_main, w_qk, tabs_s, tm=Bs, dils=unit, paired=False,
                                     keys_t=False, win_rows=None, out_dtype=F32)
    caches = [_feature_major(c) for c in (cache_k_w128, cache_v_w128, cache_k_w512, cache_v_w512,
                                          cache_k_w2048, cache_v_w2048)]
    ret_s, state_s = _sample_retention(dense_s.reshape(Bs, 1, DENSE_WIDTH), state_retention[0])
    y_p, o_s, lse_s, rolled = _merge_ffn(
        x_prompt, dense_p, ret_p, o_p, lse_p, mod_p, weights, ln, tm=512, sub=256, dils=dils,
        side=([a.reshape(Bs, 1, ATTN_WIDTH) for a in qkv_s], caches))

    def by_group(a):
        return [a[:, :, 0, g * LANES:(g + 1) * LANES].reshape(1, 1, Bs, GROUP_WIDTH) for g in range(N_GROUPS)]

    o_sg, lse_sg = by_group(o_s), by_group(lse_s)
    y_s = _merge_ffn(xs, dense_s, ret_s.reshape(1, Bs, RET_V_WIDTH), o_sg, lse_sg, mod_s, weights, ln,
                     tm=Bs, sub=Bs, dils=unit)

    return (y_p, y_s.reshape(Bs, 1, D),
            *[_row_major(w) for w in wins_p], state_p[None],
            *[_row_major(r) for r in rolled], state_s[None])
```

```python
import functools
import math

import jax
import jax.numpy as jnp
import numpy as np
from jax import lax
from jax.experimental import pallas as pl
from jax.experimental.pallas import tpu as pltpu

F32 = jnp.float32
BF16 = jnp.bfloat16

D_MODEL = 1024
PAST_LEN = 16384
DILATED_PATTERNS = ((128, 1), (512, 4), (2048, 16))
N_GROUPS = 3
HEADS_PER_GROUP = 4
HEAD_DIM = 64
GROUP_WIDTH = HEADS_PER_GROUP * HEAD_DIM
ATTN_WIDTH = N_GROUPS * GROUP_WIDTH
ATTN_BAND = 128
ATTN_SCALE = HEAD_DIM ** -0.5
ROPE_DIM = HEAD_DIM // 4
ROPE_THETA = 500000.0
RET_HEADS = 4
RET_QK_DIM = 128
RET_V_DIM = 256
RET_QK_WIDTH = RET_HEADS * RET_QK_DIM
RET_V_WIDTH = RET_HEADS * RET_V_DIM
RET_CHUNK = 128
RET_ROPE_THETA = 10000.0
IN_WIDTH = 3 * ATTN_WIDTH + 2 * RET_QK_WIDTH + 2 * RET_V_WIDTH + 2 * D_MODEL
D_FF = 2816
DEEPNORM_ALPHA = 2.0 ** 0.25
LN_EPS = 1e-5
GN_EPS = 1e-6

LANES = 128
SEG = 256
N_SEG = IN_WIDTH // SEG
FFN_CHUNK = 256
FFN_SPLIT = 1536
SCORE_LOOKAHEAD = 2
OFF_RQ, OFF_RK, OFF_RV, OFF_RG, OFF_GA, OFF_GB = 0, 512, 1024, 2048, 3072, 4096
OFF_QA, OFF_KA, OFF_VA = 5120, 5888, 6656
DENSE_WIDTH = OFF_QA
W_IN_ROTATE = 3 * ATTN_WIDTH
NEG_BIG = -1e30
VMEM_LIMIT = 60 * 1024 * 1024


def _cparams(sem):
    return pltpu.CompilerParams(dimension_semantics=sem, vmem_limit_bytes=VMEM_LIMIT)


def _resident(shape):
    nd = len(shape)
    return pl.BlockSpec(shape, lambda *_: (0,) * nd, pipeline_mode=pl.Buffered(1))


def _sigmoid(x):
    return 1.0 / (1.0 + jnp.exp(-x))


def _ret_gamma(h):
    return 1.0 - 2.0 ** (-5.0 - h)


def _ada_kernel(c_ref, w_ref, b_ref, o_ref):
    c = c_ref[...]
    a = (c * _sigmoid(c)).astype(BF16)
    o_ref[...] = jnp.dot(a, w_ref[...].astype(BF16), preferred_element_type=F32) + b_ref[...]


def _modulation(c_all, w_ada, b_ada):
    rows = c_all.shape[0]
    tn = 1024
    return pl.pallas_call(
        _ada_kernel,
        grid=(6 * D_MODEL // tn,),
        in_specs=[pl.BlockSpec((rows, D_MODEL), lambda j: (0, 0)),
                  pl.BlockSpec((D_MODEL, tn), lambda j: (0, j)),
                  pl.BlockSpec((1, tn), lambda j: (0, j))],
        out_specs=pl.BlockSpec((rows, tn), lambda j: (0, j)),
        out_shape=jax.ShapeDtypeStruct((rows, 6 * D_MODEL), F32),
        compiler_params=_cparams(("arbitrary",)),
        name="modulation",
    )(c_all, w_ada, b_ada.reshape(1, -1))


PAIR_SLOT = HEAD_DIM // 2


def _pair_source():
    src = []
    for half_dims in ((0, 16), (8, 40)):
        for h in range(HEADS_PER_GROUP):
            rot, rest = half_dims
            dims = list(range(rot, rot + ROPE_DIM // 2)) + list(range(rest, rest + PAIR_SLOT - ROPE_DIM // 2))
            src += [h * HEAD_DIM + d for d in dims]
    return src


def _pair_columns(w):
    run = ROPE_DIM // 2
    w5 = w.reshape(w.shape[0], -1, HEADS_PER_GROUP, HEAD_DIM // run, run)
    half_a = jnp.concatenate([w5[:, :, :, 0:1], w5[:, :, :, 2:5]], axis=3)
    half_b = jnp.concatenate([w5[:, :, :, 1:2], w5[:, :, :, 5:8]], axis=3)
    return jnp.stack([half_a, half_b], axis=2).reshape(w.shape)


def _rope_tables(pos, paired=False):
    posf = jnp.asarray(np.asarray(pos), jnp.int32).astype(F32)[:, None]
    lane = np.arange(LANES)
    half = ROPE_DIM // 2
    inv = jnp.exp(-math.log(ROPE_THETA) * jnp.arange(half, dtype=F32) * (2.0 / ROPE_DIM))
    ang = posf * inv[None, :]
    cos, sin = jnp.cos(ang), jnp.sin(ang)
    if paired:
        in_slot = lane % PAIR_SLOT
        f = in_slot % half
        att_c = jnp.where(in_slot < half, cos[:, f], 1.0)
        att_lo = jnp.where(in_slot < half, sin[:, f], 0.0)
        att_hi = att_lo
    else:
        in_head = lane % HEAD_DIM
        f = in_head % half
        att_c = jnp.where(in_head < ROPE_DIM, cos[:, f], 1.0)
        att_lo = jnp.where(in_head < half, -sin[:, f], 0.0)
        att_hi = jnp.where((in_head >= half) & (in_head < ROPE_DIM), sin[:, f], 0.0)
    rhalf = RET_QK_DIM // 2
    rinv = jnp.exp(-math.log(RET_ROPE_THETA) * jnp.arange(rhalf, dtype=F32) * (2.0 / RET_QK_DIM))
    rang = posf * rinv[None, :]
    rcos, rsin = jnp.cos(rang), jnp.sin(rang)
    ret_c = jnp.concatenate([rcos, rcos], axis=-1)
    ret_s = jnp.concatenate([-rsin, rsin], axis=-1)
    return att_c, att_lo, att_hi, ret_c, ret_s


def _inproj_kernel(x_ref, shift_ref, scale_ref, w_ref, wqk_ref, *refs, tm, dils, paired, keys_t, win_blocks,
                   win_first):
    n_tab = 5
    att_tab = [r[...] for r in refs[:3]]
    rc, rs = refs[3][...], refs[4][...]
    refs = refs[n_tab:]
    n_perm = sum(d > 1 for d in dils)
    dense_ref = refs[0]
    qkv_refs = refs[1:1 + N_GROUPS]
    n_out = len(refs) - (1 + n_perm if n_perm else 0)
    kt_ref = refs[1 + N_GROUPS] if keys_t else None
    win_refs = refs[1 + N_GROUPS + bool(keys_t):n_out]
    i = pl.program_id(1)
    hf = x_ref[0] * (1.0 + scale_ref[0]) + shift_ref[0]
    h = hf.astype(BF16)

    h_class = {}
    att_tabs = [att_tab] * N_GROUPS
    if n_perm:
        stage_ref = refs[n_out]
        n_x = D_MODEL // LANES
        for u in range(n_x):
            stage_ref[u] = hf[:, u * LANES:(u + 1) * LANES]
        moving = [t for t in att_tab if t.shape[0] == tm]
        for u, t in enumerate(moving):
            stage_ref[n_x + u] = t
        for g, hp_ref in zip([g for g in range(N_GROUPS) if dils[g] > 1], refs[n_out + 1:]):
            d = dils[g]
            n = tm // d
            for r in range(d):
                for u in range(n_x):
                    hp_ref[r * n:(r + 1) * n, u * LANES:(u + 1) * LANES] = (
                        stage_ref[u, pl.ds(r, n, stride=d), :].astype(BF16))
            h_class[g] = hp_ref
            regrouped = iter([jnp.concatenate([stage_ref[n_x + u, pl.ds(r, n, stride=d), :] for r in range(d)], axis=0)
                              for u in range(len(moving))])
            att_tabs[g] = [next(regrouped) if t.shape[0] == tm else t for t in att_tab]

    def att_rope(a, tab):
        ac, alo, ahi = tab
        if paired:
            lo, hi = a[:, :LANES], a[:, LANES:]
            return jnp.concatenate([lo * ac - hi * alo, hi * ac + lo * alo], axis=1)
        parts = []
        for u in range(SEG // LANES):
            xs = a[:, u * LANES:(u + 1) * LANES]
            parts.append(xs * ac + pltpu.roll(xs, LANES - ROPE_DIM // 2, 1) * alo
                         + pltpu.roll(xs, ROPE_DIM // 2, 1) * ahi)
        return jnp.concatenate(parts, axis=1)

    def ret_rope(a):
        parts = []
        for u in range(SEG // LANES):
            xs = a[:, u * LANES:(u + 1) * LANES]
            parts.append(xs * rc + pltpu.roll(xs, RET_QK_DIM // 2, 1) * rs)
        return jnp.concatenate(parts, axis=1)

    def emit_group(g, c, a):
        ref, d = qkv_refs[g], dils[g]
        n = tm // d
        for r in range(d):
            ref[0, r, :, c * SEG:(c + 1) * SEG] = a[r * n:(r + 1) * n, :].astype(ref.dtype)

    def wk_cols(g):
        return wqk_ref[:, ATTN_WIDTH + g * SEG:ATTN_WIDTH + (g + 1) * SEG]

    def wv_cols(g):
        return w_ref[:, DENSE_WIDTH + g * SEG:DENSE_WIDTH + (g + 1) * SEG]

    kv_f32 = {}

    def position_order(a, g, slot):
        d = dils[g]
        if d == 1:
            return a
        n = tm // d
        for u in range(SEG // LANES):
            for r in range(d):
                stage_ref[slot + u, pl.ds(r, n, stride=d), :] = a[r * n:(r + 1) * n, u * LANES:(u + 1) * LANES]
        return jnp.concatenate([stage_ref[slot + u] for u in range(SEG // LANES)], axis=1)

    def emit_windows(g):
        wb = win_blocks[g]
        assert dils[g] == 1 or wb == tm
        k, v = kv_f32[g]
        slot = 2 * (SEG // LANES) * (g % 2)
        at = position_order(k, g, slot)[tm - wb:, :].T
        if paired:
            dst = {f: lane for lane, f in enumerate(_pair_source())}
            at = jnp.concatenate([at[dst[f]:dst[f] + 8, :] for f in range(0, GROUP_WIDTH, 8)], axis=0)
        win_refs[2 * g][0] = at
        win_refs[2 * g + 1][0] = position_order(v, g, slot + SEG // LANES)[tm - wb:, :].T

    for seg in range(DENSE_WIDTH // SEG):
        c0 = seg * SEG
        a = jnp.dot(h, w_ref[:, c0:c0 + SEG], preferred_element_type=F32)
        if c0 < OFF_RK:
            a = ret_rope(a)
        elif c0 < OFF_RV:
            a = ret_rope(a) * (RET_QK_DIM ** -0.5)
            if keys_t:
                kt_ref[0, c0 - OFF_RK:c0 - OFF_RK + SEG, :] = a.T.astype(kt_ref.dtype)
        elif c0 < OFF_RG:
            pass
        elif c0 < OFF_GA:
            a = a * _sigmoid(a)
        else:
            a = _sigmoid(a)
        dense_ref[0, :, c0:c0 + SEG] = a.astype(dense_ref.dtype)

    for g in range(N_GROUPS):
        hg = h_class[g][...] if g in h_class else h
        wq = wqk_ref[:, g * SEG:(g + 1) * SEG]
        emit_group(g, 0, att_rope(jnp.dot(hg, wq, preferred_element_type=F32), att_tabs[g]) * ATTN_SCALE)
        k = att_rope(jnp.dot(hg, wk_cols(g), preferred_element_type=F32), att_tabs[g])
        v = jnp.dot(hg, wv_cols(g), preferred_element_type=F32)
        emit_group(g, 1, k)
        emit_group(g, 2, v)
        kv_f32[g] = (k, v)

    if win_refs:
        order = sorted(range(N_GROUPS), key=lambda g: win_first[g])

        def guarded(pos):
            if pos == len(order):
                return
            g = order[pos]

            @pl.when(i >= win_first[g])
            def _():
                emit_windows(g)
                guarded(pos + 1)

        guarded(0)


def _in_projection(x, mod, w_main, w_qk, tables, *, tm, dils, paired, keys_t, win_rows, out_dtype):
    G, R, _ = x.shape
    n_tiles = R // tm
    mod_blk = tm if mod.shape[1] == R else 1
    tab_blk = tm if tables[0].shape[0] == R else 1

    def mod_spec(col):
        if mod_blk == 1:
            return pl.BlockSpec((1, 1, D_MODEL), lambda b, i: (b, 0, col))
        return pl.BlockSpec((1, tm, D_MODEL), lambda b, i: (b, i, col))

    def tab_spec(tab):
        if tab.shape[0] == 1:
            return pl.BlockSpec((1, LANES), lambda b, i: (0, 0))
        return pl.BlockSpec((tm, LANES), lambda b, i: (i, 0))

    att = list(tables[:3])
    if paired:
        att[2] = np.zeros((1, LANES), np.float32)
    all_tabs = att + list(tables[3:])

    out_shapes = [jax.ShapeDtypeStruct((G, R, DENSE_WIDTH), out_dtype)]
    out_specs = [pl.BlockSpec((1, tm, DENSE_WIDTH), lambda b, i: (b, i, 0))]
    for d in dils:
        out_shapes.append(jax.ShapeDtypeStruct((G, d, R // d, ATTN_WIDTH), out_dtype))
        out_specs.append(pl.BlockSpec((1, d, tm // d, ATTN_WIDTH), lambda b, i: (b, 0, i, 0)))
    if keys_t:
        out_shapes.append(jax.ShapeDtypeStruct((G, RET_QK_WIDTH, R), out_dtype))
        out_specs.append(pl.BlockSpec((1, RET_QK_WIDTH, tm), lambda b, i: (b, 0, i)))
    win_blocks = win_first = ()
    if win_rows:
        win_blocks = tuple(min(w, tm) for w in win_rows)
        win_first = tuple(n_tiles - w // wb for w, wb in zip(win_rows, win_blocks))
        for g in range(N_GROUPS):
            first = win_first[g]
            for _ in range(2):
                out_shapes.append(jax.ShapeDtypeStruct((G, GROUP_WIDTH, win_rows[g]), F32))
                out_specs.append(pl.BlockSpec((1, GROUP_WIDTH, win_blocks[g]),
                                              lambda b, i, first=first: (b, 0, jnp.maximum(i - first, 0))))

    scratch = []
    if any(d > 1 for d in dils):
        scratch.append(pltpu.VMEM((D_MODEL // LANES + 2, tm, LANES), F32))
        scratch += [pltpu.VMEM((tm, D_MODEL), BF16) for d in dils if d > 1]
    kern = functools.partial(_inproj_kernel, tm=tm, dils=dils, paired=paired, keys_t=keys_t,
                             win_blocks=win_blocks, win_first=win_first)
    return pl.pallas_call(
        kern,
        grid=(G, n_tiles),
        in_specs=[pl.BlockSpec((1, tm, D_MODEL), lambda b, i: (b, i, 0)),
                  mod_spec(0), mod_spec(1),
                  _resident(w_main.shape), _resident(w_qk.shape)] + [tab_spec(t) for t in all_tabs],
        out_specs=out_specs,
        out_shape=out_shapes,
        scratch_shapes=scratch,
        compiler_params=_cparams(("arbitrary", "arbitrary")),
        name="in_projection",
    )(x, mod, mod, w_main, w_qk, *all_tabs)


def _attn_kernel(q_ref, k_ref, v_ref, kp_ref, vp_ref, o_ref, lse_ref, *, tl, nc):
    t = pl.program_id(2)
    row = lax.broadcasted_iota(jnp.int32, (ATTN_BAND, ATTN_BAND), 0)
    col = lax.broadcasted_iota(jnp.int32, (ATTN_BAND, ATTN_BAND), 1)
    lane = lax.broadcasted_iota(jnp.int32, (1, GROUP_WIDTH), 1)
    cur_ok = col <= row
    qk_lanes = [(lane % LANES) // PAIR_SLOT == h for h in range(HEADS_PER_GROUP)]
    v_lanes = [lane // HEAD_DIM == h for h in range(HEADS_PER_GROUP)]

    def scores(q, k_prev, k_cur):
        kk = jnp.concatenate([k_prev, k_cur], axis=0)
        qs = jnp.concatenate([jnp.where(sel, q, jnp.zeros_like(q)) for sel in qk_lanes], axis=0)
        return lax.dot_general(qs, kk, (((1,), (1,)), ((), ())), preferred_element_type=F32)

    def finish(s, v_prev, v_cur, prev_shift):
        prev_ok = col >= row + prev_shift
        mask = jnp.concatenate([prev_ok, cur_ok], axis=1)
        mask = jnp.concatenate([mask] * HEADS_PER_GROUP, axis=0)
        vv = jnp.concatenate([v_prev, v_cur], axis=0)
        s = jnp.where(mask, s, NEG_BIG)
        m = jnp.max(s, axis=1, keepdims=True)
        e = jnp.exp(s - m)
        l = jnp.sum(e, axis=1, keepdims=True)
        oh = jnp.dot((e / l).astype(BF16), vv, preferred_element_type=F32)
        lse = m + jnp.log(l)
        o = jnp.zeros((ATTN_BAND, GROUP_WIDTH), F32)
        lo = jnp.zeros((ATTN_BAND, GROUP_WIDTH), F32)
        for h, sel in enumerate(v_lanes):
            rows = slice(h * ATTN_BAND, (h + 1) * ATTN_BAND)
            o = jnp.where(sel, oh[rows], o)
            lo = jnp.where(sel, lse[rows], lo)
        return o, lo

    first = jnp.where(t > 0, 0, ATTN_BAND)
    blocks = [(c, j) for c in range(nc) for j in range(tl // ATTN_BAND)]

    def prev_rows(ref, prev_ref, c, j):
        return prev_ref[0, c] if j == 0 else ref[0, c, (j - 1) * ATTN_BAND:j * ATTN_BAND, :]

    def cur_rows(ref, c, j):
        return ref[0, c, j * ATTN_BAND:(j + 1) * ATTN_BAND, :]

    def block_scores(c, j):
        return scores(cur_rows(q_ref, c, j), prev_rows(k_ref, kp_ref, c, j), cur_rows(k_ref, c, j))

    ahead = [block_scores(*blk) for blk in blocks[:SCORE_LOOKAHEAD]]
    for n, (c, j) in enumerate(blocks):
        if n + SCORE_LOOKAHEAD < len(blocks):
            ahead.append(block_scores(*blocks[n + SCORE_LOOKAHEAD]))
        o, lse = finish(ahead[n], prev_rows(v_ref, vp_ref, c, j), cur_rows(v_ref, c, j), first if j == 0 else 0)
        rows = slice(j * ATTN_BAND, (j + 1) * ATTN_BAND)
        o_ref[0, c, rows, :] = o.astype(o_ref.dtype)
        lse_ref[0, c, rows, :] = lse


def _prompt_attention(qkv, g, *, tl, nc):
    B, dil, L, _ = qkv.shape
    sub = tl // ATTN_BAND

    def cur(c):
        return pl.BlockSpec((1, nc, tl, SEG), lambda b, r, t: (b, r, t, c))

    def prev(c):
        return pl.BlockSpec((1, nc, ATTN_BAND, SEG), lambda b, r, t: (b, r, jnp.maximum(t * sub - 1, 0), c))

    out_spec = pl.BlockSpec((1, nc, tl, SEG), lambda b, r, t: (b, r, t, 0))
    return pl.pallas_call(
        functools.partial(_attn_kernel, tl=tl, nc=nc),
        grid=(B, dil // nc, L // tl),
        in_specs=[cur(0), cur(1), cur(2), prev(1), prev(2)],
        out_specs=[out_spec, out_spec],
        out_shape=[jax.ShapeDtypeStruct((B, dil, L, SEG), BF16),
                   jax.ShapeDtypeStruct((B, dil, L, SEG), F32)],
        compiler_params=_cparams(("arbitrary", "arbitrary", "arbitrary")),
        name="prompt_attention_g%d" % g,
    )(qkv, qkv, qkv, qkv, qkv)


def _retention_tables():
    f32 = np.float32
    C = RET_CHUNK
    lg = np.log(f32(1.0) - f32(2.0) ** (f32(-5.0) - np.arange(RET_HEADS, dtype=f32)))
    idx = np.arange(C, dtype=f32)
    diff = idx[:, None] - idx[None, :]
    decay_in = np.where(diff >= 0, np.exp(lg[:, None, None] * np.maximum(diff, f32(0.0))), f32(0.0))
    decay_q = np.exp(lg[:, None] * (idx[None, :] + f32(1.0)))[:, :, None]
    decay_k = np.exp(lg[:, None] * (f32(C - 1.0) - idx[None, :]))[:, None, :]
    return tuple(t.astype(f32) for t in (decay_in, decay_q, decay_k))


def _retention_kernel(qk_ref, kt_ref, v_ref, din_ref, dq_ref, dk_ref, o_ref, state_ref, r_scr, *, tr):
    t = pl.program_id(1)
    n_chunks = tr // RET_CHUNK

    @pl.when(t == 0)
    def _():
        r_scr[...] = jnp.zeros_like(r_scr)

    def key_t(c, h):
        return kt_ref[0, h * RET_QK_DIM:(h + 1) * RET_QK_DIM, c * RET_CHUNK:(c + 1) * RET_CHUNK]

    def val(c, h):
        return v_ref[0, c * RET_CHUNK:(c + 1) * RET_CHUNK, h * RET_V_DIM:(h + 1) * RET_V_DIM]

    states = [[None] * RET_HEADS for _ in range(n_chunks)]
    for h in range(RET_HEADS):
        r = r_scr[h]
        for c in range(n_chunks):
            states[c][h] = r
            kd = (key_t(c, h).astype(F32) * dk_ref[h]).astype(BF16)
            r = r * (_ret_gamma(h) ** RET_CHUNK) + jnp.dot(kd, val(c, h), preferred_element_type=F32)
        r_scr[h] = r

    def query(c, h):
        return qk_ref[0, c * RET_CHUNK:(c + 1) * RET_CHUNK, OFF_RQ + h * RET_QK_DIM:OFF_RQ + (h + 1) * RET_QK_DIM]

    scores = [[(jnp.dot(query(c, h), key_t(c, h), preferred_element_type=F32) * din_ref[h]).astype(BF16)
               for h in range(RET_HEADS)] for c in range(n_chunks)]

    for c in range(n_chunks):
        rows = slice(c * RET_CHUNK, (c + 1) * RET_CHUNK)
        for h in range(RET_HEADS):
            q = query(c, h)
            o = (jnp.dot(scores[c][h], val(c, h), preferred_element_type=F32)
                 + jnp.dot(q, states[c][h].astype(BF16), preferred_element_type=F32) * dq_ref[h])
            mu = jnp.mean(o, axis=1, keepdims=True)
            d = o - mu
            var = jnp.mean(d * d, axis=1, keepdims=True)
            o_ref[0, rows, h * RET_V_DIM:(h + 1) * RET_V_DIM] = (d * lax.rsqrt(var + GN_EPS)).astype(o_ref.dtype)

    @pl.when(t == pl.num_programs(1) - 1)
    def _():
        state_ref[0] = r_scr[...]


def _prompt_retention(dense, keys_t, *, tr):
    B, S, _ = dense.shape
    tabs = _retention_tables()
    blk = RET_V_WIDTH

    def colblk(c):
        return pl.BlockSpec((1, tr, blk), lambda b, t: (b, t, c))

    return pl.pallas_call(
        functools.partial(_retention_kernel, tr=tr),
        grid=(B, S // tr),
        in_specs=[pl.BlockSpec((1, tr, RET_QK_WIDTH), lambda b, t: (b, t, OFF_RQ // RET_QK_WIDTH)),
                  pl.BlockSpec((1, RET_QK_WIDTH, tr), lambda b, t: (b, 0, t)),
                  colblk(OFF_RV // blk)]
                 + [pl.BlockSpec(tab.shape, lambda b, t: (0, 0, 0)) for tab in tabs],
        out_specs=[pl.BlockSpec((1, tr, RET_V_WIDTH), lambda b, t: (b, t, 0)),
                   pl.BlockSpec((1, RET_HEADS, RET_QK_DIM, RET_V_DIM), lambda b, t: (b, 0, 0, 0))],
        out_shape=[jax.ShapeDtypeStruct((B, S, RET_V_WIDTH), BF16),
                   jax.ShapeDtypeStruct((B, RET_HEADS, RET_QK_DIM, RET_V_DIM), F32)],
        scratch_shapes=[pltpu.VMEM((RET_HEADS, RET_QK_DIM, RET_V_DIM), F32)],
        compiler_params=_cparams(("arbitrary", "arbitrary")),
        name="prompt_retention",
    )(dense, keys_t, dense, *tabs)


def _layer_norm(u, g, b):
    mu = jnp.mean(u, axis=1, keepdims=True)
    d = u - mu
    var = jnp.mean(d * d, axis=1, keepdims=True)
    return d * lax.rsqrt(var + LN_EPS) * g + b


def _column(row):
    return jnp.broadcast_to(row, (LANES, LANES)).T


def _window_half_step(first_half, qkv_refs, cache_refs, o_ref, lse_ref, new_refs):
    pad = 16
    row_h = lax.broadcasted_iota(jnp.int32, (pad, LANES), 0)
    lane_h = lax.broadcasted_iota(jnp.int32, (pad, LANES), 1) // HEAD_DIM
    own = row_h == lane_h
    lane = lax.broadcasted_iota(jnp.int32, (LANES, LANES), 1)

    def pick(row):
        return jnp.where(first_half, row[:, :LANES], row[:, LANES:])

    def rolled(c_ref, n_ref, new_row):
        w = c_ref.shape[3]
        new_col = _column(new_row)
        nblk = w // LANES
        cur = pltpu.roll(c_ref[0, 0, :, 0:LANES], LANES - 1, 1)
        for j in range(nblk):
            nxt = pltpu.roll(c_ref[0, 0, :, (j + 1) * LANES:(j + 2) * LANES], LANES - 1, 1) if j + 1 < nblk else new_col
            n_ref[0, 0, :, j * LANES:(j + 1) * LANES] = jnp.where(lane < LANES - 1, cur, nxt)
            cur = nxt

    started = []
    for g in range(N_GROUPS):
        q = pick(qkv_refs[g][0, :, 0:SEG])
        qsel = jnp.where(own, jnp.broadcast_to(q, (pad, LANES)), 0.0)
        s = jnp.dot(qsel.astype(BF16), cache_refs[2 * g][0, 0].astype(BF16), preferred_element_type=F32)
        started.append((qsel, s))

    def finish():
        for g, (qsel, s) in enumerate(started):
            dil = DILATED_PATTERNS[g][1]
            ck_ref, cv_ref = cache_refs[2 * g:2 * g + 2]
            kn = pick(qkv_refs[g][0, :, SEG:2 * SEG])
            vn = pick(qkv_refs[g][0, :, 2 * SEG:3 * SEG])
            pos = lax.broadcasted_iota(jnp.int32, s.shape, 1)
            s = jnp.where((pos & (dil - 1)) == 0, s, NEG_BIG)
            sn = jnp.sum(qsel * kn, axis=1, keepdims=True)
            m = jnp.maximum(jnp.max(s, axis=1, keepdims=True), sn)
            e = jnp.exp(s - m)
            en = jnp.exp(sn - m)
            l = jnp.sum(e, axis=1, keepdims=True) + en
            o_all = lax.dot_general((e / l).astype(BF16), cv_ref[0, 0].astype(BF16), (((1,), (1,)), ((), ())),
                                    preferred_element_type=F32) + (en / l) * vn
            o_ref[0, 0, :, g * LANES:(g + 1) * LANES] = jnp.sum(jnp.where(own, o_all, 0.0), axis=0, keepdims=True)
            lse_ref[0, 0, :, g * LANES:(g + 1) * LANES] = jnp.sum(jnp.where(own, m + jnp.log(l), 0.0), axis=0,
                                                                  keepdims=True)
            rolled(ck_ref, new_refs[2 * g], kn)
            rolled(cv_ref, new_refs[2 * g + 1], vn)

    return finish


N_MERGE_INPUTS = 24
N_SIDE_INPUTS = N_GROUPS + 2 * N_GROUPS
N_SIDE_OUTPUTS = 2 + 2 * N_GROUPS


def _merge_ffn_kernel(*refs, tm, sub, dils, side, steps_per_batch):
    (x_ref, ga_ref, gb_ref, rg_ref, ret_ref, o0_ref, o1_ref, o2_ref, l0_ref, l1_ref, l2_ref,
     gate1_ref, shift2_ref, scale2_ref, gate2_ref, watt_ref, wret_ref, wo_ref, wfi_ref, wfo_ref,
     ln1g_ref, ln1b_ref, ln2g_ref, ln2b_ref) = refs[:N_MERGE_INPUTS]
    n_in = N_MERGE_INPUTS + (N_SIDE_INPUTS if side else 0)
    y_ref = refs[n_in]
    n_out = 1 + (N_SIDE_OUTPUTS if side else 0)
    scratch = refs[n_in + n_out:]
    if side:
        side_in = refs[N_MERGE_INPUTS:n_in]
        side_out = refs[n_in + 1:n_in + n_out]
        step = pl.program_id(0) * steps_per_batch + pl.program_id(1)
        finish_side = _window_half_step(step % 2 == 0, side_in[:N_GROUPS], side_in[N_GROUPS:], side_out[0],
                                        side_out[1], side_out[2:])
    n_sub = tm // sub
    per_sub = 1 + 2 * N_GROUPS
    def rows(s):
        return slice(s * sub, (s + 1) * sub)

    def mod_rows(ref, s):
        return ref[0] if ref.shape[1] == 1 else ref[0, rows(s), :]

    def natural(ref, g, s, which):
        d = dils[g]
        n = sub // d
        if d == 1:
            return ref[0, 0, rows(s), :].astype(F32)
        stage_ref = scratch[s * per_sub + 1 + which * N_GROUPS + g]
        for r in range(d):
            for u in range(GROUP_WIDTH // LANES):
                stage_ref[u, pl.ds(r, n, stride=d), :] = (
                    ref[0, r, s * n:(s + 1) * n, u * LANES:(u + 1) * LANES].astype(F32))
        return jnp.concatenate([stage_ref[u] for u in range(GROUP_WIDTH // LANES)], axis=1)

    def branch_dots(s):
        gated = rg_ref[0, rows(s), :].astype(F32) * ret_ref[0, rows(s), :].astype(F32)
        y_ret = jnp.dot(gated.astype(BF16), wret_ref[...], preferred_element_type=F32)
        lses = [natural(ref, g, s, 0) for g, ref in enumerate((l0_ref, l1_ref, l2_ref))]
        outs = [natural(ref, g, s, 1) for g, ref in enumerate((o0_ref, o1_ref, o2_ref))]
        m = jnp.maximum(jnp.maximum(lses[0], lses[1]), lses[2])
        es = [jnp.exp(l - m) for l in lses]
        att = (es[0] * outs[0] + es[1] * outs[1] + es[2] * outs[2]) / (es[0] + es[1] + es[2])
        y_att = jnp.dot(att.astype(BF16), watt_ref[...], preferred_element_type=F32)
        return y_att, y_ret

    def out_projection(s, y_att, y_ret):
        mix = (ga_ref[0, rows(s), :].astype(F32) * y_att + gb_ref[0, rows(s), :].astype(F32) * y_ret).astype(BF16)
        return jnp.dot(mix, wo_ref[...], preferred_element_type=F32)

    def first_norm(s, mixed):
        x1 = _layer_norm(DEEPNORM_ALPHA * x_ref[0, rows(s), :] + mod_rows(gate1_ref, s) * mixed,
                         ln1g_ref[...], ln1b_ref[...])
        return x1, (x1 * (1.0 + mod_rows(scale2_ref, s)) + mod_rows(shift2_ref, s)).astype(BF16)

    def swiglu(s, h2):
        act_scr = scratch[s * per_sub]
        for c0 in range(0, D_FF, FFN_CHUNK):
            c1 = min(c0 + FFN_CHUNK, D_FF)
            fg = jnp.dot(h2, wfi_ref[:, c0:c1], preferred_element_type=F32)
            fu = jnp.dot(h2, wfi_ref[:, D_FF + c0:D_FF + c1], preferred_element_type=F32)
            act_scr[:, c0:c1] = (fg * _sigmoid(fg) * fu).astype(BF16)

    def ffn_out(s, k0, k1):
        return jnp.dot(scratch[s * per_sub][:, k0:k1], wfo_ref[k0:k1, :], preferred_element_type=F32)

    def second_norm(s, x1, ff):
        y_ref[0, rows(s), :] = _layer_norm(DEEPNORM_ALPHA * x1 + mod_rows(gate2_ref, s) * ff,
                                           ln2g_ref[...], ln2b_ref[...])

    mixed = out_projection(0, *branch_dots(0))
    if side:
        finish_side()
    x1, h2 = first_norm(0, mixed)
    for s in range(n_sub):
        swiglu(s, h2)
        if s + 1 < n_sub:
            nxt = branch_dots(s + 1)
            ff = ffn_out(s, 0, FFN_SPLIT)
            mixed = out_projection(s + 1, *nxt)
            ff = ff + ffn_out(s, FFN_SPLIT, D_FF)
            x1_next, h2 = first_norm(s + 1, mixed)
            second_norm(s, x1, ff)
            x1 = x1_next
        else:
            second_norm(s, x1, ffn_out(s, 0, D_FF))


def _merge_ffn(x, dense, ret, o_groups, lse_groups, mod, weights, ln, *, tm, sub, dils, side=None):
    G, R, _ = x.shape
    mod_rows = mod.shape[1]
    n_tiles = R // tm
    side_args, side_in_specs, side_out_specs, side_out_shapes = [], [], [], []
    if side is not None:
        qkvs, caches = side
        n_seq = caches[0].shape[0]
        assert G * n_tiles == 2 * n_seq

        def seq_half(b, i):
            step = b * n_tiles + i
            return step // 2, step % 2

        row_spec = pl.BlockSpec((1, 1, ATTN_WIDTH), lambda b, i: (seq_half(b, i)[0], 0, 0))
        small_spec = pl.BlockSpec((1, 1, 1, N_GROUPS * LANES), lambda b, i: seq_half(b, i) + (0, 0))
        halves = [c.reshape(n_seq, 2, LANES, c.shape[2]) for c in caches]
        half_specs = [pl.BlockSpec((1, 1, LANES, c.shape[3]), lambda b, i: seq_half(b, i) + (0, 0)) for c in halves]
        side_args = list(qkvs) + halves
        side_in_specs = [row_spec] * N_GROUPS + half_specs
        side_out_specs = [small_spec, small_spec] + half_specs
        side_out_shapes = ([jax.ShapeDtypeStruct((n_seq, 2, 1, N_GROUPS * LANES), F32)] * 2
                           + [jax.ShapeDtypeStruct(c.shape, F32) for c in halves])
    sub_scratch = [pltpu.VMEM((sub, D_FF), BF16)] + [pltpu.VMEM((GROUP_WIDTH // LANES, sub, LANES), F32)] * (2 * N_GROUPS)

    def mod_spec(col):
        if mod_rows == 1:
            return pl.BlockSpec((1, 1, D_MODEL), lambda b, i: (b, 0, col))
        return pl.BlockSpec((1, tm, D_MODEL), lambda b, i: (b, i, col))

    def rows(width, col=0):
        return pl.BlockSpec((1, tm, width), lambda b, i: (b, i, col))

    group_specs = [pl.BlockSpec((1, d, tm // d, GROUP_WIDTH), lambda b, i: (b, 0, i, 0)) for d in dils]
    vec = pl.BlockSpec((1, D_MODEL), lambda b, i: (0, 0))
    outs = pl.pallas_call(
        functools.partial(_merge_ffn_kernel, tm=tm, sub=sub, dils=dils, side=side is not None,
                          steps_per_batch=n_tiles),
        grid=(G, n_tiles),
        in_specs=[rows(D_MODEL), rows(D_MODEL, OFF_GA // D_MODEL), rows(D_MODEL, OFF_GB // D_MODEL),
                  rows(D_MODEL, OFF_RG // D_MODEL), rows(RET_V_WIDTH)] + group_specs + group_specs
                 + [mod_spec(2), mod_spec(3), mod_spec(4), mod_spec(5)]
                 + [_resident(w.shape) for w in weights] + [vec] * 4 + side_in_specs,
        out_specs=[rows(D_MODEL)] + side_out_specs,
        out_shape=[jax.ShapeDtypeStruct((G, R, D_MODEL), F32)] + side_out_shapes,
        scratch_shapes=sub_scratch * (tm // sub),
        compiler_params=_cparams(("arbitrary", "arbitrary")),
        name="merge_ffn",
    )(x, dense, dense, dense, ret, *o_groups, *lse_groups, mod, mod, mod, mod, *weights, *ln, *side_args)
    if side is None:
        return outs[0]
    y, o_s, lse_s, *rolled = outs
    return y, o_s, lse_s, [r.reshape(c.shape) for r, c in zip(rolled, side[1])]


def _sample_retention_kernel(dense_ref, state_ref, ret_ref, nstate_ref):
    for b in range(dense_ref.shape[0]):
        for h in range(RET_HEADS):
            gamma = _ret_gamma(h)
            q = dense_ref[b, :, OFF_RQ + h * RET_QK_DIM:OFF_RQ + (h + 1) * RET_QK_DIM]
            k = dense_ref[b, :, OFF_RK + h * RET_QK_DIM:OFF_RK + (h + 1) * RET_QK_DIM]
            v = dense_ref[b, :, OFF_RV + h * RET_V_DIM:OFF_RV + (h + 1) * RET_V_DIM]
            r_old = state_ref[b, h]
            q_col = jnp.concatenate([_column(q)] * (RET_V_DIM // LANES), axis=1)
            k_col = jnp.concatenate([_column(k)] * (RET_V_DIM // LANES), axis=1)
            qk = jnp.sum(q * k, axis=1, keepdims=True)
            o = qk * v + jnp.sum(q_col * r_old, axis=0, keepdims=True) * gamma
            nstate_ref[b, h] = r_old * gamma + k_col * v
            mu = jnp.mean(o, axis=1, keepdims=True)
            d = o - mu
            var = jnp.mean(d * d, axis=1, keepdims=True)
            ret_ref[b, :, h * RET_V_DIM:(h + 1) * RET_V_DIM] = d * lax.rsqrt(var + GN_EPS)


def _sample_retention(dense, state, *, nb=4):
    Bs = dense.shape[0]
    row = lambda w: pl.BlockSpec((nb, 1, w), lambda b: (b, 0, 0))
    st = pl.BlockSpec((nb, RET_HEADS, RET_QK_DIM, RET_V_DIM), lambda b: (b, 0, 0, 0))
    return pl.pallas_call(
        _sample_retention_kernel,
        grid=(Bs // nb,),
        in_specs=[row(DENSE_WIDTH), st],
        out_specs=[row(RET_V_WIDTH), st],
        out_shape=[jax.ShapeDtypeStruct((Bs, 1, RET_V_WIDTH), F32), jax.ShapeDtypeStruct(state.shape, F32)],
        compiler_params=_cparams(("arbitrary",)),
        name="sample_retention",
    )(dense, state)


def _feature_major(a):
    _, b, w, _, _ = a.shape
    return jnp.transpose(a[0], (0, 2, 3, 1)).reshape(b, GROUP_WIDTH, w)


def _row_major(a):
    b, _, w = a.shape
    return jnp.transpose(a.reshape(b, HEADS_PER_GROUP, HEAD_DIM, w), (0, 3, 1, 2))[None]


def kernel(x_prompt, x_sample, cache_k_w128, cache_v_w128, cache_k_w512, cache_v_w512, cache_k_w2048,
           cache_v_w2048, state_retention, c_prompt, c_sample, w_ada, b_ada, w_in, w_att_out, w_ret_out,
           w_o, ln1_g, ln1_b, w_ffn_in, w_ffn_out, ln2_g, ln2_b):
    assert w_in.shape[0] == 1 and x_sample.shape[1] == 1
    B, S, D = x_prompt.shape
    Bs = x_sample.shape[0]
    dils = tuple(d for _, d in DILATED_PATTERNS)

    w_main = jnp.concatenate([w_in[0][:, W_IN_ROTATE:], w_in[0][:, 2 * ATTN_WIDTH:W_IN_ROTATE]], axis=1).astype(BF16)
    w_qk = w_in[0][:, :2 * ATTN_WIDTH].astype(BF16)
    w_qk_paired = _pair_columns(w_qk)
    weights = tuple(w[0].astype(BF16) for w in (w_att_out, w_ret_out, w_o, w_ffn_in, w_ffn_out))
    ln = (ln1_g, ln1_b, ln2_g, ln2_b)

    mod = _modulation(jnp.concatenate([c_prompt, c_sample], axis=0), w_ada[0], b_ada[0])
    mod_p = mod[:B].reshape(B, 1, 6 * D)
    mod_s = mod[B:].reshape(1, Bs, 6 * D)

    tabs_p = _rope_tables(np.arange(S), paired=True)
    win_rows = tuple(min(w, S) for w, _ in DILATED_PATTERNS)
    dense_p, *rest = _in_projection(x_prompt, mod_p, w_main, w_qk_paired, tabs_p, tm=256, dils=dils, paired=True,
                                    keys_t=True, win_rows=win_rows, out_dtype=BF16)
    qkv_p, keys_t_p, wins_p = rest[:N_GROUPS], rest[N_GROUPS], rest[N_GROUPS + 1:]
    o_p, lse_p = [], []
    for g, d in enumerate(dils):
        tl = min(S // d, 1024)
        o, lse = _prompt_attention(qkv_p[g], g, tl=tl, nc=min(d, 1024 // tl))
        o_p.append(o)
        lse_p.append(lse)
    ret_p, state_p = _prompt_retention(dense_p, keys_t_p, tr=1024)

    xs = x_sample.reshape(1, Bs, D)
    tabs_s = _rope_tables(PAST_LEN + np.arange(1))
    unit = (1,) * N_GROUPS
    dense_s, *qkv_s = _in_projection(xs, mod_s, w_main, w_qk, tabs_s, tm=Bs, dils=unit, paired=False,
                                     keys_t=False, win_rows=None, out_dtype=F32)
    caches = [_feature_major(c) for c in (cache_k_w128, cache_v_w128, cache_k_w512, cache_v_w512,
                                          cache_k_w2048, cache_v_w2048)]
    ret_s, state_s = _sample_retention(dense_s.reshape(Bs, 1, DENSE_WIDTH), state_retention[0])
    y_p, o_s, lse_s, rolled = _merge_ffn(
        x_prompt, dense_p, ret_p, o_p, lse_p, mod_p, weights, ln, tm=512, sub=256, dils=dils,
        side=([a.reshape(Bs, 1, ATTN_WIDTH) for a in qkv_s], caches))

    def by_group(a):
        return [a[:, :, 0, g * LANES:(g + 1) * LANES].reshape(1, 1, Bs, GROUP_WIDTH) for g in range(N_GROUPS)]

    o_sg, lse_sg = by_group(o_s), by_group(lse_s)
    y_s = _merge_ffn(xs, dense_s, ret_s.reshape(1, Bs, RET_V_WIDTH), o_sg, lse_sg, mod_s, weights, ln,
                     tm=Bs, sub=Bs, dils=unit)

    return (y_p, y_s.reshape(Bs, 1, D),
            *[_row_major(w) for w in wins_p], state_p[None],
            *[_row_major(r) for r in rolled], state_s[None])
```

```python
import functools
import math

import jax
import jax.numpy as jnp
import numpy as np
from jax import lax
from jax.experimental import pallas as pl
from jax.experimental.pallas import tpu as pltpu

F32 = jnp.float32
BF16 = jnp.bfloat16

D_MODEL = 1024
PAST_LEN = 16384
DILATED_PATTERNS = ((128, 1), (512, 4), (2048, 16))
N_GROUPS = 3
HEADS_PER_GROUP = 4
HEAD_DIM = 64
GROUP_WIDTH = HEADS_PER_GROUP * HEAD_DIM
ATTN_WIDTH = N_GROUPS * GROUP_WIDTH
ATTN_BAND = 128
ATTN_SCALE = HEAD_DIM ** -0.5
ROPE_DIM = HEAD_DIM // 4
ROPE_THETA = 500000.0
RET_HEADS = 4
RET_QK_DIM = 128
RET_V_DIM = 256
RET_QK_WIDTH = RET_HEADS * RET_QK_DIM
RET_V_WIDTH = RET_HEADS * RET_V_DIM
RET_CHUNK = 128
RET_ROPE_THETA = 10000.0
IN_WIDTH = 3 * ATTN_WIDTH + 2 * RET_QK_WIDTH + 2 * RET_V_WIDTH + 2 * D_MODEL
D_FF = 2816
DEEPNORM_ALPHA = 2.0 ** 0.25
LN_EPS = 1e-5
GN_EPS = 1e-6

LANES = 128
SEG = 256
N_SEG = IN_WIDTH // SEG
FFN_CHUNK = 256
FFN_SPLIT = 1536
SCORE_LOOKAHEAD = 2
OFF_RQ, OFF_RK, OFF_RV, OFF_RG, OFF_GA, OFF_GB = 0, 512, 1024, 2048, 3072, 4096
OFF_QA, OFF_KA, OFF_VA = 5120, 5888, 6656
DENSE_WIDTH = OFF_QA
W_IN_ROTATE = 3 * ATTN_WIDTH
NEG_BIG = -1e30
VMEM_LIMIT = 60 * 1024 * 1024


def _cparams(sem):
    return pltpu.CompilerParams(dimension_semantics=sem, vmem_limit_bytes=VMEM_LIMIT)


def _resident(shape):
    nd = len(shape)
    return pl.BlockSpec(shape, lambda *_: (0,) * nd, pipeline_mode=pl.Buffered(1))


def _sigmoid(x):
    return 0.5 * jnp.tanh(0.5 * x) + 0.5


def _ret_gamma(h):
    return 1.0 - 2.0 ** (-5.0 - h)


def _ada_kernel(c_ref, w_ref, b_ref, o_ref):
    c = c_ref[...]
    a = (c * _sigmoid(c)).astype(BF16)
    o_ref[...] = jnp.dot(a, w_ref[...].astype(BF16), preferred_element_type=F32) + b_ref[...]


def _modulation(c_all, w_ada, b_ada):
    rows = c_all.shape[0]
    tn = 1024
    return pl.pallas_call(
        _ada_kernel,
        grid=(6 * D_MODEL // tn,),
        in_specs=[pl.BlockSpec((rows, D_MODEL), lambda j: (0, 0)),
                  pl.BlockSpec((D_MODEL, tn), lambda j: (0, j)),
                  pl.BlockSpec((1, tn), lambda j: (0, j))],
        out_specs=pl.BlockSpec((rows, tn), lambda j: (0, j)),
        out_shape=jax.ShapeDtypeStruct((rows, 6 * D_MODEL), F32),
        compiler_params=_cparams(("arbitrary",)),
        name="modulation",
    )(c_all, w_ada, b_ada.reshape(1, -1))


PAIR_SLOT = HEAD_DIM // 2


def _pair_source():
    src = []
    for half_dims in ((0, 16), (8, 40)):
        for h in range(HEADS_PER_GROUP):
            rot, rest = half_dims
            dims = list(range(rot, rot + ROPE_DIM // 2)) + list(range(rest, rest + PAIR_SLOT - ROPE_DIM // 2))
            src += [h * HEAD_DIM + d for d in dims]
    return src


def _pair_columns(w):
    run = ROPE_DIM // 2
    w5 = w.reshape(w.shape[0], -1, HEADS_PER_GROUP, HEAD_DIM // run, run)
    half_a = jnp.concatenate([w5[:, :, :, 0:1], w5[:, :, :, 2:5]], axis=3)
    half_b = jnp.concatenate([w5[:, :, :, 1:2], w5[:, :, :, 5:8]], axis=3)
    return jnp.stack([half_a, half_b], axis=2).reshape(w.shape)


def _rope_tables(pos, paired=False):
    posf = jnp.asarray(np.asarray(pos), jnp.int32).astype(F32)[:, None]
    lane = np.arange(LANES)
    half = ROPE_DIM // 2
    inv = jnp.exp(-math.log(ROPE_THETA) * jnp.arange(half, dtype=F32) * (2.0 / ROPE_DIM))
    ang = posf * inv[None, :]
    cos, sin = jnp.cos(ang), jnp.sin(ang)
    if paired:
        in_slot = lane % PAIR_SLOT
        f = in_slot % half
        att_c = jnp.where(in_slot < half, cos[:, f], 1.0)
        att_lo = jnp.where(in_slot < half, sin[:, f], 0.0)
        att_hi = att_lo
    else:
        in_head = lane % HEAD_DIM
        f = in_head % half
        att_c = jnp.where(in_head < ROPE_DIM, cos[:, f], 1.0)
        att_lo = jnp.where(in_head < half, -sin[:, f], 0.0)
        att_hi = jnp.where((in_head >= half) & (in_head < ROPE_DIM), sin[:, f], 0.0)
    rhalf = RET_QK_DIM // 2
    rinv = jnp.exp(-math.log(RET_ROPE_THETA) * jnp.arange(rhalf, dtype=F32) * (2.0 / RET_QK_DIM))
    rang = posf * rinv[None, :]
    rcos, rsin = jnp.cos(rang), jnp.sin(rang)
    ret_c = jnp.concatenate([rcos, rcos], axis=-1)
    ret_s = jnp.concatenate([-rsin, rsin], axis=-1)
    return att_c, att_lo, att_hi, ret_c, ret_s


def _inproj_kernel(x_ref, shift_ref, scale_ref, w_ref, wqk_ref, *refs, tm, dils, paired, keys_t, win_blocks,
                   win_first):
    n_tab = 5
    att_tab = [r[...] for r in refs[:3]]
    rc, rs = refs[3][...], refs[4][...]
    refs = refs[n_tab:]
    n_perm = sum(d > 1 for d in dils)
    dense_ref = refs[0]
    qkv_refs = refs[1:1 + N_GROUPS]
    n_out = len(refs) - (1 + n_perm if n_perm else 0)
    kt_ref = refs[1 + N_GROUPS] if keys_t else None
    win_refs = refs[1 + N_GROUPS + bool(keys_t):n_out]
    i = pl.program_id(1)
    hf = x_ref[0] * (1.0 + scale_ref[0]) + shift_ref[0]
    h = hf.astype(BF16)

    h_class = {}
    att_tabs = [att_tab] * N_GROUPS
    if n_perm:
        stage_ref = refs[n_out]
        n_x = D_MODEL // LANES
        for u in range(n_x):
            stage_ref[u] = hf[:, u * LANES:(u + 1) * LANES]
        moving = [t for t in att_tab if t.shape[0] == tm]
        for u, t in enumerate(moving):
            stage_ref[n_x + u] = t
        for g, hp_ref in zip([g for g in range(N_GROUPS) if dils[g] > 1], refs[n_out + 1:]):
            d = dils[g]
            n = tm // d
            for r in range(d):
                for u in range(n_x):
                    hp_ref[r * n:(r + 1) * n, u * LANES:(u + 1) * LANES] = (
                        stage_ref[u, pl.ds(r, n, stride=d), :].astype(BF16))
            h_class[g] = hp_ref
            regrouped = iter([jnp.concatenate([stage_ref[n_x + u, pl.ds(r, n, stride=d), :] for r in range(d)], axis=0)
                              for u in range(len(moving))])
            att_tabs[g] = [next(regrouped) if t.shape[0] == tm else t for t in att_tab]

    def att_rope(a, tab):
        ac, alo, ahi = tab
        if paired:
            lo, hi = a[:, :LANES], a[:, LANES:]
            return jnp.concatenate([lo * ac - hi * alo, hi * ac + lo * alo], axis=1)
        parts = []
        for u in range(SEG // LANES):
            xs = a[:, u * LANES:(u + 1) * LANES]
            parts.append(xs * ac + pltpu.roll(xs, LANES - ROPE_DIM // 2, 1) * alo
                         + pltpu.roll(xs, ROPE_DIM // 2, 1) * ahi)
        return jnp.concatenate(parts, axis=1)

    def ret_rope(a):
        parts = []
        for u in range(SEG // LANES):
            xs = a[:, u * LANES:(u + 1) * LANES]
            parts.append(xs * rc + pltpu.roll(xs, RET_QK_DIM // 2, 1) * rs)
        return jnp.concatenate(parts, axis=1)

    def emit_group(g, c, a):
        ref, d = qkv_refs[g], dils[g]
        n = tm // d
        for r in range(d):
            ref[0, r, :, c * SEG:(c + 1) * SEG] = a[r * n:(r + 1) * n, :].astype(ref.dtype)

    def wk_cols(g):
        return wqk_ref[:, ATTN_WIDTH + g * SEG:ATTN_WIDTH + (g + 1) * SEG]

    def wv_cols(g):
        return w_ref[:, DENSE_WIDTH + g * SEG:DENSE_WIDTH + (g + 1) * SEG]

    kv_f32 = {}

    def position_order(a, g, slot):
        d = dils[g]
        if d == 1:
            return a
        n = tm // d
        for u in range(SEG // LANES):
            for r in range(d):
                stage_ref[slot + u, pl.ds(r, n, stride=d), :] = a[r * n:(r + 1) * n, u * LANES:(u + 1) * LANES]
        return jnp.concatenate([stage_ref[slot + u] for u in range(SEG // LANES)], axis=1)

    def emit_windows(g):
        wb = win_blocks[g]
        assert dils[g] == 1 or wb == tm
        k, v = kv_f32[g]
        slot = 2 * (SEG // LANES) * (g % 2)
        at = position_order(k, g, slot)[tm - wb:, :].T
        if paired:
            dst = {f: lane for lane, f in enumerate(_pair_source())}
            at = jnp.concatenate([at[dst[f]:dst[f] + 8, :] for f in range(0, GROUP_WIDTH, 8)], axis=0)
        win_refs[2 * g][0] = at
        win_refs[2 * g + 1][0] = position_order(v, g, slot + SEG // LANES)[tm - wb:, :].T

    for seg in range(DENSE_WIDTH // SEG):
        c0 = seg * SEG
        a = jnp.dot(h, w_ref[:, c0:c0 + SEG], preferred_element_type=F32)
        if c0 < OFF_RK:
            a = ret_rope(a)
        elif c0 < OFF_RV:
            a = ret_rope(a) * (RET_QK_DIM ** -0.5)
            if keys_t:
                kt_ref[0, c0 - OFF_RK:c0 - OFF_RK + SEG, :] = a.T.astype(kt_ref.dtype)
        elif c0 < OFF_RG:
            pass
        elif c0 < OFF_GA:
            a = a * _sigmoid(a)
        else:
            a = _sigmoid(a)
        dense_ref[0, :, c0:c0 + SEG] = a.astype(dense_ref.dtype)

    for g in range(N_GROUPS):
        hg = h_class[g][...] if g in h_class else h
        wq = wqk_ref[:, g * SEG:(g + 1) * SEG]
        emit_group(g, 0, att_rope(jnp.dot(hg, wq, preferred_element_type=F32), att_tabs[g]) * ATTN_SCALE)
        k = att_rope(jnp.dot(hg, wk_cols(g), preferred_element_type=F32), att_tabs[g])
        v = jnp.dot(hg, wv_cols(g), preferred_element_type=F32)
        emit_group(g, 1, k)
        emit_group(g, 2, v)
        kv_f32[g] = (k, v)

    if win_refs:
        order = sorted(range(N_GROUPS), key=lambda g: win_first[g])

        def guarded(pos):
            if pos == len(order):
                return
            g = order[pos]

            @pl.when(i >= win_first[g])
            def _():
                emit_windows(g)
                guarded(pos + 1)

        guarded(0)


def _in_projection(x, mod, w_main, w_qk, tables, *, tm, dils, paired, keys_t, win_rows, out_dtype):
    G, R, _ = x.shape
    n_tiles = R // tm
    mod_blk = tm if mod.shape[1] == R else 1
    tab_blk = tm if tables[0].shape[0] == R else 1

    def mod_spec(col):
        if mod_blk == 1:
            return pl.BlockSpec((1, 1, D_MODEL), lambda b, i: (b, 0, col))
        return pl.BlockSpec((1, tm, D_MODEL), lambda b, i: (b, i, col))

    def tab_spec(tab):
        if tab.shape[0] == 1:
            return pl.BlockSpec((1, LANES), lambda b, i: (0, 0))
        return pl.BlockSpec((tm, LANES), lambda b, i: (i, 0))

    att = list(tables[:3])
    if paired:
        att[2] = np.zeros((1, LANES), np.float32)
    all_tabs = att + list(tables[3:])

    out_shapes = [jax.ShapeDtypeStruct((G, R, DENSE_WIDTH), out_dtype)]
    out_specs = [pl.BlockSpec((1, tm, DENSE_WIDTH), lambda b, i: (b, i, 0))]
    for d in dils:
        out_shapes.append(jax.ShapeDtypeStruct((G, d, R // d, ATTN_WIDTH), out_dtype))
        out_specs.append(pl.BlockSpec((1, d, tm // d, ATTN_WIDTH), lambda b, i: (b, 0, i, 0)))
    if keys_t:
        out_shapes.append(jax.ShapeDtypeStruct((G, RET_QK_WIDTH, R), out_dtype))
        out_specs.append(pl.BlockSpec((1, RET_QK_WIDTH, tm), lambda b, i: (b, 0, i)))
    win_blocks = win_first = ()
    if win_rows:
        win_blocks = tuple(min(w, tm) for w in win_rows)
        win_first = tuple(n_tiles - w // wb for w, wb in zip(win_rows, win_blocks))
        for g in range(N_GROUPS):
            first = win_first[g]
            for _ in range(2):
                out_shapes.append(jax.ShapeDtypeStruct((G, GROUP_WIDTH, win_rows[g]), F32))
                out_specs.append(pl.BlockSpec((1, GROUP_WIDTH, win_blocks[g]),
                                              lambda b, i, first=first: (b, 0, jnp.maximum(i - first, 0))))

    scratch = []
    if any(d > 1 for d in dils):
        scratch.append(pltpu.VMEM((D_MODEL // LANES + 2, tm, LANES), F32))
        scratch += [pltpu.VMEM((tm, D_MODEL), BF16) for d in dils if d > 1]
    kern = functools.partial(_inproj_kernel, tm=tm, dils=dils, paired=paired, keys_t=keys_t,
                             win_blocks=win_blocks, win_first=win_first)
    return pl.pallas_call(
        kern,
        grid=(G, n_tiles),
        in_specs=[pl.BlockSpec((1, tm, D_MODEL), lambda b, i: (b, i, 0)),
                  mod_spec(0), mod_spec(1),
                  _resident(w_main.shape), _resident(w_qk.shape)] + [tab_spec(t) for t in all_tabs],
        out_specs=out_specs,
        out_shape=out_shapes,
        scratch_shapes=scratch,
        compiler_params=_cparams(("arbitrary", "arbitrary")),
        name="in_projection",
    )(x, mod, mod, w_main, w_qk, *all_tabs)


def _attn_kernel(q_ref, k_ref, v_ref, kp_ref, vp_ref, o_ref, lse_ref, *, tl, nc):
    t = pl.program_id(2)
    row = lax.broadcasted_iota(jnp.int32, (ATTN_BAND, ATTN_BAND), 0)
    col = lax.broadcasted_iota(jnp.int32, (ATTN_BAND, ATTN_BAND), 1)
    lane = lax.broadcasted_iota(jnp.int32, (1, GROUP_WIDTH), 1)
    cur_ok = col <= row
    qk_lanes = [(lane % LANES) // PAIR_SLOT == h for h in range(HEADS_PER_GROUP)]
    v_lanes = [lane // HEAD_DIM == h for h in range(HEADS_PER_GROUP)]

    def scores(q, k_prev, k_cur):
        kk = jnp.concatenate([k_prev, k_cur], axis=0)
        qs = jnp.concatenate([jnp.where(sel, q, jnp.zeros_like(q)) for sel in qk_lanes], axis=0)
        return lax.dot_general(qs, kk, (((1,), (1,)), ((), ())), preferred_element_type=F32)

    def finish(s, v_prev, v_cur, prev_shift):
        prev_ok = col >= row + prev_shift
        mask = jnp.concatenate([prev_ok, cur_ok], axis=1)
        mask = jnp.concatenate([mask] * HEADS_PER_GROUP, axis=0)
        vv = jnp.concatenate([v_prev, v_cur], axis=0)
        s = jnp.where(mask, s, NEG_BIG)
        m = jnp.max(s, axis=1, keepdims=True)
        e = jnp.exp(s - m)
        l = jnp.sum(e, axis=1, keepdims=True)
        oh = jnp.dot((e / l).astype(BF16), vv, preferred_element_type=F32)
        lse = m + jnp.log(l)
        o = jnp.zeros((ATTN_BAND, GROUP_WIDTH), F32)
        lo = jnp.zeros((ATTN_BAND, GROUP_WIDTH), F32)
        for h, sel in enumerate(v_lanes):
            rows = slice(h * ATTN_BAND, (h + 1) * ATTN_BAND)
            o = jnp.where(sel, oh[rows], o)
            lo = jnp.where(sel, lse[rows], lo)
        return o, lo

    first = jnp.where(t > 0, 0, ATTN_BAND)
    blocks = [(c, j) for c in range(nc) for j in range(tl // ATTN_BAND)]

    def prev_rows(ref, prev_ref, c, j):
        return prev_ref[0, c] if j == 0 else ref[0, c, (j - 1) * ATTN_BAND:j * ATTN_BAND, :]

    def cur_rows(ref, c, j):
        return ref[0, c, j * ATTN_BAND:(j + 1) * ATTN_BAND, :]

    def block_scores(c, j):
        return scores(cur_rows(q_ref, c, j), prev_rows(k_ref, kp_ref, c, j), cur_rows(k_ref, c, j))

    ahead = [block_scores(*blk) for blk in blocks[:SCORE_LOOKAHEAD]]
    for n, (c, j) in enumerate(blocks):
        if n + SCORE_LOOKAHEAD < len(blocks):
            ahead.append(block_scores(*blocks[n + SCORE_LOOKAHEAD]))
        o, lse = finish(ahead[n], prev_rows(v_ref, vp_ref, c, j), cur_rows(v_ref, c, j), first if j == 0 else 0)
        rows = slice(j * ATTN_BAND, (j + 1) * ATTN_BAND)
        o_ref[0, c, rows, :] = o.astype(o_ref.dtype)
        lse_ref[0, c, rows, :] = lse


def _prompt_attention(qkv, g, *, tl, nc):
    B, dil, L, _ = qkv.shape
    sub = tl // ATTN_BAND

    def cur(c):
        return pl.BlockSpec((1, nc, tl, SEG), lambda b, r, t: (b, r, t, c))

    def prev(c):
        return pl.BlockSpec((1, nc, ATTN_BAND, SEG), lambda b, r, t: (b, r, jnp.maximum(t * sub - 1, 0), c))

    out_spec = pl.BlockSpec((1, nc, tl, SEG), lambda b, r, t: (b, r, t, 0))
    return pl.pallas_call(
        functools.partial(_attn_kernel, tl=tl, nc=nc),
        grid=(B, dil // nc, L // tl),
        in_specs=[cur(0), cur(1), cur(2), prev(1), prev(2)],
        out_specs=[out_spec, out_spec],
        out_shape=[jax.ShapeDtypeStruct((B, dil, L, SEG), BF16),
                   jax.ShapeDtypeStruct((B, dil, L, SEG), F32)],
        compiler_params=_cparams(("arbitrary", "arbitrary", "arbitrary")),
        name="prompt_attention_g%d" % g,
    )(qkv, qkv, qkv, qkv, qkv)


def _retention_tables():
    f32 = np.float32
    C = RET_CHUNK
    lg = np.log(f32(1.0) - f32(2.0) ** (f32(-5.0) - np.arange(RET_HEADS, dtype=f32)))
    idx = np.arange(C, dtype=f32)
    diff = idx[:, None] - idx[None, :]
    decay_in = np.where(diff >= 0, np.exp(lg[:, None, None] * np.maximum(diff, f32(0.0))), f32(0.0))
    decay_q = np.exp(lg[:, None] * (idx[None, :] + f32(1.0)))[:, :, None]
    decay_k = np.exp(lg[:, None] * (f32(C - 1.0) - idx[None, :]))[:, None, :]
    return tuple(t.astype(f32) for t in (decay_in, decay_q, decay_k))


def _retention_kernel(qk_ref, kt_ref, v_ref, din_ref, dq_ref, dk_ref, o_ref, state_ref, r_scr, *, tr):
    t = pl.program_id(1)
    n_chunks = tr // RET_CHUNK

    @pl.when(t == 0)
    def _():
        r_scr[...] = jnp.zeros_like(r_scr)

    def key_t(c, h):
        return kt_ref[0, h * RET_QK_DIM:(h + 1) * RET_QK_DIM, c * RET_CHUNK:(c + 1) * RET_CHUNK]

    def val(c, h):
        return v_ref[0, c * RET_CHUNK:(c + 1) * RET_CHUNK, h * RET_V_DIM:(h + 1) * RET_V_DIM]

    states = [[None] * RET_HEADS for _ in range(n_chunks)]
    for h in range(RET_HEADS):
        r = r_scr[h]
        for c in range(n_chunks):
            states[c][h] = r
            kd = (key_t(c, h).astype(F32) * dk_ref[h]).astype(BF16)
            r = r * (_ret_gamma(h) ** RET_CHUNK) + jnp.dot(kd, val(c, h), preferred_element_type=F32)
        r_scr[h] = r

    def query(c, h):
        return qk_ref[0, c * RET_CHUNK:(c + 1) * RET_CHUNK, OFF_RQ + h * RET_QK_DIM:OFF_RQ + (h + 1) * RET_QK_DIM]

    scores = [[(jnp.dot(query(c, h), key_t(c, h), preferred_element_type=F32) * din_ref[h]).astype(BF16)
               for h in range(RET_HEADS)] for c in range(n_chunks)]

    for c in range(n_chunks):
        rows = slice(c * RET_CHUNK, (c + 1) * RET_CHUNK)
        for h in range(RET_HEADS):
            q = query(c, h)
            o = (jnp.dot(scores[c][h], val(c, h), preferred_element_type=F32)
                 + jnp.dot(q, states[c][h].astype(BF16), preferred_element_type=F32) * dq_ref[h])
            mu = jnp.mean(o, axis=1, keepdims=True)
            d = o - mu
            var = jnp.mean(d * d, axis=1, keepdims=True)
            o_ref[0, rows, h * RET_V_DIM:(h + 1) * RET_V_DIM] = (d * lax.rsqrt(var + GN_EPS)).astype(o_ref.dtype)

    @pl.when(t == pl.num_programs(1) - 1)
    def _():
        state_ref[0] = r_scr[...]


def _prompt_retention(dense, keys_t, *, tr):
    B, S, _ = dense.shape
    tabs = _retention_tables()
    blk = RET_V_WIDTH

    def colblk(c):
        return pl.BlockSpec((1, tr, blk), lambda b, t: (b, t, c))

    return pl.pallas_call(
        functools.partial(_retention_kernel, tr=tr),
        grid=(B, S // tr),
        in_specs=[pl.BlockSpec((1, tr, RET_QK_WIDTH), lambda b, t: (b, t, OFF_RQ // RET_QK_WIDTH)),
                  pl.BlockSpec((1, RET_QK_WIDTH, tr), lambda b, t: (b, 0, t)),
                  colblk(OFF_RV // blk)]
                 + [pl.BlockSpec(tab.shape, lambda b, t: (0, 0, 0)) for tab in tabs],
        out_specs=[pl.BlockSpec((1, tr, RET_V_WIDTH), lambda b, t: (b, t, 0)),
                   pl.BlockSpec((1, RET_HEADS, RET_QK_DIM, RET_V_DIM), lambda b, t: (b, 0, 0, 0))],
        out_shape=[jax.ShapeDtypeStruct((B, S, RET_V_WIDTH), BF16),
                   jax.ShapeDtypeStruct((B, RET_HEADS, RET_QK_DIM, RET_V_DIM), F32)],
        scratch_shapes=[pltpu.VMEM((RET_HEADS, RET_QK_DIM, RET_V_DIM), F32)],
        compiler_params=_cparams(("arbitrary", "arbitrary")),
        name="prompt_retention",
    )(dense, keys_t, dense, *tabs)


def _layer_norm(u, g, b):
    mu = jnp.mean(u, axis=1, keepdims=True)
    d = u - mu
    var = jnp.mean(d * d, axis=1, keepdims=True)
    return d * lax.rsqrt(var + LN_EPS) * g + b


def _column(row):
    return jnp.broadcast_to(row, (LANES, LANES)).T


def _window_half_step(first_half, qkv_refs, cache_refs, o_ref, lse_ref, new_refs):
    pad = 16
    row_h = lax.broadcasted_iota(jnp.int32, (pad, LANES), 0)
    lane_h = lax.broadcasted_iota(jnp.int32, (pad, LANES), 1) // HEAD_DIM
    own = row_h == lane_h
    lane = lax.broadcasted_iota(jnp.int32, (LANES, LANES), 1)

    def pick(row):
        return jnp.where(first_half, row[:, :LANES], row[:, LANES:])

    def rolled(c_ref, n_ref, new_row):
        w = c_ref.shape[3]
        new_col = _column(new_row)
        nblk = w // LANES
        cur = pltpu.roll(c_ref[0, 0, :, 0:LANES], LANES - 1, 1)
        for j in range(nblk):
            nxt = pltpu.roll(c_ref[0, 0, :, (j + 1) * LANES:(j + 2) * LANES], LANES - 1, 1) if j + 1 < nblk else new_col
            n_ref[0, 0, :, j * LANES:(j + 1) * LANES] = jnp.where(lane < LANES - 1, cur, nxt)
            cur = nxt

    started = []
    for g in range(N_GROUPS):
        q = pick(qkv_refs[g][0, :, 0:SEG])
        qsel = jnp.where(own, jnp.broadcast_to(q, (pad, LANES)), 0.0)
        s = jnp.dot(qsel.astype(BF16), cache_refs[2 * g][0, 0].astype(BF16), preferred_element_type=F32)
        started.append((qsel, s))

    def finish():
        for g, (qsel, s) in enumerate(started):
            dil = DILATED_PATTERNS[g][1]
            ck_ref, cv_ref = cache_refs[2 * g:2 * g + 2]
            kn = pick(qkv_refs[g][0, :, SEG:2 * SEG])
            vn = pick(qkv_refs[g][0, :, 2 * SEG:3 * SEG])
            pos = lax.broadcasted_iota(jnp.int32, s.shape, 1)
            s = jnp.where((pos & (dil - 1)) == 0, s, NEG_BIG)
            sn = jnp.sum(qsel * kn, axis=1, keepdims=True)
            m = jnp.maximum(jnp.max(s, axis=1, keepdims=True), sn)
            e = jnp.exp(s - m)
            en = jnp.exp(sn - m)
            l = jnp.sum(e, axis=1, keepdims=True) + en
            o_all = lax.dot_general((e / l).astype(BF16), cv_ref[0, 0].astype(BF16), (((1,), (1,)), ((), ())),
                                    preferred_element_type=F32) + (en / l) * vn
            o_ref[0, 0, :, g * LANES:(g + 1) * LANES] = jnp.sum(jnp.where(own, o_all, 0.0), axis=0, keepdims=True)
            lse_ref[0, 0, :, g * LANES:(g + 1) * LANES] = jnp.sum(jnp.where(own, m + jnp.log(l), 0.0), axis=0,
                                                                  keepdims=True)
            rolled(ck_ref, new_refs[2 * g], kn)
            rolled(cv_ref, new_refs[2 * g + 1], vn)

    return finish


N_MERGE_INPUTS = 24
N_SIDE_INPUTS = N_GROUPS + 2 * N_GROUPS
N_SIDE_OUTPUTS = 2 + 2 * N_GROUPS


def _merge_ffn_kernel(*refs, tm, sub, dils, side, steps_per_batch):
    (x_ref, ga_ref, gb_ref, rg_ref, ret_ref, o0_ref, o1_ref, o2_ref, l0_ref, l1_ref, l2_ref,
     gate1_ref, shift2_ref, scale2_ref, gate2_ref, watt_ref, wret_ref, wo_ref, wfi_ref, wfo_ref,
     ln1g_ref, ln1b_ref, ln2g_ref, ln2b_ref) = refs[:N_MERGE_INPUTS]
    n_in = N_MERGE_INPUTS + (N_SIDE_INPUTS if side else 0)
    y_ref = refs[n_in]
    n_out = 1 + (N_SIDE_OUTPUTS if side else 0)
    scratch = refs[n_in + n_out:]
    if side:
        side_in = refs[N_MERGE_INPUTS:n_in]
        side_out = refs[n_in + 1:n_in + n_out]
        step = pl.program_id(0) * steps_per_batch + pl.program_id(1)
        finish_side = _window_half_step(step % 2 == 0, side_in[:N_GROUPS], side_in[N_GROUPS:], side_out[0],
                                        side_out[1], side_out[2:])
    n_sub = tm // sub
    per_sub = 1 + 2 * N_GROUPS
    def rows(s):
        return slice(s * sub, (s + 1) * sub)

    def mod_rows(ref, s):
        return ref[0] if ref.shape[1] == 1 else ref[0, rows(s), :]

    def natural(ref, g, s, which):
        d = dils[g]
        n = sub // d
        if d == 1:
            return ref[0, 0, rows(s), :].astype(F32)
        stage_ref = scratch[s * per_sub + 1 + which * N_GROUPS + g]
        for r in range(d):
            for u in range(GROUP_WIDTH // LANES):
                stage_ref[u, pl.ds(r, n, stride=d), :] = (
                    ref[0, r, s * n:(s + 1) * n, u * LANES:(u + 1) * LANES].astype(F32))
        return jnp.concatenate([stage_ref[u] for u in range(GROUP_WIDTH // LANES)], axis=1)

    def branch_dots(s):
        gated = rg_ref[0, rows(s), :].astype(F32) * ret_ref[0, rows(s), :].astype(F32)
        y_ret = jnp.dot(gated.astype(BF16), wret_ref[...], preferred_element_type=F32)
        lses = [natural(ref, g, s, 0) for g, ref in enumerate((l0_ref, l1_ref, l2_ref))]
        outs = [natural(ref, g, s, 1) for g, ref in enumerate((o0_ref, o1_ref, o2_ref))]
        m = jnp.maximum(jnp.maximum(lses[0], lses[1]), lses[2])
        es = [jnp.exp(l - m) for l in lses]
        att = (es[0] * outs[0] + es[1] * outs[1] + es[2] * outs[2]) / (es[0] + es[1] + es[2])
        y_att = jnp.dot(att.astype(BF16), watt_ref[...], preferred_element_type=F32)
        return y_att, y_ret

    def out_projection(s, y_att, y_ret):
        mix = (ga_ref[0, rows(s), :].astype(F32) * y_att + gb_ref[0, rows(s), :].astype(F32) * y_ret).astype(BF16)
        return jnp.dot(mix, wo_ref[...], preferred_element_type=F32)

    def first_norm(s, mixed):
        x1 = _layer_norm(DEEPNORM_ALPHA * x_ref[0, rows(s), :] + mod_rows(gate1_ref, s) * mixed,
                         ln1g_ref[...], ln1b_ref[...])
        return x1, (x1 * (1.0 + mod_rows(scale2_ref, s)) + mod_rows(shift2_ref, s)).astype(BF16)

    def swiglu(s, h2):
        act_scr = scratch[s * per_sub]
        for c0 in range(0, D_FF, FFN_CHUNK):
            c1 = min(c0 + FFN_CHUNK, D_FF)
            fg = jnp.dot(h2, wfi_ref[:, c0:c1], preferred_element_type=F32)
            fu = jnp.dot(h2, wfi_ref[:, D_FF + c0:D_FF + c1], preferred_element_type=F32)
            act_scr[:, c0:c1] = (fg * _sigmoid(fg) * fu).astype(BF16)

    def ffn_out(s, k0, k1):
        return jnp.dot(scratch[s * per_sub][:, k0:k1], wfo_ref[k0:k1, :], preferred_element_type=F32)

    def second_norm(s, x1, ff):
        y_ref[0, rows(s), :] = _layer_norm(DEEPNORM_ALPHA * x1 + mod_rows(gate2_ref, s) * ff,
                                           ln2g_ref[...], ln2b_ref[...])

    mixed = out_projection(0, *branch_dots(0))
    if side:
        finish_side()
    x1, h2 = first_norm(0, mixed)
    for s in range(n_sub):
        swiglu(s, h2)
        if s + 1 < n_sub:
            nxt = branch_dots(s + 1)
            ff = ffn_out(s, 0, FFN_SPLIT)
            mixed = out_projection(s + 1, *nxt)
            ff = ff + ffn_out(s, FFN_SPLIT, D_FF)
            x1_next, h2 = first_norm(s + 1, mixed)
            second_norm(s, x1, ff)
            x1 = x1_next
        else:
            second_norm(s, x1, ffn_out(s, 0, D_FF))


def _merge_ffn(x, dense, ret, o_groups, lse_groups, mod, weights, ln, *, tm, sub, dils, side=None):
    G, R, _ = x.shape
    mod_rows = mod.shape[1]
    n_tiles = R // tm
    side_args, side_in_specs, side_out_specs, side_out_shapes = [], [], [], []
    if side is not None:
        qkvs, caches = side
        n_seq = caches[0].shape[0]
        assert G * n_tiles == 2 * n_seq

        def seq_half(b, i):
            step = b * n_tiles + i
            return step // 2, step % 2

        row_spec = pl.BlockSpec((1, 1, ATTN_WIDTH), lambda b, i: (seq_half(b, i)[0], 0, 0))
        small_spec = pl.BlockSpec((1, 1, 1, N_GROUPS * LANES), lambda b, i: seq_half(b, i) + (0, 0))
        halves = [c.reshape(n_seq, 2, LANES, c.shape[2]) for c in caches]
        half_specs = [pl.BlockSpec((1, 1, LANES, c.shape[3]), lambda b, i: seq_half(b, i) + (0, 0)) for c in halves]
        side_args = list(qkvs) + halves
        side_in_specs = [row_spec] * N_GROUPS + half_specs
        side_out_specs = [small_spec, small_spec] + half_specs
        side_out_shapes = ([jax.ShapeDtypeStruct((n_seq, 2, 1, N_GROUPS * LANES), F32)] * 2
                           + [jax.ShapeDtypeStruct(c.shape, F32) for c in halves])
    sub_scratch = [pltpu.VMEM((sub, D_FF), BF16)] + [pltpu.VMEM((GROUP_WIDTH // LANES, sub, LANES), F32)] * (2 * N_GROUPS)

    def mod_spec(col):
        if mod_rows == 1:
            return pl.BlockSpec((1, 1, D_MODEL), lambda b, i: (b, 0, col))
        return pl.BlockSpec((1, tm, D_MODEL), lambda b, i: (b, i, col))

    def rows(width, col=0):
        return pl.BlockSpec((1, tm, width), lambda b, i: (b, i, col))

    group_specs = [pl.BlockSpec((1, d, tm // d, GROUP_WIDTH), lambda b, i: (b, 0, i, 0)) for d in dils]
    vec = pl.BlockSpec((1, D_MODEL), lambda b, i: (0, 0))
    outs = pl.pallas_call(
        functools.partial(_merge_ffn_kernel, tm=tm, sub=sub, dils=dils, side=side is not None,
                          steps_per_batch=n_tiles),
        grid=(G, n_tiles),
        in_specs=[rows(D_MODEL), rows(D_MODEL, OFF_GA // D_MODEL), rows(D_MODEL, OFF_GB // D_MODEL),
                  rows(D_MODEL, OFF_RG // D_MODEL), rows(RET_V_WIDTH)] + group_specs + group_specs
                 + [mod_spec(2), mod_spec(3), mod_spec(4), mod_spec(5)]
                 + [_resident(w.shape) for w in weights] + [vec] * 4 + side_in_specs,
        out_specs=[rows(D_MODEL)] + side_out_specs,
        out_shape=[jax.ShapeDtypeStruct((G, R, D_MODEL), F32)] + side_out_shapes,
        scratch_shapes=sub_scratch * (tm // sub),
        compiler_params=_cparams(("arbitrary", "arbitrary")),
        name="merge_ffn",
    )(x, dense, dense, dense, ret, *o_groups, *lse_groups, mod, mod, mod, mod, *weights, *ln, *side_args)
    if side is None:
        return outs[0]
    y, o_s, lse_s, *rolled = outs
    return y, o_s, lse_s, [r.reshape(c.shape) for r, c in zip(rolled, side[1])]


def _sample_retention_kernel(dense_ref, state_ref, ret_ref, nstate_ref):
    for b in range(dense_ref.shape[0]):
        for h in range(RET_HEADS):
            gamma = _ret_gamma(h)
            q = dense_ref[b, :, OFF_RQ + h * RET_QK_DIM:OFF_RQ + (h + 1) * RET_QK_DIM]
            k = dense_ref[b, :, OFF_RK + h * RET_QK_DIM:OFF_RK + (h + 1) * RET_QK_DIM]
            v = dense_ref[b, :, OFF_RV + h * RET_V_DIM:OFF_RV + (h + 1) * RET_V_DIM]
            r_old = state_ref[b, h]
            q_col = jnp.concatenate([_column(q)] * (RET_V_DIM // LANES), axis=1)
            k_col = jnp.concatenate([_column(k)] * (RET_V_DIM // LANES), axis=1)
            qk = jnp.sum(q * k, axis=1, keepdims=True)
            o = qk * v + jnp.sum(q_col * r_old, axis=0, keepdims=True) * gamma
            nstate_ref[b, h] = r_old * gamma + k_col * v
            mu = jnp.mean(o, axis=1, keepdims=True)
            d = o - mu
            var = jnp.mean(d * d, axis=1, keepdims=True)
            ret_ref[b, :, h * RET_V_DIM:(h + 1) * RET_V_DIM] = d * lax.rsqrt(var + GN_EPS)


def _sample_retention(dense, state, *, nb=4):
    Bs = dense.shape[0]
    row = lambda w: pl.BlockSpec((nb, 1, w), lambda b: (b, 0, 0))
    st = pl.BlockSpec((nb, RET_HEADS, RET_QK_DIM, RET_V_DIM), lambda b: (b, 0, 0, 0))
    return pl.pallas_call(
        _sample_retention_kernel,
        grid=(Bs // nb,),
        in_specs=[row(DENSE_WIDTH), st],
        out_specs=[row(RET_V_WIDTH), st],
        out_shape=[jax.ShapeDtypeStruct((Bs, 1, RET_V_WIDTH), F32), jax.ShapeDtypeStruct(state.shape, F32)],
        compiler_params=_cparams(("arbitrary",)),
        name="sample_retention",
    )(dense, state)


def _feature_major(a):
    _, b, w, _, _ = a.shape
    return jnp.transpose(a[0], (0, 2, 3, 1)).reshape(b, GROUP_WIDTH, w)


def _row_major(a):
    b, _, w = a.shape
    return jnp.transpose(a.reshape(b, HEADS_PER_GROUP, HEAD_DIM, w), (0, 3, 1, 2))[None]


def kernel(x_prompt, x_sample, cache_k_w128, cache_v_w128, cache_k_w512, cache_v_w512, cache_k_w2048,
           cache_v_w2048, state_retention, c_prompt, c_sample, w_ada, b_ada, w_in, w_att_out, w_ret_out,
           w_o, ln1_g, ln1_b, w_ffn_in, w_ffn_out, ln2_g, ln2_b):
    assert w_in.shape[0] == 1 and x_sample.shape[1] == 1
    B, S, D = x_prompt.shape
    Bs = x_sample.shape[0]
    dils = tuple(d for _, d in DILATED_PATTERNS)

    w_main = jnp.concatenate([w_in[0][:, W_IN_ROTATE:], w_in[0][:, 2 * ATTN_WIDTH:W_IN_ROTATE]], axis=1).astype(BF16)
    w_qk = w_in[0][:, :2 * ATTN_WIDTH].astype(BF16)
    w_qk_paired = _pair_columns(w_qk)
    weights = tuple(w[0].astype(BF16) for w in (w_att_out, w_ret_out, w_o, w_ffn_in, w_ffn_out))
    ln = (ln1_g, ln1_b, ln2_g, ln2_b)

    mod = _modulation(jnp.concatenate([c_prompt, c_sample], axis=0), w_ada[0], b_ada[0])
    mod_p = mod[:B].reshape(B, 1, 6 * D)
    mod_s = mod[B:].reshape(1, Bs, 6 * D)

    tabs_p = _rope_tables(np.arange(S), paired=True)
    win_rows = tuple(min(w, S) for w, _ in DILATED_PATTERNS)
    dense_p, *rest = _in_projection(x_prompt, mod_p, w_main, w_qk_paired, tabs_p, tm=256, dils=dils, paired=True,
                                    keys_t=True, win_rows=win_rows, out_dtype=BF16)
    qkv_p, keys_t_p, wins_p = rest[:N_GROUPS], rest[N_GROUPS], rest[N_GROUPS + 1:]
    o_p, lse_p = [], []
    for g, d in enumerate(dils):
        tl = min(S // d, 1024)
        o, lse = _prompt_attention(qkv_p[g], g, tl=tl, nc=min(d, 1024 // tl))
        o_p.append(o)
        lse_p.append(lse)
    ret_p, state_p = _prompt_retention(dense_p, keys_t_p, tr=1024)

    xs = x_sample.reshape(1, Bs, D)
    tabs_s = _rope_tables(PAST_LEN + np.arange(1))
    unit = (1,) * N_GROUPS
    dense_s, *qkv_s = _in_projection(xs, mod_s, w_main, w_qk, tabs_s, tm=Bs, dils=unit, paired=False,
                                     keys_t=False, win_rows=None, out_dtype=F32)
    caches = [_feature_major(c) for c in (cache_k_w128, cache_v_w128, cache_k_w512, cache_v_w512,
                                          cache_k_w2048, cache_v_w2048)]
    ret_s, state_s = _sample_retention(dense_s.reshape(Bs, 1, DENSE_WIDTH), state_retention[0])
    y_p, o_s, lse_s, rolled = _merge_ffn(
        x_prompt, dense_p, ret_p, o_p, lse_p, mod_p, weights, ln, tm=512, sub=256, dils=dils,
        side=([a.reshape(Bs, 1, ATTN_WIDTH) for a in qkv_s], caches))

    def by_group(a):
        return [a[:, :, 0, g * LANES:(g + 1) * LANES].reshape(1, 1, Bs, GROUP_WIDTH) for g in range(N_GROUPS)]

    o_sg, lse_sg = by_group(o_s), by_group(lse_s)
    y_s = _merge_ffn(xs, dense_s, ret_s.reshape(1, Bs, RET_V_WIDTH), o_sg, lse_sg, mod_s, weights, ln,
                     tm=Bs, sub=Bs, dils=unit)

    return (y_p, y_s.reshape(Bs, 1, D),
            *[_row_major(w) for w in wins_p], state_p[None],
            *[_row_major(r) for r in rolled], state_s[None])
```

```python
import functools
import math

import jax
import jax.numpy as jnp
import numpy as np
from jax import lax
from jax.experimental import pallas as pl
from jax.experimental.pallas import tpu as pltpu

F32 = jnp.float32
BF16 = jnp.bfloat16

D_MODEL = 1024
PAST_LEN = 16384
DILATED_PATTERNS = ((128, 1), (512, 4), (2048, 16))
N_GROUPS = 3
HEADS_PER_GROUP = 4
HEAD_DIM = 64
GROUP_WIDTH = HEADS_PER_GROUP * HEAD_DIM
ATTN_WIDTH = N_GROUPS * GROUP_WIDTH
ATTN_BAND = 128
ATTN_SCALE = HEAD_DIM ** -0.5
ROPE_DIM = HEAD_DIM // 4
ROPE_THETA = 500000.0
RET_HEADS = 4
RET_QK_DIM = 128
RET_V_DIM = 256
RET_QK_WIDTH = RET_HEADS * RET_QK_DIM
RET_V_WIDTH = RET_HEADS * RET_V_DIM
RET_CHUNK = 128
RET_ROPE_THETA = 10000.0
IN_WIDTH = 3 * ATTN_WIDTH + 2 * RET_QK_WIDTH + 2 * RET_V_WIDTH + 2 * D_MODEL
D_FF = 2816
DEEPNORM_ALPHA = 2.0 ** 0.25
LN_EPS = 1e-5
GN_EPS = 1e-6

LANES = 128
SEG = 256
FFN_CHUNK = 256
FFN_SPLIT = 1536
SCORE_LOOKAHEAD = 2
OFF_RQ, OFF_RK, OFF_RV, OFF_RG, OFF_GA, OFF_GB = 0, 512, 1024, 2048, 3072, 4096
DENSE_WIDTH = IN_WIDTH - 3 * ATTN_WIDTH
W_IN_ROTATE = 3 * ATTN_WIDTH
NEG_BIG = -1e30
VMEM_LIMIT = 60 * 1024 * 1024


def _cparams(sem):
    return pltpu.CompilerParams(dimension_semantics=sem, vmem_limit_bytes=VMEM_LIMIT)


def _resident(shape):
    nd = len(shape)
    return pl.BlockSpec(shape, lambda *_: (0,) * nd, pipeline_mode=pl.Buffered(1))


def _sigmoid(x):
    return 0.5 * jnp.tanh(0.5 * x) + 0.5


def _ret_gamma(h):
    return 1.0 - 2.0 ** (-5.0 - h)


def _ada_kernel(c_ref, w_ref, b_ref, o_ref):
    c = c_ref[...]
    a = (c * _sigmoid(c)).astype(BF16)
    o_ref[...] = jnp.dot(a, w_ref[...].astype(BF16), preferred_element_type=F32) + b_ref[...]


def _modulation(c_all, w_ada, b_ada):
    rows = c_all.shape[0]
    tn = 1024
    return pl.pallas_call(
        _ada_kernel,
        grid=(6 * D_MODEL // tn,),
        in_specs=[pl.BlockSpec((rows, D_MODEL), lambda j: (0, 0)),
                  pl.BlockSpec((D_MODEL, tn), lambda j: (0, j)),
                  pl.BlockSpec((1, tn), lambda j: (0, j))],
        out_specs=pl.BlockSpec((rows, tn), lambda j: (0, j)),
        out_shape=jax.ShapeDtypeStruct((rows, 6 * D_MODEL), F32),
        compiler_params=_cparams(("arbitrary",)),
        name="modulation",
    )(c_all, w_ada, b_ada.reshape(1, -1))


PAIR_SLOT = HEAD_DIM // 2


def _pair_source():
    src = []
    for half_dims in ((0, 16), (8, 40)):
        for h in range(HEADS_PER_GROUP):
            rot, rest = half_dims
            dims = list(range(rot, rot + ROPE_DIM // 2)) + list(range(rest, rest + PAIR_SLOT - ROPE_DIM // 2))
            src += [h * HEAD_DIM + d for d in dims]
    return src


def _pair_columns(w):
    run = ROPE_DIM // 2
    w5 = w.reshape(w.shape[0], -1, HEADS_PER_GROUP, HEAD_DIM // run, run)
    half_a = jnp.concatenate([w5[:, :, :, 0:1], w5[:, :, :, 2:5]], axis=3)
    half_b = jnp.concatenate([w5[:, :, :, 1:2], w5[:, :, :, 5:8]], axis=3)
    return jnp.stack([half_a, half_b], axis=2).reshape(w.shape)


def _rope_tables(pos, paired=False):
    posf = jnp.asarray(np.asarray(pos), jnp.int32).astype(F32)[:, None]
    lane = np.arange(LANES)
    half = ROPE_DIM // 2
    inv = jnp.exp(-math.log(ROPE_THETA) * jnp.arange(half, dtype=F32) * (2.0 / ROPE_DIM))
    ang = posf * inv[None, :]
    cos, sin = jnp.cos(ang), jnp.sin(ang)
    if paired:
        in_slot = lane % PAIR_SLOT
        f = in_slot % half
        att_c = jnp.where(in_slot < half, cos[:, f], 1.0)
        att_lo = jnp.where(in_slot < half, sin[:, f], 0.0)
        att_hi = att_lo
    else:
        in_head = lane % HEAD_DIM
        f = in_head % half
        att_c = jnp.where(in_head < ROPE_DIM, cos[:, f], 1.0)
        att_lo = jnp.where(in_head < half, -sin[:, f], 0.0)
        att_hi = jnp.where((in_head >= half) & (in_head < ROPE_DIM), sin[:, f], 0.0)
    rhalf = RET_QK_DIM // 2
    rinv = jnp.exp(-math.log(RET_ROPE_THETA) * jnp.arange(rhalf, dtype=F32) * (2.0 / RET_QK_DIM))
    rang = posf * rinv[None, :]
    rcos, rsin = jnp.cos(rang), jnp.sin(rang)
    ret_c = jnp.concatenate([rcos, rcos], axis=-1)
    ret_s = jnp.concatenate([-rsin, rsin], axis=-1)
    return att_c, att_lo, att_hi, ret_c, ret_s


def _inproj_kernel(x_ref, shift_ref, scale_ref, w_ref, wqk_ref, *refs, tm, dils, paired, keys_t, win_blocks,
                   win_first):
    n_tab = 5
    att_tab = [r[...] for r in refs[:3]]
    rc, rs = refs[3][...], refs[4][...]
    refs = refs[n_tab:]
    n_perm = sum(d > 1 for d in dils)
    dense_ref = refs[0]
    qkv_refs = refs[1:1 + N_GROUPS]
    n_out = len(refs) - (1 + n_perm if n_perm else 0)
    kt_ref = refs[1 + N_GROUPS] if keys_t else None
    win_refs = refs[1 + N_GROUPS + bool(keys_t):n_out]
    i = pl.program_id(1)
    hf = x_ref[0] * (1.0 + scale_ref[0]) + shift_ref[0]
    h = hf.astype(BF16)

    h_class = {}
    att_tabs = [att_tab] * N_GROUPS
    if n_perm:
        stage_ref = refs[n_out]
        n_x = D_MODEL // LANES
        for u in range(n_x):
            stage_ref[u] = hf[:, u * LANES:(u + 1) * LANES]
        moving = [t for t in att_tab if t.shape[0] == tm]
        for u, t in enumerate(moving):
            stage_ref[n_x + u] = t
        for g, hp_ref in zip([g for g in range(N_GROUPS) if dils[g] > 1], refs[n_out + 1:]):
            d = dils[g]
            n = tm // d
            for r in range(d):
                for u in range(n_x):
                    hp_ref[r * n:(r + 1) * n, u * LANES:(u + 1) * LANES] = (
                        stage_ref[u, pl.ds(r, n, stride=d), :].astype(BF16))
            h_class[g] = hp_ref
            regrouped = iter([jnp.concatenate([stage_ref[n_x + u, pl.ds(r, n, stride=d), :] for r in range(d)], axis=0)
                              for u in range(len(moving))])
            att_tabs[g] = [next(regrouped) if t.shape[0] == tm else t for t in att_tab]

    def att_rope(a, tab):
        ac, alo, ahi = tab
        if paired:
            lo, hi = a[:, :LANES], a[:, LANES:]
            return jnp.concatenate([lo * ac - hi * alo, hi * ac + lo * alo], axis=1)
        parts = []
        for u in range(SEG // LANES):
            xs = a[:, u * LANES:(u + 1) * LANES]
            parts.append(xs * ac + pltpu.roll(xs, LANES - ROPE_DIM // 2, 1) * alo
                         + pltpu.roll(xs, ROPE_DIM // 2, 1) * ahi)
        return jnp.concatenate(parts, axis=1)

    def ret_rope(a):
        parts = []
        for u in range(SEG // LANES):
            xs = a[:, u * LANES:(u + 1) * LANES]
            parts.append(xs * rc + pltpu.roll(xs, RET_QK_DIM // 2, 1) * rs)
        return jnp.concatenate(parts, axis=1)

    def emit_group(g, c, a):
        ref, d = qkv_refs[g], dils[g]
        n = tm // d
        for r in range(d):
            ref[0, r, :, c * SEG:(c + 1) * SEG] = a[r * n:(r + 1) * n, :].astype(ref.dtype)

    def wk_cols(g):
        return wqk_ref[:, ATTN_WIDTH + g * SEG:ATTN_WIDTH + (g + 1) * SEG]

    def wv_cols(g):
        return w_ref[:, DENSE_WIDTH + g * SEG:DENSE_WIDTH + (g + 1) * SEG]

    kv_f32 = {}

    def position_order(a, g, slot):
        d = dils[g]
        if d == 1:
            return a
        n = tm // d
        for u in range(SEG // LANES):
            for r in range(d):
                stage_ref[slot + u, pl.ds(r, n, stride=d), :] = a[r * n:(r + 1) * n, u * LANES:(u + 1) * LANES]
        return jnp.concatenate([stage_ref[slot + u] for u in range(SEG // LANES)], axis=1)

    def emit_windows(g):
        wb = win_blocks[g]
        assert dils[g] == 1 or wb == tm
        k, v = kv_f32[g]
        slot = 2 * (SEG // LANES) * (g % 2)
        at = position_order(k, g, slot)[tm - wb:, :].T
        if paired:
            dst = {f: lane for lane, f in enumerate(_pair_source())}
            at = jnp.concatenate([at[dst[f]:dst[f] + 8, :] for f in range(0, GROUP_WIDTH, 8)], axis=0)
        win_refs[2 * g][0] = at
        win_refs[2 * g + 1][0] = position_order(v, g, slot + SEG // LANES)[tm - wb:, :].T

    for seg in range(DENSE_WIDTH // SEG):
        c0 = seg * SEG
        a = jnp.dot(h, w_ref[:, c0:c0 + SEG], preferred_element_type=F32)
        if c0 < OFF_RK:
            a = ret_rope(a)
        elif c0 < OFF_RV:
            a = ret_rope(a) * (RET_QK_DIM ** -0.5)
            if keys_t:
                kt_ref[0, c0 - OFF_RK:c0 - OFF_RK + SEG, :] = a.T.astype(kt_ref.dtype)
        elif c0 < OFF_RG:
            pass
        elif c0 < OFF_GA:
            a = a * _sigmoid(a)
        else:
            a = _sigmoid(a)
        dense_ref[0, :, c0:c0 + SEG] = a.astype(dense_ref.dtype)

    for g in range(N_GROUPS):
        hg = h_class[g][...] if g in h_class else h
        wq = wqk_ref[:, g * SEG:(g + 1) * SEG]
        emit_group(g, 0, att_rope(jnp.dot(hg, wq, preferred_element_type=F32), att_tabs[g]) * ATTN_SCALE)
        k = att_rope(jnp.dot(hg, wk_cols(g), preferred_element_type=F32), att_tabs[g])
        v = jnp.dot(hg, wv_cols(g), preferred_element_type=F32)
        emit_group(g, 1, k)
        emit_group(g, 2, v)
        kv_f32[g] = (k, v)

    if win_refs:
        order = sorted(range(N_GROUPS), key=lambda g: win_first[g])

        def guarded(pos):
            if pos == len(order):
                return
            g = order[pos]

            @pl.when(i >= win_first[g])
            def _():
                emit_windows(g)
                guarded(pos + 1)

        guarded(0)


def _in_projection(x, mod, w_main, w_qk, tables, *, tm, dils, paired, keys_t, win_rows, out_dtype):
    G, R, _ = x.shape
    n_tiles = R // tm
    mod_blk = tm if mod.shape[1] == R else 1

    def mod_spec(col):
        if mod_blk == 1:
            return pl.BlockSpec((1, 1, D_MODEL), lambda b, i: (b, 0, col))
        return pl.BlockSpec((1, tm, D_MODEL), lambda b, i: (b, i, col))

    def tab_spec(tab):
        if tab.shape[0] == 1:
            return pl.BlockSpec((1, LANES), lambda b, i: (0, 0))
        return pl.BlockSpec((tm, LANES), lambda b, i: (i, 0))

    att = list(tables[:3])
    if paired:
        att[2] = np.zeros((1, LANES), np.float32)
    all_tabs = att + list(tables[3:])

    out_shapes = [jax.ShapeDtypeStruct((G, R, DENSE_WIDTH), out_dtype)]
    out_specs = [pl.BlockSpec((1, tm, DENSE_WIDTH), lambda b, i: (b, i, 0))]
    for d in dils:
        out_shapes.append(jax.ShapeDtypeStruct((G, d, R // d, ATTN_WIDTH), out_dtype))
        out_specs.append(pl.BlockSpec((1, d, tm // d, ATTN_WIDTH), lambda b, i: (b, 0, i, 0)))
    if keys_t:
        out_shapes.append(jax.ShapeDtypeStruct((G, RET_QK_WIDTH, R), out_dtype))
        out_specs.append(pl.BlockSpec((1, RET_QK_WIDTH, tm), lambda b, i: (b, 0, i)))
    win_blocks = win_first = ()
    if win_rows:
        win_blocks = tuple(min(w, tm) for w in win_rows)
        win_first = tuple(n_tiles - w // wb for w, wb in zip(win_rows, win_blocks))
        for g in range(N_GROUPS):
            first = win_first[g]
            for _ in range(2):
                out_shapes.append(jax.ShapeDtypeStruct((G, GROUP_WIDTH, win_rows[g]), F32))
                out_specs.append(pl.BlockSpec((1, GROUP_WIDTH, win_blocks[g]),
                                              lambda b, i, first=first: (b, 0, jnp.maximum(i - first, 0))))

    scratch = []
    if any(d > 1 for d in dils):
        scratch.append(pltpu.VMEM((D_MODEL // LANES + 2, tm, LANES), F32))
        scratch += [pltpu.VMEM((tm, D_MODEL), BF16) for d in dils if d > 1]
    kern = functools.partial(_inproj_kernel, tm=tm, dils=dils, paired=paired, keys_t=keys_t,
                             win_blocks=win_blocks, win_first=win_first)
    return pl.pallas_call(
        kern,
        grid=(G, n_tiles),
        in_specs=[pl.BlockSpec((1, tm, D_MODEL), lambda b, i: (b, i, 0)),
                  mod_spec(0), mod_spec(1),
                  _resident(w_main.shape), _resident(w_qk.shape)] + [tab_spec(t) for t in all_tabs],
        out_specs=out_specs,
        out_shape=out_shapes,
        scratch_shapes=scratch,
        compiler_params=_cparams(("arbitrary", "arbitrary")),
        name="in_projection",
    )(x, mod, mod, w_main, w_qk, *all_tabs)


def _attn_kernel(q_ref, k_ref, v_ref, kp_ref, vp_ref, o_ref, lse_ref, *, tl, nc):
    t = pl.program_id(2)
    row = lax.broadcasted_iota(jnp.int32, (ATTN_BAND, ATTN_BAND), 0)
    col = lax.broadcasted_iota(jnp.int32, (ATTN_BAND, ATTN_BAND), 1)
    lane = lax.broadcasted_iota(jnp.int32, (1, GROUP_WIDTH), 1)
    cur_ok = col <= row
    qk_lanes = [(lane % LANES) // PAIR_SLOT == h for h in range(HEADS_PER_GROUP)]
    v_lanes = [lane // HEAD_DIM == h for h in range(HEADS_PER_GROUP)]

    def scores(q, k_prev, k_cur):
        kk = jnp.concatenate([k_prev, k_cur], axis=0)
        qs = jnp.concatenate([jnp.where(sel, q, jnp.zeros_like(q)) for sel in qk_lanes], axis=0)
        return lax.dot_general(qs, kk, (((1,), (1,)), ((), ())), preferred_element_type=F32)

    def finish(s, v_prev, v_cur, prev_shift):
        prev_ok = col >= row + prev_shift
        mask = jnp.concatenate([prev_ok, cur_ok], axis=1)
        mask = jnp.concatenate([mask] * HEADS_PER_GROUP, axis=0)
        vv = jnp.concatenate([v_prev, v_cur], axis=0)
        s = jnp.where(mask, s, NEG_BIG)
        m = jnp.max(s, axis=1, keepdims=True)
        e = jnp.exp(s - m)
        l = jnp.sum(e, axis=1, keepdims=True)
        oh = jnp.dot((e / l).astype(BF16), vv, preferred_element_type=F32)
        lse = m + jnp.log(l)
        o = jnp.zeros((ATTN_BAND, GROUP_WIDTH), F32)
        lo = jnp.zeros((ATTN_BAND, GROUP_WIDTH), F32)
        for h, sel in enumerate(v_lanes):
            rows = slice(h * ATTN_BAND, (h + 1) * ATTN_BAND)
            o = jnp.where(sel, oh[rows], o)
            lo = jnp.where(sel, lse[rows], lo)
        return o, lo

    first = jnp.where(t > 0, 0, ATTN_BAND)
    blocks = [(c, j) for c in range(nc) for j in range(tl // ATTN_BAND)]

    def prev_rows(ref, prev_ref, c, j):
        return prev_ref[0, c] if j == 0 else ref[0, c, (j - 1) * ATTN_BAND:j * ATTN_BAND, :]

    def cur_rows(ref, c, j):
        return ref[0, c, j * ATTN_BAND:(j + 1) * ATTN_BAND, :]

    def block_scores(c, j):
        return scores(cur_rows(q_ref, c, j), prev_rows(k_ref, kp_ref, c, j), cur_rows(k_ref, c, j))

    ahead = [block_scores(*blk) for blk in blocks[:SCORE_LOOKAHEAD]]
    for n, (c, j) in enumerate(blocks):
        if n + SCORE_LOOKAHEAD < len(blocks):
            ahead.append(block_scores(*blocks[n + SCORE_LOOKAHEAD]))
        o, lse = finish(ahead[n], prev_rows(v_ref, vp_ref, c, j), cur_rows(v_ref, c, j), first if j == 0 else 0)
        rows = slice(j * ATTN_BAND, (j + 1) * ATTN_BAND)
        o_ref[0, c, rows, :] = o.astype(o_ref.dtype)
        lse_ref[0, c, rows, :] = lse


def _prompt_attention(qkv, g, *, tl, nc):
    B, dil, L, _ = qkv.shape
    sub = tl // ATTN_BAND

    def cur(c):
        return pl.BlockSpec((1, nc, tl, SEG), lambda b, r, t: (b, r, t, c))

    def prev(c):
        return pl.BlockSpec((1, nc, ATTN_BAND, SEG), lambda b, r, t: (b, r, jnp.maximum(t * sub - 1, 0), c))

    out_spec = pl.BlockSpec((1, nc, tl, SEG), lambda b, r, t: (b, r, t, 0))
    return pl.pallas_call(
        functools.partial(_attn_kernel, tl=tl, nc=nc),
        grid=(B, dil // nc, L // tl),
        in_specs=[cur(0), cur(1), cur(2), prev(1), prev(2)],
        out_specs=[out_spec, out_spec],
        out_shape=[jax.ShapeDtypeStruct((B, dil, L, SEG), BF16),
                   jax.ShapeDtypeStruct((B, dil, L, SEG), F32)],
        compiler_params=_cparams(("arbitrary", "arbitrary", "arbitrary")),
        name="prompt_attention_g%d" % g,
    )(qkv, qkv, qkv, qkv, qkv)


def _retention_tables():
    f32 = np.float32
    C = RET_CHUNK
    lg = np.log(f32(1.0) - f32(2.0) ** (f32(-5.0) - np.arange(RET_HEADS, dtype=f32)))
    idx = np.arange(C, dtype=f32)
    diff = idx[:, None] - idx[None, :]
    decay_in = np.where(diff >= 0, np.exp(lg[:, None, None] * np.maximum(diff, f32(0.0))), f32(0.0))
    decay_q = np.exp(lg[:, None] * (idx[None, :] + f32(1.0)))[:, :, None]
    decay_k = np.exp(lg[:, None] * (f32(C - 1.0) - idx[None, :]))[:, None, :]
    return tuple(t.astype(f32) for t in (decay_in, decay_q, decay_k))


def _retention_kernel(qk_ref, kt_ref, v_ref, din_ref, dq_ref, dk_ref, o_ref, state_ref, r_scr, *, tr):
    t = pl.program_id(1)
    n_chunks = tr // RET_CHUNK

    @pl.when(t == 0)
    def _():
        r_scr[...] = jnp.zeros_like(r_scr)

    def key_t(c, h):
        return kt_ref[0, h * RET_QK_DIM:(h + 1) * RET_QK_DIM, c * RET_CHUNK:(c + 1) * RET_CHUNK]

    def val(c, h):
        return v_ref[0, c * RET_CHUNK:(c + 1) * RET_CHUNK, h * RET_V_DIM:(h + 1) * RET_V_DIM]

    states = [[None] * RET_HEADS for _ in range(n_chunks)]
    for h in range(RET_HEADS):
        r = r_scr[h]
        for c in range(n_chunks):
            states[c][h] = r
            kd = (key_t(c, h).astype(F32) * dk_ref[h]).astype(BF16)
            r = r * (_ret_gamma(h) ** RET_CHUNK) + jnp.dot(kd, val(c, h), preferred_element_type=F32)
        r_scr[h] = r

    def query(c, h):
        return qk_ref[0, c * RET_CHUNK:(c + 1) * RET_CHUNK, OFF_RQ + h * RET_QK_DIM:OFF_RQ + (h + 1) * RET_QK_DIM]

    scores = [[(jnp.dot(query(c, h), key_t(c, h), preferred_element_type=F32) * din_ref[h]).astype(BF16)
               for h in range(RET_HEADS)] for c in range(n_chunks)]

    for c in range(n_chunks):
        rows = slice(c * RET_CHUNK, (c + 1) * RET_CHUNK)
        for h in range(RET_HEADS):
            q = query(c, h)
            o = (jnp.dot(scores[c][h], val(c, h), preferred_element_type=F32)
                 + jnp.dot(q, states[c][h].astype(BF16), preferred_element_type=F32) * dq_ref[h])
            mu = jnp.mean(o, axis=1, keepdims=True)
            d = o - mu
            var = jnp.mean(d * d, axis=1, keepdims=True)
            o_ref[0, rows, h * RET_V_DIM:(h + 1) * RET_V_DIM] = (d * lax.rsqrt(var + GN_EPS)).astype(o_ref.dtype)

    @pl.when(t == pl.num_programs(1) - 1)
    def _():
        state_ref[0] = r_scr[...]


def _prompt_retention(dense, keys_t, *, tr):
    B, S, _ = dense.shape
    tabs = _retention_tables()
    blk = RET_V_WIDTH

    def colblk(c):
        return pl.BlockSpec((1, tr, blk), lambda b, t: (b, t, c))

    return pl.pallas_call(
        functools.partial(_retention_kernel, tr=tr),
        grid=(B, S // tr),
        in_specs=[pl.BlockSpec((1, tr, RET_QK_WIDTH), lambda b, t: (b, t, OFF_RQ // RET_QK_WIDTH)),
                  pl.BlockSpec((1, RET_QK_WIDTH, tr), lambda b, t: (b, 0, t)),
                  colblk(OFF_RV // blk)]
                 + [pl.BlockSpec(tab.shape, lambda b, t: (0, 0, 0)) for tab in tabs],
        out_specs=[pl.BlockSpec((1, tr, RET_V_WIDTH), lambda b, t: (b, t, 0)),
                   pl.BlockSpec((1, RET_HEADS, RET_QK_DIM, RET_V_DIM), lambda b, t: (b, 0, 0, 0))],
        out_shape=[jax.ShapeDtypeStruct((B, S, RET_V_WIDTH), BF16),
                   jax.ShapeDtypeStruct((B, RET_HEADS, RET_QK_DIM, RET_V_DIM), F32)],
        scratch_shapes=[pltpu.VMEM((RET_HEADS, RET_QK_DIM, RET_V_DIM), F32)],
        compiler_params=_cparams(("arbitrary", "arbitrary")),
        name="prompt_retention",
    )(dense, keys_t, dense, *tabs)


def _layer_norm(u, g, b):
    mu = jnp.mean(u, axis=1, keepdims=True)
    d = u - mu
    var = jnp.mean(d * d, axis=1, keepdims=True)
    return d * lax.rsqrt(var + LN_EPS) * g + b


def _column(row):
    return jnp.broadcast_to(row, (LANES, LANES)).T


def _window_half_step(first_half, qkv_refs, cache_refs, o_ref, lse_ref, new_refs):
    pad = 16
    row_h = lax.broadcasted_iota(jnp.int32, (pad, LANES), 0)
    lane_h = lax.broadcasted_iota(jnp.int32, (pad, LANES), 1) // HEAD_DIM
    own = row_h == lane_h
    lane = lax.broadcasted_iota(jnp.int32, (LANES, LANES), 1)

    def pick(row):
        return jnp.where(first_half, row[:, :LANES], row[:, LANES:])

    def rolled(c_ref, n_ref, new_row):
        w = c_ref.shape[3]
        new_col = _column(new_row)
        nblk = w // LANES
        cur = pltpu.roll(c_ref[0, 0, :, 0:LANES], LANES - 1, 1)
        for j in range(nblk):
            nxt = pltpu.roll(c_ref[0, 0, :, (j + 1) * LANES:(j + 2) * LANES], LANES - 1, 1) if j + 1 < nblk else new_col
            n_ref[0, 0, :, j * LANES:(j + 1) * LANES] = jnp.where(lane < LANES - 1, cur, nxt)
            cur = nxt

    started = []
    for g in range(N_GROUPS):
        q = pick(qkv_refs[g][0, :, 0:SEG])
        qsel = jnp.where(own, jnp.broadcast_to(q, (pad, LANES)), 0.0)
        s = jnp.dot(qsel.astype(BF16), cache_refs[2 * g][0, 0].astype(BF16), preferred_element_type=F32)
        started.append((qsel, s))

    def finish():
        for g, (qsel, s) in enumerate(started):
            dil = DILATED_PATTERNS[g][1]
            ck_ref, cv_ref = cache_refs[2 * g:2 * g + 2]
            kn = pick(qkv_refs[g][0, :, SEG:2 * SEG])
            vn = pick(qkv_refs[g][0, :, 2 * SEG:3 * SEG])
            pos = lax.broadcasted_iota(jnp.int32, s.shape, 1)
            s = jnp.where((pos & (dil - 1)) == 0, s, NEG_BIG)
            sn = jnp.sum(qsel * kn, axis=1, keepdims=True)
            m = jnp.maximum(jnp.max(s, axis=1, keepdims=True), sn)
            e = jnp.exp(s - m)
            en = jnp.exp(sn - m)
            l = jnp.sum(e, axis=1, keepdims=True) + en
            o_all = lax.dot_general((e / l).astype(BF16), cv_ref[0, 0].astype(BF16), (((1,), (1,)), ((), ())),
                                    preferred_element_type=F32) + (en / l) * vn
            o_ref[0, 0, :, g * LANES:(g + 1) * LANES] = jnp.sum(jnp.where(own, o_all, 0.0), axis=0, keepdims=True)
            lse_ref[0, 0, :, g * LANES:(g + 1) * LANES] = jnp.sum(jnp.where(own, m + jnp.log(l), 0.0), axis=0,
                                                                  keepdims=True)
            rolled(ck_ref, new_refs[2 * g], kn)
            rolled(cv_ref, new_refs[2 * g + 1], vn)

    return finish


N_MERGE_INPUTS = 24
N_SIDE_INPUTS = N_GROUPS + 2 * N_GROUPS
N_SIDE_OUTPUTS = 2 + 2 * N_GROUPS


def _merge_ffn_kernel(*refs, tm, sub, dils, side, steps_per_batch):
    (x_ref, ga_ref, gb_ref, rg_ref, ret_ref, o0_ref, o1_ref, o2_ref, l0_ref, l1_ref, l2_ref,
     gate1_ref, shift2_ref, scale2_ref, gate2_ref, watt_ref, wret_ref, wo_ref, wfi_ref, wfo_ref,
     ln1g_ref, ln1b_ref, ln2g_ref, ln2b_ref) = refs[:N_MERGE_INPUTS]
    n_in = N_MERGE_INPUTS + (N_SIDE_INPUTS if side else 0)
    y_ref = refs[n_in]
    n_out = 1 + (N_SIDE_OUTPUTS if side else 0)
    scratch = refs[n_in + n_out:]
    if side:
        side_in = refs[N_MERGE_INPUTS:n_in]
        side_out = refs[n_in + 1:n_in + n_out]
        step = pl.program_id(0) * steps_per_batch + pl.program_id(1)
        finish_side = _window_half_step(step % 2 == 0, side_in[:N_GROUPS], side_in[N_GROUPS:], side_out[0],
                                        side_out[1], side_out[2:])
    n_sub = tm // sub
    per_sub = 1 + 2 * N_GROUPS
    def rows(s):
        return slice(s * sub, (s + 1) * sub)

    def mod_rows(ref, s):
        return ref[0] if ref.shape[1] == 1 else ref[0, rows(s), :]

    def natural(ref, g, s, which):
        d = dils[g]
        n = sub // d
        if d == 1:
            return ref[0, 0, rows(s), :].astype(F32)
        stage_ref = scratch[s * per_sub + 1 + which * N_GROUPS + g]
        for r in range(d):
            for u in range(GROUP_WIDTH // LANES):
                stage_ref[u, pl.ds(r, n, stride=d), :] = (
                    ref[0, r, s * n:(s + 1) * n, u * LANES:(u + 1) * LANES].astype(F32))
        return jnp.concatenate([stage_ref[u] for u in range(GROUP_WIDTH // LANES)], axis=1)

    def branch_dots(s):
        gated = rg_ref[0, rows(s), :].astype(F32) * ret_ref[0, rows(s), :].astype(F32)
        y_ret = jnp.dot(gated.astype(BF16), wret_ref[...], preferred_element_type=F32)
        lses = [natural(ref, g, s, 0) for g, ref in enumerate((l0_ref, l1_ref, l2_ref))]
        outs = [natural(ref, g, s, 1) for g, ref in enumerate((o0_ref, o1_ref, o2_ref))]
        m = jnp.maximum(jnp.maximum(lses[0], lses[1]), lses[2])
        es = [jnp.exp(l - m) for l in lses]
        att = (es[0] * outs[0] + es[1] * outs[1] + es[2] * outs[2]) / (es[0] + es[1] + es[2])
        y_att = jnp.dot(att.astype(BF16), watt_ref[...], preferred_element_type=F32)
        return y_att, y_ret

    def out_projection(s, y_att, y_ret):
        mix = (ga_ref[0, rows(s), :].astype(F32) * y_att + gb_ref[0, rows(s), :].astype(F32) * y_ret).astype(BF16)
        return jnp.dot(mix, wo_ref[...], preferred_element_type=F32)

    def first_norm(s, mixed):
        x1 = _layer_norm(DEEPNORM_ALPHA * x_ref[0, rows(s), :] + mod_rows(gate1_ref, s) * mixed,
                         ln1g_ref[...], ln1b_ref[...])
        return x1, (x1 * (1.0 + mod_rows(scale2_ref, s)) + mod_rows(shift2_ref, s)).astype(BF16)

    def swiglu(s, h2):
        act_scr = scratch[s * per_sub]
        for c0 in range(0, D_FF, FFN_CHUNK):
            c1 = min(c0 + FFN_CHUNK, D_FF)
            fg = jnp.dot(h2, wfi_ref[:, c0:c1], preferred_element_type=F32)
            fu = jnp.dot(h2, wfi_ref[:, D_FF + c0:D_FF + c1], preferred_element_type=F32)
            act_scr[:, c0:c1] = (fg * _sigmoid(fg) * fu).astype(BF16)

    def ffn_out(s, k0, k1):
        return jnp.dot(scratch[s * per_sub][:, k0:k1], wfo_ref[k0:k1, :], preferred_element_type=F32)

    def second_norm(s, x1, ff):
        y_ref[0, rows(s), :] = _layer_norm(DEEPNORM_ALPHA * x1 + mod_rows(gate2_ref, s) * ff,
                                           ln2g_ref[...], ln2b_ref[...])

    mixed = out_projection(0, *branch_dots(0))
    if side:
        finish_side()
    x1, h2 = first_norm(0, mixed)
    for s in range(n_sub):
        swiglu(s, h2)
        if s + 1 < n_sub:
            nxt = branch_dots(s + 1)
            ff = ffn_out(s, 0, FFN_SPLIT)
            mixed = out_projection(s + 1, *nxt)
            ff = ff + ffn_out(s, FFN_SPLIT, D_FF)
            x1_next, h2 = first_norm(s + 1, mixed)
            second_norm(s, x1, ff)
            x1 = x1_next
        else:
            second_norm(s, x1, ffn_out(s, 0, D_FF))


def _merge_ffn(x, dense, ret, o_groups, lse_groups, mod, weights, ln, *, tm, sub, dils, side=None):
    G, R, _ = x.shape
    mod_rows = mod.shape[1]
    n_tiles = R // tm
    side_args, side_in_specs, side_out_specs, side_out_shapes = [], [], [], []
    if side is not None:
        qkvs, caches = side
        n_seq = caches[0].shape[0]
        assert G * n_tiles == 2 * n_seq

        def seq_half(b, i):
            step = b * n_tiles + i
            return step // 2, step % 2

        row_spec = pl.BlockSpec((1, 1, ATTN_WIDTH), lambda b, i: (seq_half(b, i)[0], 0, 0))
        small_spec = pl.BlockSpec((1, 1, 1, N_GROUPS * LANES), lambda b, i: seq_half(b, i) + (0, 0))
        halves = [c.reshape(n_seq, 2, LANES, c.shape[2]) for c in caches]
        half_specs = [pl.BlockSpec((1, 1, LANES, c.shape[3]), lambda b, i: seq_half(b, i) + (0, 0)) for c in halves]
        side_args = list(qkvs) + halves
        side_in_specs = [row_spec] * N_GROUPS + half_specs
        side_out_specs = [small_spec, small_spec] + half_specs
        side_out_shapes = ([jax.ShapeDtypeStruct((n_seq, 2, 1, N_GROUPS * LANES), F32)] * 2
                           + [jax.ShapeDtypeStruct(c.shape, F32) for c in halves])
    sub_scratch = [pltpu.VMEM((sub, D_FF), BF16)] + [pltpu.VMEM((GROUP_WIDTH // LANES, sub, LANES), F32)] * (2 * N_GROUPS)

    def mod_spec(col):
        if mod_rows == 1:
            return pl.BlockSpec((1, 1, D_MODEL), lambda b, i: (b, 0, col))
        return pl.BlockSpec((1, tm, D_MODEL), lambda b, i: (b, i, col))

    def rows(width, col=0):
        return pl.BlockSpec((1, tm, width), lambda b, i: (b, i, col))

    group_specs = [pl.BlockSpec((1, d, tm // d, GROUP_WIDTH), lambda b, i: (b, 0, i, 0)) for d in dils]
    vec = pl.BlockSpec((1, D_MODEL), lambda b, i: (0, 0))
    outs = pl.pallas_call(
        functools.partial(_merge_ffn_kernel, tm=tm, sub=sub, dils=dils, side=side is not None,
                          steps_per_batch=n_tiles),
        grid=(G, n_tiles),
        in_specs=[rows(D_MODEL), rows(D_MODEL, OFF_GA // D_MODEL), rows(D_MODEL, OFF_GB // D_MODEL),
                  rows(D_MODEL, OFF_RG // D_MODEL), rows(RET_V_WIDTH)] + group_specs + group_specs
                 + [mod_spec(2), mod_spec(3), mod_spec(4), mod_spec(5)]
                 + [_resident(w.shape) for w in weights] + [vec] * 4 + side_in_specs,
        out_specs=[rows(D_MODEL)] + side_out_specs,
        out_shape=[jax.ShapeDtypeStruct((G, R, D_MODEL), F32)] + side_out_shapes,
        scratch_shapes=sub_scratch * (tm // sub),
        compiler_params=_cparams(("arbitrary", "arbitrary")),
        name="merge_ffn",
    )(x, dense, dense, dense, ret, *o_groups, *lse_groups, mod, mod, mod, mod, *weights, *ln, *side_args)
    if side is None:
        return outs[0]
    y, o_s, lse_s, *rolled = outs
    return y, o_s, lse_s, [r.reshape(c.shape) for r, c in zip(rolled, side[1])]


def _sample_retention_kernel(dense_ref, state_ref, ret_ref, nstate_ref):
    for b in range(dense_ref.shape[0]):
        for h in range(RET_HEADS):
            gamma = _ret_gamma(h)
            q = dense_ref[b, :, OFF_RQ + h * RET_QK_DIM:OFF_RQ + (h + 1) * RET_QK_DIM]
            k = dense_ref[b, :, OFF_RK + h * RET_QK_DIM:OFF_RK + (h + 1) * RET_QK_DIM]
            v = dense_ref[b, :, OFF_RV + h * RET_V_DIM:OFF_RV + (h + 1) * RET_V_DIM]
            r_old = state_ref[b, h]
            q_col = jnp.concatenate([_column(q)] * (RET_V_DIM // LANES), axis=1)
            k_col = jnp.concatenate([_column(k)] * (RET_V_DIM // LANES), axis=1)
            qk = jnp.sum(q * k, axis=1, keepdims=True)
            o = qk * v + jnp.sum(q_col * r_old, axis=0, keepdims=True) * gamma
            nstate_ref[b, h] = r_old * gamma + k_col * v
            mu = jnp.mean(o, axis=1, keepdims=True)
            d = o - mu
            var = jnp.mean(d * d, axis=1, keepdims=True)
            ret_ref[b, :, h * RET_V_DIM:(h + 1) * RET_V_DIM] = d * lax.rsqrt(var + GN_EPS)


def _sample_retention(dense, state, *, nb=4):
    Bs = dense.shape[0]
    row = lambda w: pl.BlockSpec((nb, 1, w), lambda b: (b, 0, 0))
    st = pl.BlockSpec((nb, RET_HEADS, RET_QK_DIM, RET_V_DIM), lambda b: (b, 0, 0, 0))
    return pl.pallas_call(
        _sample_retention_kernel,
        grid=(Bs // nb,),
        in_specs=[row(DENSE_WIDTH), st],
        out_specs=[row(RET_V_WIDTH), st],
        out_shape=[jax.ShapeDtypeStruct((Bs, 1, RET_V_WIDTH), F32), jax.ShapeDtypeStruct(state.shape, F32)],
        compiler_params=_cparams(("arbitrary",)),
        name="sample_retention",
    )(dense, state)


def _feature_major(a):
    _, b, w, _, _ = a.shape
    return jnp.transpose(a[0], (0, 2, 3, 1)).reshape(b, GROUP_WIDTH, w)


def _row_major(a):
    b, _, w = a.shape
    return jnp.transpose(a.reshape(b, HEADS_PER_GROUP, HEAD_DIM, w), (0, 3, 1, 2))[None]


def kernel(x_prompt, x_sample, cache_k_w128, cache_v_w128, cache_k_w512, cache_v_w512, cache_k_w2048,
           cache_v_w2048, state_retention, c_prompt, c_sample, w_ada, b_ada, w_in, w_att_out, w_ret_out,
           w_o, ln1_g, ln1_b, w_ffn_in, w_ffn_out, ln2_g, ln2_b):
    assert w_in.shape[0] == 1 and x_sample.shape[1] == 1
    B, S, D = x_prompt.shape
    Bs = x_sample.shape[0]
    dils = tuple(d for _, d in DILATED_PATTERNS)

    w_main = jnp.concatenate([w_in[0][:, W_IN_ROTATE:], w_in[0][:, 2 * ATTN_WIDTH:W_IN_ROTATE]], axis=1).astype(BF16)
    w_qk = w_in[0][:, :2 * ATTN_WIDTH].astype(BF16)
    w_qk_paired = _pair_columns(w_qk)
    weights = tuple(w[0].astype(BF16) for w in (w_att_out, w_ret_out, w_o, w_ffn_in, w_ffn_out))
    ln = (ln1_g, ln1_b, ln2_g, ln2_b)

    mod = _modulation(jnp.concatenate([c_prompt, c_sample], axis=0), w_ada[0], b_ada[0])
    mod_p = mod[:B].reshape(B, 1, 6 * D)
    mod_s = mod[B:].reshape(1, Bs, 6 * D)

    tabs_p = _rope_tables(np.arange(S), paired=True)
    win_rows = tuple(min(w, S) for w, _ in DILATED_PATTERNS)
    dense_p, *rest = _in_projection(x_prompt, mod_p, w_main, w_qk_paired, tabs_p, tm=256, dils=dils, paired=True,
                                    keys_t=True, win_rows=win_rows, out_dtype=BF16)
    qkv_p, keys_t_p, wins_p = rest[:N_GROUPS], rest[N_GROUPS], rest[N_GROUPS + 1:]
    o_p, lse_p = [], []
    for g, d in enumerate(dils):
        tl = min(S // d, 1024)
        o, lse = _prompt_attention(qkv_p[g], g, tl=tl, nc=min(d, 1024 // tl))
        o_p.append(o)
        lse_p.append(lse)
    ret_p, state_p = _prompt_retention(dense_p, keys_t_p, tr=1024)

    xs = x_sample.reshape(1, Bs, D)
    tabs_s = _rope_tables(PAST_LEN + np.arange(1))
    unit = (1,) * N_GROUPS
    dense_s, *qkv_s = _in_projection(xs, mod_s, w_main, w_qk, tabs_s, tm=Bs, dils=unit, paired=False,
                                     keys_t=False, win_rows=None, out_dtype=F32)
    caches = [_feature_major(c) for c in (cache_k_w128, cache_v_w128, cache_k_w512, cache_v_w512,
                                          cache_k_w2048, cache_v_w2048)]
    ret_s, state_s = _sample_retention(dense_s.reshape(Bs, 1, DENSE_WIDTH), state_retention[0])
    y_p, o_s, lse_s, rolled = _merge_ffn(
        x_prompt, dense_p, ret_p, o_p, lse_p, mod_p, weights, ln, tm=512, sub=256, dils=dils,
        side=([a.reshape(Bs, 1, ATTN_WIDTH) for a in qkv_s], caches))

    def by_group(a):
        return [a[:, :, 0, g * LANES:(g + 1) * LANES].reshape(1, 1, Bs, GROUP_WIDTH) for g in range(N_GROUPS)]

    o_sg, lse_sg = by_group(o_s), by_group(lse_s)
    y_s = _merge_ffn(xs, dense_s, ret_s.reshape(1, Bs, RET_V_WIDTH), o_sg, lse_sg, mod_s, weights, ln,
                     tm=Bs, sub=Bs, dils=unit)

    return (y_p, y_s.reshape(Bs, 1, D),
            *[_row_major(w) for w in wins_p], state_p[None],
            *[_row_major(r) for r in rolled], state_s[None])
```

```python
import functools
import math

import jax
import jax.numpy as jnp
import numpy as np
from jax import lax
from jax.experimental import pallas as pl
from jax.experimental.pallas import tpu as pltpu

F32 = jnp.float32
BF16 = jnp.bfloat16

D_MODEL = 1024
PAST_LEN = 16384
DILATED_PATTERNS = ((128, 1), (512, 4), (2048, 16))
N_GROUPS = 3
HEADS_PER_GROUP = 4
HEAD_DIM = 64
GROUP_WIDTH = HEADS_PER_GROUP * HEAD_DIM
ATTN_WIDTH = N_GROUPS * GROUP_WIDTH
ATTN_BAND = 128
ATTN_SCALE = HEAD_DIM ** -0.5
ROPE_DIM = HEAD_DIM // 4
ROPE_THETA = 500000.0
RET_HEADS = 4
RET_QK_DIM = 128
RET_V_DIM = 256
RET_QK_WIDTH = RET_HEADS * RET_QK_DIM
RET_V_WIDTH = RET_HEADS * RET_V_DIM
RET_CHUNK = 128
RET_ROPE_THETA = 10000.0
IN_WIDTH = 3 * ATTN_WIDTH + 2 * RET_QK_WIDTH + 2 * RET_V_WIDTH + 2 * D_MODEL
D_FF = 2816
DEEPNORM_ALPHA = 2.0 ** 0.25
LN_EPS = 1e-5
GN_EPS = 1e-6

LANES = 128
SEG = 256
FFN_CHUNK = 256
FFN_SPLIT = 1536
SCORE_LOOKAHEAD = 2
OFF_RQ, OFF_RK, OFF_RV, OFF_RG, OFF_GA, OFF_GB = 0, 512, 1024, 2048, 3072, 4096
DENSE_WIDTH = IN_WIDTH - 3 * ATTN_WIDTH
W_IN_ROTATE = 3 * ATTN_WIDTH
NEG_BIG = -1e30
VMEM_LIMIT = 60 * 1024 * 1024


def _cparams(sem):
    return pltpu.CompilerParams(dimension_semantics=sem, vmem_limit_bytes=VMEM_LIMIT)


def _resident(shape):
    nd = len(shape)
    return pl.BlockSpec(shape, lambda *_: (0,) * nd, pipeline_mode=pl.Buffered(1))


def _sigmoid(x):
    return 0.5 * jnp.tanh(0.5 * x) + 0.5


def _ret_gamma(h):
    return 1.0 - 2.0 ** (-5.0 - h)


def _ada_kernel(c_ref, w_ref, b_ref, o_ref):
    c = c_ref[...]
    a = (c * _sigmoid(c)).astype(BF16)
    o_ref[...] = jnp.dot(a, w_ref[...].astype(BF16), preferred_element_type=F32) + b_ref[...]


def _modulation(c_all, w_ada, b_ada):
    rows = c_all.shape[0]
    tn = 1024
    return pl.pallas_call(
        _ada_kernel,
        grid=(6 * D_MODEL // tn,),
        in_specs=[pl.BlockSpec((rows, D_MODEL), lambda j: (0, 0)),
                  pl.BlockSpec((D_MODEL, tn), lambda j: (0, j)),
                  pl.BlockSpec((1, tn), lambda j: (0, j))],
        out_specs=pl.BlockSpec((rows, tn), lambda j: (0, j)),
        out_shape=jax.ShapeDtypeStruct((rows, 6 * D_MODEL), F32),
        compiler_params=_cparams(("arbitrary",)),
        name="modulation",
    )(c_all, w_ada, b_ada.reshape(1, -1))


PAIR_SLOT = HEAD_DIM // 2


def _pair_source():
    src = []
    for half_dims in ((0, 16), (8, 40)):
        for h in range(HEADS_PER_GROUP):
            rot, rest = half_dims
            dims = list(range(rot, rot + ROPE_DIM // 2)) + list(range(rest, rest + PAIR_SLOT - ROPE_DIM // 2))
            src += [h * HEAD_DIM + d for d in dims]
    return src


def _pair_columns(w):
    run = ROPE_DIM // 2
    w5 = w.reshape(w.shape[0], -1, HEADS_PER_GROUP, HEAD_DIM // run, run)
    half_a = jnp.concatenate([w5[:, :, :, 0:1], w5[:, :, :, 2:5]], axis=3)
    half_b = jnp.concatenate([w5[:, :, :, 1:2], w5[:, :, :, 5:8]], axis=3)
    return jnp.stack([half_a, half_b], axis=2).reshape(w.shape)


def _rope_tables(pos, paired=False):
    posf = jnp.asarray(np.asarray(pos), jnp.int32).astype(F32)[:, None]
    lane = np.arange(LANES)
    half = ROPE_DIM // 2
    inv = jnp.exp(-math.log(ROPE_THETA) * jnp.arange(half, dtype=F32) * (2.0 / ROPE_DIM))
    ang = posf * inv[None, :]
    cos, sin = jnp.cos(ang), jnp.sin(ang)
    if paired:
        in_slot = lane % PAIR_SLOT
        f = in_slot % half
        att_c = jnp.where(in_slot < half, cos[:, f], 1.0)
        att_lo = jnp.where(in_slot < half, sin[:, f], 0.0)
        att_hi = att_lo
    else:
        in_head = lane % HEAD_DIM
        f = in_head % half
        att_c = jnp.where(in_head < ROPE_DIM, cos[:, f], 1.0)
        att_lo = jnp.where(in_head < half, -sin[:, f], 0.0)
        att_hi = jnp.where((in_head >= half) & (in_head < ROPE_DIM), sin[:, f], 0.0)
    rhalf = RET_QK_DIM // 2
    rinv = jnp.exp(-math.log(RET_ROPE_THETA) * jnp.arange(rhalf, dtype=F32) * (2.0 / RET_QK_DIM))
    rang = posf * rinv[None, :]
    rcos, rsin = jnp.cos(rang), jnp.sin(rang)
    ret_c = jnp.concatenate([rcos, rcos], axis=-1)
    ret_s = jnp.concatenate([-rsin, rsin], axis=-1)
    return att_c, att_lo, att_hi, ret_c, ret_s


def _inproj_kernel(x_ref, shift_ref, scale_ref, w_ref, wqk_ref, *refs, tm, dils, paired, keys_t, win_blocks,
                   win_first):
    n_tab = 5
    att_tab = [r[...] for r in refs[:3]]
    rc, rs = refs[3][...], refs[4][...]
    refs = refs[n_tab:]
    n_perm = sum(d > 1 for d in dils)
    dense_ref = refs[0]
    qkv_refs = refs[1:1 + N_GROUPS]
    n_out = len(refs) - (1 + n_perm if n_perm else 0)
    kt_ref = refs[1 + N_GROUPS] if keys_t else None
    win_refs = refs[1 + N_GROUPS + bool(keys_t):n_out]
    i = pl.program_id(1)
    hf = x_ref[0] * (1.0 + scale_ref[0]) + shift_ref[0]
    h = hf.astype(BF16)

    h_class = {}
    att_tabs = [att_tab] * N_GROUPS
    if n_perm:
        stage_ref = refs[n_out]
        n_x = D_MODEL // LANES
        for u in range(n_x):
            stage_ref[u] = hf[:, u * LANES:(u + 1) * LANES]
        moving = [t for t in att_tab if t.shape[0] == tm]
        for u, t in enumerate(moving):
            stage_ref[n_x + u] = t
        for g, hp_ref in zip([g for g in range(N_GROUPS) if dils[g] > 1], refs[n_out + 1:]):
            d = dils[g]
            n = tm // d
            for r in range(d):
                for u in range(n_x):
                    hp_ref[r * n:(r + 1) * n, u * LANES:(u + 1) * LANES] = (
                        stage_ref[u, pl.ds(r, n, stride=d), :].astype(BF16))
            h_class[g] = hp_ref
            regrouped = iter([jnp.concatenate([stage_ref[n_x + u, pl.ds(r, n, stride=d), :] for r in range(d)], axis=0)
                              for u in range(len(moving))])
            att_tabs[g] = [next(regrouped) if t.shape[0] == tm else t for t in att_tab]

    def att_rope(a, tab):
        ac, alo, ahi = tab
        if paired:
            lo, hi = a[:, :LANES], a[:, LANES:]
            return jnp.concatenate([lo * ac - hi * alo, hi * ac + lo * alo], axis=1)
        parts = []
        for u in range(SEG // LANES):
            xs = a[:, u * LANES:(u + 1) * LANES]
            parts.append(xs * ac + pltpu.roll(xs, LANES - ROPE_DIM // 2, 1) * alo
                         + pltpu.roll(xs, ROPE_DIM // 2, 1) * ahi)
        return jnp.concatenate(parts, axis=1)

    def ret_rope(a):
        parts = []
        for u in range(SEG // LANES):
            xs = a[:, u * LANES:(u + 1) * LANES]
            parts.append(xs * rc + pltpu.roll(xs, RET_QK_DIM // 2, 1) * rs)
        return jnp.concatenate(parts, axis=1)

    def emit_group(g, c, a):
        ref, d = qkv_refs[g], dils[g]
        n = tm // d
        for r in range(d):
            ref[0, r, :, c * SEG:(c + 1) * SEG] = a[r * n:(r + 1) * n, :].astype(ref.dtype)

    def wk_cols(g):
        return wqk_ref[:, ATTN_WIDTH + g * SEG:ATTN_WIDTH + (g + 1) * SEG]

    def wv_cols(g):
        return w_ref[:, DENSE_WIDTH + g * SEG:DENSE_WIDTH + (g + 1) * SEG]

    kv_f32 = {}

    def position_order(a, g, slot):
        d = dils[g]
        if d == 1:
            return a
        n = tm // d
        for u in range(SEG // LANES):
            for r in range(d):
                stage_ref[slot + u, pl.ds(r, n, stride=d), :] = a[r * n:(r + 1) * n, u * LANES:(u + 1) * LANES]
        return jnp.concatenate([stage_ref[slot + u] for u in range(SEG // LANES)], axis=1)

    def emit_windows(g):
        wb = win_blocks[g]
        assert dils[g] == 1 or wb == tm
        k, v = kv_f32[g]
        slot = 2 * (SEG // LANES) * (g % 2)
        at = position_order(k, g, slot)[tm - wb:, :].T
        if paired:
            dst = {f: lane for lane, f in enumerate(_pair_source())}
            at = jnp.concatenate([at[dst[f]:dst[f] + 8, :] for f in range(0, GROUP_WIDTH, 8)], axis=0)
        win_refs[2 * g][0] = at
        win_refs[2 * g + 1][0] = position_order(v, g, slot + SEG // LANES)[tm - wb:, :].T

    for seg in range(DENSE_WIDTH // SEG):
        c0 = seg * SEG
        a = jnp.dot(h, w_ref[:, c0:c0 + SEG], preferred_element_type=F32)
        if c0 < OFF_RK:
            a = ret_rope(a)
        elif c0 < OFF_RV:
            a = ret_rope(a) * (RET_QK_DIM ** -0.5)
            if keys_t:
                kt_ref[0, c0 - OFF_RK:c0 - OFF_RK + SEG, :] = a.T.astype(kt_ref.dtype)
        elif c0 < OFF_RG:
            pass
        elif c0 < OFF_GA:
            a = a * _sigmoid(a)
        else:
            a = _sigmoid(a)
        dense_ref[0, :, c0:c0 + SEG] = a.astype(dense_ref.dtype)

    for g in range(N_GROUPS):
        hg = h_class[g][...] if g in h_class else h
        wq = wqk_ref[:, g * SEG:(g + 1) * SEG]
        emit_group(g, 0, att_rope(jnp.dot(hg, wq, preferred_element_type=F32), att_tabs[g]) * ATTN_SCALE)
        k = att_rope(jnp.dot(hg, wk_cols(g), preferred_element_type=F32), att_tabs[g])
        v = jnp.dot(hg, wv_cols(g), preferred_element_type=F32)
        emit_group(g, 1, k)
        emit_group(g, 2, v)
        kv_f32[g] = (k, v)

    if win_refs:
        order = sorted(range(N_GROUPS), key=lambda g: win_first[g])

        def guarded(pos):
            if pos == len(order):
                return
            g = order[pos]

            @pl.when(i >= win_first[g])
            def _():
                emit_windows(g)
                guarded(pos + 1)

        guarded(0)


def _in_projection(x, mod, w_main, w_qk, tables, *, tm, dils, paired, keys_t, win_rows, out_dtype):
    G, R, _ = x.shape
    n_tiles = R // tm
    mod_blk = tm if mod.shape[1] == R else 1

    def mod_spec(col):
        if mod_blk == 1:
            return pl.BlockSpec((1, 1, D_MODEL), lambda b, i: (b, 0, col))
        return pl.BlockSpec((1, tm, D_MODEL), lambda b, i: (b, i, col))

    def tab_spec(tab):
        if tab.shape[0] == 1:
            return pl.BlockSpec((1, LANES), lambda b, i: (0, 0))
        return pl.BlockSpec((tm, LANES), lambda b, i: (i, 0))

    att = list(tables[:3])
    if paired:
        att[2] = np.zeros((1, LANES), np.float32)
    all_tabs = att + list(tables[3:])

    out_shapes = [jax.ShapeDtypeStruct((G, R, DENSE_WIDTH), out_dtype)]
    out_specs = [pl.BlockSpec((1, tm, DENSE_WIDTH), lambda b, i: (b, i, 0))]
    for d in dils:
        out_shapes.append(jax.ShapeDtypeStruct((G, d, R // d, ATTN_WIDTH), out_dtype))
        out_specs.append(pl.BlockSpec((1, d, tm // d, ATTN_WIDTH), lambda b, i: (b, 0, i, 0)))
    if keys_t:
        out_shapes.append(jax.ShapeDtypeStruct((G, RET_QK_WIDTH, R), out_dtype))
        out_specs.append(pl.BlockSpec((1, RET_QK_WIDTH, tm), lambda b, i: (b, 0, i)))
    win_blocks = win_first = ()
    if win_rows:
        win_blocks = tuple(min(w, tm) for w in win_rows)
        win_first = tuple(n_tiles - w // wb for w, wb in zip(win_rows, win_blocks))
        for g in range(N_GROUPS):
            first = win_first[g]
            for _ in range(2):
                out_shapes.append(jax.ShapeDtypeStruct((G, GROUP_WIDTH, win_rows[g]), F32))
                out_specs.append(pl.BlockSpec((1, GROUP_WIDTH, win_blocks[g]),
                                              lambda b, i, first=first: (b, 0, jnp.maximum(i - first, 0))))

    scratch = []
    if any(d > 1 for d in dils):
        scratch.append(pltpu.VMEM((D_MODEL // LANES + 2, tm, LANES), F32))
        scratch += [pltpu.VMEM((tm, D_MODEL), BF16) for d in dils if d > 1]
    kern = functools.partial(_inproj_kernel, tm=tm, dils=dils, paired=paired, keys_t=keys_t,
                             win_blocks=win_blocks, win_first=win_first)
    return pl.pallas_call(
        kern,
        grid=(G, n_tiles),
        in_specs=[pl.BlockSpec((1, tm, D_MODEL), lambda b, i: (b, i, 0)),
                  mod_spec(0), mod_spec(1),
                  _resident(w_main.shape), _resident(w_qk.shape)] + [tab_spec(t) for t in all_tabs],
        out_specs=out_specs,
        out_shape=out_shapes,
        scratch_shapes=scratch,
        compiler_params=_cparams(("arbitrary", "arbitrary")),
        name="in_projection",
    )(x, mod, mod, w_main, w_qk, *all_tabs)


def _attn_kernel(q_ref, k_ref, v_ref, kp_ref, vp_ref, o_ref, lse_ref, *, tl, nc):
    t = pl.program_id(2)
    row = lax.broadcasted_iota(jnp.int32, (ATTN_BAND, ATTN_BAND), 0)
    col = lax.broadcasted_iota(jnp.int32, (ATTN_BAND, ATTN_BAND), 1)
    lane = lax.broadcasted_iota(jnp.int32, (1, GROUP_WIDTH), 1)
    cur_ok = col <= row
    qk_lanes = [(lane % LANES) // PAIR_SLOT == h for h in range(HEADS_PER_GROUP)]
    v_lanes = [lane // HEAD_DIM == h for h in range(HEADS_PER_GROUP)]

    def scores(q, k_prev, k_cur):
        kk = jnp.concatenate([k_prev, k_cur], axis=0)
        qs = jnp.concatenate([jnp.where(sel, q, jnp.zeros_like(q)) for sel in qk_lanes], axis=0)
        return lax.dot_general(qs, kk, (((1,), (1,)), ((), ())), preferred_element_type=F32)

    def finish(s, v_prev, v_cur, prev_shift):
        prev_ok = col >= row + prev_shift
        mask = jnp.concatenate([prev_ok, cur_ok], axis=1)
        mask = jnp.concatenate([mask] * HEADS_PER_GROUP, axis=0)
        vv = jnp.concatenate([v_prev, v_cur], axis=0)
        s = jnp.where(mask, s, NEG_BIG)
        m = jnp.max(s, axis=1, keepdims=True)
        e = jnp.exp(s - m)
        l = jnp.sum(e, axis=1, keepdims=True)
        oh = jnp.dot((e / l).astype(BF16), vv, preferred_element_type=F32)
        lse = m + jnp.log(l)
        o = jnp.zeros((ATTN_BAND, GROUP_WIDTH), F32)
        lo = jnp.zeros((ATTN_BAND, GROUP_WIDTH), F32)
        for h, sel in enumerate(v_lanes):
            rows = slice(h * ATTN_BAND, (h + 1) * ATTN_BAND)
            o = jnp.where(sel, oh[rows], o)
            lo = jnp.where(sel, lse[rows], lo)
        return o, lo

    first = jnp.where(t > 0, 0, ATTN_BAND)
    blocks = [(c, j) for c in range(nc) for j in range(tl // ATTN_BAND)]

    def prev_rows(ref, prev_ref, c, j):
        return prev_ref[0, c] if j == 0 else ref[0, c, (j - 1) * ATTN_BAND:j * ATTN_BAND, :]

    def cur_rows(ref, c, j):
        return ref[0, c, j * ATTN_BAND:(j + 1) * ATTN_BAND, :]

    def block_scores(c, j):
        return scores(cur_rows(q_ref, c, j), prev_rows(k_ref, kp_ref, c, j), cur_rows(k_ref, c, j))

    ahead = [block_scores(*blk) for blk in blocks[:SCORE_LOOKAHEAD]]
    for n, (c, j) in enumerate(blocks):
        if n + SCORE_LOOKAHEAD < len(blocks):
            ahead.append(block_scores(*blocks[n + SCORE_LOOKAHEAD]))
        o, lse = finish(ahead[n], prev_rows(v_ref, vp_ref, c, j), cur_rows(v_ref, c, j), first if j == 0 else 0)
        rows = slice(j * ATTN_BAND, (j + 1) * ATTN_BAND)
        o_ref[0, c, rows, :] = o.astype(o_ref.dtype)
        lse_ref[0, c, rows, :] = lse


def _prompt_attention(qkv, g, *, tl, nc):
    B, dil, L, _ = qkv.shape
    sub = tl // ATTN_BAND

    def cur(c):
        return pl.BlockSpec((1, nc, tl, SEG), lambda b, r, t: (b, r, t, c))

    def prev(c):
        return pl.BlockSpec((1, nc, ATTN_BAND, SEG), lambda b, r, t: (b, r, jnp.maximum(t * sub - 1, 0), c))

    out_spec = pl.BlockSpec((1, nc, tl, SEG), lambda b, r, t: (b, r, t, 0))
    return pl.pallas_call(
        functools.partial(_attn_kernel, tl=tl, nc=nc),
        grid=(B, dil // nc, L // tl),
        in_specs=[cur(0), cur(1), cur(2), prev(1), prev(2)],
        out_specs=[out_spec, out_spec],
        out_shape=[jax.ShapeDtypeStruct((B, dil, L, SEG), BF16),
                   jax.ShapeDtypeStruct((B, dil, L, SEG), F32)],
        compiler_params=_cparams(("arbitrary", "arbitrary", "arbitrary")),
        name="prompt_attention_g%d" % g,
    )(qkv, qkv, qkv, qkv, qkv)


def _retention_tables():
    f32 = np.float32
    C = RET_CHUNK
    lg = np.log(f32(1.0) - f32(2.0) ** (f32(-5.0) - np.arange(RET_HEADS, dtype=f32)))
    idx = np.arange(C, dtype=f32)
    diff = idx[:, None] - idx[None, :]
    decay_in = np.where(diff >= 0, np.exp(lg[:, None, None] * np.maximum(diff, f32(0.0))), f32(0.0))
    decay_q = np.exp(lg[:, None] * (idx[None, :] + f32(1.0)))[:, :, None]
    decay_k = np.exp(lg[:, None] * (f32(C - 1.0) - idx[None, :]))[:, None, :]
    return tuple(t.astype(f32) for t in (decay_in, decay_q, decay_k))


def _retention_kernel(qk_ref, kt_ref, v_ref, din_ref, dq_ref, dk_ref, o_ref, state_ref, r_scr, *, tr):
    t = pl.program_id(1)
    n_chunks = tr // RET_CHUNK

    @pl.when(t == 0)
    def _():
        r_scr[...] = jnp.zeros_like(r_scr)

    def key_t(c, h):
        return kt_ref[0, h * RET_QK_DIM:(h + 1) * RET_QK_DIM, c * RET_CHUNK:(c + 1) * RET_CHUNK]

    def val(c, h):
        return v_ref[0, c * RET_CHUNK:(c + 1) * RET_CHUNK, h * RET_V_DIM:(h + 1) * RET_V_DIM]

    states = [[None] * RET_HEADS for _ in range(n_chunks)]
    for h in range(RET_HEADS):
        r = r_scr[h]
        for c in range(n_chunks):
            states[c][h] = r
            kd = (key_t(c, h).astype(F32) * dk_ref[h]).astype(BF16)
            r = r * (_ret_gamma(h) ** RET_CHUNK) + jnp.dot(kd, val(c, h), preferred_element_type=F32)
        r_scr[h] = r

    def query(c, h):
        return qk_ref[0, c * RET_CHUNK:(c + 1) * RET_CHUNK, OFF_RQ + h * RET_QK_DIM:OFF_RQ + (h + 1) * RET_QK_DIM]

    scores = [[(jnp.dot(query(c, h), key_t(c, h), preferred_element_type=F32) * din_ref[h]).astype(BF16)
               for h in range(RET_HEADS)] for c in range(n_chunks)]

    for c in range(n_chunks):
        rows = slice(c * RET_CHUNK, (c + 1) * RET_CHUNK)
        for h in range(RET_HEADS):
            q = query(c, h)
            o = (jnp.dot(scores[c][h], val(c, h), preferred_element_type=F32)
                 + jnp.dot(q, states[c][h].astype(BF16), preferred_element_type=F32) * dq_ref[h])
            mu = jnp.mean(o, axis=1, keepdims=True)
            d = o - mu
            var = jnp.mean(d * d, axis=1, keepdims=True)
            o_ref[0, rows, h * RET_V_DIM:(h + 1) * RET_V_DIM] = (d * lax.rsqrt(var + GN_EPS)).astype(o_ref.dtype)

    @pl.when(t == pl.num_programs(1) - 1)
    def _():
        state_ref[0] = r_scr[...]


def _prompt_retention(dense, keys_t, *, tr):
    B, S, _ = dense.shape
    tabs = _retention_tables()
    blk = RET_V_WIDTH

    def colblk(c):
        return pl.BlockSpec((1, tr, blk), lambda b, t: (b, t, c))

    return pl.pallas_call(
        functools.partial(_retention_kernel, tr=tr),
        grid=(B, S // tr),
        in_specs=[pl.BlockSpec((1, tr, RET_QK_WIDTH), lambda b, t: (b, t, OFF_RQ // RET_QK_WIDTH)),
                  pl.BlockSpec((1, RET_QK_WIDTH, tr), lambda b, t: (b, 0, t)),
                  colblk(OFF_RV // blk)]
                 + [pl.BlockSpec(tab.shape, lambda b, t: (0, 0, 0)) for tab in tabs],
        out_specs=[pl.BlockSpec((1, tr, RET_V_WIDTH), lambda b, t: (b, t, 0)),
                   pl.BlockSpec((1, RET_HEADS, RET_QK_DIM, RET_V_DIM), lambda b, t: (b, 0, 0, 0))],
        out_shape=[jax.ShapeDtypeStruct((B, S, RET_V_WIDTH), BF16),
                   jax.ShapeDtypeStruct((B, RET_HEADS, RET_QK_DIM, RET_V_DIM), F32)],
        scratch_shapes=[pltpu.VMEM((RET_HEADS, RET_QK_DIM, RET_V_DIM), F32)],
        compiler_params=_cparams(("arbitrary", "arbitrary")),
        name="prompt_retention",
    )(dense, keys_t, dense, *tabs)


def _layer_norm(u, g, b):
    mu = jnp.mean(u, axis=1, keepdims=True)
    d = u - mu
    var = jnp.mean(d * d, axis=1, keepdims=True)
    return d * lax.rsqrt(var + LN_EPS) * g + b


def _column(row):
    return jnp.broadcast_to(row, (LANES, LANES)).T


def _window_half_step(first_half, qkv_refs, cache_refs, o_ref, lse_ref, new_refs):
    pad = 16
    row_h = lax.broadcasted_iota(jnp.int32, (pad, LANES), 0)
    lane_h = lax.broadcasted_iota(jnp.int32, (pad, LANES), 1) // HEAD_DIM
    own = row_h == lane_h
    lane = lax.broadcasted_iota(jnp.int32, (LANES, LANES), 1)

    def pick(row):
        return jnp.where(first_half, row[:, :LANES], row[:, LANES:])

    def rolled(c_ref, n_ref, new_row):
        w = c_ref.shape[3]
        new_col = _column(new_row)
        nblk = w // LANES
        cur = pltpu.roll(c_ref[0, 0, :, 0:LANES], LANES - 1, 1)
        for j in range(nblk):
            nxt = pltpu.roll(c_ref[0, 0, :, (j + 1) * LANES:(j + 2) * LANES], LANES - 1, 1) if j + 1 < nblk else new_col
            n_ref[0, 0, :, j * LANES:(j + 1) * LANES] = jnp.where(lane < LANES - 1, cur, nxt)
            cur = nxt

    started = []
    for g in range(N_GROUPS):
        q = pick(qkv_refs[g][0, :, 0:SEG])
        qsel = jnp.where(own, jnp.broadcast_to(q, (pad, LANES)), 0.0)
        s = jnp.dot(qsel.astype(BF16), cache_refs[2 * g][0, 0].astype(BF16), preferred_element_type=F32)
        started.append((qsel, s))

    def finish():
        weights = []
        for g, (qsel, s) in enumerate(started):
            dil = DILATED_PATTERNS[g][1]
            kn = pick(qkv_refs[g][0, :, SEG:2 * SEG])
            vn = pick(qkv_refs[g][0, :, 2 * SEG:3 * SEG])
            pos = lax.broadcasted_iota(jnp.int32, s.shape, 1)
            s = jnp.where((pos & (dil - 1)) == 0, s, NEG_BIG)
            sn = jnp.sum(qsel * kn, axis=1, keepdims=True)
            m = jnp.maximum(jnp.max(s, axis=1, keepdims=True), sn)
            e = jnp.exp(s - m)
            en = jnp.exp(sn - m)
            l = jnp.sum(e, axis=1, keepdims=True) + en
            weights.append(((e / l).astype(BF16), (en / l) * vn, m + jnp.log(l), kn, vn))
        mixed = [lax.dot_general(p, cache_refs[2 * g + 1][0, 0].astype(BF16), (((1,), (1,)), ((), ())),
                                 preferred_element_type=F32) for g, (p, _, _, _, _) in enumerate(weights)]
        for g, (_, new_part, lse, kn, vn) in enumerate(weights):
            o_all = mixed[g] + new_part
            o_ref[0, 0, :, g * LANES:(g + 1) * LANES] = jnp.sum(jnp.where(own, o_all, 0.0), axis=0, keepdims=True)
            lse_ref[0, 0, :, g * LANES:(g + 1) * LANES] = jnp.sum(jnp.where(own, lse, 0.0), axis=0, keepdims=True)
            rolled(cache_refs[2 * g], new_refs[2 * g], kn)
            rolled(cache_refs[2 * g + 1], new_refs[2 * g + 1], vn)

    return finish


N_MERGE_INPUTS = 24
N_SIDE_INPUTS = N_GROUPS + 2 * N_GROUPS
N_SIDE_OUTPUTS = 2 + 2 * N_GROUPS


def _merge_ffn_kernel(*refs, tm, sub, dils, side, steps_per_batch):
    (x_ref, ga_ref, gb_ref, rg_ref, ret_ref, o0_ref, o1_ref, o2_ref, l0_ref, l1_ref, l2_ref,
     gate1_ref, shift2_ref, scale2_ref, gate2_ref, watt_ref, wret_ref, wo_ref, wfi_ref, wfo_ref,
     ln1g_ref, ln1b_ref, ln2g_ref, ln2b_ref) = refs[:N_MERGE_INPUTS]
    n_in = N_MERGE_INPUTS + (N_SIDE_INPUTS if side else 0)
    y_ref = refs[n_in]
    n_out = 1 + (N_SIDE_OUTPUTS if side else 0)
    scratch = refs[n_in + n_out:]
    if side:
        side_in = refs[N_MERGE_INPUTS:n_in]
        side_out = refs[n_in + 1:n_in + n_out]
        step = pl.program_id(0) * steps_per_batch + pl.program_id(1)
        finish_side = _window_half_step(step % 2 == 0, side_in[:N_GROUPS], side_in[N_GROUPS:], side_out[0],
                                        side_out[1], side_out[2:])
    n_sub = tm // sub
    per_sub = 1 + 2 * N_GROUPS
    def rows(s):
        return slice(s * sub, (s + 1) * sub)

    def mod_rows(ref, s):
        return ref[0] if ref.shape[1] == 1 else ref[0, rows(s), :]

    def natural(ref, g, s, which):
        d = dils[g]
        n = sub // d
        if d == 1:
            return ref[0, 0, rows(s), :].astype(F32)
        stage_ref = scratch[s * per_sub + 1 + which * N_GROUPS + g]
        for r in range(d):
            for u in range(GROUP_WIDTH // LANES):
                stage_ref[u, pl.ds(r, n, stride=d), :] = (
                    ref[0, r, s * n:(s + 1) * n, u * LANES:(u + 1) * LANES].astype(F32))
        return jnp.concatenate([stage_ref[u] for u in range(GROUP_WIDTH // LANES)], axis=1)

    def branch_dots(s):
        gated = rg_ref[0, rows(s), :].astype(F32) * ret_ref[0, rows(s), :].astype(F32)
        y_ret = jnp.dot(gated.astype(BF16), wret_ref[...], preferred_element_type=F32)
        lses = [natural(ref, g, s, 0) for g, ref in enumerate((l0_ref, l1_ref, l2_ref))]
        outs = [natural(ref, g, s, 1) for g, ref in enumerate((o0_ref, o1_ref, o2_ref))]
        m = jnp.maximum(jnp.maximum(lses[0], lses[1]), lses[2])
        es = [jnp.exp(l - m) for l in lses]
        att = (es[0] * outs[0] + es[1] * outs[1] + es[2] * outs[2]) / (es[0] + es[1] + es[2])
        y_att = jnp.dot(att.astype(BF16), watt_ref[...], preferred_element_type=F32)
        return y_att, y_ret

    def out_projection(s, y_att, y_ret):
        mix = (ga_ref[0, rows(s), :].astype(F32) * y_att + gb_ref[0, rows(s), :].astype(F32) * y_ret).astype(BF16)
        return jnp.dot(mix, wo_ref[...], preferred_element_type=F32)

    def first_norm(s, mixed):
        x1 = _layer_norm(DEEPNORM_ALPHA * x_ref[0, rows(s), :] + mod_rows(gate1_ref, s) * mixed,
                         ln1g_ref[...], ln1b_ref[...])
        return x1, (x1 * (1.0 + mod_rows(scale2_ref, s)) + mod_rows(shift2_ref, s)).astype(BF16)

    def swiglu(s, h2):
        act_scr = scratch[s * per_sub]
        for c0 in range(0, D_FF, FFN_CHUNK):
            c1 = min(c0 + FFN_CHUNK, D_FF)
            fg = jnp.dot(h2, wfi_ref[:, c0:c1], preferred_element_type=F32)
            fu = jnp.dot(h2, wfi_ref[:, D_FF + c0:D_FF + c1], preferred_element_type=F32)
            act_scr[:, c0:c1] = (fg * _sigmoid(fg) * fu).astype(BF16)

    def ffn_out(s, k0, k1):
        return jnp.dot(scratch[s * per_sub][:, k0:k1], wfo_ref[k0:k1, :], preferred_element_type=F32)

    def second_norm(s, x1, ff):
        y_ref[0, rows(s), :] = _layer_norm(DEEPNORM_ALPHA * x1 + mod_rows(gate2_ref, s) * ff,
                                           ln2g_ref[...], ln2b_ref[...])

    mixed = out_projection(0, *branch_dots(0))
    if side:
        finish_side()
    x1, h2 = first_norm(0, mixed)
    for s in range(n_sub):
        swiglu(s, h2)
        if s + 1 < n_sub:
            nxt = branch_dots(s + 1)
            ff = ffn_out(s, 0, FFN_SPLIT)
            mixed = out_projection(s + 1, *nxt)
            ff = ff + ffn_out(s, FFN_SPLIT, D_FF)
            x1_next, h2 = first_norm(s + 1, mixed)
            second_norm(s, x1, ff)
            x1 = x1_next
        else:
            second_norm(s, x1, ffn_out(s, 0, D_FF))


def _merge_ffn(x, dense, ret, o_groups, lse_groups, mod, weights, ln, *, tm, sub, dils, side=None):
    G, R, _ = x.shape
    mod_rows = mod.shape[1]
    n_tiles = R // tm
    side_args, side_in_specs, side_out_specs, side_out_shapes = [], [], [], []
    if side is not None:
        qkvs, caches = side
        n_seq = caches[0].shape[0]
        assert G * n_tiles == 2 * n_seq

        def seq_half(b, i):
            step = b * n_tiles + i
            return step // 2, step % 2

        row_spec = pl.BlockSpec((1, 1, ATTN_WIDTH), lambda b, i: (seq_half(b, i)[0], 0, 0))
        small_spec = pl.BlockSpec((1, 1, 1, N_GROUPS * LANES), lambda b, i: seq_half(b, i) + (0, 0))
        halves = [c.reshape(n_seq, 2, LANES, c.shape[2]) for c in caches]
        half_specs = [pl.BlockSpec((1, 1, LANES, c.shape[3]), lambda b, i: seq_half(b, i) + (0, 0)) for c in halves]
        side_args = list(qkvs) + halves
        side_in_specs = [row_spec] * N_GROUPS + half_specs
        side_out_specs = [small_spec, small_spec] + half_specs
        side_out_shapes = ([jax.ShapeDtypeStruct((n_seq, 2, 1, N_GROUPS * LANES), F32)] * 2
                           + [jax.ShapeDtypeStruct(c.shape, F32) for c in halves])
    sub_scratch = [pltpu.VMEM((sub, D_FF), BF16)] + [pltpu.VMEM((GROUP_WIDTH // LANES, sub, LANES), F32)] * (2 * N_GROUPS)

    def mod_spec(col):
        if mod_rows == 1:
            return pl.BlockSpec((1, 1, D_MODEL), lambda b, i: (b, 0, col))
        return pl.BlockSpec((1, tm, D_MODEL), lambda b, i: (b, i, col))

    def rows(width, col=0):
        return pl.BlockSpec((1, tm, width), lambda b, i: (b, i, col))

    group_specs = [pl.BlockSpec((1, d, tm // d, GROUP_WIDTH), lambda b, i: (b, 0, i, 0)) for d in dils]
    vec = pl.BlockSpec((1, D_MODEL), lambda b, i: (0, 0))
    outs = pl.pallas_call(
        functools.partial(_merge_ffn_kernel, tm=tm, sub=sub, dils=dils, side=side is not None,
                          steps_per_batch=n_tiles),
        grid=(G, n_tiles),
        in_specs=[rows(D_MODEL), rows(D_MODEL, OFF_GA // D_MODEL), rows(D_MODEL, OFF_GB // D_MODEL),
                  rows(D_MODEL, OFF_RG // D_MODEL), rows(RET_V_WIDTH)] + group_specs + group_specs
                 + [mod_spec(2), mod_spec(3), mod_spec(4), mod_spec(5)]
                 + [_resident(w.shape) for w in weights] + [vec] * 4 + side_in_specs,
        out_specs=[rows(D_MODEL)] + side_out_specs,
        out_shape=[jax.ShapeDtypeStruct((G, R, D_MODEL), F32)] + side_out_shapes,
        scratch_shapes=sub_scratch * (tm // sub),
        compiler_params=_cparams(("arbitrary", "arbitrary")),
        name="merge_ffn",
    )(x, dense, dense, dense, ret, *o_groups, *lse_groups, mod, mod, mod, mod, *weights, *ln, *side_args)
    if side is None:
        return outs[0]
    y, o_s, lse_s, *rolled = outs
    return y, o_s, lse_s, [r.reshape(c.shape) for r, c in zip(rolled, side[1])]


def _sample_retention_kernel(dense_ref, state_ref, ret_ref, nstate_ref):
    for b in range(dense_ref.shape[0]):
        for h in range(RET_HEADS):
            gamma = _ret_gamma(h)
            q = dense_ref[b, :, OFF_RQ + h * RET_QK_DIM:OFF_RQ + (h + 1) * RET_QK_DIM]
            k = dense_ref[b, :, OFF_RK + h * RET_QK_DIM:OFF_RK + (h + 1) * RET_QK_DIM]
            v = dense_ref[b, :, OFF_RV + h * RET_V_DIM:OFF_RV + (h + 1) * RET_V_DIM]
            r_old = state_ref[b, h]
            q_col = jnp.concatenate([_column(q)] * (RET_V_DIM // LANES), axis=1)
            k_col = jnp.concatenate([_column(k)] * (RET_V_DIM // LANES), axis=1)
            qk = jnp.sum(q * k, axis=1, keepdims=True)
            o = qk * v + jnp.sum(q_col * r_old, axis=0, keepdims=True) * gamma
            nstate_ref[b, h] = r_old * gamma + k_col * v
            mu = jnp.mean(o, axis=1, keepdims=True)
            d = o - mu
            var = jnp.mean(d * d, axis=1, keepdims=True)
            ret_ref[b, :, h * RET_V_DIM:(h + 1) * RET_V_DIM] = d * lax.rsqrt(var + GN_EPS)


def _sample_retention(dense, state, *, nb=4):
    Bs = dense.shape[0]
    row = lambda w: pl.BlockSpec((nb, 1, w), lambda b: (b, 0, 0))
    st = pl.BlockSpec((nb, RET_HEADS, RET_QK_DIM, RET_V_DIM), lambda b: (b, 0, 0, 0))
    return pl.pallas_call(
        _sample_retention_kernel,
        grid=(Bs // nb,),
        in_specs=[row(DENSE_WIDTH), st],
        out_specs=[row(RET_V_WIDTH), st],
        out_shape=[jax.ShapeDtypeStruct((Bs, 1, RET_V_WIDTH), F32), jax.ShapeDtypeStruct(state.shape, F32)],
        compiler_params=_cparams(("arbitrary",)),
        name="sample_retention",
    )(dense, state)


def _feature_major(a):
    _, b, w, _, _ = a.shape
    return jnp.transpose(a[0], (0, 2, 3, 1)).reshape(b, GROUP_WIDTH, w)


def _row_major(a):
    b, _, w = a.shape
    return jnp.transpose(a.reshape(b, HEADS_PER_GROUP, HEAD_DIM, w), (0, 3, 1, 2))[None]


def kernel(x_prompt, x_sample, cache_k_w128, cache_v_w128, cache_k_w512, cache_v_w512, cache_k_w2048,
           cache_v_w2048, state_retention, c_prompt, c_sample, w_ada, b_ada, w_in, w_att_out, w_ret_out,
           w_o, ln1_g, ln1_b, w_ffn_in, w_ffn_out, ln2_g, ln2_b):
    assert w_in.shape[0] == 1 and x_sample.shape[1] == 1
    B, S, D = x_prompt.shape
    Bs = x_sample.shape[0]
    dils = tuple(d for _, d in DILATED_PATTERNS)

    w_main = jnp.concatenate([w_in[0][:, W_IN_ROTATE:], w_in[0][:, 2 * ATTN_WIDTH:W_IN_ROTATE]], axis=1).astype(BF16)
    w_qk = w_in[0][:, :2 * ATTN_WIDTH].astype(BF16)
    w_qk_paired = _pair_columns(w_qk)
    weights = tuple(w[0].astype(BF16) for w in (w_att_out, w_ret_out, w_o, w_ffn_in, w_ffn_out))
    ln = (ln1_g, ln1_b, ln2_g, ln2_b)

    mod = _modulation(jnp.concatenate([c_prompt, c_sample], axis=0), w_ada[0], b_ada[0])
    mod_p = mod[:B].reshape(B, 1, 6 * D)
    mod_s = mod[B:].reshape(1, Bs, 6 * D)

    tabs_p = _rope_tables(np.arange(S), paired=True)
    win_rows = tuple(min(w, S) for w, _ in DILATED_PATTERNS)
    dense_p, *rest = _in_projection(x_prompt, mod_p, w_main, w_qk_paired, tabs_p, tm=256, dils=dils, paired=True,
                                    keys_t=True, win_rows=win_rows, out_dtype=BF16)
    qkv_p, keys_t_p, wins_p = rest[:N_GROUPS], rest[N_GROUPS], rest[N_GROUPS + 1:]
    o_p, lse_p = [], []
    for g, d in enumerate(dils):
        tl = min(S // d, 1024)
        o, lse = _prompt_attention(qkv_p[g], g, tl=tl, nc=min(d, 1024 // tl))
        o_p.append(o)
        lse_p.append(lse)
    ret_p, state_p = _prompt_retention(dense_p, keys_t_p, tr=1024)

    xs = x_sample.reshape(1, Bs, D)
    tabs_s = _rope_tables(PAST_LEN + np.arange(1))
    unit = (1,) * N_GROUPS
    dense_s, *qkv_s = _in_projection(xs, mod_s, w_main, w_qk, tabs_s, tm=Bs, dils=unit, paired=False,
                                     keys_t=False, win_rows=None, out_dtype=F32)
    caches = [_feature_major(c) for c in (cache_k_w128, cache_v_w128, cache_k_w512, cache_v_w512,
                                          cache_k_w2048, cache_v_w2048)]
    ret_s, state_s = _sample_retention(dense_s.reshape(Bs, 1, DENSE_WIDTH), state_retention[0])
    y_p, o_s, lse_s, rolled = _merge_ffn(
        x_prompt, dense_p, ret_p, o_p, lse_p, mod_p, weights, ln, tm=512, sub=256, dils=dils,
        side=([a.reshape(Bs, 1, ATTN_WIDTH) for a in qkv_s], caches))

    def by_group(a):
        return [a[:, :, 0, g * LANES:(g + 1) * LANES].reshape(1, 1, Bs, GROUP_WIDTH) for g in range(N_GROUPS)]

    o_sg, lse_sg = by_group(o_s), by_group(lse_s)
    y_s = _merge_ffn(xs, dense_s, ret_s.reshape(1, Bs, RET_V_WIDTH), o_sg, lse_sg, mod_s, weights, ln,
                     tm=Bs, sub=Bs, dils=unit)

    return (y_p, y_s.reshape(Bs, 1, D),
            *[_row_major(w) for w in wins_p], state_p[None],
            *[_row_major(r) for r in rolled], state_s[None])
```
